```python
import math
import jax
import jax.numpy as jnp
from jax import lax
import numpy as np

D_MODEL = 2048
BATCH = 4
SEQ = 2048
DEPTH = 4

GRID_W = 64
CTX_LEN = 256
N_EVEN = (DEPTH + 1) // 2
N_ODD = DEPTH // 2
Q_BLOCK = 128
ROPE_THETA = 10000.0
EPS = 1e-6
N_MOD = 6
HEAD_DIM = 128
A_Q_HEADS = 12
A_KV_HEADS = 4
A_GROUP = A_Q_HEADS // A_KV_HEADS
A_WIDTH = A_Q_HEADS * HEAD_DIM
A_KV_WIDTH = A_KV_HEADS * HEAD_DIM
B_WIDTH = 512
B_CONV = 3
AB_IN = A_WIDTH + 2 * A_KV_WIDTH + 3 * B_WIDTH
AB_OUT = A_WIDTH + B_WIDTH
C_HEADS = 12
C_NOPE = 128
C_ROPE = 64
C_V = 128
C_Q_RANK = 512
C_KV_RANK = 256
C_WIDTH = C_HEADS * C_V
D_WIDTH = 512
S5_GROUP = 16
S5_GROUPS = D_WIDTH // S5_GROUP
S5_STATE = 64
CD_IN = C_Q_RANK + C_KV_RANK + C_ROPE + D_WIDTH
CD_OUT = C_WIDTH + D_WIDTH
N_GROUPS = 4
EXPERTS_PER_GROUP = 4
N_EXPERTS = N_GROUPS * EXPERTS_PER_GROUP
TOP_K = 2
D_EXPERT = 512

kernel_name = "hybrid_dit_gqa_conv_mla_s5_hmoe"


def rmsnorm(x, g):
    xf = x.astype(jnp.float32)
    xf = xf * lax.rsqrt(jnp.mean(xf * xf, axis=-1, keepdims=True) + EPS)
    return xf.astype(x.dtype) * g


def modulate(h, shift, scale):
    return h * (1 + scale) + shift


def axial_rope_tables(row, col, rot_dim):
    d_axis = rot_dim // 2
    inv = ROPE_THETA ** (-jnp.arange(0, d_axis, 2, dtype=jnp.float32) / d_axis)
    ang = jnp.concatenate([row[:, None] * inv, col[:, None] * inv], axis=-1)
    return jnp.cos(ang), jnp.sin(ang)


def apply_rope(x, cos, sin):
    half = x.shape[-1] // 2
    shape = (x.shape[1],) + (1,) * (x.ndim - 3) + (half,)
    cos = cos.reshape(shape).astype(x.dtype)
    sin = sin.reshape(shape).astype(x.dtype)
    x1, x2 = x[..., :half], x[..., half:]
    return jnp.concatenate([x1 * cos - x2 * sin, x2 * cos + x1 * sin], axis=-1)


def block_attention(q, k, v, scale):
    bsz, lq, hkv, grp, dk = q.shape
    nblk = lq // Q_BLOCK
    qb = jnp.moveaxis(q.reshape(bsz, nblk, Q_BLOCK, hkv, grp, dk), 1, 0)

    def one_block(qi):
        s = jnp.einsum('bqhgd,bkhd->bhgqk', qi, k, preferred_element_type=jnp.float32) * scale
        p = jax.nn.softmax(s, axis=-1)
        return jnp.einsum('bhgqk,bkhd->bqhgd', p.astype(v.dtype), v)

    ob = lax.map(one_block, qb)
    return jnp.moveaxis(ob, 0, 1).reshape(bsz, lq, hkv * grp * v.shape[-1])


def short_conv(u, w):
    up = jnp.pad(u, ((0, 0), (1, 1), (0, 0)))
    return up[:, :-2] * w[0] + up[:, 1:-1] * w[1] + up[:, 2:] * w[2]


def gated_short_conv(gates, w):
    bg, cg, ug = gates
    return bg * short_conv(cg * ug, w)


def _gqa_conv_project(h, w_in, q_norm, k_norm, need_q):
    bsz, length = h.shape[0], h.shape[1]
    p = jnp.matmul(h, w_in)
    q, k, v, bg, cg, ug = jnp.split(p, [A_WIDTH, A_WIDTH + A_KV_WIDTH, A_WIDTH + 2 * A_KV_WIDTH, A_WIDTH + 2 * A_KV_WIDTH + B_WIDTH, A_WIDTH + 2 * A_KV_WIDTH + 2 * B_WIDTH], axis=-1)
    k = rmsnorm(k.reshape(bsz, length, A_KV_HEADS, HEAD_DIM), k_norm)
    v = v.reshape(bsz, length, A_KV_HEADS, HEAD_DIM)
    q = rmsnorm(q.reshape(bsz, length, A_KV_HEADS, A_GROUP, HEAD_DIM), q_norm) if need_q else None
    return q, k, v, (bg, cg, ug)


def mixer_ab(h_ctx, h_lat, cos, sin, w_in, q_norm, k_norm, conv_w, w_out, need_ctx):
    q_c, k_c, v_c, g_c = _gqa_conv_project(h_ctx, w_in, q_norm, k_norm, need_ctx)
    q_l, k_l, v_l, g_l = _gqa_conv_project(h_lat, w_in, q_norm, k_norm, True)
    q_l = apply_rope(q_l, cos, sin)
    k_l = apply_rope(k_l, cos, sin)
    scale = HEAD_DIM ** -0.5
    o_l = block_attention(q_l, jnp.concatenate([k_c, k_l], axis=1), jnp.concatenate([v_c, v_l], axis=1), scale)
    y_l = jnp.matmul(jnp.concatenate([o_l, gated_short_conv(g_l, conv_w)], axis=-1), w_out)
    y_c = None
    if need_ctx:
        o_c = block_attention(q_c, k_c, v_c, scale)
        y_c = jnp.matmul(jnp.concatenate([o_c, gated_short_conv(g_c, conv_w)], axis=-1), w_out)
    return y_c, y_l


def _mla_project(h, w_in, q_norm, kv_norm, w_uq, w_ukv, need_q):
    bsz, length = h.shape[0], h.shape[1]
    p = jnp.matmul(h, w_in)
    cq, ckv, kr, u = jnp.split(p, [C_Q_RANK, C_Q_RANK + C_KV_RANK, C_Q_RANK + C_KV_RANK + C_ROPE], axis=-1)
    kv = jnp.matmul(rmsnorm(ckv, kv_norm), w_ukv).reshape(bsz, length, C_HEADS, C_NOPE + C_V)
    q = jnp.matmul(rmsnorm(cq, q_norm), w_uq).reshape(bsz, length, C_HEADS, C_NOPE + C_ROPE) if need_q else None
    return q, kv[..., :C_NOPE], kv[..., C_NOPE:], kr, u


def _mla_keys(k_nope, kr):
    return jnp.concatenate([k_nope, jnp.broadcast_to(kr[:, :, None, :], k_nope.shape[:3] + (C_ROPE,))], axis=-1)


def _s5_discretise(lam_re, lam_im, log_dt, b_re, b_im):
    lam = lax.complex(lam_re.astype(jnp.float32), lam_im.astype(jnp.float32))
    dt = jnp.exp(log_dt.astype(jnp.float32))[:, None]
    a = jnp.exp(lam * dt)
    bbar = ((a - 1.0) / lam)[..., None] * lax.complex(b_re.astype(jnp.float32), b_im.astype(jnp.float32))
    return a, bbar


def _linear_recurrence(a, bu, reverse):
    def combine(e1, e2):
        a1, b1 = e1
        a2, b2 = e2
        return a1 * a2, a2 * b1 + b2
    return lax.associative_scan(combine, (jnp.broadcast_to(a, bu.shape), bu), reverse=reverse, axis=1)


def s5_bidirectional(u_ctx, u_lat, lam_re, lam_im, log_dt, b_re, b_im, c_re, c_im, d_skip, need_ctx):
    uc = u_ctx.astype(jnp.float32).reshape(u_ctx.shape[0], u_ctx.shape[1], S5_GROUPS, S5_GROUP)
    ul = u_lat.astype(jnp.float32).reshape(u_lat.shape[0], u_lat.shape[1], S5_GROUPS, S5_GROUP)
    dsk = d_skip.astype(jnp.float32).reshape(S5_GROUPS, S5_GROUP)
    y_l = dsk * ul
    y_c = dsk * uc if need_ctx else None
    for direction, reverse in ((0, False), (1, True)):
        a, bbar = _s5_discretise(lam_re[direction], lam_im[direction], log_dt[direction], b_re[direction], b_im[direction])
        cmat = lax.complex(c_re[direction].astype(jnp.float32), c_im[direction].astype(jnp.float32))
        _, h_c = _linear_recurrence(a, jnp.einsum('gnc,blgc->blgn', bbar, uc.astype(jnp.complex64)), reverse)
        s0 = h_c[:, 0] if reverse else h_c[:, -1]
        a_pow, h_l = _linear_recurrence(a, jnp.einsum('gnc,blgc->blgn', bbar, ul.astype(jnp.complex64)), reverse)
        h_l = h_l + a_pow * s0[:, None]
        y_l = y_l + jnp.real(jnp.einsum('gcn,blgn->blgc', cmat, h_l))
        if need_ctx:
            y_c = y_c + jnp.real(jnp.einsum('gcn,blgn->blgc', cmat, h_c))
    out_l = y_l.reshape(u_lat.shape).astype(u_lat.dtype)
    out_c = y_c.reshape(u_ctx.shape).astype(u_ctx.dtype) if need_ctx else None
    return out_c, out_l


def s5_glu(y, w_glu):
    z = jax.nn.gelu(y)
    return z * jax.nn.sigmoid(jnp.matmul(z, w_glu))


def mixer_cd(h_ctx, h_lat, cos, sin, w_in, q_norm, kv_norm, w_uq, w_ukv, lam_re, lam_im, log_dt, b_re, b_im, c_re, c_im, d_skip, w_glu, w_out, need_ctx):
    q_c, kn_c, v_c, kr_c, u_c = _mla_project(h_ctx, w_in, q_norm, kv_norm, w_uq, w_ukv, need_ctx)
    q_l, kn_l, v_l, kr_l, u_l = _mla_project(h_lat, w_in, q_norm, kv_norm, w_uq, w_ukv, True)
    q_l = jnp.concatenate([q_l[..., :C_NOPE], apply_rope(q_l[..., C_NOPE:], cos, sin)], axis=-1)
    k_c = _mla_keys(kn_c, kr_c)
    k_l = _mla_keys(kn_l, apply_rope(kr_l, cos, sin))
    scale = (C_NOPE + C_ROPE) ** -0.5
    o_l = block_attention(q_l[:, :, :, None, :], jnp.concatenate([k_c, k_l], axis=1), jnp.concatenate([v_c, v_l], axis=1), scale)
    s_c, s_l = s5_bidirectional(u_c, u_l, lam_re, lam_im, log_dt, b_re, b_im, c_re, c_im, d_skip, need_ctx)
    y_l = jnp.matmul(jnp.concatenate([o_l, s5_glu(s_l, w_glu)], axis=-1), w_out)
    y_c = None
    if need_ctx:
        o_c = block_attention(q_c[:, :, :, None, :], k_c, v_c, scale)
        y_c = jnp.matmul(jnp.concatenate([o_c, s5_glu(s_c, w_glu)], axis=-1), w_out)
    return y_c, y_l


def hier_moe(h, w_group, w_expert, w_gate, w_up, w_down):
    shp = h.shape
    t = h.reshape(-1, shp[-1])
    p_group = jax.nn.softmax(jnp.matmul(t, w_group, preferred_element_type=jnp.float32), axis=-1)
    g_w, g_idx = lax.top_k(p_group, 1)
    logits_all = jnp.einsum('nd,gde->nge', t, w_expert, preferred_element_type=jnp.float32)
    logits_e = jnp.einsum('ng,nge->ne', jax.nn.one_hot(g_idx[:, 0], N_GROUPS, dtype=jnp.float32), logits_all)
    e_w, e_idx = lax.top_k(jax.nn.softmax(logits_e, axis=-1), TOP_K)
    e_w = e_w / jnp.sum(e_w, axis=-1, keepdims=True)
    expert_id = g_idx * EXPERTS_PER_GROUP + e_idx
    gate = jnp.sum(jax.nn.one_hot(expert_id, N_EXPERTS, dtype=jnp.float32) * (g_w * e_w)[..., None], axis=1)
    hg = jnp.einsum('nd,edf->nef', t, w_gate)
    hu = jnp.einsum('nd,edf->nef', t, w_up)
    act = jax.nn.silu(hg) * hu * gate[:, :, None].astype(t.dtype)
    return jnp.einsum('nef,efd->nd', act, w_down).reshape(shp)


def setup_inputs(seed: int = 0) -> dict:
    key = jax.random.key(seed)
    ks = iter(jax.random.split(key, 48))
    f32 = jnp.float32

    def nrm(shape, scale):
        return jax.random.normal(next(ks), shape, f32) * scale

    def gain(shape):
        return 1.0 + nrm(shape, 0.05)

    lam_im_base = jnp.pi * jnp.arange(S5_STATE, dtype=f32)
    return {
        "x": nrm((BATCH, SEQ, D_MODEL), 1.0),
        "c": nrm((BATCH, D_MODEL), 1.0),
        "ctx": nrm((BATCH, CTX_LEN, D_MODEL), 1.0),
        "c_ctx": nrm((D_MODEL,), 1.0),
        "mod_w": nrm((DEPTH, D_MODEL, N_MOD * D_MODEL), 0.5 * D_MODEL ** -0.5),
        "mod_b": nrm((DEPTH, N_MOD * D_MODEL), 0.01),
        "norm_mix": gain((DEPTH, D_MODEL)),
        "norm_ffn": gain((DEPTH, D_MODEL)),
        "ab_w_in": nrm((N_EVEN, D_MODEL, AB_IN), D_MODEL ** -0.5),
        "ab_q_norm": gain((N_EVEN, HEAD_DIM)),
        "ab_k_norm": gain((N_EVEN, HEAD_DIM)),
        "ab_conv_w": nrm((N_EVEN, B_CONV, B_WIDTH), B_CONV ** -0.5),
        "ab_w_out": nrm((N_EVEN, AB_OUT, D_MODEL), AB_OUT ** -0.5),
        "cd_w_in": nrm((N_ODD, D_MODEL, CD_IN), D_MODEL ** -0.5),
        "cd_q_norm": gain((N_ODD, C_Q_RANK)),
        "cd_kv_norm": gain((N_ODD, C_KV_RANK)),
        "cd_w_uq": nrm((N_ODD, C_Q_RANK, C_HEADS * (C_NOPE + C_ROPE)), C_Q_RANK ** -0.5),
        "cd_w_ukv": nrm((N_ODD, C_KV_RANK, C_HEADS * (C_NOPE + C_V)), C_KV_RANK ** -0.5),
        "s5_lam_re": -0.5 + nrm((N_ODD, 2, S5_GROUPS, S5_STATE), 0.01),
        "s5_lam_im": lam_im_base + nrm((N_ODD, 2, S5_GROUPS, S5_STATE), 0.01),
        "s5_log_dt": jax.random.uniform(next(ks), (N_ODD, 2, S5_GROUPS), f32, math.log(1e-3), math.log(1e-1)),
        "s5_b_re": nrm((N_ODD, 2, S5_GROUPS, S5_STATE, S5_GROUP), (2 * S5_GROUP) ** -0.5),
        "s5_b_im": nrm((N_ODD, 2, S5_GROUPS, S5_STATE, S5_GROUP), (2 * S5_GROUP) ** -0.5),
        "s5_c_re": nrm((N_ODD, 2, S5_GROUPS, S5_GROUP, S5_STATE), (2 * S5_STATE) ** -0.5),
        "s5_c_im": nrm((N_ODD, 2, S5_GROUPS, S5_GROUP, S5_STATE), (2 * S5_STATE) ** -0.5),
        "s5_d": nrm((N_ODD, D_WIDTH), 1.0),
        "s5_w_glu": nrm((N_ODD, D_WIDTH, D_WIDTH), D_WIDTH ** -0.5),
        "cd_w_out": nrm((N_ODD, CD_OUT, D_MODEL), CD_OUT ** -0.5),
        "moe_w_group": nrm((DEPTH, D_MODEL, N_GROUPS), D_MODEL ** -0.5),
        "moe_w_expert": nrm((DEPTH, N_GROUPS, D_MODEL, EXPERTS_PER_GROUP), D_MODEL ** -0.5),
        "moe_w_gate": nrm((DEPTH, N_EXPERTS, D_MODEL, D_EXPERT), D_MODEL ** -0.5),
        "moe_w_up": nrm((DEPTH, N_EXPERTS, D_MODEL, D_EXPERT), D_MODEL ** -0.5),
        "moe_w_down": nrm((DEPTH, N_EXPERTS, D_EXPERT, D_MODEL), D_EXPERT ** -0.5),
        "final_norm": gain((D_MODEL,)),
    }


def reference(x, c, ctx, c_ctx, mod_w, mod_b, norm_mix, norm_ffn, ab_w_in, ab_q_norm, ab_k_norm, ab_conv_w, ab_w_out, cd_w_in, cd_q_norm, cd_kv_norm, cd_w_uq, cd_w_ukv, s5_lam_re, s5_lam_im, s5_log_dt, s5_b_re, s5_b_im, s5_c_re, s5_c_im, s5_d, s5_w_glu, cd_w_out, moe_w_group, moe_w_expert, moe_w_gate, moe_w_up, moe_w_down, final_norm):
    seq = x.shape[1]
    rows = seq // GRID_W
    row_ids = jnp.repeat(jnp.arange(rows, dtype=jnp.float32), GRID_W)
    col_ids = jnp.tile(jnp.arange(GRID_W, dtype=jnp.float32), rows)
    cos_a, sin_a = axial_rope_tables(row_ids, col_ids, HEAD_DIM)
    cos_c, sin_c = axial_rope_tables(row_ids, col_ids, C_ROPE)

    x_lat, x_ctx = x, ctx
    for i in range(DEPTH):
        need_ctx = i < DEPTH - 1
        j = i // 2
        mod_l = jnp.split(jnp.matmul(jax.nn.silu(c), mod_w[i]) + mod_b[i], N_MOD, axis=-1)
        mod_c = jnp.split(jnp.matmul(jax.nn.silu(c_ctx), mod_w[i]) + mod_b[i], N_MOD, axis=-1)
        sh1, sc1, g1, sh2, sc2, g2 = [m[:, None, :] for m in mod_l]
        h_l = modulate(rmsnorm(x_lat, norm_mix[i]), sh1, sc1)
        h_c = modulate(rmsnorm(x_ctx, norm_mix[i]), mod_c[0], mod_c[1])
        if i % 2 == 0:
            y_c, y_l = mixer_ab(h_c, h_l, cos_a, sin_a, ab_w_in[j], ab_q_norm[j], ab_k_norm[j], ab_conv_w[j], ab_w_out[j], need_ctx)
        else:
            y_c, y_l = mixer_cd(h_c, h_l, cos_c, sin_c, cd_w_in[j], cd_q_norm[j], cd_kv_norm[j], cd_w_uq[j], cd_w_ukv[j], s5_lam_re[j], s5_lam_im[j], s5_log_dt[j], s5_b_re[j], s5_b_im[j], s5_c_re[j], s5_c_im[j], s5_d[j], s5_w_glu[j], cd_w_out[j], need_ctx)
        x_lat = x_lat + g1 * y_l
        h_l = modulate(rmsnorm(x_lat, norm_ffn[i]), sh2, sc2)
        x_lat = x_lat + g2 * hier_moe(h_l, moe_w_group[i], moe_w_expert[i], moe_w_gate[i], moe_w_up[i], moe_w_down[i])
        if need_ctx:
            x_ctx = x_ctx + mod_c[2] * y_c
            h_c = modulate(rmsnorm(x_ctx, norm_ffn[i]), mod_c[3], mod_c[4])
            x_ctx = x_ctx + mod_c[5] * hier_moe(h_c, moe_w_group[i], moe_w_expert[i], moe_w_gate[i], moe_w_up[i], moe_w_down[i])
    return rmsnorm(x_lat, final_norm)
```

```python
import functools
import math

import numpy as np
import jax
import jax.numpy as jnp
from jax import lax
from jax.experimental import pallas as pl
from jax.experimental.pallas import tpu as pltpu

F32 = jnp.float32
BF16 = jnp.bfloat16

D_MODEL = 2048
BATCH = 4
SEQ = 2048
DEPTH = 4
GRID_W = 64
CTX_LEN = 256
ROPE_THETA = 10000.0
EPS = 1e-6
N_MOD = 6
HEAD_DIM = 128
A_Q_HEADS = 12
A_KV_HEADS = 4
A_GROUP = A_Q_HEADS // A_KV_HEADS
A_WIDTH = A_Q_HEADS * HEAD_DIM
A_KV_WIDTH = A_KV_HEADS * HEAD_DIM
B_WIDTH = 512
AB_IN = A_WIDTH + 2 * A_KV_WIDTH + 3 * B_WIDTH
C_HEADS = 12
C_NOPE = 128
C_ROPE = 64
C_V = 128
C_Q_RANK = 512
C_KV_RANK = 256
C_WIDTH = C_HEADS * C_V
D_WIDTH = 512
S5_GROUP = 16
S5_GROUPS = D_WIDTH // S5_GROUP
S5_STATE = 64
N_GROUPS = 4
EXPERTS_PER_GROUP = 4
N_EXPERTS = N_GROUPS * EXPERTS_PER_GROUP
D_EXPERT = 512

N_LAT = BATCH * SEQ
N_CTX = BATCH * CTX_LEN
N_TOK = N_LAT + N_CTX
CTX_ROW = BATCH
CD_IN_PAD = 1408

TM = 512
TQ = 256
TE = 256
N_SORT = 2 * N_TOK + N_EXPERTS * TE
N_ETILES = N_SORT // TE
S5_CHUNK = 16
S5_NCHUNK = (CTX_LEN + SEQ) // S5_CHUNK
S5_CTX_CHUNKS = CTX_LEN // S5_CHUNK
S5_BPAD = 8
S5_ROWS = S5_NCHUNK * S5_BPAD
S5_PAIRS = S5_GROUPS // 2
VMEM_LIMIT = 48 * 1024 * 1024


def _cparams(sem):
    return pltpu.CompilerParams(dimension_semantics=sem, vmem_limit_bytes=VMEM_LIMIT)


def _mod_row(tile, tile_rows):
    r0 = tile * tile_rows
    return jnp.where(r0 >= N_LAT, CTX_ROW, r0 // SEQ)


def _rope_block(tile, tile_rows):
    r0 = tile * tile_rows
    return jnp.where(r0 >= N_LAT, SEQ // tile_rows, (r0 % SEQ) // tile_rows)


def _rms(x):
    return x * lax.rsqrt(jnp.mean(x * x, axis=-1, keepdims=True) + EPS)


def _mod_kernel(cc_ref, w_ref, b_ref, o_ref):
    cc = cc_ref[...]
    s = (cc * jax.nn.sigmoid(cc)).astype(BF16)
    o_ref[...] = jnp.dot(s, w_ref[...].astype(BF16), preferred_element_type=F32) + b_ref[...]


def _modulation(cc, mod_w, mod_b):
    tn = 1024
    nout = N_MOD * D_MODEL
    return pl.pallas_call(
        _mod_kernel,
        out_shape=jax.ShapeDtypeStruct((DEPTH, 8, nout), F32),
        grid=(DEPTH, nout // tn),
        in_specs=[
            pl.BlockSpec((8, D_MODEL), lambda l, j: (0, 0)),
            pl.BlockSpec((None, D_MODEL, tn), lambda l, j: (l, 0, j)),
            pl.BlockSpec((None, 1, tn), lambda l, j: (l, 0, j)),
        ],
        out_specs=pl.BlockSpec((None, 8, tn), lambda l, j: (l, 0, j)),
        compiler_params=_cparams(("arbitrary", "arbitrary")),
        name="modulation",
    )(cc, mod_w, mod_b.reshape(DEPTH, 1, nout))


def _norm_linear_kernel(x_ref, g_ref, sh_ref, sc_ref, w_ref, o_ref, h_scr, *, modulate, rows):
    @pl.when(pl.program_id(1) == 0)
    def _():
        for r in range(0, rows, 256):
            h = _rms(x_ref[r:r + 256, :]) * g_ref[...]
            if modulate:
                h = h * (1.0 + sc_ref[...]) + sh_ref[...]
            h_scr[r:r + 256, :] = h.astype(BF16)

    o_ref[...] = jnp.dot(h_scr[...], w_ref[...], preferred_element_type=F32)


def _norm_linear(x, xcol, kdim, g, mod, which, w, tn):
    nout = w.shape[1]
    modulate = mod is not None
    if modulate:
        sh_spec = pl.BlockSpec((None, None, 1, kdim), lambda i, j: (_mod_row(i, TM), which, 0, 0))
        sc_spec = pl.BlockSpec((None, None, 1, kdim), lambda i, j: (_mod_row(i, TM), which + 1, 0, 0))
        sh = sc = mod
    else:
        sh_spec = sc_spec = pl.BlockSpec((1, kdim), lambda i, j: (0, 0))
        sh = sc = g
    return pl.pallas_call(
        functools.partial(_norm_linear_kernel, modulate=modulate, rows=TM),
        out_shape=jax.ShapeDtypeStruct((N_TOK, nout), F32),
        grid=(N_TOK // TM, nout // tn),
        in_specs=[
            pl.BlockSpec((TM, kdim), lambda i, j: (i, xcol)),
            pl.BlockSpec((1, kdim), lambda i, j: (0, 0)),
            sh_spec, sc_spec,
            pl.BlockSpec((kdim, tn), lambda i, j: (0, j)),
        ],
        out_specs=pl.BlockSpec((TM, tn), lambda i, j: (i, j)),
        scratch_shapes=[pltpu.VMEM((TM, kdim), BF16)],
        compiler_params=_cparams(("arbitrary", "arbitrary")),
        name="norm_linear",
    )(x, g, sh, sc, w)


def _out_linear_kernel(a1_ref, a2_ref, w1_ref, w2_ref, x_ref, gate_ref, o_ref):
    acc = jnp.dot(a1_ref[...], w1_ref[...], preferred_element_type=F32)
    acc = acc + jnp.dot(a2_ref[...], w2_ref[...], preferred_element_type=F32)
    o_ref[...] = x_ref[...] + gate_ref[...] * acc


def _out_linear(a1, a2, w, x, mod, which):
    tn = 1024
    k1, k2 = a1.shape[1], a2.shape[1]
    return pl.pallas_call(
        _out_linear_kernel,
        out_shape=jax.ShapeDtypeStruct((N_TOK, D_MODEL), F32),
        grid=(N_TOK // TM, D_MODEL // tn),
        in_specs=[
            pl.BlockSpec((TM, k1), lambda i, j: (i, 0)),
            pl.BlockSpec((TM, k2), lambda i, j: (i, 0)),
            pl.BlockSpec((k1, tn), lambda i, j: (0, j)),
            pl.BlockSpec((k2, tn), lambda i, j: (k1 // k2, j)),
            pl.BlockSpec((TM, tn), lambda i, j: (i, j)),
            pl.BlockSpec((None, None, 1, tn), lambda i, j: (_mod_row(i, TM), which, 0, j)),
        ],
        out_specs=pl.BlockSpec((TM, tn), lambda i, j: (i, j)),
        compiler_params=_cparams(("arbitrary", "arbitrary")),
        name="out_linear",
    )(a1, a2, w, w, x, mod)


def _rope_tables(rot_dim):
    rows = SEQ // GRID_W
    row_ids = np.repeat(np.arange(rows, dtype=np.float32), GRID_W)
    col_ids = np.tile(np.arange(GRID_W, dtype=np.float32), rows)
    d_axis = rot_dim // 2
    inv = (np.float32(ROPE_THETA) ** (-np.arange(0, d_axis, 2, dtype=np.float32) / np.float32(d_axis))).astype(np.float32)
    ang = np.concatenate([row_ids[:, None] * inv, col_ids[:, None] * inv], axis=-1).astype(np.float32)
    cos, sin = np.cos(ang).astype(np.float32), np.sin(ang).astype(np.float32)
    reps = 128 // rot_dim
    cos_f = np.tile(np.concatenate([cos, cos], axis=-1), (1, reps))
    sin_f = np.tile(np.concatenate([-sin, sin], axis=-1), (1, reps))
    cos_f = np.concatenate([cos_f, np.ones((512, 128), np.float32)], axis=0)
    sin_f = np.concatenate([sin_f, np.zeros((512, 128), np.float32)], axis=0)
    return jnp.asarray(cos_f), jnp.asarray(sin_f)


def _rope128(x, cos, sin):
    return x * cos + pltpu.roll(x, 64, 1) * sin


def _rope64(x, cos, sin):
    lane = lax.broadcasted_iota(jnp.int32, x.shape, 1)
    swapped = jnp.where((lane % 64) < 32, pltpu.roll(x, 96, 1), pltpu.roll(x, 32, 1))
    return x * cos + swapped * sin


def _gqa_kv_kernel(k_ref, v_ref, kn_ref, cos_ref, sin_ref, o_ref):
    cos, sin = cos_ref[...], sin_ref[...]
    for h in range(A_KV_HEADS):
        k = _rms(k_ref[:, h * 128:(h + 1) * 128]) * kn_ref[...]
        o_ref[:, h * 128:(h + 1) * 128] = _rope128(k, cos, sin).astype(BF16)
    o_ref[:, A_KV_WIDTH:] = v_ref[...].astype(BF16)


def _gqa_kv(p, k_norm, cos, sin):
    t = 512
    return pl.pallas_call(
        _gqa_kv_kernel,
        out_shape=jax.ShapeDtypeStruct((N_TOK, 2 * A_KV_WIDTH), BF16),
        grid=(N_TOK // t,),
        in_specs=[
            pl.BlockSpec((t, A_KV_WIDTH), lambda i: (i, A_WIDTH // A_KV_WIDTH)),
            pl.BlockSpec((t, A_KV_WIDTH), lambda i: (i, A_WIDTH // A_KV_WIDTH + 1)),
            pl.BlockSpec((1, 128), lambda i: (0, 0)),
            pl.BlockSpec((t, 128), lambda i: (_rope_block(i, t), 0)),
            pl.BlockSpec((t, 128), lambda i: (_rope_block(i, t), 0)),
        ],
        out_specs=pl.BlockSpec((t, 2 * A_KV_WIDTH), lambda i: (i, 0)),
        compiler_params=_cparams(("arbitrary",)),
        name="gqa_kv",
    )(p, p, k_norm, cos, sin)


def _softmax_pv(q, keys, vals):
    dn = (((1,), (1,)), ((), ()))
    s = [lax.dot_general(q, k, dn, preferred_element_type=F32) for k in keys]
    m = functools.reduce(jnp.maximum, [jnp.max(si, axis=-1, keepdims=True) for si in s])
    p = [jnp.exp(si - m) for si in s]
    l = functools.reduce(jnp.add, [jnp.sum(pi, axis=-1, keepdims=True) for pi in p])
    acc = functools.reduce(jnp.add, [jnp.dot(pi.astype(BF16), v, preferred_element_type=F32)
                                     for pi, v in zip(p, vals)])
    return acc / l


def _gqa_attn_kernel(q_ref, qn_ref, cos_ref, sin_ref, kc_ref, vc_ref, kl_ref, vl_ref, o_ref):
    cos, sin = cos_ref[...], sin_ref[...]
    scale = HEAD_DIM ** -0.5
    qs = []
    for g in range(A_GROUP):
        q = _rms(q_ref[:, g * 128:(g + 1) * 128]) * qn_ref[...]
        qs.append((_rope128(q, cos, sin) * scale).astype(BF16))
    q = jnp.concatenate(qs, axis=0)

    def emit(o):
        for g in range(A_GROUP):
            o_ref[:, g * 128:(g + 1) * 128] = o[g * TQ:(g + 1) * TQ].astype(BF16)

    is_ctx = pl.program_id(2) == SEQ // TQ

    @pl.when(jnp.logical_not(is_ctx))
    def _():
        emit(_softmax_pv(q, [kc_ref[...], kl_ref[...]], [vc_ref[...], vl_ref[...]]))

    @pl.when(is_ctx)
    def _():
        emit(_softmax_pv(q, [kc_ref[...]], [vc_ref[...]]))


def _attn_qrow(b, t):
    nq = SEQ // TQ
    return jnp.where(t == nq, N_LAT // TQ + b, b * nq + t)


def _gqa_attention(p, kv, q_norm, cos, sin):
    qw = A_GROUP * HEAD_DIM
    ctx_blk = N_LAT // CTX_LEN
    qspec = pl.BlockSpec((TQ, qw), lambda b, h, t: (_attn_qrow(b, t), h))
    rope = pl.BlockSpec((TQ, 128), lambda b, h, t: (_rope_block(_attn_qrow(b, t), TQ), 0))
    return pl.pallas_call(
        _gqa_attn_kernel,
        out_shape=jax.ShapeDtypeStruct((N_TOK, A_WIDTH), BF16),
        grid=(BATCH, A_KV_HEADS, SEQ // TQ + 1),
        in_specs=[
            qspec,
            pl.BlockSpec((1, 128), lambda b, h, t: (0, 0)),
            rope, rope,
            pl.BlockSpec((CTX_LEN, 128), lambda b, h, t: (ctx_blk + b, h)),
            pl.BlockSpec((CTX_LEN, 128), lambda b, h, t: (ctx_blk + b, A_KV_HEADS + h)),
            pl.BlockSpec((SEQ, 128), lambda b, h, t: (b, h)),
            pl.BlockSpec((SEQ, 128), lambda b, h, t: (b, A_KV_HEADS + h)),
        ],
        out_specs=qspec,
        compiler_params=_cparams(("arbitrary",) * 3),
        name="gqa_attn",
    )(p, q_norm, cos, sin, kv, kv, kv, kv)


def _conv_kernel(bg_ref, cg_ref, ug_ref, cgp_ref, ugp_ref, cgn_ref, ugn_ref, w_ref, o_ref, *, rows):
    i = pl.program_id(0)
    per_seq = SEQ // rows
    is_ctx = i >= N_LAT // rows
    is_start = jnp.logical_or(is_ctx, i % per_seq == 0)
    is_end = jnp.logical_or(is_ctx, i % per_seq == per_seq - 1)
    m = cg_ref[...] * ug_ref[...]
    m_prev = jnp.where(is_start, 0.0, cgp_ref[7:8, :] * ugp_ref[7:8, :])
    m_next = jnp.where(is_end, 0.0, cgn_ref[0:1, :] * ugn_ref[0:1, :])
    row = lax.broadcasted_iota(jnp.int32, m.shape, 0)
    down = jnp.where(row == 0, m_prev, pltpu.roll(m, 1, 0))
    up = jnp.where(row == rows - 1, m_next, pltpu.roll(m, rows - 1, 0))
    conv = down * w_ref[0:1, :] + m * w_ref[1:2, :] + up * w_ref[2:3, :]
    o_ref[...] = (bg_ref[...] * conv).astype(BF16)


def _gated_conv(p, conv_w):
    rows = CTX_LEN
    base = (A_WIDTH + 2 * A_KV_WIDTH) // B_WIDTH
    halo = rows // 8
    last = N_TOK // 8 - 1
    main = lambda c: pl.BlockSpec((rows, B_WIDTH), lambda i: (i, base + c))
    prev = lambda c: pl.BlockSpec((8, B_WIDTH), lambda i: (jnp.maximum(i * halo - 1, 0), base + c))
    nxt = lambda c: pl.BlockSpec((8, B_WIDTH), lambda i: (jnp.minimum((i + 1) * halo, last), base + c))
    return pl.pallas_call(
        functools.partial(_conv_kernel, rows=rows),
        out_shape=jax.ShapeDtypeStruct((N_TOK, B_WIDTH), BF16),
        grid=(N_TOK // rows,),
        in_specs=[main(0), main(1), main(2), prev(1), prev(2), nxt(1), nxt(2),
                  pl.BlockSpec((3, B_WIDTH), lambda i: (0, 0))],
        out_specs=pl.BlockSpec((rows, B_WIDTH), lambda i: (i, 0)),
        compiler_params=_cparams(("arbitrary",)),
        name="gated_conv",
    )(p, p, p, p, p, p, p, conv_w)


def _mla_kr_kernel(kr_ref, cos_ref, sin_ref, o_ref):
    x = kr_ref[...]
    lane = lax.broadcasted_iota(jnp.int32, x.shape, 1)
    r = jnp.where(lane < C_ROPE, _rope64(x, cos_ref[...], sin_ref[...]), 0.0)
    o_ref[:, 0:128] = r.astype(BF16)
    o_ref[:, 128:256] = pltpu.roll(r, 64, 1).astype(BF16)


def _mla_kr(p, cos, sin):
    t = 512
    col = (C_Q_RANK + C_KV_RANK + D_WIDTH) // 128
    return pl.pallas_call(
        _mla_kr_kernel,
        out_shape=jax.ShapeDtypeStruct((N_TOK, 256), BF16),
        grid=(N_TOK // t,),
        in_specs=[
            pl.BlockSpec((t, 128), lambda i: (i, col)),
            pl.BlockSpec((t, 128), lambda i: (_rope_block(i, t), 0)),
            pl.BlockSpec((t, 128), lambda i: (_rope_block(i, t), 0)),
        ],
        out_specs=pl.BlockSpec((t, 256), lambda i: (i, 0)),
        compiler_params=_cparams(("arbitrary",)),
        name="mla_kr",
    )(p, cos, sin)


def _mla_attn_kernel(qn_ref, qr_ref, cos_ref, sin_ref, kvc_ref, krc_ref, kvl_ref, krl_ref, o_ref):
    scale = (C_NOPE + C_ROPE) ** -0.5
    qr = _rope64(qr_ref[...], cos_ref[...], sin_ref[...])
    lane = lax.broadcasted_iota(jnp.int32, qr.shape, 1)
    is_ctx = pl.program_id(2) == SEQ // TQ
    for hh in range(2):
        sel = (lane < 64) if hh == 0 else (lane >= 64)
        q = jnp.concatenate([qn_ref[:, hh * 128:(hh + 1) * 128], jnp.where(sel, qr, 0.0)], axis=1)
        q = (q * scale).astype(BF16)
        kn = slice(hh * 256, hh * 256 + 128)
        vv = slice(hh * 256 + 128, hh * 256 + 256)
        kr = slice(hh * 128, (hh + 1) * 128)
        out = slice(hh * 128, (hh + 1) * 128)

        def keys_vals(kv_ref, kr_ref):
            return (jnp.concatenate([kv_ref[:, kn].astype(BF16), kr_ref[:, kr]], axis=1),
                    kv_ref[:, vv].astype(BF16))

        @pl.when(jnp.logical_not(is_ctx))
        def _():
            kc, vc = keys_vals(kvc_ref, krc_ref)
            kl, vl = keys_vals(kvl_ref, krl_ref)
            o_ref[:, out] = _softmax_pv(q, [kc, kl], [vc, vl]).astype(BF16)

        @pl.when(is_ctx)
        def _():
            kc, vc = keys_vals(kvc_ref, krc_ref)
            o_ref[:, out] = _softmax_pv(q, [kc], [vc]).astype(BF16)


def _mla_attention(q, kv, krr, cos, sin):
    ctx_blk = N_LAT // CTX_LEN
    rope_col = C_HEADS * C_NOPE // 128
    rope = pl.BlockSpec((TQ, 128), lambda b, h, t: (_rope_block(_attn_qrow(b, t), TQ), 0))
    return pl.pallas_call(
        _mla_attn_kernel,
        out_shape=jax.ShapeDtypeStruct((N_TOK, C_WIDTH), BF16),
        grid=(BATCH, C_HEADS // 2, SEQ // TQ + 1),
        in_specs=[
            pl.BlockSpec((TQ, 256), lambda b, h, t: (_attn_qrow(b, t), h)),
            pl.BlockSpec((TQ, 128), lambda b, h, t: (_attn_qrow(b, t), rope_col + h)),
            rope, rope,
            pl.BlockSpec((CTX_LEN, 512), lambda b, h, t: (ctx_blk + b, h)),
            pl.BlockSpec((CTX_LEN, 256), lambda b, h, t: (ctx_blk + b, 0)),
            pl.BlockSpec((SEQ, 512), lambda b, h, t: (b, h)),
            pl.BlockSpec((SEQ, 256), lambda b, h, t: (b, 0)),
        ],
        out_specs=pl.BlockSpec((TQ, 256), lambda b, h, t: (_attn_qrow(b, t), h)),
        compiler_params=_cparams(("arbitrary",) * 3),
        name="mla_attn",
    )(q, q, cos, sin, kv, krr, kv, krr)


def _s5_matrices(lam_re, lam_im, log_dt, b_re, b_im, c_re, c_im):
    hi = lax.Precision.HIGHEST
    lam_re, lam_im = lam_re.astype(F32), lam_im.astype(F32)
    dt = jnp.exp(log_dt.astype(F32))[..., None]
    ks = jnp.arange(S5_CHUNK + 1, dtype=F32)[:, None, None, None]
    mag = jnp.exp(lam_re[None] * dt[None] * ks)
    ang = lam_im[None] * dt[None] * ks
    pw_re, pw_im = mag * jnp.cos(ang), mag * jnp.sin(ang)
    a_re, a_im = pw_re[1], pw_im[1]
    den = lam_re * lam_re + lam_im * lam_im
    f_re = ((a_re - 1.0) * lam_re + a_im * lam_im) / den
    f_im = (a_im * lam_re - (a_re - 1.0) * lam_im) / den
    b_re, b_im = b_re.astype(F32), b_im.astype(F32)
    bb_re = f_re[..., None] * b_re - f_im[..., None] * b_im
    bb_im = f_re[..., None] * b_im + f_im[..., None] * b_re
    c_re, c_im = c_re.astype(F32), c_im.astype(F32)

    ab_re = pw_re[:S5_CHUNK, ..., None] * bb_re[None] - pw_im[:S5_CHUNK, ..., None] * bb_im[None]
    ab_im = pw_re[:S5_CHUNK, ..., None] * bb_im[None] + pw_im[:S5_CHUNK, ..., None] * bb_re[None]
    kern = (jnp.einsum('dgcn,ldgne->ldgce', c_re, ab_re, precision=hi)
            - jnp.einsum('dgcn,ldgne->ldgce', c_im, ab_im, precision=hi))
    t_idx = np.arange(S5_CHUNK)
    out = {}
    mts, pres, pims, qres, qims = [], [], [], [], []
    for d in range(2):
        lag = (t_idx[:, None] - t_idx[None, :]) if d == 0 else (t_idx[None, :] - t_idx[:, None])
        valid = jnp.asarray(lag >= 0, F32)
        kd = kern[:, d][np.clip(lag, 0, S5_CHUNK - 1)]
        kd = kd * valid[:, :, None, None, None]
        mts.append(jnp.transpose(kd, (2, 1, 4, 0, 3)).reshape(S5_GROUPS, 256, 256))
        p_pow = (S5_CHUNK - 1 - t_idx) if d == 0 else t_idx
        pr = pw_re[p_pow, d][..., None] * bb_re[d][None] - pw_im[p_pow, d][..., None] * bb_im[d][None]
        pi = pw_re[p_pow, d][..., None] * bb_im[d][None] + pw_im[p_pow, d][..., None] * bb_re[d][None]
        pres.append(jnp.transpose(pr, (1, 0, 3, 2)).reshape(S5_GROUPS, 256, S5_STATE))
        pims.append(jnp.transpose(pi, (1, 0, 3, 2)).reshape(S5_GROUPS, 256, S5_STATE))
        q_pow = (t_idx + 1) if d == 0 else (S5_CHUNK - t_idx)
        ct_re = jnp.transpose(c_re[d], (0, 2, 1))
        ct_im = jnp.transpose(c_im[d], (0, 2, 1))
        aq_re = jnp.transpose(pw_re[q_pow, d], (1, 2, 0))
        aq_im = jnp.transpose(pw_im[q_pow, d], (1, 2, 0))
        qr = ct_re[:, :, None, :] * aq_re[..., None] - ct_im[:, :, None, :] * aq_im[..., None]
        qi = ct_re[:, :, None, :] * aq_im[..., None] + ct_im[:, :, None, :] * aq_re[..., None]
        qres.append(qr.reshape(S5_GROUPS, S5_STATE, 256))
        qims.append((-qi).reshape(S5_GROUPS, S5_STATE, 256))

    def pair_diag(m):
        g, r, c = m.shape
        m = m.reshape(g // 2, 2, r, c)
        z = jnp.zeros((g // 2, r, c), m.dtype)
        top = jnp.concatenate([m[:, 0], z], axis=2)
        bot = jnp.concatenate([z, m[:, 1]], axis=2)
        return jnp.concatenate([top, bot], axis=1)

    stack = lambda ms: jnp.stack([pair_diag(m) for m in ms]).astype(BF16)
    out["mt"] = stack(mts)
    out["pre"], out["pim"] = stack(pres), stack(pims)
    out["qre"], out["qim"] = stack(qres), stack(qims)
    out["a_re"] = pw_re[S5_CHUNK].reshape(2, 1, S5_GROUPS * S5_STATE)
    out["a_im"] = pw_im[S5_CHUNK].reshape(2, 1, S5_GROUPS * S5_STATE)
    return out


def _s5_z_kernel(u_ref, pre_ref, pim_ref, zre_ref, zim_ref):
    u = u_ref[...].astype(BF16)
    zre_ref[...] = jnp.dot(u, pre_ref[...], preferred_element_type=F32)
    zim_ref[...] = jnp.dot(u, pim_ref[...], preferred_element_type=F32)


def _s5_scan_kernel(zre_ref, zim_ref, are_ref, aim_ref, sre_ref, sim_ref):
    d = pl.program_id(0)
    ar, ai = are_ref[...], aim_ref[...]
    last = S5_NCHUNK - 1

    def body(step, carry):
        sr, si = carry
        k = jnp.where(d == 0, step,
                      jnp.where(step < S5_CTX_CHUNKS, S5_CTX_CHUNKS - 1 - step, last + S5_CTX_CHUNKS - step))
        r0 = pl.multiple_of(k * S5_BPAD, S5_BPAD)
        sre_ref[pl.ds(r0, S5_BPAD), :] = sr
        sim_ref[pl.ds(r0, S5_BPAD), :] = si
        zr = zre_ref[pl.ds(r0, S5_BPAD), :]
        zi = zim_ref[pl.ds(r0, S5_BPAD), :]
        return ar * sr - ai * si + zr, ar * si + ai * sr + zi

    zero = jnp.zeros((S5_BPAD, sre_ref.shape[1]), F32)
    lax.fori_loop(0, S5_NCHUNK, body, (zero, zero))


def _s5_y_kernel(u_ref, mt_ref, sre_ref, sim_ref, qre_ref, qim_ref, dv_ref, y_ref):
    u = u_ref[...]
    ub = u.astype(BF16)
    y = u * dv_ref[...]
    for d in range(2):
        y = y + jnp.dot(ub, mt_ref[d], preferred_element_type=F32)
        y = y + jnp.dot(sre_ref[d].astype(BF16), qre_ref[d], preferred_element_type=F32)
        y = y + jnp.dot(sim_ref[d].astype(BF16), qim_ref[d], preferred_element_type=F32)
    y_ref[...] = y


def _s5(u_chunks, mats, dvec):
    nstate = S5_GROUPS * S5_STATE
    zre, zim = pl.pallas_call(
        _s5_z_kernel,
        out_shape=[jax.ShapeDtypeStruct((2, S5_ROWS, nstate), F32)] * 2,
        grid=(2, S5_PAIRS),
        in_specs=[
            pl.BlockSpec((S5_ROWS, 512), lambda d, p: (0, p)),
            pl.BlockSpec((None, None, 512, 128), lambda d, p: (d, p, 0, 0)),
            pl.BlockSpec((None, None, 512, 128), lambda d, p: (d, p, 0, 0)),
        ],
        out_specs=[pl.BlockSpec((None, S5_ROWS, 128), lambda d, p: (d, 0, p))] * 2,
        compiler_params=_cparams(("arbitrary", "arbitrary")),
        name="s5_chunk_state",
    )(u_chunks, mats["pre"], mats["pim"])
    lanes = 512
    sre, sim = pl.pallas_call(
        _s5_scan_kernel,
        out_shape=[jax.ShapeDtypeStruct((2, S5_ROWS, nstate), F32)] * 2,
        grid=(2, nstate // lanes),
        in_specs=[
            pl.BlockSpec((None, S5_ROWS, lanes), lambda d, j: (d, 0, j)),
            pl.BlockSpec((None, S5_ROWS, lanes), lambda d, j: (d, 0, j)),
            pl.BlockSpec((None, 1, lanes), lambda d, j: (d, 0, j)),
            pl.BlockSpec((None, 1, lanes), lambda d, j: (d, 0, j)),
        ],
        out_specs=[pl.BlockSpec((None, S5_ROWS, lanes), lambda d, j: (d, 0, j))] * 2,
        compiler_params=_cparams(("arbitrary", "arbitrary")),
        name="s5_scan",
    )(zre, zim, mats["a_re"], mats["a_im"])
    return pl.pallas_call(
        _s5_y_kernel,
        out_shape=jax.ShapeDtypeStruct((S5_ROWS, S5_GROUPS * 256), F32),
        grid=(S5_PAIRS,),
        in_specs=[
            pl.BlockSpec((S5_ROWS, 512), lambda p: (0, p)),
            pl.BlockSpec((2, None, 512, 512), lambda p: (0, p, 0, 0)),
            pl.BlockSpec((2, S5_ROWS, 128), lambda p: (0, 0, p)),
            pl.BlockSpec((2, S5_ROWS, 128), lambda p: (0, 0, p)),
            pl.BlockSpec((2, None, 128, 512), lambda p: (0, p, 0, 0)),
            pl.BlockSpec((2, None, 128, 512), lambda p: (0, p, 0, 0)),
            pl.BlockSpec((None, 1, 512), lambda p: (p, 0, 0)),
        ],
        out_specs=pl.BlockSpec((S5_ROWS, 512), lambda p: (0, p)),
        compiler_params=_cparams(("arbitrary",)),
        name="s5_output",
    )(u_chunks, mats["mt"], sre, sim, mats["qre"], mats["qim"], dvec)


def _to_chunks(u):
    def seg(x, length):
        x = x.reshape(BATCH, length // S5_CHUNK, S5_CHUNK, S5_GROUPS, S5_GROUP)
        return jnp.transpose(x, (1, 0, 3, 2, 4))
    x = jnp.concatenate([seg(u[N_LAT:], CTX_LEN), seg(u[:N_LAT], SEQ)], axis=0)
    x = jnp.pad(x, ((0, 0), (0, S5_BPAD - BATCH), (0, 0), (0, 0), (0, 0)))
    return x.reshape(S5_ROWS, S5_GROUPS * 256)


def _from_chunks(y):
    y = y.reshape(S5_NCHUNK, S5_BPAD, S5_GROUPS, S5_CHUNK, S5_GROUP)[:, :BATCH]
    def seg(x):
        x = jnp.transpose(x, (1, 0, 3, 2, 4))
        return x.reshape(-1, D_WIDTH)
    return jnp.concatenate([seg(y[S5_CTX_CHUNKS:]), seg(y[:S5_CTX_CHUNKS])], axis=0)


def _glu_kernel(y_ref, w_ref, o_ref):
    y = y_ref[...]
    z = y * (0.5 * (1.0 + jnp.tanh(math.sqrt(2.0 / math.pi) * (y + 0.044715 * (y * y * y)))))
    gate = jnp.dot(z.astype(BF16), w_ref[...], preferred_element_type=F32)
    o_ref[...] = (z * jax.nn.sigmoid(gate)).astype(BF16)


def _s5_glu(y, w_glu):
    t = 512
    return pl.pallas_call(
        _glu_kernel,
        out_shape=jax.ShapeDtypeStruct((N_TOK, D_WIDTH), BF16),
        grid=(N_TOK // t,),
        in_specs=[pl.BlockSpec((t, D_WIDTH), lambda i: (i, 0)),
                  pl.BlockSpec((D_WIDTH, D_WIDTH), lambda i: (0, 0))],
        out_specs=pl.BlockSpec((t, D_WIDTH), lambda i: (i, 0)),
        compiler_params=_cparams(("arbitrary",)),
        name="s5_glu",
    )(y, w_glu)


LANE_CHUNKS = D_MODEL // 128


def _store_token_major(ref, val):
    rows = val.shape[0]
    for c in range(LANE_CHUNKS):
        ref[pl.ds(c, rows, stride=LANE_CHUNKS), :] = val[:, c * 128:(c + 1) * 128]


def _load_token_major(ref, rows):
    return jnp.concatenate([ref[pl.ds(c, rows, stride=LANE_CHUNKS), :] for c in range(LANE_CHUNKS)], axis=-1)


def _router_kernel(x_ref, g_ref, sh_ref, sc_ref, wr_ref, h_ref, info_ref, gw_ref, cnt_ref, h_scr, carry):
    i = pl.program_id(0)

    @pl.when(i == 0)
    def _():
        carry[...] = jnp.zeros_like(carry)

    for r in range(0, TM, 256):
        h = _rms(x_ref[r:r + 256, :]) * g_ref[...]
        h_scr[r:r + 256, :] = h * (1.0 + sc_ref[...]) + sh_ref[...]
    _store_token_major(h_ref, h_scr[...])
    logits = jnp.dot(h_scr[...], wr_ref[...], preferred_element_type=F32, precision=lax.Precision.HIGHEST)
    lane = lax.broadcasted_iota(jnp.int32, logits.shape, 1)
    neg = -jnp.inf
    big = jnp.int32(1 << 20)

    def first_argmax(v, vmax):
        return jnp.min(jnp.where(v == vmax, lane, big), axis=-1, keepdims=True)

    lg = jnp.where(lane < N_GROUPS, logits, neg)
    mg = jnp.max(lg, axis=-1, keepdims=True)
    g_w = 1.0 / jnp.sum(jnp.exp(lg - mg), axis=-1, keepdims=True)
    g_idx = first_argmax(lg, mg)
    lo = N_GROUPS + EXPERTS_PER_GROUP * g_idx
    le = jnp.where(jnp.logical_and(lane >= lo, lane < lo + EXPERTS_PER_GROUP), logits, neg)
    m1 = jnp.max(le, axis=-1, keepdims=True)
    i1 = first_argmax(le, m1)
    le2 = jnp.where(lane == i1, neg, le)
    m2 = jnp.max(le2, axis=-1, keepdims=True)
    i2 = first_argmax(le2, m2)
    r21 = jnp.exp(m2 - m1)
    w1 = g_w / (1.0 + r21)
    w2 = g_w * r21 / (1.0 + r21)
    oh = jnp.logical_or(lane == i1, lane == i2)
    ohb = jnp.where(oh, 1.0, 0.0).astype(BF16)
    rr = lax.broadcasted_iota(jnp.int32, (TM, TM), 0)
    cc = lax.broadcasted_iota(jnp.int32, (TM, TM), 1)
    lower = jnp.where(rr > cc, 1.0, 0.0).astype(BF16)
    before = jnp.dot(lower, ohb, preferred_element_type=F32) + carry[...]
    rank1 = jnp.sum(jnp.where(lane == i1, before, 0.0), axis=-1, keepdims=True).astype(jnp.int32)
    rank2 = jnp.sum(jnp.where(lane == i2, before, 0.0), axis=-1, keepdims=True).astype(jnp.int32)
    carry[...] = carry[...] + jnp.sum(ohb.astype(F32), axis=0, keepdims=True)
    cnt_ref[...] = jnp.broadcast_to(carry[...], cnt_ref.shape)
    info = jnp.where(lane == 0, i1 - N_GROUPS, jnp.where(lane == 1, i2 - N_GROUPS,
                     jnp.where(lane == 2, rank1, jnp.where(lane == 3, rank2, 0))))
    info_ref[...] = info
    gw_ref[...] = jnp.where(lane == 0, w1, jnp.where(lane == 1, w2, 0.0))


def _router(x, g, mod, w_router):
    return pl.pallas_call(
        _router_kernel,
        out_shape=[jax.ShapeDtypeStruct((N_TOK * LANE_CHUNKS, 128), F32),
                   jax.ShapeDtypeStruct((N_TOK, 128), jnp.int32),
                   jax.ShapeDtypeStruct((N_TOK, 128), F32),
                   jax.ShapeDtypeStruct((8, 128), F32)],
        grid=(N_TOK // TM,),
        in_specs=[
            pl.BlockSpec((TM, D_MODEL), lambda i: (i, 0)),
            pl.BlockSpec((1, D_MODEL), lambda i: (0, 0)),
            pl.BlockSpec((None, None, 1, D_MODEL), lambda i: (_mod_row(i, TM), 3, 0, 0)),
            pl.BlockSpec((None, None, 1, D_MODEL), lambda i: (_mod_row(i, TM), 4, 0, 0)),
            pl.BlockSpec((D_MODEL, 128), lambda i: (0, 0)),
        ],
        out_specs=[pl.BlockSpec((TM * LANE_CHUNKS, 128), lambda i: (i, 0)),
                   pl.BlockSpec((TM, 128), lambda i: (i, 0)),
                   pl.BlockSpec((TM, 128), lambda i: (i, 0)),
                   pl.BlockSpec((8, 128), lambda i: (0, 0))],
        scratch_shapes=[pltpu.VMEM((TM, D_MODEL), F32), pltpu.VMEM((1, 128), F32)],
        compiler_params=_cparams(("arbitrary",)),
        name="moe_router",
    )(x, g, mod, mod, w_router)


DISPATCH_WINDOW = 64


def _dispatch_kernel(pos_ref, zrow_ref, zvalid_ref, h_hbm, xs_hbm, zbuf, zsem, sem):
    zbuf[...] = jnp.zeros_like(zbuf)

    def slab(ref, row, rows):
        start = pl.multiple_of(row * LANE_CHUNKS, LANE_CHUNKS)
        return ref.at[pl.ds(start, rows * LANE_CHUNKS), :]

    def zero_copy(e):
        return pltpu.make_async_copy(zbuf, slab(xs_hbm, zrow_ref[e], TE), zsem)

    for e in range(2 * N_EXPERTS):
        @pl.when(zvalid_ref[e] > 0)
        def _():
            zero_copy(e).start()
    for e in range(2 * N_EXPERTS):
        @pl.when(zvalid_ref[e] > 0)
        def _():
            zero_copy(e).wait()

    def row_copy(t, k):
        return pltpu.make_async_copy(slab(h_hbm, t, 1), slab(xs_hbm, pos_ref[2 * t + k], 1), sem)

    def body(t, _):
        row_copy(t, 0).start()
        row_copy(t, 1).start()

        @pl.when(t >= DISPATCH_WINDOW)
        def _():
            row_copy(t - DISPATCH_WINDOW, 0).wait()
            row_copy(t - DISPATCH_WINDOW, 1).wait()
        return 0

    lax.fori_loop(0, N_TOK, body, 0)

    def drain(t, _):
        row_copy(t, 0).wait()
        row_copy(t, 1).wait()
        return 0

    lax.fori_loop(N_TOK - DISPATCH_WINDOW, N_TOK, drain, 0)


def _dispatch(pos_flat, zrow, zvalid, h):
    return pl.pallas_call(
        _dispatch_kernel,
        out_shape=jax.ShapeDtypeStruct((N_SORT * LANE_CHUNKS, 128), F32),
        grid_spec=pltpu.PrefetchScalarGridSpec(
            num_scalar_prefetch=3,
            grid=(1,),
            in_specs=[pl.BlockSpec(memory_space=pl.ANY)],
            out_specs=pl.BlockSpec(memory_space=pl.ANY),
            scratch_shapes=[pltpu.VMEM((TE * LANE_CHUNKS, 128), F32), pltpu.SemaphoreType.DMA,
                            pltpu.SemaphoreType.DMA],
        ),
        compiler_params=_cparams(("arbitrary",)),
        name="moe_dispatch",
    )(pos_flat, zrow, zvalid, h)


def _ffn_kernel(te_ref, nt_ref, xs_ref, wg_ref, wu_ref, wd_ref, o_ref):
    @pl.when(pl.program_id(0) >= nt_ref[0])
    def _():
        o_ref[...] = jnp.zeros_like(o_ref)

    @pl.when(pl.program_id(0) < nt_ref[0])
    def _():
        x = _load_token_major(xs_ref, TE).astype(BF16)
        hg = jnp.dot(x, wg_ref[...], preferred_element_type=F32)
        hu = jnp.dot(x, wu_ref[...], preferred_element_type=F32)
        act = (hg * jax.nn.sigmoid(hg) * hu).astype(BF16)
        _store_token_major(o_ref, jnp.dot(act, wd_ref[...], preferred_element_type=F32))


def _expert_ffn(tile_expert, num_tiles, xs, w_gate, w_up, w_down):
    row = lambda t, te, nt: (jnp.minimum(t, nt[0] - 1), 0)
    wsel = lambda t, te, nt: (te[t], 0, 0)
    return pl.pallas_call(
        _ffn_kernel,
        out_shape=jax.ShapeDtypeStruct((N_SORT * LANE_CHUNKS, 128), F32),
        grid_spec=pltpu.PrefetchScalarGridSpec(
            num_scalar_prefetch=2,
            grid=(N_ETILES,),
            in_specs=[
                pl.BlockSpec((TE * LANE_CHUNKS, 128), row),
                pl.BlockSpec((None, D_MODEL, D_EXPERT), wsel),
                pl.BlockSpec((None, D_MODEL, D_EXPERT), wsel),
                pl.BlockSpec((None, D_EXPERT, D_MODEL), wsel),
            ],
            out_specs=pl.BlockSpec((TE * LANE_CHUNKS, 128), lambda t, te, nt: (t, 0)),
        ),
        compiler_params=_cparams(("arbitrary",)),
        name="moe_expert_ffn",
    )(tile_expert, num_tiles, xs, w_gate, w_up, w_down)


TC = 256


def _combine_kernel(pos_ref, x_ref, gate_ref, gw_ref, fn_ref, ys_hbm, o_ref, buf_a, buf_b, sem, *, final_norm):
    base = pl.program_id(0) * TC

    def slab(ref, row):
        return ref.at[pl.ds(pl.multiple_of(row * LANE_CHUNKS, LANE_CHUNKS), LANE_CHUNKS), :]

    def row_copy(r, k):
        buf = buf_a if k == 0 else buf_b
        return pltpu.make_async_copy(slab(ys_hbm, pos_ref[2 * (base + r) + k]), slab(buf, r), sem)

    def issue(r, _):
        row_copy(r, 0).start()
        row_copy(r, 1).start()
        return 0

    def drain(r, _):
        row_copy(r, 0).wait()
        row_copy(r, 1).wait()
        return 0

    lax.fori_loop(0, TC, issue, 0)
    lax.fori_loop(0, TC, drain, 0)
    w0 = gw_ref[:, 0:1]
    w1 = gw_ref[:, 1:2]
    y = x_ref[...] + gate_ref[...] * (w0 * _load_token_major(buf_a, TC) + w1 * _load_token_major(buf_b, TC))
    if final_norm:
        y = _rms(y) * fn_ref[...]
    o_ref[...] = y


def _combine(pos_flat, x, mod, gw, ys, final_g):
    final_norm = final_g is not None
    fn = final_g if final_norm else jnp.ones((1, D_MODEL), F32)
    return pl.pallas_call(
        functools.partial(_combine_kernel, final_norm=final_norm),
        out_shape=jax.ShapeDtypeStruct((N_TOK, D_MODEL), F32),
        grid_spec=pltpu.PrefetchScalarGridSpec(
            num_scalar_prefetch=1,
            grid=(N_TOK // TC,),
            in_specs=[
                pl.BlockSpec((TC, D_MODEL), lambda i, pos: (i, 0)),
                pl.BlockSpec((None, None, 1, D_MODEL), lambda i, pos: (_mod_row(i, TC), 5, 0, 0)),
                pl.BlockSpec((TC, 128), lambda i, pos: (i, 0)),
                pl.BlockSpec((1, D_MODEL), lambda i, pos: (0, 0)),
                pl.BlockSpec(memory_space=pl.ANY),
            ],
            out_specs=pl.BlockSpec((TC, D_MODEL), lambda i, pos: (i, 0)),
            scratch_shapes=[pltpu.VMEM((TC * LANE_CHUNKS, 128), F32), pltpu.VMEM((TC * LANE_CHUNKS, 128), F32),
                            pltpu.SemaphoreType.DMA],
        ),
        compiler_params=_cparams(("arbitrary",)),
        name="moe_combine",
    )(pos_flat, x, mod, gw, fn, ys)


def _moe(x, g, mod, w_router, w_gate, w_up, w_down, final_g):
    h, info, gw, cnt = _router(x, g, mod, w_router)
    counts = cnt[0, N_GROUPS:N_GROUPS + N_EXPERTS].astype(jnp.int32)
    padded = ((counts + TE - 1) // TE) * TE
    ends = jnp.cumsum(padded)
    starts = ends - padded
    pos = starts[info[:, 0:2]] + info[:, 2:4]
    pos_flat = pos.reshape(-1)
    tile_ends = ends // TE
    num_tiles = tile_ends[-1]
    tiles = jnp.arange(N_ETILES, dtype=jnp.int32)
    tile_expert = jnp.searchsorted(tile_ends, jnp.minimum(tiles, num_tiles - 1), side="right").astype(jnp.int32)
    tail = (num_tiles + jnp.arange(N_EXPERTS, dtype=jnp.int32)) * TE
    zrow = jnp.concatenate([jnp.maximum(ends - TE, 0), jnp.minimum(tail, N_SORT - TE)]).astype(jnp.int32)
    zvalid = jnp.concatenate([padded > 0, tail < N_SORT]).astype(jnp.int32)
    xs = _dispatch(pos_flat, zrow, zvalid, h)
    ys = _expert_ffn(tile_expert, num_tiles.reshape(1).astype(jnp.int32), xs, w_gate, w_up, w_down)
    return _combine(pos_flat, x, mod, gw, ys, final_g)


def kernel(x, c, ctx, c_ctx, mod_w, mod_b, norm_mix, norm_ffn, ab_w_in, ab_q_norm, ab_k_norm, ab_conv_w, ab_w_out, cd_w_in, cd_q_norm, cd_kv_norm, cd_w_uq, cd_w_ukv, s5_lam_re, s5_lam_im, s5_log_dt, s5_b_re, s5_b_im, s5_c_re, s5_c_im, s5_d, s5_w_glu, cd_w_out, moe_w_group, moe_w_expert, moe_w_gate, moe_w_up, moe_w_down, final_norm):
    cc = jnp.concatenate([c, c_ctx[None, :], jnp.zeros((8 - BATCH - 1, D_MODEL), F32)], axis=0)
    mods = _modulation(cc, mod_w, mod_b).reshape(DEPTH, 8, N_MOD, 1, D_MODEL)
    xs = jnp.concatenate([x.reshape(N_LAT, D_MODEL), ctx.reshape(N_CTX, D_MODEL)], axis=0)
    cos_a, sin_a = _rope_tables(HEAD_DIM)
    cos_c, sin_c = _rope_tables(C_ROPE)

    for i in range(DEPTH):
        j = i // 2
        mod = mods[i]
        if i % 2 == 0:
            p = _norm_linear(xs, 0, D_MODEL, norm_mix[i][None, :], mod, 0, ab_w_in[j].astype(BF16), 1024)
            kv = _gqa_kv(p, ab_k_norm[j][None, :], cos_a, sin_a)
            qn = ab_q_norm[j][None, :]
            o = _gqa_attention(p, kv, qn, cos_a, sin_a)
            side = _gated_conv(p, ab_conv_w[j])
            w_out = ab_w_out[j].astype(BF16)
        else:
            w_in = cd_w_in[j]
            a, b_, c_ = C_Q_RANK + C_KV_RANK, C_Q_RANK + C_KV_RANK + C_ROPE, CD_IN_PAD - (C_Q_RANK + C_KV_RANK + C_ROPE + D_WIDTH)
            w_in = jnp.concatenate([w_in[:, :a], w_in[:, b_:], w_in[:, a:b_], jnp.zeros((D_MODEL, c_), F32)], axis=1)
            p = _norm_linear(xs, 0, D_MODEL, norm_mix[i][None, :], mod, 0, w_in.astype(BF16), CD_IN_PAD)
            w_uq = cd_w_uq[j].reshape(C_Q_RANK, C_HEADS, C_NOPE + C_ROPE)
            w_uq = jnp.concatenate([w_uq[:, :, :C_NOPE].reshape(C_Q_RANK, -1), w_uq[:, :, C_NOPE:].reshape(C_Q_RANK, -1)], axis=1)
            q = _norm_linear(p, 0, C_Q_RANK, cd_q_norm[j][None, :], None, 0, w_uq.astype(BF16), w_uq.shape[1])
            kvu = _norm_linear(p, C_Q_RANK // C_KV_RANK, C_KV_RANK, cd_kv_norm[j][None, :], None, 0,
                               cd_w_ukv[j].astype(BF16), 1024)
            krr = _mla_kr(p, cos_c, sin_c)
            o = _mla_attention(q, kvu, krr, cos_c, sin_c)
            mats = _s5_matrices(s5_lam_re[j], s5_lam_im[j], s5_log_dt[j], s5_b_re[j], s5_b_im[j], s5_c_re[j], s5_c_im[j])
            u = p[:, C_Q_RANK + C_KV_RANK:C_Q_RANK + C_KV_RANK + D_WIDTH]
            dvec = jnp.broadcast_to(s5_d[j].astype(F32).reshape(S5_PAIRS, 2, 1, S5_GROUP),
                                    (S5_PAIRS, 2, S5_CHUNK, S5_GROUP)).reshape(S5_PAIRS, 1, 512)
            y = _from_chunks(_s5(_to_chunks(u), mats, dvec))
            side = _s5_glu(y, s5_w_glu[j].astype(BF16))
            w_out = cd_w_out[j].astype(BF16)
        xs = _out_linear(o, side, w_out, xs, mod, 2)
        w_router = jnp.concatenate(
            [moe_w_group[i], jnp.transpose(moe_w_expert[i], (1, 0, 2)).reshape(D_MODEL, N_EXPERTS),
             jnp.zeros((D_MODEL, 128 - N_GROUPS - N_EXPERTS), F32)], axis=1)
        xs = _moe(xs, norm_ffn[i][None, :], mod, w_router, moe_w_gate[i].astype(BF16), moe_w_up[i].astype(BF16),
                  moe_w_down[i].astype(BF16), final_norm[None, :] if i == DEPTH - 1 else None)
    return xs[:N_LAT].reshape(BATCH, SEQ, D_MODEL)
```

```python
import functools
import math

import numpy as np
import jax
import jax.numpy as jnp
from jax import lax
from jax.experimental import pallas as pl
from jax.experimental.pallas import tpu as pltpu

F32 = jnp.float32
BF16 = jnp.bfloat16

D_MODEL = 2048
BATCH = 4
SEQ = 2048
DEPTH = 4
GRID_W = 64
CTX_LEN = 256
ROPE_THETA = 10000.0
EPS = 1e-6
N_MOD = 6
HEAD_DIM = 128
A_Q_HEADS = 12
A_KV_HEADS = 4
A_GROUP = A_Q_HEADS // A_KV_HEADS
A_WIDTH = A_Q_HEADS * HEAD_DIM
A_KV_WIDTH = A_KV_HEADS * HEAD_DIM
B_WIDTH = 512
AB_IN = A_WIDTH + 2 * A_KV_WIDTH + 3 * B_WIDTH
C_HEADS = 12
C_NOPE = 128
C_ROPE = 64
C_V = 128
C_Q_RANK = 512
C_KV_RANK = 256
C_WIDTH = C_HEADS * C_V
D_WIDTH = 512
S5_GROUP = 16
S5_GROUPS = D_WIDTH // S5_GROUP
S5_STATE = 64
N_GROUPS = 4
EXPERTS_PER_GROUP = 4
N_EXPERTS = N_GROUPS * EXPERTS_PER_GROUP
D_EXPERT = 512

N_LAT = BATCH * SEQ
N_CTX = BATCH * CTX_LEN
N_TOK = N_LAT + N_CTX
CTX_ROW = BATCH
CD_IN_PAD = 1536

TM = 512
TQ = 256
TE = 256
N_SORT = 2 * N_TOK + N_EXPERTS * TE
N_ETILES = N_SORT // TE
S5_CHUNK = 16
S5_CTX_CHUNKS = CTX_LEN // S5_CHUNK
S5_ROWS = N_TOK // S5_CHUNK
S5_GSUB = 128 // S5_GROUP
S5_QUADS = S5_GROUPS // S5_GSUB
VMEM_LIMIT = 48 * 1024 * 1024


def _cparams(sem):
    return pltpu.CompilerParams(dimension_semantics=sem, vmem_limit_bytes=VMEM_LIMIT)


def _mod_row(tile, tile_rows):
    r0 = tile * tile_rows
    return jnp.where(r0 >= N_LAT, CTX_ROW, r0 // SEQ)


def _rope_block(tile, tile_rows):
    r0 = tile * tile_rows
    return jnp.where(r0 >= N_LAT, SEQ // tile_rows, (r0 % SEQ) // tile_rows)


def _rms(x):
    return x * lax.rsqrt(jnp.mean(x * x, axis=-1, keepdims=True) + EPS)


def _mod_kernel(cc_ref, w_ref, b_ref, o_ref):
    cc = cc_ref[...]
    s = (cc * jax.nn.sigmoid(cc)).astype(BF16)
    o_ref[...] = jnp.dot(s, w_ref[...].astype(BF16), preferred_element_type=F32) + b_ref[...]


def _modulation(cc, mod_w, mod_b):
    tn = 1024
    nout = N_MOD * D_MODEL
    return pl.pallas_call(
        _mod_kernel,
        out_shape=jax.ShapeDtypeStruct((DEPTH, 8, nout), F32),
        grid=(DEPTH, nout // tn),
        in_specs=[
            pl.BlockSpec((8, D_MODEL), lambda l, j: (0, 0)),
            pl.BlockSpec((None, D_MODEL, tn), lambda l, j: (l, 0, j)),
            pl.BlockSpec((None, 1, tn), lambda l, j: (l, 0, j)),
        ],
        out_specs=pl.BlockSpec((None, 8, tn), lambda l, j: (l, 0, j)),
        compiler_params=_cparams(("arbitrary", "arbitrary")),
        name="modulation",
    )(cc, mod_w, mod_b.reshape(DEPTH, 1, nout))


def _norm_linear_kernel(x_ref, g_ref, sh_ref, sc_ref, w_ref, o_ref, h_scr, *, modulate, rows):
    @pl.when(pl.program_id(1) == 0)
    def _():
        for r in range(0, rows, 256):
            h = _rms(x_ref[r:r + 256, :]) * g_ref[...]
            if modulate:
                h = h * (1.0 + sc_ref[...]) + sh_ref[...]
            h_scr[r:r + 256, :] = h.astype(BF16)

    o_ref[...] = jnp.dot(h_scr[...], w_ref[...].astype(BF16), preferred_element_type=F32)


def _norm_linear(x, xcol, kdim, g, mod, which, w, layer, tn):
    nout = w.shape[2]
    modulate = mod is not None
    if modulate:
        sh_spec = pl.BlockSpec((None, None, 1, kdim), lambda i, j: (_mod_row(i, TM), which, 0, 0))
        sc_spec = pl.BlockSpec((None, None, 1, kdim), lambda i, j: (_mod_row(i, TM), which + 1, 0, 0))
        sh = sc = mod
    else:
        sh_spec = sc_spec = pl.BlockSpec((1, kdim), lambda i, j: (0, 0))
        sh = sc = g
    return pl.pallas_call(
        functools.partial(_norm_linear_kernel, modulate=modulate, rows=TM),
        out_shape=jax.ShapeDtypeStruct((N_TOK, nout), F32),
        grid=(N_TOK // TM, nout // tn),
        in_specs=[
            pl.BlockSpec((TM, kdim), lambda i, j: (i, xcol)),
            pl.BlockSpec((1, kdim), lambda i, j: (0, 0)),
            sh_spec, sc_spec,
            pl.BlockSpec((None, kdim, tn), lambda i, j: (layer, 0, j)),
        ],
        out_specs=pl.BlockSpec((TM, tn), lambda i, j: (i, j)),
        scratch_shapes=[pltpu.VMEM((TM, kdim), BF16)],
        compiler_params=_cparams(("arbitrary", "arbitrary")),
        name="norm_linear",
    )(x, g, sh, sc, w)


def _out_linear_kernel(a1_ref, a2_ref, w1_ref, w2_ref, x_ref, gate_ref, o_ref):
    acc = jnp.dot(a1_ref[...], w1_ref[...].astype(BF16), preferred_element_type=F32)
    acc = acc + jnp.dot(a2_ref[...], w2_ref[...].astype(BF16), preferred_element_type=F32)
    o_ref[...] = x_ref[...] + gate_ref[...] * acc


def _out_linear(a1, a2, w, layer, x, mod, which):
    tn = 1024
    k1, k2 = a1.shape[1], a2.shape[1]
    return pl.pallas_call(
        _out_linear_kernel,
        out_shape=jax.ShapeDtypeStruct((N_TOK, D_MODEL), F32),
        grid=(N_TOK // TM, D_MODEL // tn),
        in_specs=[
            pl.BlockSpec((TM, k1), lambda i, j: (i, 0)),
            pl.BlockSpec((TM, k2), lambda i, j: (i, 0)),
            pl.BlockSpec((None, k1, tn), lambda i, j: (layer, 0, j)),
            pl.BlockSpec((None, k2, tn), lambda i, j: (layer, k1 // k2, j)),
            pl.BlockSpec((TM, tn), lambda i, j: (i, j)),
            pl.BlockSpec((None, None, 1, tn), lambda i, j: (_mod_row(i, TM), which, 0, j)),
        ],
        out_specs=pl.BlockSpec((TM, tn), lambda i, j: (i, j)),
        compiler_params=_cparams(("arbitrary", "arbitrary")),
        name="out_linear",
    )(a1, a2, w, w, x, mod)


def _rope_tables(rot_dim):
    rows = SEQ // GRID_W
    row_ids = np.repeat(np.arange(rows, dtype=np.float32), GRID_W)
    col_ids = np.tile(np.arange(GRID_W, dtype=np.float32), rows)
    d_axis = rot_dim // 2
    inv = (np.float32(ROPE_THETA) ** (-np.arange(0, d_axis, 2, dtype=np.float32) / np.float32(d_axis))).astype(np.float32)
    ang = np.concatenate([row_ids[:, None] * inv, col_ids[:, None] * inv], axis=-1).astype(np.float32)
    cos, sin = np.cos(ang).astype(np.float32), np.sin(ang).astype(np.float32)
    reps = 128 // rot_dim
    cos_f = np.tile(np.concatenate([cos, cos], axis=-1), (1, reps))
    sin_f = np.tile(np.concatenate([-sin, sin], axis=-1), (1, reps))
    cos_f = np.concatenate([cos_f, np.ones((512, 128), np.float32)], axis=0)
    sin_f = np.concatenate([sin_f, np.zeros((512, 128), np.float32)], axis=0)
    return jnp.asarray(cos_f), jnp.asarray(sin_f)


def _rope128(x, cos, sin):
    return x * cos + pltpu.roll(x, 64, 1) * sin


def _rope64(x, cos, sin):
    lane = lax.broadcasted_iota(jnp.int32, x.shape, 1)
    swapped = jnp.where((lane % 64) < 32, pltpu.roll(x, 96, 1), pltpu.roll(x, 32, 1))
    return x * cos + swapped * sin


def _gqa_kv_kernel(k_ref, v_ref, kn_ref, cos_ref, sin_ref, o_ref):
    cos, sin = cos_ref[...], sin_ref[...]
    for h in range(A_KV_HEADS):
        k = _rms(k_ref[:, h * 128:(h + 1) * 128]) * kn_ref[...]
        o_ref[:, h * 128:(h + 1) * 128] = _rope128(k, cos, sin).astype(BF16)
    o_ref[:, A_KV_WIDTH:] = v_ref[...].astype(BF16)


def _gqa_kv(p, k_norm, cos, sin):
    t = 512
    return pl.pallas_call(
        _gqa_kv_kernel,
        out_shape=jax.ShapeDtypeStruct((N_TOK, 2 * A_KV_WIDTH), BF16),
        grid=(N_TOK // t,),
        in_specs=[
            pl.BlockSpec((t, A_KV_WIDTH), lambda i: (i, A_WIDTH // A_KV_WIDTH)),
            pl.BlockSpec((t, A_KV_WIDTH), lambda i: (i, A_WIDTH // A_KV_WIDTH + 1)),
            pl.BlockSpec((1, 128), lambda i: (0, 0)),
            pl.BlockSpec((t, 128), lambda i: (_rope_block(i, t), 0)),
            pl.BlockSpec((t, 128), lambda i: (_rope_block(i, t), 0)),
        ],
        out_specs=pl.BlockSpec((t, 2 * A_KV_WIDTH), lambda i: (i, 0)),
        compiler_params=_cparams(("arbitrary",)),
        name="gqa_kv",
    )(p, p, k_norm, cos, sin)


LOG2E = math.log2(math.e)


def _softmax_pv(q, keys, vals):
    dn = (((1,), (1,)), ((), ()))
    s = [lax.dot_general(q, k, dn, preferred_element_type=F32) for k in keys]
    m = functools.reduce(jnp.maximum, [jnp.max(si, axis=-1, keepdims=True) for si in s])
    acc = None
    for si, v in zip(s, vals):
        lane = lax.broadcasted_iota(jnp.int32, v.shape, 1)
        v_ext = jnp.concatenate([v, jnp.where(lane == 0, 1.0, 0.0).astype(BF16)], axis=1)
        part = jnp.dot(jnp.exp2(si - m).astype(BF16), v_ext, preferred_element_type=F32)
        acc = part if acc is None else acc + part
    return acc[:, :128] / acc[:, 128:129]


def _gqa_attn_kernel(q_ref, qn_ref, cos_ref, sin_ref, kc_ref, vc_ref, kl_ref, vl_ref, o_ref):
    cos, sin = cos_ref[...], sin_ref[...]
    scale = HEAD_DIM ** -0.5 * LOG2E
    qs = []
    for g in range(A_GROUP):
        q = _rms(q_ref[:, g * 128:(g + 1) * 128]) * qn_ref[...]
        qs.append((_rope128(q, cos, sin) * scale).astype(BF16))

    is_ctx = pl.program_id(2) == SEQ // TQ

    @pl.when(jnp.logical_not(is_ctx))
    def _():
        for g in range(A_GROUP):
            o = _softmax_pv(qs[g], [kc_ref[...], kl_ref[...]], [vc_ref[...], vl_ref[...]])
            o_ref[:, g * 128:(g + 1) * 128] = o.astype(BF16)

    @pl.when(is_ctx)
    def _():
        for g in range(A_GROUP):
            o_ref[:, g * 128:(g + 1) * 128] = _softmax_pv(qs[g], [kc_ref[...]], [vc_ref[...]]).astype(BF16)


def _attn_qrow(b, t):
    nq = SEQ // TQ
    return jnp.where(t == nq, N_LAT // TQ + b, b * nq + t)


def _gqa_attention(p, kv, q_norm, cos, sin):
    qw = A_GROUP * HEAD_DIM
    ctx_blk = N_LAT // CTX_LEN
    qspec = pl.BlockSpec((TQ, qw), lambda b, h, t: (_attn_qrow(b, t), h))
    rope = pl.BlockSpec((TQ, 128), lambda b, h, t: (_rope_block(_attn_qrow(b, t), TQ), 0))
    return pl.pallas_call(
        _gqa_attn_kernel,
        out_shape=jax.ShapeDtypeStruct((N_TOK, A_WIDTH), BF16),
        grid=(BATCH, A_KV_HEADS, SEQ // TQ + 1),
        in_specs=[
            qspec,
            pl.BlockSpec((1, 128), lambda b, h, t: (0, 0)),
            rope, rope,
            pl.BlockSpec((CTX_LEN, 128), lambda b, h, t: (ctx_blk + b, h)),
            pl.BlockSpec((CTX_LEN, 128), lambda b, h, t: (ctx_blk + b, A_KV_HEADS + h)),
            pl.BlockSpec((SEQ, 128), lambda b, h, t: (b, h)),
            pl.BlockSpec((SEQ, 128), lambda b, h, t: (b, A_KV_HEADS + h)),
        ],
        out_specs=qspec,
        compiler_params=_cparams(("arbitrary",) * 3),
        name="gqa_attn",
    )(p, q_norm, cos, sin, kv, kv, kv, kv)


def _conv_kernel(bg_ref, cg_ref, ug_ref, cgp_ref, ugp_ref, cgn_ref, ugn_ref, w_ref, o_ref, *, rows):
    i = pl.program_id(0)
    per_seq = SEQ // rows
    is_ctx = i >= N_LAT // rows
    is_start = jnp.logical_or(is_ctx, i % per_seq == 0)
    is_end = jnp.logical_or(is_ctx, i % per_seq == per_seq - 1)
    m = cg_ref[...] * ug_ref[...]
    m_prev = jnp.where(is_start, 0.0, cgp_ref[7:8, :] * ugp_ref[7:8, :])
    m_next = jnp.where(is_end, 0.0, cgn_ref[0:1, :] * ugn_ref[0:1, :])
    row = lax.broadcasted_iota(jnp.int32, m.shape, 0)
    down = jnp.where(row == 0, m_prev, pltpu.roll(m, 1, 0))
    up = jnp.where(row == rows - 1, m_next, pltpu.roll(m, rows - 1, 0))
    conv = down * w_ref[0:1, :] + m * w_ref[1:2, :] + up * w_ref[2:3, :]
    o_ref[...] = (bg_ref[...] * conv).astype(BF16)


def _gated_conv(p, conv_w):
    rows = CTX_LEN
    base = (A_WIDTH + 2 * A_KV_WIDTH) // B_WIDTH
    halo = rows // 8
    last = N_TOK // 8 - 1
    main = lambda c: pl.BlockSpec((rows, B_WIDTH), lambda i: (i, base + c))
    prev = lambda c: pl.BlockSpec((8, B_WIDTH), lambda i: (jnp.maximum(i * halo - 1, 0), base + c))
    nxt = lambda c: pl.BlockSpec((8, B_WIDTH), lambda i: (jnp.minimum((i + 1) * halo, last), base + c))
    return pl.pallas_call(
        functools.partial(_conv_kernel, rows=rows),
        out_shape=jax.ShapeDtypeStruct((N_TOK, B_WIDTH), BF16),
        grid=(N_TOK // rows,),
        in_specs=[main(0), main(1), main(2), prev(1), prev(2), nxt(1), nxt(2),
                  pl.BlockSpec((3, B_WIDTH), lambda i: (0, 0))],
        out_specs=pl.BlockSpec((rows, B_WIDTH), lambda i: (i, 0)),
        compiler_params=_cparams(("arbitrary",)),
        name="gated_conv",
    )(p, p, p, p, p, p, p, conv_w)


def _mla_kr_kernel(kr_ref, cos_ref, sin_ref, o_ref):
    x = kr_ref[...]
    lane = lax.broadcasted_iota(jnp.int32, x.shape, 1)
    r = jnp.where(lane < C_ROPE, _rope64(x, cos_ref[...], sin_ref[...]), 0.0)
    o_ref[:, 0:128] = r.astype(BF16)
    o_ref[:, 128:256] = pltpu.roll(r, 64, 1).astype(BF16)


def _mla_kr(p, cos, sin):
    t = 512
    col = (C_Q_RANK + C_KV_RANK + D_WIDTH) // 128
    return pl.pallas_call(
        _mla_kr_kernel,
        out_shape=jax.ShapeDtypeStruct((N_TOK, 256), BF16),
        grid=(N_TOK // t,),
        in_specs=[
            pl.BlockSpec((t, 128), lambda i: (i, col)),
            pl.BlockSpec((t, 128), lambda i: (_rope_block(i, t), 0)),
            pl.BlockSpec((t, 128), lambda i: (_rope_block(i, t), 0)),
        ],
        out_specs=pl.BlockSpec((t, 256), lambda i: (i, 0)),
        compiler_params=_cparams(("arbitrary",)),
        name="mla_kr",
    )(p, cos, sin)


def _mla_attn_kernel(qn_ref, qr_ref, cos_ref, sin_ref, kvc_ref, krc_ref, kvl_ref, krl_ref, o_ref):
    scale = (C_NOPE + C_ROPE) ** -0.5 * LOG2E
    qr = _rope64(qr_ref[...], cos_ref[...], sin_ref[...])
    lane = lax.broadcasted_iota(jnp.int32, qr.shape, 1)
    is_ctx = pl.program_id(2) == SEQ // TQ
    for hh in range(2):
        sel = (lane < 64) if hh == 0 else (lane >= 64)
        q = jnp.concatenate([qn_ref[:, hh * 128:(hh + 1) * 128], jnp.where(sel, qr, 0.0)], axis=1)
        q = (q * scale).astype(BF16)
        kn = slice(hh * 256, hh * 256 + 128)
        vv = slice(hh * 256 + 128, hh * 256 + 256)
        kr = slice(hh * 128, (hh + 1) * 128)
        out = slice(hh * 128, (hh + 1) * 128)

        def keys_vals(kv_ref, kr_ref):
            return (jnp.concatenate([kv_ref[:, kn].astype(BF16), kr_ref[:, kr]], axis=1),
                    kv_ref[:, vv].astype(BF16))

        @pl.when(jnp.logical_not(is_ctx))
        def _():
            kc, vc = keys_vals(kvc_ref, krc_ref)
            kl, vl = keys_vals(kvl_ref, krl_ref)
            o_ref[:, out] = _softmax_pv(q, [kc, kl], [vc, vl]).astype(BF16)

        @pl.when(is_ctx)
        def _():
            kc, vc = keys_vals(kvc_ref, krc_ref)
            o_ref[:, out] = _softmax_pv(q, [kc], [vc]).astype(BF16)


def _mla_attention(q, kv, krr, cos, sin):
    ctx_blk = N_LAT // CTX_LEN
    rope_col = C_HEADS * C_NOPE // 128
    rope = pl.BlockSpec((TQ, 128), lambda b, h, t: (_rope_block(_attn_qrow(b, t), TQ), 0))
    return pl.pallas_call(
        _mla_attn_kernel,
        out_shape=jax.ShapeDtypeStruct((N_TOK, C_WIDTH), BF16),
        grid=(BATCH, C_HEADS // 2, SEQ // TQ + 1),
        in_specs=[
            pl.BlockSpec((TQ, 256), lambda b, h, t: (_attn_qrow(b, t), h)),
            pl.BlockSpec((TQ, 128), lambda b, h, t: (_attn_qrow(b, t), rope_col + h)),
            rope, rope,
            pl.BlockSpec((CTX_LEN, 512), lambda b, h, t: (ctx_blk + b, h)),
            pl.BlockSpec((CTX_LEN, 256), lambda b, h, t: (ctx_blk + b, 0)),
            pl.BlockSpec((SEQ, 512), lambda b, h, t: (b, h)),
            pl.BlockSpec((SEQ, 256), lambda b, h, t: (b, 0)),
        ],
        out_specs=pl.BlockSpec((TQ, 256), lambda b, h, t: (_attn_qrow(b, t), h)),
        compiler_params=_cparams(("arbitrary",) * 3),
        name="mla_attn",
    )(q, q, cos, sin, kv, krr, kv, krr)


def _s5_matrices(lam_re, lam_im, log_dt, b_re, b_im, c_re, c_im):
    hi = lax.Precision.HIGHEST
    lam_re, lam_im = lam_re.astype(F32), lam_im.astype(F32)
    dt = jnp.exp(log_dt.astype(F32))[..., None]
    ks = jnp.arange(S5_CHUNK + 1, dtype=F32)[:, None, None, None]
    mag = jnp.exp(lam_re[None] * dt[None] * ks)
    ang = lam_im[None] * dt[None] * ks
    pw_re, pw_im = mag * jnp.cos(ang), mag * jnp.sin(ang)
    a_re, a_im = pw_re[1], pw_im[1]
    den = lam_re * lam_re + lam_im * lam_im
    f_re = ((a_re - 1.0) * lam_re + a_im * lam_im) / den
    f_im = (a_im * lam_re - (a_re - 1.0) * lam_im) / den
    b_re, b_im = b_re.astype(F32), b_im.astype(F32)
    bb_re = f_re[..., None] * b_re - f_im[..., None] * b_im
    bb_im = f_re[..., None] * b_im + f_im[..., None] * b_re
    c_re, c_im = c_re.astype(F32), c_im.astype(F32)

    ab_re = pw_re[:S5_CHUNK, ..., None] * bb_re[None] - pw_im[:S5_CHUNK, ..., None] * bb_im[None]
    ab_im = pw_re[:S5_CHUNK, ..., None] * bb_im[None] + pw_im[:S5_CHUNK, ..., None] * bb_re[None]
    kern = (jnp.einsum('dgcn,ldgne->ldgce', c_re, ab_re, precision=hi)
            - jnp.einsum('dgcn,ldgne->ldgce', c_im, ab_im, precision=hi))
    t_idx = np.arange(S5_CHUNK)
    eye = jnp.eye(S5_GSUB, dtype=F32)
    sub = lambda m, ax: m.reshape(m.shape[:ax] + (S5_QUADS, S5_GSUB) + m.shape[ax + 1:])
    klag = jnp.einsum('ldqgce,gh->dqlgehc', sub(kern, 2), eye).reshape(2, S5_QUADS, S5_CHUNK, 128, 128)
    ps, qs = [], []
    for d in range(2):
        p_pow = (S5_CHUNK - 1 - t_idx) if d == 0 else t_idx
        pr = pw_re[p_pow, d][..., None] * bb_re[d][None] - pw_im[p_pow, d][..., None] * bb_im[d][None]
        pi = pw_re[p_pow, d][..., None] * bb_im[d][None] + pw_im[p_pow, d][..., None] * bb_re[d][None]
        pc = jnp.stack([pr, pi], axis=0)
        ps.append(jnp.einsum('rsqgnc,gh->qsgcrhn', sub(pc, 2), eye).reshape(S5_QUADS, S5_CHUNK * 128, 1024))
        q_pow = (t_idx + 1) if d == 0 else (S5_CHUNK - t_idx)
        ct_re = jnp.transpose(c_re[d], (0, 2, 1))
        ct_im = jnp.transpose(c_im[d], (0, 2, 1))
        aq_re = jnp.transpose(pw_re[q_pow, d], (1, 2, 0))
        aq_im = jnp.transpose(pw_im[q_pow, d], (1, 2, 0))
        qr = ct_re[:, :, None, :] * aq_re[..., None] - ct_im[:, :, None, :] * aq_im[..., None]
        qi = ct_re[:, :, None, :] * aq_im[..., None] + ct_im[:, :, None, :] * aq_re[..., None]
        qc = jnp.stack([qr, -qi], axis=0)
        qs.append(jnp.einsum('rqgntc,gh->qrgnthc', sub(qc, 1), eye).reshape(S5_QUADS, 1024, S5_CHUNK * 128))
    return dict(
        klag=klag.astype(BF16),
        p=jnp.stack(ps).astype(BF16),
        q=jnp.stack(qs).astype(BF16),
        a_re=pw_re[S5_CHUNK].reshape(2, 1, S5_GROUPS * S5_STATE),
        a_im=pw_im[S5_CHUNK].reshape(2, 1, S5_GROUPS * S5_STATE),
    )


def _s5_chunk_rows(u_ref):
    n = N_TOK // S5_CHUNK
    return jnp.concatenate([u_ref[pl.ds(s, n, stride=S5_CHUNK), :] for s in range(S5_CHUNK)], axis=-1).astype(BF16)


def _s5_z_kernel(u_ref, p_ref, zre_ref, zim_ref):
    z = jnp.dot(_s5_chunk_rows(u_ref), p_ref[...], preferred_element_type=F32)
    half = z.shape[1] // 2
    zre_ref[...] = z[:, :half]
    zim_ref[...] = z[:, half:]


def _s5_scan_kernel(zre_ref, zim_ref, are_ref, aim_ref, sre_ref, sim_ref):
    d = pl.program_id(0)
    ar, ai = are_ref[...], aim_ref[...]
    nl, nc = SEQ // S5_CHUNK, S5_CTX_CHUNKS

    def segment(base, count, carry):
        def body(step, carry):
            k = jnp.where(d == 0, step, count - 1 - step)
            out = []
            for b in range(BATCH):
                sr, si = carry[2 * b], carry[2 * b + 1]
                row = pl.ds(base + b * count + k, 1)
                sre_ref[row, :] = sr
                sim_ref[row, :] = si
                zr, zi = zre_ref[row, :], zim_ref[row, :]
                out += [ar * sr - ai * si + zr, ar * si + ai * sr + zi]
            return tuple(out)
        return lax.fori_loop(0, count, body, carry)

    zero = jnp.zeros((1, sre_ref.shape[1]), F32)
    carry = segment(BATCH * nl, nc, (zero,) * (2 * BATCH))
    segment(0, nl, carry)


def _s5_y_kernel(u_ref, klag_ref, sre_ref, sim_ref, q_ref, dv_ref, y_ref, m_scr):
    d = pl.program_id(1)
    n = N_TOK // S5_CHUNK
    m_scr[...] = jnp.zeros_like(m_scr)
    for rev in range(2):
        @pl.when(d == rev)
        def _():
            for s in range(S5_CHUNK):
                for t in range(S5_CHUNK):
                    lag = (s - t) if rev else (t - s)
                    if lag >= 0:
                        m_scr[s * 128:(s + 1) * 128, t * 128:(t + 1) * 128] = klag_ref[lag]
    y = jnp.dot(_s5_chunk_rows(u_ref), m_scr[...], preferred_element_type=F32)
    state = jnp.concatenate([sre_ref[...], sim_ref[...]], axis=1).astype(BF16)
    y = y + jnp.dot(state, q_ref[...], preferred_element_type=F32)
    for t in range(S5_CHUNK):
        rows = pl.ds(t, n, stride=S5_CHUNK)
        yt = y[:, t * 128:(t + 1) * 128]

        @pl.when(d == 0)
        def _():
            y_ref[rows, :] = yt + u_ref[rows, :] * dv_ref[...]

        @pl.when(d == 1)
        def _():
            y_ref[rows, :] = y_ref[rows, :] + yt


def _s5(p, mats, dskip):
    nstate = S5_GROUPS * S5_STATE
    ucol = (C_Q_RANK + C_KV_RANK) // 128
    zre, zim = pl.pallas_call(
        _s5_z_kernel,
        out_shape=[jax.ShapeDtypeStruct((2, S5_ROWS, nstate), F32)] * 2,
        grid=(2, S5_QUADS),
        in_specs=[
            pl.BlockSpec((N_TOK, 128), lambda d, q: (0, ucol + q)),
            pl.BlockSpec((None, None, S5_CHUNK * 128, 1024), lambda d, q: (d, q, 0, 0)),
        ],
        out_specs=[pl.BlockSpec((None, S5_ROWS, 512), lambda d, q: (d, 0, q))] * 2,
        compiler_params=_cparams(("arbitrary", "arbitrary")),
        name="s5_chunk_state",
    )(p, mats["p"])
    lanes = 512
    sre, sim = pl.pallas_call(
        _s5_scan_kernel,
        out_shape=[jax.ShapeDtypeStruct((2, S5_ROWS, nstate), F32)] * 2,
        grid=(2, nstate // lanes),
        in_specs=[
            pl.BlockSpec((None, S5_ROWS, lanes), lambda d, j: (d, 0, j)),
            pl.BlockSpec((None, S5_ROWS, lanes), lambda d, j: (d, 0, j)),
            pl.BlockSpec((None, 1, lanes), lambda d, j: (d, 0, j)),
            pl.BlockSpec((None, 1, lanes), lambda d, j: (d, 0, j)),
        ],
        out_specs=[pl.BlockSpec((None, S5_ROWS, lanes), lambda d, j: (d, 0, j))] * 2,
        compiler_params=_cparams(("arbitrary", "arbitrary")),
        name="s5_scan",
    )(zre, zim, mats["a_re"], mats["a_im"])
    return pl.pallas_call(
        _s5_y_kernel,
        out_shape=jax.ShapeDtypeStruct((N_TOK, D_WIDTH), F32),
        grid=(S5_QUADS, 2),
        in_specs=[
            pl.BlockSpec((N_TOK, 128), lambda q, d: (0, ucol + q)),
            pl.BlockSpec((None, None, S5_CHUNK, 128, 128), lambda q, d: (d, q, 0, 0, 0)),
            pl.BlockSpec((None, S5_ROWS, 512), lambda q, d: (d, 0, q)),
            pl.BlockSpec((None, S5_ROWS, 512), lambda q, d: (d, 0, q)),
            pl.BlockSpec((None, None, 1024, S5_CHUNK * 128), lambda q, d: (d, q, 0, 0)),
            pl.BlockSpec((1, 128), lambda q, d: (0, q)),
        ],
        out_specs=pl.BlockSpec((N_TOK, 128), lambda q, d: (0, q)),
        scratch_shapes=[pltpu.VMEM((S5_CHUNK * 128, S5_CHUNK * 128), BF16)],
        compiler_params=pltpu.CompilerParams(dimension_semantics=("arbitrary", "arbitrary"),
                                             vmem_limit_bytes=56 * 1024 * 1024),
        name="s5_output",
    )(p, mats["klag"], sre, sim, mats["q"], dskip)


def _glu_kernel(y_ref, w_ref, o_ref):
    y = y_ref[...]
    z = y * (0.5 * (1.0 + jnp.tanh(math.sqrt(2.0 / math.pi) * (y + 0.044715 * (y * y * y)))))
    gate = jnp.dot(z.astype(BF16), w_ref[...].astype(BF16), preferred_element_type=F32)
    o_ref[...] = (z * jax.nn.sigmoid(gate)).astype(BF16)


def _s5_glu(y, w_glu, layer):
    t = 512
    return pl.pallas_call(
        _glu_kernel,
        out_shape=jax.ShapeDtypeStruct((N_TOK, D_WIDTH), BF16),
        grid=(N_TOK // t,),
        in_specs=[pl.BlockSpec((t, D_WIDTH), lambda i: (i, 0)),
                  pl.BlockSpec((None, D_WIDTH, D_WIDTH), lambda i: (layer, 0, 0))],
        out_specs=pl.BlockSpec((t, D_WIDTH), lambda i: (i, 0)),
        compiler_params=_cparams(("arbitrary",)),
        name="s5_glu",
    )(y, w_glu)


LANE_CHUNKS = D_MODEL // 128


def _store_token_major(ref, val):
    rows = val.shape[0]
    for c in range(LANE_CHUNKS):
        ref[pl.ds(c, rows, stride=LANE_CHUNKS), :] = val[:, c * 128:(c + 1) * 128]


def _load_token_major(ref, rows):
    return jnp.concatenate([ref[pl.ds(c, rows, stride=LANE_CHUNKS), :] for c in range(LANE_CHUNKS)], axis=-1)


def _router_kernel(x_ref, g_ref, sh_ref, sc_ref, wr_ref, h_ref, info_ref, gw_ref, cnt_ref, h_scr, carry):
    i = pl.program_id(0)

    @pl.when(i == 0)
    def _():
        carry[...] = jnp.zeros_like(carry)

    for r in range(0, TM, 256):
        h = _rms(x_ref[r:r + 256, :]) * g_ref[...]
        h_scr[r:r + 256, :] = h * (1.0 + sc_ref[...]) + sh_ref[...]
    _store_token_major(h_ref, h_scr[...])
    logits = jnp.dot(h_scr[...], wr_ref[...], preferred_element_type=F32, precision=lax.Precision.HIGHEST)
    lane = lax.broadcasted_iota(jnp.int32, logits.shape, 1)
    neg = -jnp.inf
    big = jnp.int32(1 << 20)

    def first_argmax(v, vmax):
        return jnp.min(jnp.where(v == vmax, lane, big), axis=-1, keepdims=True)

    lg = jnp.where(lane < N_GROUPS, logits, neg)
    mg = jnp.max(lg, axis=-1, keepdims=True)
    g_w = 1.0 / jnp.sum(jnp.exp(lg - mg), axis=-1, keepdims=True)
    g_idx = first_argmax(lg, mg)
    lo = N_GROUPS + EXPERTS_PER_GROUP * g_idx
    le = jnp.where(jnp.logical_and(lane >= lo, lane < lo + EXPERTS_PER_GROUP), logits, neg)
    m1 = jnp.max(le, axis=-1, keepdims=True)
    i1 = first_argmax(le, m1)
    le2 = jnp.where(lane == i1, neg, le)
    m2 = jnp.max(le2, axis=-1, keepdims=True)
    i2 = first_argmax(le2, m2)
    r21 = jnp.exp(m2 - m1)
    w1 = g_w / (1.0 + r21)
    w2 = g_w * r21 / (1.0 + r21)
    oh = jnp.logical_or(lane == i1, lane == i2)
    ohb = jnp.where(oh, 1.0, 0.0).astype(BF16)
    rr = lax.broadcasted_iota(jnp.int32, (TM, TM), 0)
    cc = lax.broadcasted_iota(jnp.int32, (TM, TM), 1)
    lower = jnp.where(rr > cc, 1.0, 0.0).astype(BF16)
    before = jnp.dot(lower, ohb, preferred_element_type=F32) + carry[...]
    rank1 = jnp.sum(jnp.where(lane == i1, before, 0.0), axis=-1, keepdims=True).astype(jnp.int32)
    rank2 = jnp.sum(jnp.where(lane == i2, before, 0.0), axis=-1, keepdims=True).astype(jnp.int32)
    carry[...] = carry[...] + jnp.sum(ohb.astype(F32), axis=0, keepdims=True)
    cnt_ref[...] = jnp.broadcast_to(carry[...], cnt_ref.shape)
    info = jnp.where(lane == 0, i1 - N_GROUPS, jnp.where(lane == 1, i2 - N_GROUPS,
                     jnp.where(lane == 2, rank1, jnp.where(lane == 3, rank2, 0))))
    info_ref[...] = info
    gw_ref[...] = jnp.where(lane == 0, w1, jnp.where(lane == 1, w2, 0.0))


def _router(x, g, mod, w_router):
    return pl.pallas_call(
        _router_kernel,
        out_shape=[jax.ShapeDtypeStruct((N_TOK * LANE_CHUNKS, 128), F32),
                   jax.ShapeDtypeStruct((N_TOK, 128), jnp.int32),
                   jax.ShapeDtypeStruct((N_TOK, 128), F32),
                   jax.ShapeDtypeStruct((8, 128), F32)],
        grid=(N_TOK // TM,),
        in_specs=[
            pl.BlockSpec((TM, D_MODEL), lambda i: (i, 0)),
            pl.BlockSpec((1, D_MODEL), lambda i: (0, 0)),
            pl.BlockSpec((None, None, 1, D_MODEL), lambda i: (_mod_row(i, TM), 3, 0, 0)),
            pl.BlockSpec((None, None, 1, D_MODEL), lambda i: (_mod_row(i, TM), 4, 0, 0)),
            pl.BlockSpec((D_MODEL, 128), lambda i: (0, 0)),
        ],
        out_specs=[pl.BlockSpec((TM * LANE_CHUNKS, 128), lambda i: (i, 0)),
                   pl.BlockSpec((TM, 128), lambda i: (i, 0)),
                   pl.BlockSpec((TM, 128), lambda i: (i, 0)),
                   pl.BlockSpec((8, 128), lambda i: (0, 0))],
        scratch_shapes=[pltpu.VMEM((TM, D_MODEL), F32), pltpu.VMEM((1, 128), F32)],
        compiler_params=_cparams(("arbitrary",)),
        name="moe_router",
    )(x, g, mod, mod, w_router)


def _ffn_kernel(pos_ref, te_ref, meta_ref, h_hbm, wg_ref, wu_ref, wd_ref, o_ref,
                src, xbuf, wg_b, wu_b, wd_b, sem):
    t = pl.program_id(0)
    nt = meta_ref[0]

    def gather(tile, slot):
        def body(r, _):
            tok = src[tile * TE + r]
            pltpu.make_async_copy(
                h_hbm.at[pl.ds(pl.multiple_of(tok * LANE_CHUNKS, LANE_CHUNKS), LANE_CHUNKS), :],
                xbuf.at[slot, pl.ds(pl.multiple_of(r * LANE_CHUNKS, LANE_CHUNKS), LANE_CHUNKS), :],
                sem.at[slot]).start()
            return 0
        lax.fori_loop(0, TE, body, 0, unroll=8)

    @pl.when(t == 0)
    def _():
        for e in range(N_EXPERTS):
            def clear(i, _):
                src[i] = 0
                return 0
            lax.fori_loop(meta_ref[1 + e], meta_ref[1 + N_EXPERTS + e], clear, 0)

        def fill(tok, _):
            src[pos_ref[2 * tok]] = tok
            src[pos_ref[2 * tok + 1]] = tok
            return 0
        lax.fori_loop(0, N_TOK, fill, 0, unroll=8)
        gather(0, 0)

    @pl.when(t >= nt)
    def _():
        o_ref[...] = jnp.zeros_like(o_ref)

    @pl.when(t < nt)
    def _():
        slot = t % 2
        pltpu.make_async_copy(xbuf.at[slot], xbuf.at[slot], sem.at[slot]).wait()

        @pl.when(t + 1 < nt)
        def _():
            gather(t + 1, 1 - slot)

        @pl.when(jnp.logical_or(t == 0, te_ref[t] != te_ref[jnp.maximum(t - 1, 0)]))
        def _():
            wg_b[...] = wg_ref[...].astype(BF16)
            wu_b[...] = wu_ref[...].astype(BF16)
            wd_b[...] = wd_ref[...].astype(BF16)

        x = jnp.concatenate([xbuf[slot, pl.ds(c, TE, stride=LANE_CHUNKS), :] for c in range(LANE_CHUNKS)],
                            axis=-1).astype(BF16)
        hg = jnp.dot(x, wg_b[...], preferred_element_type=F32)
        hu = jnp.dot(x, wu_b[...], preferred_element_type=F32)
        act = (hg * jax.nn.sigmoid(hg) * hu).astype(BF16)
        _store_token_major(o_ref, jnp.dot(act, wd_b[...], preferred_element_type=F32))


def _expert_ffn(pos_flat, tile_expert, meta, h, w_gate, w_up, w_down, layer):
    wsel = lambda t, pos, te, meta: (layer, te[t], 0, 0)
    return pl.pallas_call(
        _ffn_kernel,
        out_shape=jax.ShapeDtypeStruct((N_SORT * LANE_CHUNKS, 128), F32),
        grid_spec=pltpu.PrefetchScalarGridSpec(
            num_scalar_prefetch=3,
            grid=(N_ETILES,),
            in_specs=[
                pl.BlockSpec(memory_space=pl.ANY),
                pl.BlockSpec((None, None, D_MODEL, D_EXPERT), wsel),
                pl.BlockSpec((None, None, D_MODEL, D_EXPERT), wsel),
                pl.BlockSpec((None, None, D_EXPERT, D_MODEL), wsel),
            ],
            out_specs=pl.BlockSpec((TE * LANE_CHUNKS, 128), lambda t, pos, te, meta: (t, 0)),
            scratch_shapes=[
                pltpu.SMEM((N_SORT,), jnp.int32),
                pltpu.VMEM((2, TE * LANE_CHUNKS, 128), F32),
                pltpu.VMEM((D_MODEL, D_EXPERT), BF16),
                pltpu.VMEM((D_MODEL, D_EXPERT), BF16),
                pltpu.VMEM((D_EXPERT, D_MODEL), BF16),
                pltpu.SemaphoreType.DMA((2,)),
            ],
        ),
        compiler_params=pltpu.CompilerParams(dimension_semantics=("arbitrary",),
                                             vmem_limit_bytes=56 * 1024 * 1024),
        name="moe_expert_ffn",
    )(pos_flat, tile_expert, meta, h, w_gate, w_up, w_down)


TC = 256


def _combine_kernel(pos_ref, x_ref, gate_ref, gw_ref, fn_ref, ys_hbm, o_ref, buf_a, buf_b, sem, *, final_norm):
    base = pl.program_id(0) * TC

    def slab(ref, row):
        return ref.at[pl.ds(pl.multiple_of(row * LANE_CHUNKS, LANE_CHUNKS), LANE_CHUNKS), :]

    def issue(r, _):
        pltpu.make_async_copy(slab(ys_hbm, pos_ref[2 * (base + r)]), slab(buf_a, r), sem.at[0]).start()
        pltpu.make_async_copy(slab(ys_hbm, pos_ref[2 * (base + r) + 1]), slab(buf_b, r), sem.at[1]).start()
        return 0

    lax.fori_loop(0, TC, issue, 0, unroll=8)
    pltpu.make_async_copy(buf_a, buf_a, sem.at[0]).wait()
    pltpu.make_async_copy(buf_b, buf_b, sem.at[1]).wait()
    w0 = gw_ref[:, 0:1]
    w1 = gw_ref[:, 1:2]
    y = x_ref[...] + gate_ref[...] * (w0 * _load_token_major(buf_a, TC) + w1 * _load_token_major(buf_b, TC))
    if final_norm:
        y = _rms(y) * fn_ref[...]
    o_ref[...] = y


def _combine(pos_flat, x, mod, gw, ys, final_g):
    final_norm = final_g is not None
    fn = final_g if final_norm else jnp.ones((1, D_MODEL), F32)
    return pl.pallas_call(
        functools.partial(_combine_kernel, final_norm=final_norm),
        out_shape=jax.ShapeDtypeStruct((N_TOK, D_MODEL), F32),
        grid_spec=pltpu.PrefetchScalarGridSpec(
            num_scalar_prefetch=1,
            grid=(N_TOK // TC,),
            in_specs=[
                pl.BlockSpec((TC, D_MODEL), lambda i, pos: (i, 0)),
                pl.BlockSpec((None, None, 1, D_MODEL), lambda i, pos: (_mod_row(i, TC), 5, 0, 0)),
                pl.BlockSpec((TC, 128), lambda i, pos: (i, 0)),
                pl.BlockSpec((1, D_MODEL), lambda i, pos: (0, 0)),
                pl.BlockSpec(memory_space=pl.ANY),
            ],
            out_specs=pl.BlockSpec((TC, D_MODEL), lambda i, pos: (i, 0)),
            scratch_shapes=[pltpu.VMEM((TC * LANE_CHUNKS, 128), F32), pltpu.VMEM((TC * LANE_CHUNKS, 128), F32),
                            pltpu.SemaphoreType.DMA((2,))],
        ),
        compiler_params=_cparams(("arbitrary",)),
        name="moe_combine",
    )(pos_flat, x, mod, gw, fn, ys)


def _moe(x, g, mod, w_router, w_gate, w_up, w_down, layer, final_g):
    h, info, gw, cnt = _router(x, g, mod, w_router)
    counts = cnt[0, N_GROUPS:N_GROUPS + N_EXPERTS].astype(jnp.int32)
    padded = ((counts + TE - 1) // TE) * TE
    ends = jnp.cumsum(padded)
    starts = ends - padded
    experts = jnp.arange(N_EXPERTS, dtype=jnp.int32)
    start_of = jnp.sum(jnp.where(info[:, 0:2, None] == experts, starts, 0), axis=-1)
    pos_flat = (start_of + info[:, 2:4]).reshape(-1)
    tile_ends = ends // TE
    num_tiles = tile_ends[-1]
    tiles = jnp.minimum(jnp.arange(N_ETILES, dtype=jnp.int32), num_tiles - 1)
    tile_expert = jnp.sum((tile_ends[None, :] <= tiles[:, None]).astype(jnp.int32), axis=-1)
    meta = jnp.concatenate([num_tiles[None], starts + counts, ends]).astype(jnp.int32)
    ys = _expert_ffn(pos_flat, tile_expert, meta, h, w_gate, w_up, w_down, layer)
    return _combine(pos_flat, x, mod, gw, ys, final_g)


def kernel(x, c, ctx, c_ctx, mod_w, mod_b, norm_mix, norm_ffn, ab_w_in, ab_q_norm, ab_k_norm, ab_conv_w, ab_w_out, cd_w_in, cd_q_norm, cd_kv_norm, cd_w_uq, cd_w_ukv, s5_lam_re, s5_lam_im, s5_log_dt, s5_b_re, s5_b_im, s5_c_re, s5_c_im, s5_d, s5_w_glu, cd_w_out, moe_w_group, moe_w_expert, moe_w_gate, moe_w_up, moe_w_down, final_norm):
    cc = jnp.concatenate([c, c_ctx[None, :], jnp.zeros((8 - BATCH - 1, D_MODEL), F32)], axis=0)
    mods = _modulation(cc, mod_w, mod_b).reshape(DEPTH, 8, N_MOD, 1, D_MODEL)
    xs = jnp.concatenate([x.reshape(N_LAT, D_MODEL), ctx.reshape(N_CTX, D_MODEL)], axis=0)
    cos_a, sin_a = _rope_tables(HEAD_DIM)
    cos_c, sin_c = _rope_tables(C_ROPE)

    for i in range(DEPTH):
        j = i // 2
        mod = mods[i]
        if i % 2 == 0:
            p = _norm_linear(xs, 0, D_MODEL, norm_mix[i][None, :], mod, 0, ab_w_in, j, 1024)
            kv = _gqa_kv(p, ab_k_norm[j][None, :], cos_a, sin_a)
            qn = ab_q_norm[j][None, :]
            o = _gqa_attention(p, kv, qn, cos_a, sin_a)
            side = _gated_conv(p, ab_conv_w[j])
            w_out = ab_w_out
        else:
            w_in = cd_w_in[j]
            a, b_, c_ = C_Q_RANK + C_KV_RANK, C_Q_RANK + C_KV_RANK + C_ROPE, CD_IN_PAD - (C_Q_RANK + C_KV_RANK + C_ROPE + D_WIDTH)
            w_in = jnp.concatenate([w_in[:, :a], w_in[:, b_:], w_in[:, a:b_], jnp.zeros((D_MODEL, c_), F32)], axis=1)
            p = _norm_linear(xs, 0, D_MODEL, norm_mix[i][None, :], mod, 0, w_in[None], 0, CD_IN_PAD // 2)
            w_uq = cd_w_uq[j].reshape(C_Q_RANK, C_HEADS, C_NOPE + C_ROPE)
            w_uq = jnp.concatenate([w_uq[:, :, :C_NOPE].reshape(C_Q_RANK, -1), w_uq[:, :, C_NOPE:].reshape(C_Q_RANK, -1)], axis=1)
            q = _norm_linear(p, 0, C_Q_RANK, cd_q_norm[j][None, :], None, 0, w_uq[None], 0, w_uq.shape[1])
            kvu = _norm_linear(p, C_Q_RANK // C_KV_RANK, C_KV_RANK, cd_kv_norm[j][None, :], None, 0,
                               cd_w_ukv, j, 1024)
            krr = _mla_kr(p, cos_c, sin_c)
            o = _mla_attention(q, kvu, krr, cos_c, sin_c)
            mats = _s5_matrices(s5_lam_re[j], s5_lam_im[j], s5_log_dt[j], s5_b_re[j], s5_b_im[j], s5_c_re[j], s5_c_im[j])
            y = _s5(p, mats, s5_d[j].astype(F32)[None, :])
            side = _s5_glu(y, s5_w_glu, j)
            w_out = cd_w_out
        xs = _out_linear(o, side, w_out, j, xs, mod, 2)
        w_router = jnp.concatenate(
            [moe_w_group[i], jnp.transpose(moe_w_expert[i], (1, 0, 2)).reshape(D_MODEL, N_EXPERTS),
             jnp.zeros((D_MODEL, 128 - N_GROUPS - N_EXPERTS), F32)], axis=1)
        xs = _moe(xs, norm_ffn[i][None, :], mod, w_router, moe_w_gate, moe_w_up, moe_w_down, i,
                  final_norm[None, :] if i == DEPTH - 1 else None)
    return xs[:N_LAT].reshape(BATCH, SEQ, D_MODEL)
```

```python
import functools
import math

import numpy as np
import jax
import jax.numpy as jnp
from jax import lax
from jax.experimental import pallas as pl
from jax.experimental.pallas import tpu as pltpu

F32 = jnp.float32
BF16 = jnp.bfloat16

D_MODEL = 2048
BATCH = 4
SEQ = 2048
DEPTH = 4
GRID_W = 64
CTX_LEN = 256
ROPE_THETA = 10000.0
EPS = 1e-6
N_MOD = 6
HEAD_DIM = 128
A_Q_HEADS = 12
A_KV_HEADS = 4
A_GROUP = A_Q_HEADS // A_KV_HEADS
A_WIDTH = A_Q_HEADS * HEAD_DIM
A_KV_WIDTH = A_KV_HEADS * HEAD_DIM
B_WIDTH = 512
AB_IN = A_WIDTH + 2 * A_KV_WIDTH + 3 * B_WIDTH
C_HEADS = 12
C_NOPE = 128
C_ROPE = 64
C_V = 128
C_Q_RANK = 512
C_KV_RANK = 256
C_WIDTH = C_HEADS * C_V
D_WIDTH = 512
S5_GROUP = 16
S5_GROUPS = D_WIDTH // S5_GROUP
S5_STATE = 64
N_GROUPS = 4
EXPERTS_PER_GROUP = 4
N_EXPERTS = N_GROUPS * EXPERTS_PER_GROUP
D_EXPERT = 512

N_LAT = BATCH * SEQ
N_CTX = BATCH * CTX_LEN
N_TOK = N_LAT + N_CTX
CTX_ROW = BATCH
CD_IN_PAD = 1536

TM = 512
TQ = 512
TE = 256
N_SORT = 2 * N_TOK + N_EXPERTS * TE
N_ETILES = N_SORT // TE
S5_CHUNK = 16
S5_CTX_CHUNKS = CTX_LEN // S5_CHUNK
S5_ROWS = N_TOK // S5_CHUNK
S5_GSUB = 128 // S5_GROUP
S5_QUADS = S5_GROUPS // S5_GSUB
VMEM_LIMIT = 48 * 1024 * 1024


def _cparams(sem):
    return pltpu.CompilerParams(dimension_semantics=sem, vmem_limit_bytes=VMEM_LIMIT)


def _mod_row(tile, tile_rows):
    r0 = tile * tile_rows
    return jnp.where(r0 >= N_LAT, CTX_ROW, r0 // SEQ)


def _rope_block(tile, tile_rows):
    r0 = tile * tile_rows
    return jnp.where(r0 >= N_LAT, SEQ // tile_rows, (r0 % SEQ) // tile_rows)


def _rms(x):
    return x * lax.rsqrt(jnp.mean(x * x, axis=-1, keepdims=True) + EPS)


def _mod_kernel(cc_ref, w_ref, b_ref, o_ref):
    cc = cc_ref[...]
    s = (cc * jax.nn.sigmoid(cc)).astype(BF16)
    o_ref[...] = jnp.dot(s, w_ref[...].astype(BF16), preferred_element_type=F32) + b_ref[...]


def _modulation(cc, mod_w, mod_b):
    tn = 1024
    nout = N_MOD * D_MODEL
    return pl.pallas_call(
        _mod_kernel,
        out_shape=jax.ShapeDtypeStruct((DEPTH, 8, nout), F32),
        grid=(DEPTH, nout // tn),
        in_specs=[
            pl.BlockSpec((8, D_MODEL), lambda l, j: (0, 0)),
            pl.BlockSpec((None, D_MODEL, tn), lambda l, j: (l, 0, j)),
            pl.BlockSpec((None, 1, tn), lambda l, j: (l, 0, j)),
        ],
        out_specs=pl.BlockSpec((None, 8, tn), lambda l, j: (l, 0, j)),
        compiler_params=_cparams(("arbitrary", "arbitrary")),
        name="modulation",
    )(cc, mod_w, mod_b.reshape(DEPTH, 1, nout))


def _norm_linear_kernel(x_ref, g_ref, sh_ref, sc_ref, w_ref, o_ref, h_scr, *, modulate, rows):
    @pl.when(pl.program_id(1) == 0)
    def _():
        for r in range(0, rows, 256):
            h = _rms(x_ref[r:r + 256, :]) * g_ref[...]
            if modulate:
                h = h * (1.0 + sc_ref[...]) + sh_ref[...]
            h_scr[r:r + 256, :] = h.astype(BF16)

    o_ref[...] = jnp.dot(h_scr[...], w_ref[...], preferred_element_type=F32)


def _norm_linear(x, xcol, kdim, g, mod, which, w, layer, tn):
    nout = w.shape[2]
    modulate = mod is not None
    if modulate:
        sh_spec = pl.BlockSpec((None, None, 1, kdim), lambda i, j: (_mod_row(i, TM), which, 0, 0))
        sc_spec = pl.BlockSpec((None, None, 1, kdim), lambda i, j: (_mod_row(i, TM), which + 1, 0, 0))
        sh = sc = mod
    else:
        sh_spec = sc_spec = pl.BlockSpec((1, kdim), lambda i, j: (0, 0))
        sh = sc = g
    return pl.pallas_call(
        functools.partial(_norm_linear_kernel, modulate=modulate, rows=TM),
        out_shape=jax.ShapeDtypeStruct((N_TOK, nout), F32),
        grid=(N_TOK // TM, nout // tn),
        in_specs=[
            pl.BlockSpec((TM, kdim), lambda i, j: (i, xcol)),
            pl.BlockSpec((1, kdim), lambda i, j: (0, 0)),
            sh_spec, sc_spec,
            pl.BlockSpec((None, kdim, tn), lambda i, j: (layer, 0, j)),
        ],
        out_specs=pl.BlockSpec((TM, tn), lambda i, j: (i, j)),
        scratch_shapes=[pltpu.VMEM((TM, kdim), BF16)],
        compiler_params=_cparams(("arbitrary", "arbitrary")),
        name="norm_linear",
    )(x, g, sh, sc, w)


def _out_linear_kernel(a1_ref, a2_ref, w1_ref, w2_ref, x_ref, gate_ref, o_ref):
    acc = jnp.dot(a1_ref[...], w1_ref[...], preferred_element_type=F32)
    acc = acc + jnp.dot(a2_ref[...], w2_ref[...], preferred_element_type=F32)
    o_ref[...] = x_ref[...] + gate_ref[...] * acc


def _out_linear(a1, a2, w, layer, x, mod, which):
    tn = D_MODEL
    k1, k2 = a1.shape[1], a2.shape[1]
    return pl.pallas_call(
        _out_linear_kernel,
        out_shape=jax.ShapeDtypeStruct((N_TOK, D_MODEL), F32),
        grid=(N_TOK // TM, D_MODEL // tn),
        in_specs=[
            pl.BlockSpec((TM, k1), lambda i, j: (i, 0)),
            pl.BlockSpec((TM, k2), lambda i, j: (i, 0)),
            pl.BlockSpec((None, k1, tn), lambda i, j: (layer, 0, j)),
            pl.BlockSpec((None, k2, tn), lambda i, j: (layer, k1 // k2, j)),
            pl.BlockSpec((TM, tn), lambda i, j: (i, j)),
            pl.BlockSpec((None, None, 1, tn), lambda i, j: (_mod_row(i, TM), which, 0, j)),
        ],
        out_specs=pl.BlockSpec((TM, tn), lambda i, j: (i, j)),
        compiler_params=_cparams(("arbitrary", "arbitrary")),
        name="out_linear",
    )(a1, a2, w, w, x, mod)


def _rope_tables(rot_dim):
    rows = SEQ // GRID_W
    row_ids = np.repeat(np.arange(rows, dtype=np.float32), GRID_W)
    col_ids = np.tile(np.arange(GRID_W, dtype=np.float32), rows)
    d_axis = rot_dim // 2
    inv = (np.float32(ROPE_THETA) ** (-np.arange(0, d_axis, 2, dtype=np.float32) / np.float32(d_axis))).astype(np.float32)
    ang = np.concatenate([row_ids[:, None] * inv, col_ids[:, None] * inv], axis=-1).astype(np.float32)
    cos, sin = np.cos(ang).astype(np.float32), np.sin(ang).astype(np.float32)
    reps = 128 // rot_dim
    cos_f = np.tile(np.concatenate([cos, cos], axis=-1), (1, reps))
    sin_f = np.tile(np.concatenate([-sin, sin], axis=-1), (1, reps))
    cos_f = np.concatenate([cos_f, np.ones((512, 128), np.float32)], axis=0)
    sin_f = np.concatenate([sin_f, np.zeros((512, 128), np.float32)], axis=0)
    return jnp.asarray(cos_f), jnp.asarray(sin_f)


def _rope128(x, cos, sin):
    return x * cos + pltpu.roll(x, 64, 1) * sin


def _rope64(x, cos, sin):
    lane = lax.broadcasted_iota(jnp.int32, x.shape, 1)
    swapped = jnp.where((lane % 64) < 32, pltpu.roll(x, 96, 1), pltpu.roll(x, 32, 1))
    return x * cos + swapped * sin


def _ones_column(rows):
    lane = lax.broadcasted_iota(jnp.int32, (rows, 128), 1)
    return jnp.where(lane == 0, 1.0, 0.0).astype(BF16)


def _gqa_kv_kernel(k_ref, v_ref, kn_ref, cos_ref, sin_ref, ko_ref, vo_ref):
    cos, sin = cos_ref[...], sin_ref[...]
    ones = _ones_column(k_ref.shape[0])
    for h in range(A_KV_HEADS):
        k = _rms(k_ref[:, h * 128:(h + 1) * 128]) * kn_ref[...]
        ko_ref[:, h * 128:(h + 1) * 128] = _rope128(k, cos, sin).astype(BF16)
        vo_ref[:, h * 256:h * 256 + 128] = v_ref[:, h * 128:(h + 1) * 128].astype(BF16)
        vo_ref[:, h * 256 + 128:(h + 1) * 256] = ones


def _gqa_kv(p, k_norm, cos, sin):
    t = 512
    return pl.pallas_call(
        _gqa_kv_kernel,
        out_shape=[jax.ShapeDtypeStruct((N_TOK, A_KV_WIDTH), BF16),
                   jax.ShapeDtypeStruct((N_TOK, 2 * A_KV_WIDTH), BF16)],
        grid=(N_TOK // t,),
        in_specs=[
            pl.BlockSpec((t, A_KV_WIDTH), lambda i: (i, A_WIDTH // A_KV_WIDTH)),
            pl.BlockSpec((t, A_KV_WIDTH), lambda i: (i, A_WIDTH // A_KV_WIDTH + 1)),
            pl.BlockSpec((1, 128), lambda i: (0, 0)),
            pl.BlockSpec((t, 128), lambda i: (_rope_block(i, t), 0)),
            pl.BlockSpec((t, 128), lambda i: (_rope_block(i, t), 0)),
        ],
        out_specs=[pl.BlockSpec((t, A_KV_WIDTH), lambda i: (i, 0)),
                   pl.BlockSpec((t, 2 * A_KV_WIDTH), lambda i: (i, 0))],
        compiler_params=_cparams(("arbitrary",)),
        name="gqa_kv",
    )(p, p, k_norm, cos, sin)


LOG2E = math.log2(math.e)


def _softmax_pv(q, keys, vals):
    dn = (((1,), (1,)), ((), ()))
    s = [lax.dot_general(q, k, dn, preferred_element_type=F32) for k in keys]
    m = functools.reduce(jnp.maximum, [jnp.max(si, axis=-1, keepdims=True) for si in s])
    acc = None
    for si, v in zip(s, vals):
        part = jnp.dot(jnp.exp2(si - m).astype(BF16), v, preferred_element_type=F32)
        acc = part if acc is None else acc + part
    return acc[:, :128] / acc[:, 128:129]


def _attn_qrow(b, t):
    nq = SEQ // TQ
    return jnp.where(t == nq, (N_LAT + b * CTX_LEN) // TQ, b * nq + t)


def _ctx_half():
    return pl.multiple_of((pl.program_id(0) % (TQ // CTX_LEN)) * CTX_LEN, CTX_LEN)


def _gqa_attn_kernel(q_ref, qn_ref, cos_ref, sin_ref, kc_ref, vc_ref, kl_ref, vl_ref, ol_ref, oc_ref):
    scale = HEAD_DIM ** -0.5 * LOG2E
    is_ctx = pl.program_id(2) == SEQ // TQ

    def query(rows, g):
        q = _rms(q_ref[rows, g * 128:(g + 1) * 128]) * qn_ref[...]
        return (_rope128(q, cos_ref[rows, :], sin_ref[rows, :]) * scale).astype(BF16)

    @pl.when(jnp.logical_not(is_ctx))
    def _():
        for g in range(A_GROUP):
            o = _softmax_pv(query(slice(None), g), [kc_ref[...], kl_ref[...]], [vc_ref[...], vl_ref[...]])
            ol_ref[:, g * 128:(g + 1) * 128] = o.astype(BF16)

    @pl.when(is_ctx)
    def _():
        rows = pl.ds(_ctx_half(), CTX_LEN)
        for g in range(A_GROUP):
            oc_ref[:, g * 128:(g + 1) * 128] = _softmax_pv(query(rows, g), [kc_ref[...]], [vc_ref[...]]).astype(BF16)


def _gqa_attention(p, k, vext, q_norm, cos, sin):
    qw = A_GROUP * HEAD_DIM
    nq = SEQ // TQ
    ctx_blk = N_LAT // CTX_LEN
    rope = pl.BlockSpec((TQ, 128), lambda b, h, t: (_rope_block(_attn_qrow(b, t), TQ), 0))
    return pl.pallas_call(
        _gqa_attn_kernel,
        out_shape=[jax.ShapeDtypeStruct((N_LAT, A_WIDTH), BF16), jax.ShapeDtypeStruct((N_CTX, A_WIDTH), BF16)],
        grid=(BATCH, A_KV_HEADS, nq + 1),
        in_specs=[
            pl.BlockSpec((TQ, qw), lambda b, h, t: (_attn_qrow(b, t), h)),
            pl.BlockSpec((1, 128), lambda b, h, t: (0, 0)),
            rope, rope,
            pl.BlockSpec((CTX_LEN, 128), lambda b, h, t: (ctx_blk + b, h)),
            pl.BlockSpec((CTX_LEN, 256), lambda b, h, t: (ctx_blk + b, h)),
            pl.BlockSpec((SEQ, 128), lambda b, h, t: (b, h)),
            pl.BlockSpec((SEQ, 256), lambda b, h, t: (b, h)),
        ],
        out_specs=[pl.BlockSpec((TQ, qw), lambda b, h, t: (b * nq + jnp.minimum(t, nq - 1), h)),
                   pl.BlockSpec((CTX_LEN, qw), lambda b, h, t: (b, h))],
        compiler_params=_cparams(("arbitrary",) * 3),
        name="gqa_attn",
    )(p, q_norm, cos, sin, k, vext, k, vext)


def _conv_kernel(bg_ref, cg_ref, ug_ref, cgp_ref, ugp_ref, cgn_ref, ugn_ref, w_ref, o_ref, *, rows):
    i = pl.program_id(0)
    per_seq = SEQ // rows
    is_ctx = i >= N_LAT // rows
    is_start = jnp.logical_or(is_ctx, i % per_seq == 0)
    is_end = jnp.logical_or(is_ctx, i % per_seq == per_seq - 1)
    m = cg_ref[...] * ug_ref[...]
    m_prev = jnp.where(is_start, 0.0, cgp_ref[7:8, :] * ugp_ref[7:8, :])
    m_next = jnp.where(is_end, 0.0, cgn_ref[0:1, :] * ugn_ref[0:1, :])
    row = lax.broadcasted_iota(jnp.int32, m.shape, 0)
    down = jnp.where(row == 0, m_prev, pltpu.roll(m, 1, 0))
    up = jnp.where(row == rows - 1, m_next, pltpu.roll(m, rows - 1, 0))
    conv = down * w_ref[0:1, :] + m * w_ref[1:2, :] + up * w_ref[2:3, :]
    o_ref[...] = (bg_ref[...] * conv).astype(BF16)


def _gated_conv(p, conv_w):
    rows = CTX_LEN
    base = (A_WIDTH + 2 * A_KV_WIDTH) // B_WIDTH
    halo = rows // 8
    last = N_TOK // 8 - 1
    main = lambda c: pl.BlockSpec((rows, B_WIDTH), lambda i: (i, base + c))
    prev = lambda c: pl.BlockSpec((8, B_WIDTH), lambda i: (jnp.maximum(i * halo - 1, 0), base + c))
    nxt = lambda c: pl.BlockSpec((8, B_WIDTH), lambda i: (jnp.minimum((i + 1) * halo, last), base + c))
    return pl.pallas_call(
        functools.partial(_conv_kernel, rows=rows),
        out_shape=jax.ShapeDtypeStruct((N_TOK, B_WIDTH), BF16),
        grid=(N_TOK // rows,),
        in_specs=[main(0), main(1), main(2), prev(1), prev(2), nxt(1), nxt(2),
                  pl.BlockSpec((3, B_WIDTH), lambda i: (0, 0))],
        out_specs=pl.BlockSpec((rows, B_WIDTH), lambda i: (i, 0)),
        compiler_params=_cparams(("arbitrary",)),
        name="gated_conv",
    )(p, p, p, p, p, p, p, conv_w)


def _mla_kv_kernel(kv_ref, kr_ref, cos_ref, sin_ref, ko_ref, vo_ref):
    x = kr_ref[...]
    lane = lax.broadcasted_iota(jnp.int32, x.shape, 1)
    r = jnp.where(lane < C_ROPE, _rope64(x, cos_ref[...], sin_ref[...]), 0.0)
    kr_low = r.astype(BF16)
    kr_high = pltpu.roll(r, 64, 1).astype(BF16)
    ones = _ones_column(x.shape[0])
    for h in range(C_HEADS):
        ko_ref[:, h * 256:h * 256 + 128] = kv_ref[:, h * 256:h * 256 + 128].astype(BF16)
        ko_ref[:, h * 256 + 128:(h + 1) * 256] = kr_low if h % 2 == 0 else kr_high
        vo_ref[:, h * 256:h * 256 + 128] = kv_ref[:, h * 256 + 128:(h + 1) * 256].astype(BF16)
        vo_ref[:, h * 256 + 128:(h + 1) * 256] = ones


def _mla_kv(kv, p, cos, sin):
    t = 256
    col = (C_Q_RANK + C_KV_RANK + D_WIDTH) // 128
    width = C_HEADS * 256
    return pl.pallas_call(
        _mla_kv_kernel,
        out_shape=[jax.ShapeDtypeStruct((N_TOK, width), BF16)] * 2,
        grid=(N_TOK // t,),
        in_specs=[
            pl.BlockSpec((t, width), lambda i: (i, 0)),
            pl.BlockSpec((t, 128), lambda i: (i, col)),
            pl.BlockSpec((t, 128), lambda i: (_rope_block(i, t), 0)),
            pl.BlockSpec((t, 128), lambda i: (_rope_block(i, t), 0)),
        ],
        out_specs=[pl.BlockSpec((t, width), lambda i: (i, 0))] * 2,
        compiler_params=_cparams(("arbitrary",)),
        name="mla_kv",
    )(kv, p, cos, sin)


def _mla_attn_kernel(qn_ref, qr_ref, cos_ref, sin_ref, kc_ref, vc_ref, kl_ref, vl_ref, ol_ref, oc_ref):
    scale = (C_NOPE + C_ROPE) ** -0.5 * LOG2E
    is_ctx = pl.program_id(2) == SEQ // TQ

    def queries(rows):
        qr = _rope64(qr_ref[rows, :], cos_ref[rows, :], sin_ref[rows, :])
        lane = lax.broadcasted_iota(jnp.int32, qr.shape, 1)
        out = []
        for hh in range(2):
            sel = (lane < 64) if hh == 0 else (lane >= 64)
            q = jnp.concatenate([qn_ref[rows, hh * 128:(hh + 1) * 128], jnp.where(sel, qr, 0.0)], axis=1)
            out.append((q * scale).astype(BF16))
        return out

    @pl.when(jnp.logical_not(is_ctx))
    def _():
        for hh, q in enumerate(queries(slice(None))):
            blk = slice(hh * 256, (hh + 1) * 256)
            o = _softmax_pv(q, [kc_ref[:, blk], kl_ref[:, blk]], [vc_ref[:, blk], vl_ref[:, blk]])
            ol_ref[:, hh * 128:(hh + 1) * 128] = o.astype(BF16)

    @pl.when(is_ctx)
    def _():
        for hh, q in enumerate(queries(pl.ds(_ctx_half(), CTX_LEN))):
            blk = slice(hh * 256, (hh + 1) * 256)
            oc_ref[:, hh * 128:(hh + 1) * 128] = _softmax_pv(q, [kc_ref[:, blk]], [vc_ref[:, blk]]).astype(BF16)


def _mla_attention(q, kcat, vext, cos, sin):
    nq = SEQ // TQ
    ctx_blk = N_LAT // CTX_LEN
    rope_col = C_HEADS * C_NOPE // 128
    qrow = _attn_qrow
    rope = pl.BlockSpec((TQ, 128), lambda b, h, t: (_rope_block(qrow(b, t), TQ), 0))
    ctx = pl.BlockSpec((CTX_LEN, 512), lambda b, h, t: (ctx_blk + b, h))
    lat = pl.BlockSpec((SEQ, 512), lambda b, h, t: (b, h))
    return pl.pallas_call(
        _mla_attn_kernel,
        out_shape=[jax.ShapeDtypeStruct((N_LAT, C_WIDTH), BF16), jax.ShapeDtypeStruct((N_CTX, C_WIDTH), BF16)],
        grid=(BATCH, C_HEADS // 2, nq + 1),
        in_specs=[
            pl.BlockSpec((TQ, 256), lambda b, h, t: (qrow(b, t), h)),
            pl.BlockSpec((TQ, 128), lambda b, h, t: (qrow(b, t), rope_col + h)),
            rope, rope, ctx, ctx, lat, lat,
        ],
        out_specs=[pl.BlockSpec((TQ, 256), lambda b, h, t: (b * nq + jnp.minimum(t, nq - 1), h)),
                   pl.BlockSpec((CTX_LEN, 256), lambda b, h, t: (b, h))],
        compiler_params=_cparams(("arbitrary",) * 3),
        name="mla_attn",
    )(q, q, cos, sin, kcat, vext, kcat, vext)


def _s5_matrices(lam_re, lam_im, log_dt, b_re, b_im, c_re, c_im):
    hi = lax.Precision.HIGHEST
    lam_re, lam_im = lam_re.astype(F32), lam_im.astype(F32)
    dt = jnp.exp(log_dt.astype(F32))[..., None]
    ks = jnp.arange(S5_CHUNK + 1, dtype=F32)[:, None, None, None]
    mag = jnp.exp(lam_re[None] * dt[None] * ks)
    ang = lam_im[None] * dt[None] * ks
    pw_re, pw_im = mag * jnp.cos(ang), mag * jnp.sin(ang)
    a_re, a_im = pw_re[1], pw_im[1]
    den = lam_re * lam_re + lam_im * lam_im
    f_re = ((a_re - 1.0) * lam_re + a_im * lam_im) / den
    f_im = (a_im * lam_re - (a_re - 1.0) * lam_im) / den
    b_re, b_im = b_re.astype(F32), b_im.astype(F32)
    bb_re = f_re[..., None] * b_re - f_im[..., None] * b_im
    bb_im = f_re[..., None] * b_im + f_im[..., None] * b_re
    c_re, c_im = c_re.astype(F32), c_im.astype(F32)

    ab_re = pw_re[:S5_CHUNK, ..., None] * bb_re[None] - pw_im[:S5_CHUNK, ..., None] * bb_im[None]
    ab_im = pw_re[:S5_CHUNK, ..., None] * bb_im[None] + pw_im[:S5_CHUNK, ..., None] * bb_re[None]
    kern = (jnp.einsum('dgcn,ldgne->ldgce', c_re, ab_re, precision=hi)
            - jnp.einsum('dgcn,ldgne->ldgce', c_im, ab_im, precision=hi))
    t_idx = np.arange(S5_CHUNK)
    sub = lambda m, ax: m.reshape(m.shape[:ax] + (S5_QUADS, S5_GSUB) + m.shape[ax + 1:])
    klag = jnp.transpose(sub(kern, 2), (1, 2, 0, 5, 3, 4)).reshape(2, S5_QUADS, S5_CHUNK, S5_GROUP, 128)
    ps, qs = [], []
    for d in range(2):
        p_pow = (S5_CHUNK - 1 - t_idx) if d == 0 else t_idx
        pr = pw_re[p_pow, d][..., None] * bb_re[d][None] - pw_im[p_pow, d][..., None] * bb_im[d][None]
        pi = pw_re[p_pow, d][..., None] * bb_im[d][None] + pw_im[p_pow, d][..., None] * bb_re[d][None]
        pc = jnp.stack([pr, pi], axis=0)
        ps.append(jnp.transpose(sub(pc, 2), (2, 1, 3, 5, 0, 4)).reshape(S5_QUADS, S5_CHUNK * 128, 128))
        q_pow = (t_idx + 1) if d == 0 else (S5_CHUNK - t_idx)
        ct_re = jnp.transpose(c_re[d], (0, 2, 1))
        ct_im = jnp.transpose(c_im[d], (0, 2, 1))
        aq_re = jnp.transpose(pw_re[q_pow, d], (1, 2, 0))
        aq_im = jnp.transpose(pw_im[q_pow, d], (1, 2, 0))
        qr = ct_re[:, :, None, :] * aq_re[..., None] - ct_im[:, :, None, :] * aq_im[..., None]
        qi = ct_re[:, :, None, :] * aq_im[..., None] + ct_im[:, :, None, :] * aq_re[..., None]
        qc = jnp.stack([qr, -qi], axis=0)
        qs.append(jnp.transpose(sub(qc, 1), (1, 0, 3, 4, 2, 5)).reshape(S5_QUADS, 128, S5_CHUNK * 128))
    a16_re, a16_im = pw_re[S5_CHUNK], pw_im[S5_CHUNK]
    lanes = S5_GROUPS * 2 * S5_STATE
    return dict(
        klag=klag.astype(BF16),
        p=jnp.stack(ps).astype(BF16),
        q=jnp.stack(qs).astype(BF16),
        a_mul=jnp.concatenate([a16_re, a16_re], axis=-1).reshape(2, 1, lanes),
        a_swp=jnp.concatenate([-a16_im, a16_im], axis=-1).reshape(2, 1, lanes),
    )


def _s5_expand_p(pc):
    rep = jnp.concatenate([pc] * S5_GSUB, axis=1)
    row = lax.broadcasted_iota(jnp.int32, rep.shape, 0)
    col = lax.broadcasted_iota(jnp.int32, rep.shape, 1)
    return jnp.where((row // S5_GROUP) % S5_GSUB == col // 128, rep, jnp.zeros_like(rep))


def _s5_expand_q(qc):
    rep = jnp.concatenate([qc] * S5_GSUB, axis=0)
    row = lax.broadcasted_iota(jnp.int32, rep.shape, 0)
    col = lax.broadcasted_iota(jnp.int32, rep.shape, 1)
    return jnp.where(row // 128 == (col // S5_GROUP) % S5_GSUB, rep, jnp.zeros_like(rep))


def _s5_chunk_rows(u_ref):
    n = N_TOK // S5_CHUNK
    return jnp.concatenate([u_ref[pl.ds(s, n, stride=S5_CHUNK), :] for s in range(S5_CHUNK)], axis=-1).astype(BF16)


def _s5_z_kernel(u_ref, p_ref, z_ref):
    z_ref[...] = jnp.dot(_s5_chunk_rows(u_ref), _s5_expand_p(p_ref[...]), preferred_element_type=F32)


def _s5_scan_kernel(z_ref, amul_ref, aswp_ref, s_ref, zs_scr):
    d = pl.program_id(0)
    a_mul, a_swp = amul_ref[...], aswp_ref[...]
    lanes = s_ref.shape[1]
    z = z_ref[...]
    low_half = lax.broadcasted_iota(jnp.int32, z.shape, 1) % 128 < S5_STATE
    zs_scr[...] = jnp.where(low_half, pltpu.roll(z, lanes - S5_STATE, 1), pltpu.roll(z, S5_STATE, 1))
    nl, nc = SEQ // S5_CHUNK, S5_CTX_CHUNKS

    def segment(base, count, carry):
        def body(step, carry):
            k = jnp.where(d == 0, step, count - 1 - step)
            out = []
            for b in range(BATCH):
                s, w = carry[2 * b], carry[2 * b + 1]
                row = pl.ds(base + b * count + k, 1)
                s_ref[row, :] = s
                out += [a_mul * s + a_swp * w + z_ref[row, :], a_mul * w - a_swp * s + zs_scr[row, :]]
            return tuple(out)
        return lax.fori_loop(0, count, body, carry)

    zero = jnp.zeros((1, lanes), F32)
    carry = segment(BATCH * nl, nc, (zero,) * (2 * BATCH))
    segment(0, nl, carry)


def _s5_y_kernel(u_ref, klag_ref, s_ref, q_ref, dv_ref, y_ref, m_scr):
    d = pl.program_id(1)
    n = N_TOK // S5_CHUNK
    m_scr[...] = jnp.zeros_like(m_scr)
    row = lax.broadcasted_iota(jnp.int32, (128, 128), 0)
    col = lax.broadcasted_iota(jnp.int32, (128, 128), 1)
    same_group = row // S5_GROUP == col // S5_GROUP
    blocks = [jnp.where(same_group, jnp.concatenate([klag_ref[lag]] * S5_GSUB, axis=0), jnp.zeros((128, 128), BF16))
              for lag in range(S5_CHUNK)]
    for rev in range(2):
        @pl.when(d == rev)
        def _():
            for s in range(S5_CHUNK):
                for t in range(S5_CHUNK):
                    lag = (s - t) if rev else (t - s)
                    if lag >= 0:
                        m_scr[s * 128:(s + 1) * 128, t * 128:(t + 1) * 128] = blocks[lag]
    y = jnp.dot(_s5_chunk_rows(u_ref), m_scr[...], preferred_element_type=F32)
    y = y + jnp.dot(s_ref[...].astype(BF16), _s5_expand_q(q_ref[...]), preferred_element_type=F32)
    for t in range(S5_CHUNK):
        rows = pl.ds(t, n, stride=S5_CHUNK)
        yt = y[:, t * 128:(t + 1) * 128]

        @pl.when(d == 0)
        def _():
            y_ref[rows, :] = yt + u_ref[rows, :] * dv_ref[...]

        @pl.when(d == 1)
        def _():
            y_ref[rows, :] = y_ref[rows, :] + yt


def _s5(p, mats, layer, dskip):
    nstate = S5_GROUPS * 2 * S5_STATE
    qlanes = nstate // S5_QUADS
    slanes = 512
    ucol = (C_Q_RANK + C_KV_RANK) // 128
    z = pl.pallas_call(
        _s5_z_kernel,
        out_shape=jax.ShapeDtypeStruct((2, S5_ROWS, nstate), F32),
        grid=(2, S5_QUADS),
        in_specs=[
            pl.BlockSpec((N_TOK, 128), lambda d, q: (0, ucol + q)),
            pl.BlockSpec((None, None, None, S5_CHUNK * 128, 128), lambda d, q: (layer, d, q, 0, 0)),
        ],
        out_specs=pl.BlockSpec((None, S5_ROWS, qlanes), lambda d, q: (d, 0, q)),
        compiler_params=_cparams(("arbitrary", "arbitrary")),
        name="s5_chunk_state",
    )(p, mats["p"])
    s = pl.pallas_call(
        _s5_scan_kernel,
        out_shape=jax.ShapeDtypeStruct((2, S5_ROWS, nstate), F32),
        grid=(2, nstate // slanes),
        in_specs=[
            pl.BlockSpec((None, S5_ROWS, slanes), lambda d, q: (d, 0, q)),
            pl.BlockSpec((None, None, 1, slanes), lambda d, q: (layer, d, 0, q)),
            pl.BlockSpec((None, None, 1, slanes), lambda d, q: (layer, d, 0, q)),
        ],
        out_specs=pl.BlockSpec((None, S5_ROWS, slanes), lambda d, q: (d, 0, q)),
        scratch_shapes=[pltpu.VMEM((S5_ROWS, slanes), F32)],
        compiler_params=_cparams(("arbitrary", "arbitrary")),
        name="s5_scan",
    )(z, mats["a_mul"], mats["a_swp"])
    return pl.pallas_call(
        _s5_y_kernel,
        out_shape=jax.ShapeDtypeStruct((N_TOK, D_WIDTH), F32),
        grid=(S5_QUADS, 2),
        in_specs=[
            pl.BlockSpec((N_TOK, 128), lambda q, d: (0, ucol + q)),
            pl.BlockSpec((None, None, None, S5_CHUNK, S5_GROUP, 128), lambda q, d: (layer, d, q, 0, 0, 0)),
            pl.BlockSpec((None, S5_ROWS, qlanes), lambda q, d: (d, 0, q)),
            pl.BlockSpec((None, None, None, 128, S5_CHUNK * 128), lambda q, d: (layer, d, q, 0, 0)),
            pl.BlockSpec((None, 1, 128), lambda q, d: (layer, 0, q)),
        ],
        out_specs=pl.BlockSpec((N_TOK, 128), lambda q, d: (0, q)),
        scratch_shapes=[pltpu.VMEM((S5_CHUNK * 128, S5_CHUNK * 128), BF16)],
        compiler_params=pltpu.CompilerParams(dimension_semantics=("arbitrary", "arbitrary"),
                                             vmem_limit_bytes=56 * 1024 * 1024),
        name="s5_output",
    )(p, mats["klag"], s, mats["q"], dskip)


def _glu_kernel(y_ref, w_ref, o_ref):
    y = y_ref[...]
    z = y * (0.5 * (1.0 + jnp.tanh(math.sqrt(2.0 / math.pi) * (y + 0.044715 * (y * y * y)))))
    gate = jnp.dot(z.astype(BF16), w_ref[...], preferred_element_type=F32)
    o_ref[...] = (z * jax.nn.sigmoid(gate)).astype(BF16)


def _s5_glu(y, w_glu, layer):
    t = 512
    return pl.pallas_call(
        _glu_kernel,
        out_shape=jax.ShapeDtypeStruct((N_TOK, D_WIDTH), BF16),
        grid=(N_TOK // t,),
        in_specs=[pl.BlockSpec((t, D_WIDTH), lambda i: (i, 0)),
                  pl.BlockSpec((None, D_WIDTH, D_WIDTH), lambda i: (layer, 0, 0))],
        out_specs=pl.BlockSpec((t, D_WIDTH), lambda i: (i, 0)),
        compiler_params=_cparams(("arbitrary",)),
        name="s5_glu",
    )(y, w_glu)


LANE_CHUNKS = D_MODEL // 128


def _store_token_major(ref, val):
    rows = val.shape[0]
    for c in range(LANE_CHUNKS):
        ref[pl.ds(c, rows, stride=LANE_CHUNKS), :] = val[:, c * 128:(c + 1) * 128]


def _load_token_major(ref, rows):
    return jnp.concatenate([ref[pl.ds(c, rows, stride=LANE_CHUNKS), :] for c in range(LANE_CHUNKS)], axis=-1)


def _router_kernel(x_ref, g_ref, sh_ref, sc_ref, wr_ref, h_ref, info_ref, gw_ref, cnt_ref, h_scr, carry):
    i = pl.program_id(0)

    @pl.when(i == 0)
    def _():
        carry[...] = jnp.zeros_like(carry)

    for r in range(0, TM, 256):
        h = _rms(x_ref[r:r + 256, :]) * g_ref[...]
        h_scr[r:r + 256, :] = h * (1.0 + sc_ref[...]) + sh_ref[...]
    _store_token_major(h_ref, h_scr[...])
    logits = jnp.dot(h_scr[...], wr_ref[...], preferred_element_type=F32, precision=lax.Precision.HIGHEST)
    lane = lax.broadcasted_iota(jnp.int32, logits.shape, 1)
    neg = -jnp.inf
    big = jnp.int32(1 << 20)

    def first_argmax(v, vmax):
        return jnp.min(jnp.where(v == vmax, lane, big), axis=-1, keepdims=True)

    lg = jnp.where(lane < N_GROUPS, logits, neg)
    mg = jnp.max(lg, axis=-1, keepdims=True)
    g_w = 1.0 / jnp.sum(jnp.exp(lg - mg), axis=-1, keepdims=True)
    g_idx = first_argmax(lg, mg)
    lo = N_GROUPS + EXPERTS_PER_GROUP * g_idx
    le = jnp.where(jnp.logical_and(lane >= lo, lane < lo + EXPERTS_PER_GROUP), logits, neg)
    m1 = jnp.max(le, axis=-1, keepdims=True)
    i1 = first_argmax(le, m1)
    le2 = jnp.where(lane == i1, neg, le)
    m2 = jnp.max(le2, axis=-1, keepdims=True)
    i2 = first_argmax(le2, m2)
    r21 = jnp.exp(m2 - m1)
    w1 = g_w / (1.0 + r21)
    w2 = g_w * r21 / (1.0 + r21)
    oh = jnp.logical_or(lane == i1, lane == i2)
    ohb = jnp.where(oh, 1.0, 0.0).astype(BF16)
    rr = lax.broadcasted_iota(jnp.int32, (TM, TM), 0)
    cc = lax.broadcasted_iota(jnp.int32, (TM, TM), 1)
    lower = jnp.where(rr > cc, 1.0, 0.0).astype(BF16)
    before = jnp.dot(lower, ohb, preferred_element_type=F32) + carry[...]
    rank1 = jnp.sum(jnp.where(lane == i1, before, 0.0), axis=-1, keepdims=True).astype(jnp.int32)
    rank2 = jnp.sum(jnp.where(lane == i2, before, 0.0), axis=-1, keepdims=True).astype(jnp.int32)
    carry[...] = carry[...] + jnp.sum(ohb.astype(F32), axis=0, keepdims=True)
    cnt_ref[...] = jnp.broadcast_to(carry[...], cnt_ref.shape)
    info = jnp.where(lane == 0, i1 - N_GROUPS, jnp.where(lane == 1, i2 - N_GROUPS,
                     jnp.where(lane == 2, rank1, jnp.where(lane == 3, rank2, 0))))
    info_ref[...] = info
    gw_ref[...] = jnp.where(lane == 0, w1, jnp.where(lane == 1, w2, 0.0))


def _router(x, g, mod, w_router, layer):
    return pl.pallas_call(
        _router_kernel,
        out_shape=[jax.ShapeDtypeStruct((N_TOK * LANE_CHUNKS, 128), F32),
                   jax.ShapeDtypeStruct((N_TOK, 128), jnp.int32),
                   jax.ShapeDtypeStruct((N_TOK, 128), F32),
                   jax.ShapeDtypeStruct((8, 128), F32)],
        grid=(N_TOK // TM,),
        in_specs=[
            pl.BlockSpec((TM, D_MODEL), lambda i: (i, 0)),
            pl.BlockSpec((1, D_MODEL), lambda i: (0, 0)),
            pl.BlockSpec((None, None, 1, D_MODEL), lambda i: (_mod_row(i, TM), 3, 0, 0)),
            pl.BlockSpec((None, None, 1, D_MODEL), lambda i: (_mod_row(i, TM), 4, 0, 0)),
            pl.BlockSpec((None, D_MODEL, 128), lambda i: (layer, 0, 0)),
        ],
        out_specs=[pl.BlockSpec((TM * LANE_CHUNKS, 128), lambda i: (i, 0)),
                   pl.BlockSpec((TM, 128), lambda i: (i, 0)),
                   pl.BlockSpec((TM, 128), lambda i: (i, 0)),
                   pl.BlockSpec((8, 128), lambda i: (0, 0))],
        scratch_shapes=[pltpu.VMEM((TM, D_MODEL), F32), pltpu.VMEM((1, 128), F32)],
        compiler_params=_cparams(("arbitrary",)),
        name="moe_router",
    )(x, g, mod, mod, w_router)


FFN_ISSUE_GROUPS = 8


def _ffn_kernel(pos_ref, te_ref, meta_ref, h_hbm, wg_ref, wu_ref, wd_ref, o_ref,
                src, xbuf, wg_b, wu_b, wd_b, sem):
    t = pl.program_id(0)
    nt = meta_ref[0]

    def row_copy(tile, slot, r):
        tok = src[tile * TE + r]
        return pltpu.make_async_copy(
            h_hbm.at[pl.ds(pl.multiple_of(tok * LANE_CHUNKS, LANE_CHUNKS), LANE_CHUNKS), :],
            xbuf.at[slot, pl.ds(pl.multiple_of(r * LANE_CHUNKS, LANE_CHUNKS), LANE_CHUNKS), :],
            sem.at[slot])

    def gather(tile, slot):
        def body(r, _):
            row_copy(tile, slot, r).start()
            return 0
        lax.fori_loop(0, TE, body, 0, unroll=8)

    def wait_tile(slot):
        pltpu.make_async_copy(xbuf.at[slot], xbuf.at[slot], sem.at[slot]).wait()

    @pl.when(t == 0)
    def _():
        for e in range(N_EXPERTS):
            def clear(i, _):
                src[i] = 0
                return 0
            lax.fori_loop(meta_ref[1 + e], meta_ref[1 + N_EXPERTS + e], clear, 0)

        def fill(tok, _):
            src[pos_ref[2 * tok]] = tok
            src[pos_ref[2 * tok + 1]] = tok
            return 0
        lax.fori_loop(0, N_TOK, fill, 0, unroll=8)
        gather(0, 0)

    @pl.when(t >= nt)
    def _():
        o_ref[...] = jnp.zeros_like(o_ref)

    @pl.when(t < nt)
    def _():
        slot = t % 2
        wait_tile(slot)

        @pl.when(jnp.logical_or(t == 0, te_ref[t] != te_ref[jnp.maximum(t - 1, 0)]))
        def _():
            wg_b[...] = wg_ref[...].astype(BF16)
            wu_b[...] = wu_ref[...].astype(BF16)
            wd_b[...] = wd_ref[...].astype(BF16)

        nxt = jnp.minimum(t + 1, nt - 1)
        per = TE // FFN_ISSUE_GROUPS

        def issue(g):
            for r in range(g * per, (g + 1) * per):
                row_copy(nxt, 1 - slot, r).start()

        x = _load_token_major(xbuf.at[slot], TE).astype(BF16)
        half = D_EXPERT // 2
        hg, hu = [], []
        for n in range(2):
            issue(n)
            hg.append(jnp.dot(x, wg_b[:, n * half:(n + 1) * half], preferred_element_type=F32))
        for n in range(2):
            issue(2 + n)
            hu.append(jnp.dot(x, wu_b[:, n * half:(n + 1) * half], preferred_element_type=F32))
        hg, hu = jnp.concatenate(hg, axis=1), jnp.concatenate(hu, axis=1)
        act = (hg * jax.nn.sigmoid(hg) * hu).astype(BF16)
        quarter = D_MODEL // 4
        for n in range(4):
            issue(4 + n)
            y = jnp.dot(act, wd_b[:, n * quarter:(n + 1) * quarter], preferred_element_type=F32)
            for c in range(quarter // 128):
                o_ref[pl.ds(n * (quarter // 128) + c, TE, stride=LANE_CHUNKS), :] = y[:, c * 128:(c + 1) * 128]

        @pl.when(t == nt - 1)
        def _():
            wait_tile(1 - slot)


def _expert_ffn(pos_flat, tile_expert, meta, h, w_gate, w_up, w_down, layer):
    wsel = lambda t, pos, te, meta: (layer, te[t], 0, 0)
    return pl.pallas_call(
        _ffn_kernel,
        out_shape=jax.ShapeDtypeStruct((N_SORT * LANE_CHUNKS, 128), F32),
        grid_spec=pltpu.PrefetchScalarGridSpec(
            num_scalar_prefetch=3,
            grid=(N_ETILES,),
            in_specs=[
                pl.BlockSpec(memory_space=pl.ANY),
                pl.BlockSpec((None, None, D_MODEL, D_EXPERT), wsel),
                pl.BlockSpec((None, None, D_MODEL, D_EXPERT), wsel),
                pl.BlockSpec((None, None, D_EXPERT, D_MODEL), wsel),
            ],
            out_specs=pl.BlockSpec((TE * LANE_CHUNKS, 128), lambda t, pos, te, meta: (t, 0)),
            scratch_shapes=[
                pltpu.SMEM((N_SORT,), jnp.int32),
                pltpu.VMEM((2, TE * LANE_CHUNKS, 128), F32),
                pltpu.VMEM((D_MODEL, D_EXPERT), BF16),
                pltpu.VMEM((D_MODEL, D_EXPERT), BF16),
                pltpu.VMEM((D_EXPERT, D_MODEL), BF16),
                pltpu.SemaphoreType.DMA((2,)),
            ],
        ),
        compiler_params=pltpu.CompilerParams(dimension_semantics=("arbitrary",),
                                             vmem_limit_bytes=56 * 1024 * 1024),
        name="moe_expert_ffn",
    )(pos_flat, tile_expert, meta, h, w_gate, w_up, w_down)


TC = 256


def _combine_kernel(pos_ref, x_ref, gate_ref, gw_ref, fn_ref, ys_hbm, o_ref, buf_a, buf_b, sem, *, final_norm):
    i = pl.program_id(0)
    slot = i % 2

    def slab(ref, row):
        return ref.at[pl.ds(pl.multiple_of(row * LANE_CHUNKS, LANE_CHUNKS), LANE_CHUNKS), :]

    def gather(tile, slot):
        def body(r, _):
            tok = tile * TC + r
            pltpu.make_async_copy(slab(ys_hbm, pos_ref[2 * tok]), slab(buf_a.at[slot], r), sem.at[slot, 0]).start()
            pltpu.make_async_copy(slab(ys_hbm, pos_ref[2 * tok + 1]), slab(buf_b.at[slot], r), sem.at[slot, 1]).start()
            return 0
        lax.fori_loop(0, TC, body, 0, unroll=8)

    @pl.when(i == 0)
    def _():
        gather(0, 0)

    @pl.when(i + 1 < pl.num_programs(0))
    def _():
        gather(i + 1, 1 - slot)

    pltpu.make_async_copy(buf_a.at[slot], buf_a.at[slot], sem.at[slot, 0]).wait()
    pltpu.make_async_copy(buf_b.at[slot], buf_b.at[slot], sem.at[slot, 1]).wait()
    w0 = gw_ref[:, 0:1]
    w1 = gw_ref[:, 1:2]
    y = x_ref[...] + gate_ref[...] * (w0 * _load_token_major(buf_a.at[slot], TC)
                                      + w1 * _load_token_major(buf_b.at[slot], TC))
    if final_norm:
        y = _rms(y) * fn_ref[...]
    o_ref[...] = y


def _combine(pos_flat, x, mod, gw, ys, final_g):
    final_norm = final_g is not None
    fn = final_g if final_norm else jnp.ones((1, D_MODEL), F32)
    rows = N_LAT if final_norm else N_TOK
    return pl.pallas_call(
        functools.partial(_combine_kernel, final_norm=final_norm),
        out_shape=jax.ShapeDtypeStruct((rows, D_MODEL), F32),
        grid_spec=pltpu.PrefetchScalarGridSpec(
            num_scalar_prefetch=1,
            grid=(rows // TC,),
            in_specs=[
                pl.BlockSpec((TC, D_MODEL), lambda i, pos: (i, 0)),
                pl.BlockSpec((None, None, 1, D_MODEL), lambda i, pos: (_mod_row(i, TC), 5, 0, 0)),
                pl.BlockSpec((TC, 128), lambda i, pos: (i, 0)),
                pl.BlockSpec((1, D_MODEL), lambda i, pos: (0, 0)),
                pl.BlockSpec(memory_space=pl.ANY),
            ],
            out_specs=pl.BlockSpec((TC, D_MODEL), lambda i, pos: (i, 0)),
            scratch_shapes=[pltpu.VMEM((2, TC * LANE_CHUNKS, 128), F32), pltpu.VMEM((2, TC * LANE_CHUNKS, 128), F32),
                            pltpu.SemaphoreType.DMA((2, 2))],
        ),
        compiler_params=_cparams(("arbitrary",)),
        name="moe_combine",
    )(pos_flat, x, mod, gw, fn, ys)


def _moe(x, g, mod, w_router, w_gate, w_up, w_down, layer, final_g):
    h, info, gw, cnt = _router(x, g, mod, w_router, layer)
    counts = cnt[0, N_GROUPS:N_GROUPS + N_EXPERTS].astype(jnp.int32)
    padded = ((counts + TE - 1) // TE) * TE
    ends = jnp.cumsum(padded)
    starts = ends - padded
    experts = jnp.arange(N_EXPERTS, dtype=jnp.int32)
    start_of = jnp.sum(jnp.where(info[:, 0:2, None] == experts, starts, 0), axis=-1)
    pos_flat = (start_of + info[:, 2:4]).reshape(-1)
    tile_ends = ends // TE
    num_tiles = tile_ends[-1]
    tiles = jnp.minimum(jnp.arange(N_ETILES, dtype=jnp.int32), num_tiles - 1)
    tile_expert = jnp.sum((tile_ends[None, :] <= tiles[:, None]).astype(jnp.int32), axis=-1)
    meta = jnp.concatenate([num_tiles[None], starts + counts, ends]).astype(jnp.int32)
    ys = _expert_ffn(pos_flat, tile_expert, meta, h, w_gate, w_up, w_down, layer)
    return _combine(pos_flat, x, mod, gw, ys, final_g)


def kernel(x, c, ctx, c_ctx, mod_w, mod_b, norm_mix, norm_ffn, ab_w_in, ab_q_norm, ab_k_norm, ab_conv_w, ab_w_out, cd_w_in, cd_q_norm, cd_kv_norm, cd_w_uq, cd_w_ukv, s5_lam_re, s5_lam_im, s5_log_dt, s5_b_re, s5_b_im, s5_c_re, s5_c_im, s5_d, s5_w_glu, cd_w_out, moe_w_group, moe_w_expert, moe_w_gate, moe_w_up, moe_w_down, final_norm):
    cc = jnp.concatenate([c, c_ctx[None, :], jnp.zeros((8 - BATCH - 1, D_MODEL), F32)], axis=0)
    mods = _modulation(cc, mod_w, mod_b).reshape(DEPTH, 8, N_MOD, 1, D_MODEL)
    xs = jnp.concatenate([x.reshape(N_LAT, D_MODEL), ctx.reshape(N_CTX, D_MODEL)], axis=0)
    cos_a, sin_a = _rope_tables(HEAD_DIM)
    cos_c, sin_c = _rope_tables(C_ROPE)

    ab_in_b, ab_out_b = ab_w_in.astype(BF16), ab_w_out.astype(BF16)
    cd_out_b, ukv_b, glu_b = cd_w_out.astype(BF16), cd_w_ukv.astype(BF16), s5_w_glu.astype(BF16)
    a, b_ = C_Q_RANK + C_KV_RANK, C_Q_RANK + C_KV_RANK + C_ROPE
    pad = jnp.zeros(cd_w_in.shape[:2] + (CD_IN_PAD - cd_w_in.shape[2],), F32)
    cd_in_b = jnp.concatenate([cd_w_in[..., :a], cd_w_in[..., b_:], cd_w_in[..., a:b_], pad], axis=-1).astype(BF16)
    w_uq = cd_w_uq.reshape(-1, C_Q_RANK, C_HEADS, C_NOPE + C_ROPE)
    uq_b = jnp.concatenate([w_uq[..., :C_NOPE].reshape(-1, C_Q_RANK, C_HEADS * C_NOPE),
                            w_uq[..., C_NOPE:].reshape(-1, C_Q_RANK, C_HEADS * C_ROPE)], axis=-1).astype(BF16)

    mats = jax.vmap(_s5_matrices)(s5_lam_re, s5_lam_im, s5_log_dt, s5_b_re, s5_b_im, s5_c_re, s5_c_im)
    w_router = jnp.concatenate(
        [moe_w_group, jnp.transpose(moe_w_expert, (0, 2, 1, 3)).reshape(DEPTH, D_MODEL, N_EXPERTS),
         jnp.zeros((DEPTH, D_MODEL, 128 - N_GROUPS - N_EXPERTS), F32)], axis=-1)

    for i in range(DEPTH):
        j = i // 2
        mod = mods[i]
        if i % 2 == 0:
            p = _norm_linear(xs, 0, D_MODEL, norm_mix[i][None, :], mod, 0, ab_in_b, j, 2048)
            k, vext = _gqa_kv(p, ab_k_norm[j][None, :], cos_a, sin_a)
            o = jnp.concatenate(_gqa_attention(p, k, vext, ab_q_norm[j][None, :], cos_a, sin_a), axis=0)
            side = _gated_conv(p, ab_conv_w[j])
            w_out = ab_out_b
        else:
            p = _norm_linear(xs, 0, D_MODEL, norm_mix[i][None, :], mod, 0, cd_in_b, j, CD_IN_PAD)
            q = _norm_linear(p, 0, C_Q_RANK, cd_q_norm[j][None, :], None, 0, uq_b, j, uq_b.shape[2])
            kvu = _norm_linear(p, C_Q_RANK // C_KV_RANK, C_KV_RANK, cd_kv_norm[j][None, :], None, 0,
                               ukv_b, j, ukv_b.shape[2])
            kcat, vext = _mla_kv(kvu, p, cos_c, sin_c)
            o = jnp.concatenate(_mla_attention(q, kcat, vext, cos_c, sin_c), axis=0)
            y = _s5(p, mats, j, s5_d.astype(F32)[:, None, :])
            side = _s5_glu(y, glu_b, j)
            w_out = cd_out_b
        xs = _out_linear(o, side, w_out, j, xs, mod, 2)
        xs = _moe(xs, norm_ffn[i][None, :], mod, w_router, moe_w_gate, moe_w_up, moe_w_down, i,
                  final_norm[None, :] if i == DEPTH - 1 else None)
    return xs.reshape(BATCH, SEQ, D_MODEL)
```

```python
import functools
import math

import numpy as np
import jax
import jax.numpy as jnp
from jax import lax
from jax.experimental import pallas as pl
from jax.experimental.pallas import tpu as pltpu

F32 = jnp.float32
BF16 = jnp.bfloat16

D_MODEL = 2048
BATCH = 4
SEQ = 2048
DEPTH = 4
GRID_W = 64
CTX_LEN = 256
ROPE_THETA = 10000.0
EPS = 1e-6
N_MOD = 6
HEAD_DIM = 128
A_Q_HEADS = 12
A_KV_HEADS = 4
A_GROUP = A_Q_HEADS // A_KV_HEADS
A_WIDTH = A_Q_HEADS * HEAD_DIM
A_KV_WIDTH = A_KV_HEADS * HEAD_DIM
B_WIDTH = 512
AB_IN = A_WIDTH + 2 * A_KV_WIDTH + 3 * B_WIDTH
C_HEADS = 12
C_NOPE = 128
C_ROPE = 64
C_V = 128
C_Q_RANK = 512
C_KV_RANK = 256
C_WIDTH = C_HEADS * C_V
D_WIDTH = 512
S5_GROUP = 16
S5_GROUPS = D_WIDTH // S5_GROUP
S5_STATE = 64
N_GROUPS = 4
EXPERTS_PER_GROUP = 4
N_EXPERTS = N_GROUPS * EXPERTS_PER_GROUP
D_EXPERT = 512

N_LAT = BATCH * SEQ
N_CTX = BATCH * CTX_LEN
N_TOK = N_LAT + N_CTX
CTX_ROW = BATCH
CD_IN_PAD = 1536

TM = 512
TQ = 512
TE = 256
N_SORT = 2 * N_TOK + N_EXPERTS * TE
N_ETILES = N_SORT // TE
S5_CHUNK = 16
S5_CTX_CHUNKS = CTX_LEN // S5_CHUNK
S5_ROWS = N_TOK // S5_CHUNK
S5_GSUB = 128 // S5_GROUP
S5_QUADS = S5_GROUPS // S5_GSUB
VMEM_LIMIT = 48 * 1024 * 1024


def _cparams(sem):
    return pltpu.CompilerParams(dimension_semantics=sem, vmem_limit_bytes=VMEM_LIMIT)


def _mod_row(tile, tile_rows):
    r0 = tile * tile_rows
    return jnp.where(r0 >= N_LAT, CTX_ROW, r0 // SEQ)


def _rope_block(tile, tile_rows):
    r0 = tile * tile_rows
    return jnp.where(r0 >= N_LAT, SEQ // tile_rows, (r0 % SEQ) // tile_rows)


def _rms(x):
    return x * lax.rsqrt(jnp.mean(x * x, axis=-1, keepdims=True) + EPS)


def _mod_kernel(cc_ref, w_ref, b_ref, o_ref):
    cc = cc_ref[...]
    s = (cc * jax.nn.sigmoid(cc)).astype(BF16)
    o_ref[...] = jnp.dot(s, w_ref[...].astype(BF16), preferred_element_type=F32) + b_ref[...]


def _modulation(cc, mod_w, mod_b):
    tn = 1024
    nout = N_MOD * D_MODEL
    return pl.pallas_call(
        _mod_kernel,
        out_shape=jax.ShapeDtypeStruct((DEPTH, 8, nout), F32),
        grid=(DEPTH, nout // tn),
        in_specs=[
            pl.BlockSpec((8, D_MODEL), lambda l, j: (0, 0)),
            pl.BlockSpec((None, D_MODEL, tn), lambda l, j: (l, 0, j)),
            pl.BlockSpec((None, 1, tn), lambda l, j: (l, 0, j)),
        ],
        out_specs=pl.BlockSpec((None, 8, tn), lambda l, j: (l, 0, j)),
        compiler_params=_cparams(("arbitrary", "arbitrary")),
        name="modulation",
    )(cc, mod_w, mod_b.reshape(DEPTH, 1, nout))


def _norm_linear_kernel(x_ref, g_ref, sh_ref, sc_ref, w_ref, o_ref, h_scr, *, modulate, rows):
    @pl.when(pl.program_id(1) == 0)
    def _():
        for r in range(0, rows, 256):
            h = _rms(x_ref[r:r + 256, :]) * g_ref[...]
            if modulate:
                h = h * (1.0 + sc_ref[...]) + sh_ref[...]
            h_scr[r:r + 256, :] = h.astype(BF16)

    o_ref[...] = jnp.dot(h_scr[...], w_ref[...], preferred_element_type=F32)


def _norm_linear(x, xcol, kdim, g, mod, which, w, layer, tn):
    nout = w.shape[2]
    modulate = mod is not None
    if modulate:
        sh_spec = pl.BlockSpec((None, None, 1, kdim), lambda i, j: (_mod_row(i, TM), which, 0, 0))
        sc_spec = pl.BlockSpec((None, None, 1, kdim), lambda i, j: (_mod_row(i, TM), which + 1, 0, 0))
        sh = sc = mod
    else:
        sh_spec = sc_spec = pl.BlockSpec((1, kdim), lambda i, j: (0, 0))
        sh = sc = g
    return pl.pallas_call(
        functools.partial(_norm_linear_kernel, modulate=modulate, rows=TM),
        out_shape=jax.ShapeDtypeStruct((N_TOK, nout), F32),
        grid=(N_TOK // TM, nout // tn),
        in_specs=[
            pl.BlockSpec((TM, kdim), lambda i, j: (i, xcol)),
            pl.BlockSpec((1, kdim), lambda i, j: (0, 0)),
            sh_spec, sc_spec,
            pl.BlockSpec((None, kdim, tn), lambda i, j: (layer, 0, j)),
        ],
        out_specs=pl.BlockSpec((TM, tn), lambda i, j: (i, j)),
        scratch_shapes=[pltpu.VMEM((TM, kdim), BF16)],
        compiler_params=_cparams(("arbitrary", "arbitrary")),
        name="norm_linear",
    )(x, g, sh, sc, w)


def _out_linear_kernel(a1l_ref, a1c_ref, a2_ref, w1_ref, w2_ref, x_ref, gate_ref, o_ref):
    a1 = jnp.where(pl.program_id(0) >= N_LAT // TM, a1c_ref[...], a1l_ref[...])
    acc = jnp.dot(a1, w1_ref[...], preferred_element_type=F32)
    acc = acc + jnp.dot(a2_ref[...], w2_ref[...], preferred_element_type=F32)
    o_ref[...] = x_ref[...] + gate_ref[...] * acc


def _out_linear(a1_lat, a1_ctx, a2, w, layer, x, mod, which):
    tn = D_MODEL
    k1, k2 = a1_lat.shape[1], a2.shape[1]
    lat_tiles = N_LAT // TM
    return pl.pallas_call(
        _out_linear_kernel,
        out_shape=jax.ShapeDtypeStruct((N_TOK, D_MODEL), F32),
        grid=(N_TOK // TM, D_MODEL // tn),
        in_specs=[
            pl.BlockSpec((TM, k1), lambda i, j: (jnp.minimum(i, lat_tiles - 1), 0)),
            pl.BlockSpec((TM, k1), lambda i, j: (jnp.maximum(i - lat_tiles, 0), 0)),
            pl.BlockSpec((TM, k2), lambda i, j: (i, 0)),
            pl.BlockSpec((None, k1, tn), lambda i, j: (layer, 0, j)),
            pl.BlockSpec((None, k2, tn), lambda i, j: (layer, k1 // k2, j)),
            pl.BlockSpec((TM, tn), lambda i, j: (i, j)),
            pl.BlockSpec((None, None, 1, tn), lambda i, j: (_mod_row(i, TM), which, 0, j)),
        ],
        out_specs=pl.BlockSpec((TM, tn), lambda i, j: (i, j)),
        compiler_params=_cparams(("arbitrary", "arbitrary")),
        name="out_linear",
    )(a1_lat, a1_ctx, a2, w, w, x, mod)


def _rope_tables(rot_dim):
    rows = SEQ // GRID_W
    row_ids = np.repeat(np.arange(rows, dtype=np.float32), GRID_W)
    col_ids = np.tile(np.arange(GRID_W, dtype=np.float32), rows)
    d_axis = rot_dim // 2
    inv = (np.float32(ROPE_THETA) ** (-np.arange(0, d_axis, 2, dtype=np.float32) / np.float32(d_axis))).astype(np.float32)
    ang = np.concatenate([row_ids[:, None] * inv, col_ids[:, None] * inv], axis=-1).astype(np.float32)
    cos, sin = np.cos(ang).astype(np.float32), np.sin(ang).astype(np.float32)
    reps = 128 // rot_dim
    cos_f = np.tile(np.concatenate([cos, cos], axis=-1), (1, reps))
    sin_f = np.tile(np.concatenate([-sin, sin], axis=-1), (1, reps))
    cos_f = np.concatenate([cos_f, np.ones((512, 128), np.float32)], axis=0)
    sin_f = np.concatenate([sin_f, np.zeros((512, 128), np.float32)], axis=0)
    return jnp.asarray(cos_f), jnp.asarray(sin_f)


def _rope128(x, cos, sin):
    return x * cos + pltpu.roll(x, 64, 1) * sin


def _rope64(x, cos, sin):
    lane = lax.broadcasted_iota(jnp.int32, x.shape, 1)
    swapped = jnp.where((lane % 64) < 32, pltpu.roll(x, 96, 1), pltpu.roll(x, 32, 1))
    return x * cos + swapped * sin


def _ones_column(rows):
    lane = lax.broadcasted_iota(jnp.int32, (rows, 128), 1)
    return jnp.where(lane == 0, 1.0, 0.0).astype(BF16)


def _gqa_kv_kernel(k_ref, v_ref, kn_ref, cos_ref, sin_ref, ko_ref, vo_ref):
    cos, sin = cos_ref[...], sin_ref[...]
    ones = _ones_column(k_ref.shape[0])
    for h in range(A_KV_HEADS):
        k = _rms(k_ref[:, h * 128:(h + 1) * 128]) * kn_ref[...]
        ko_ref[:, h * 128:(h + 1) * 128] = _rope128(k, cos, sin).astype(BF16)
        vo_ref[:, h * 256:h * 256 + 128] = v_ref[:, h * 128:(h + 1) * 128].astype(BF16)
        vo_ref[:, h * 256 + 128:(h + 1) * 256] = ones


def _gqa_kv(p, k_norm, cos, sin):
    t = 512
    return pl.pallas_call(
        _gqa_kv_kernel,
        out_shape=[jax.ShapeDtypeStruct((N_TOK, A_KV_WIDTH), BF16),
                   jax.ShapeDtypeStruct((N_TOK, 2 * A_KV_WIDTH), BF16)],
        grid=(N_TOK // t,),
        in_specs=[
            pl.BlockSpec((t, A_KV_WIDTH), lambda i: (i, A_WIDTH // A_KV_WIDTH)),
            pl.BlockSpec((t, A_KV_WIDTH), lambda i: (i, A_WIDTH // A_KV_WIDTH + 1)),
            pl.BlockSpec((1, 128), lambda i: (0, 0)),
            pl.BlockSpec((t, 128), lambda i: (_rope_block(i, t), 0)),
            pl.BlockSpec((t, 128), lambda i: (_rope_block(i, t), 0)),
        ],
        out_specs=[pl.BlockSpec((t, A_KV_WIDTH), lambda i: (i, 0)),
                   pl.BlockSpec((t, 2 * A_KV_WIDTH), lambda i: (i, 0))],
        compiler_params=_cparams(("arbitrary",)),
        name="gqa_kv",
    )(p, p, k_norm, cos, sin)


LOG2E = math.log2(math.e)
KEY_PIECE = 1024


def _softmax_pv(q, keys, vals):
    dn = (((1,), (1,)), ((), ()))
    m = acc = None
    for k, v in zip(keys, vals):
        s = lax.dot_general(q, k, dn, preferred_element_type=F32)
        m_piece = jnp.max(s, axis=-1, keepdims=True)
        if m is None:
            m_new = m_piece
        else:
            m_new = jnp.maximum(m, m_piece)
            acc = acc * jnp.exp2(m - m_new)
        part = jnp.dot(jnp.exp2(s - m_new).astype(BF16), v, preferred_element_type=F32)
        acc = part if acc is None else acc + part
        m = m_new
    return acc[:, :128] / acc[:, 128:129]


def _attn_qrow(b, t):
    nq = SEQ // TQ
    return jnp.where(t == nq, (N_LAT + b * CTX_LEN) // TQ, b * nq + t)


def _ctx_half():
    return pl.multiple_of((pl.program_id(0) % (TQ // CTX_LEN)) * CTX_LEN, CTX_LEN)


def _gqa_attn_kernel(q_ref, qn_ref, cos_ref, sin_ref, kc_ref, vc_ref, kl_ref, vl_ref, ol_ref, oc_ref):
    scale = HEAD_DIM ** -0.5 * LOG2E
    is_ctx = pl.program_id(2) == SEQ // TQ

    def query(rows, g):
        q = _rms(q_ref[rows, g * 128:(g + 1) * 128]) * qn_ref[...]
        return (_rope128(q, cos_ref[rows, :], sin_ref[rows, :]) * scale).astype(BF16)

    @pl.when(jnp.logical_not(is_ctx))
    def _():
        pieces = [slice(j * KEY_PIECE, (j + 1) * KEY_PIECE) for j in range(SEQ // KEY_PIECE)]
        qs = [query(slice(None), g) for g in range(A_GROUP)]
        for g in range(A_GROUP):
            o = _softmax_pv(qs[g], [kc_ref[...]] + [kl_ref[r, :] for r in pieces],
                            [vc_ref[...]] + [vl_ref[r, :] for r in pieces])
            ol_ref[:, g * 128:(g + 1) * 128] = o.astype(BF16)

    @pl.when(is_ctx)
    def _():
        rows = pl.ds(_ctx_half(), CTX_LEN)
        for g in range(A_GROUP):
            oc_ref[:, g * 128:(g + 1) * 128] = _softmax_pv(query(rows, g), [kc_ref[...]], [vc_ref[...]]).astype(BF16)


def _gqa_attention(p, k, vext, q_norm, cos, sin):
    qw = A_GROUP * HEAD_DIM
    nq = SEQ // TQ
    ctx_blk = N_LAT // CTX_LEN
    rope = pl.BlockSpec((TQ, 128), lambda b, h, t: (_rope_block(_attn_qrow(b, t), TQ), 0))
    return pl.pallas_call(
        _gqa_attn_kernel,
        out_shape=[jax.ShapeDtypeStruct((N_LAT, A_WIDTH), BF16), jax.ShapeDtypeStruct((N_CTX, A_WIDTH), BF16)],
        grid=(BATCH, A_KV_HEADS, nq + 1),
        in_specs=[
            pl.BlockSpec((TQ, qw), lambda b, h, t: (_attn_qrow(b, t), h)),
            pl.BlockSpec((1, 128), lambda b, h, t: (0, 0)),
            rope, rope,
            pl.BlockSpec((CTX_LEN, 128), lambda b, h, t: (ctx_blk + b, h)),
            pl.BlockSpec((CTX_LEN, 256), lambda b, h, t: (ctx_blk + b, h)),
            pl.BlockSpec((SEQ, 128), lambda b, h, t: (b, h)),
            pl.BlockSpec((SEQ, 256), lambda b, h, t: (b, h)),
        ],
        out_specs=[pl.BlockSpec((TQ, qw), lambda b, h, t: (b * nq + jnp.minimum(t, nq - 1), h)),
                   pl.BlockSpec((CTX_LEN, qw), lambda b, h, t: (b, h))],
        compiler_params=_cparams(("arbitrary",) * 3),
        name="gqa_attn",
    )(p, q_norm, cos, sin, k, vext, k, vext)


def _conv_kernel(bg_ref, cg_ref, ug_ref, cgp_ref, ugp_ref, cgn_ref, ugn_ref, w_ref, o_ref, *, rows):
    i = pl.program_id(0)
    per_seq = SEQ // rows
    is_ctx = i >= N_LAT // rows
    is_start = jnp.logical_or(is_ctx, i % per_seq == 0)
    is_end = jnp.logical_or(is_ctx, i % per_seq == per_seq - 1)
    m = cg_ref[...] * ug_ref[...]
    m_prev = jnp.where(is_start, 0.0, cgp_ref[7:8, :] * ugp_ref[7:8, :])
    m_next = jnp.where(is_end, 0.0, cgn_ref[0:1, :] * ugn_ref[0:1, :])
    row = lax.broadcasted_iota(jnp.int32, m.shape, 0)
    down = jnp.where(row == 0, m_prev, pltpu.roll(m, 1, 0))
    up = jnp.where(row == rows - 1, m_next, pltpu.roll(m, rows - 1, 0))
    conv = down * w_ref[0:1, :] + m * w_ref[1:2, :] + up * w_ref[2:3, :]
    o_ref[...] = (bg_ref[...] * conv).astype(BF16)


def _gated_conv(p, conv_w):
    rows = CTX_LEN
    base = (A_WIDTH + 2 * A_KV_WIDTH) // B_WIDTH
    halo = rows // 8
    last = N_TOK // 8 - 1
    main = lambda c: pl.BlockSpec((rows, B_WIDTH), lambda i: (i, base + c))
    prev = lambda c: pl.BlockSpec((8, B_WIDTH), lambda i: (jnp.maximum(i * halo - 1, 0), base + c))
    nxt = lambda c: pl.BlockSpec((8, B_WIDTH), lambda i: (jnp.minimum((i + 1) * halo, last), base + c))
    return pl.pallas_call(
        functools.partial(_conv_kernel, rows=rows),
        out_shape=jax.ShapeDtypeStruct((N_TOK, B_WIDTH), BF16),
        grid=(N_TOK // rows,),
        in_specs=[main(0), main(1), main(2), prev(1), prev(2), nxt(1), nxt(2),
                  pl.BlockSpec((3, B_WIDTH), lambda i: (0, 0))],
        out_specs=pl.BlockSpec((rows, B_WIDTH), lambda i: (i, 0)),
        compiler_params=_cparams(("arbitrary",)),
        name="gated_conv",
    )(p, p, p, p, p, p, p, conv_w)


def _mla_kv_kernel(kv_ref, kr_ref, cos_ref, sin_ref, ko_ref, vo_ref):
    x = kr_ref[...]
    lane = lax.broadcasted_iota(jnp.int32, x.shape, 1)
    r = jnp.where(lane < C_ROPE, _rope64(x, cos_ref[...], sin_ref[...]), 0.0)
    kr_low = r.astype(BF16)
    kr_high = pltpu.roll(r, 64, 1).astype(BF16)
    ones = _ones_column(x.shape[0])
    for h in range(C_HEADS):
        ko_ref[:, h * 256:h * 256 + 128] = kv_ref[:, h * 256:h * 256 + 128].astype(BF16)
        ko_ref[:, h * 256 + 128:(h + 1) * 256] = kr_low if h % 2 == 0 else kr_high
        vo_ref[:, h * 256:h * 256 + 128] = kv_ref[:, h * 256 + 128:(h + 1) * 256].astype(BF16)
        vo_ref[:, h * 256 + 128:(h + 1) * 256] = ones


def _mla_kv(kv, p, cos, sin):
    t = 256
    col = (C_Q_RANK + C_KV_RANK + D_WIDTH) // 128
    width = C_HEADS * 256
    return pl.pallas_call(
        _mla_kv_kernel,
        out_shape=[jax.ShapeDtypeStruct((N_TOK, width), BF16)] * 2,
        grid=(N_TOK // t,),
        in_specs=[
            pl.BlockSpec((t, width), lambda i: (i, 0)),
            pl.BlockSpec((t, 128), lambda i: (i, col)),
            pl.BlockSpec((t, 128), lambda i: (_rope_block(i, t), 0)),
            pl.BlockSpec((t, 128), lambda i: (_rope_block(i, t), 0)),
        ],
        out_specs=[pl.BlockSpec((t, width), lambda i: (i, 0))] * 2,
        compiler_params=_cparams(("arbitrary",)),
        name="mla_kv",
    )(kv, p, cos, sin)


def _mla_attn_kernel(qn_ref, qr_ref, cos_ref, sin_ref, kc_ref, vc_ref, kl_ref, vl_ref, ol_ref, oc_ref):
    scale = (C_NOPE + C_ROPE) ** -0.5 * LOG2E
    is_ctx = pl.program_id(2) == SEQ // TQ

    def queries(rows):
        qr = _rope64(qr_ref[rows, :], cos_ref[rows, :], sin_ref[rows, :])
        lane = lax.broadcasted_iota(jnp.int32, qr.shape, 1)
        out = []
        for hh in range(2):
            sel = (lane < 64) if hh == 0 else (lane >= 64)
            q = jnp.concatenate([qn_ref[rows, hh * 128:(hh + 1) * 128], jnp.where(sel, qr, 0.0)], axis=1)
            out.append((q * scale).astype(BF16))
        return out

    @pl.when(jnp.logical_not(is_ctx))
    def _():
        for hh, q in enumerate(queries(slice(None))):
            blk = slice(hh * 256, (hh + 1) * 256)
            pieces = [slice(j * KEY_PIECE, (j + 1) * KEY_PIECE) for j in range(SEQ // KEY_PIECE)]
            o = _softmax_pv(q, [kc_ref[:, blk]] + [kl_ref[r, blk] for r in pieces],
                            [vc_ref[:, blk]] + [vl_ref[r, blk] for r in pieces])
            ol_ref[:, hh * 128:(hh + 1) * 128] = o.astype(BF16)

    @pl.when(is_ctx)
    def _():
        for hh, q in enumerate(queries(pl.ds(_ctx_half(), CTX_LEN))):
            blk = slice(hh * 256, (hh + 1) * 256)
            oc_ref[:, hh * 128:(hh + 1) * 128] = _softmax_pv(q, [kc_ref[:, blk]], [vc_ref[:, blk]]).astype(BF16)


def _mla_attention(q, kcat, vext, cos, sin):
    nq = SEQ // TQ
    ctx_blk = N_LAT // CTX_LEN
    rope_col = C_HEADS * C_NOPE // 128
    qrow = _attn_qrow
    rope = pl.BlockSpec((TQ, 128), lambda b, h, t: (_rope_block(qrow(b, t), TQ), 0))
    ctx = pl.BlockSpec((CTX_LEN, 512), lambda b, h, t: (ctx_blk + b, h))
    lat = pl.BlockSpec((SEQ, 512), lambda b, h, t: (b, h))
    return pl.pallas_call(
        _mla_attn_kernel,
        out_shape=[jax.ShapeDtypeStruct((N_LAT, C_WIDTH), BF16), jax.ShapeDtypeStruct((N_CTX, C_WIDTH), BF16)],
        grid=(BATCH, C_HEADS // 2, nq + 1),
        in_specs=[
            pl.BlockSpec((TQ, 256), lambda b, h, t: (qrow(b, t), h)),
            pl.BlockSpec((TQ, 128), lambda b, h, t: (qrow(b, t), rope_col + h)),
            rope, rope, ctx, ctx, lat, lat,
        ],
        out_specs=[pl.BlockSpec((TQ, 256), lambda b, h, t: (b * nq + jnp.minimum(t, nq - 1), h)),
                   pl.BlockSpec((CTX_LEN, 256), lambda b, h, t: (b, h))],
        compiler_params=_cparams(("arbitrary",) * 3),
        name="mla_attn",
    )(q, q, cos, sin, kcat, vext, kcat, vext)


def _s5_matrices(lam_re, lam_im, log_dt, b_re, b_im, c_re, c_im):
    hi = lax.Precision.HIGHEST
    lam_re, lam_im = lam_re.astype(F32), lam_im.astype(F32)
    dt = jnp.exp(log_dt.astype(F32))[..., None]
    ks = jnp.arange(S5_CHUNK + 1, dtype=F32)[:, None, None, None]
    mag = jnp.exp(lam_re[None] * dt[None] * ks)
    ang = lam_im[None] * dt[None] * ks
    pw_re, pw_im = mag * jnp.cos(ang), mag * jnp.sin(ang)
    a_re, a_im = pw_re[1], pw_im[1]
    den = lam_re * lam_re + lam_im * lam_im
    f_re = ((a_re - 1.0) * lam_re + a_im * lam_im) / den
    f_im = (a_im * lam_re - (a_re - 1.0) * lam_im) / den
    b_re, b_im = b_re.astype(F32), b_im.astype(F32)
    bb_re = f_re[..., None] * b_re - f_im[..., None] * b_im
    bb_im = f_re[..., None] * b_im + f_im[..., None] * b_re
    c_re, c_im = c_re.astype(F32), c_im.astype(F32)

    ab_re = pw_re[:S5_CHUNK, ..., None] * bb_re[None] - pw_im[:S5_CHUNK, ..., None] * bb_im[None]
    ab_im = pw_re[:S5_CHUNK, ..., None] * bb_im[None] + pw_im[:S5_CHUNK, ..., None] * bb_re[None]
    kern = (jnp.einsum('dgcn,ldgne->ldgce', c_re, ab_re, precision=hi)
            - jnp.einsum('dgcn,ldgne->ldgce', c_im, ab_im, precision=hi))
    t_idx = np.arange(S5_CHUNK)
    sub = lambda m, ax: m.reshape(m.shape[:ax] + (S5_QUADS, S5_GSUB) + m.shape[ax + 1:])
    klag = jnp.transpose(sub(kern, 2), (1, 2, 0, 5, 3, 4)).reshape(2, S5_QUADS, S5_CHUNK, S5_GROUP, 128)
    ps, qs = [], []
    for d in range(2):
        p_pow = (S5_CHUNK - 1 - t_idx) if d == 0 else t_idx
        pr = pw_re[p_pow, d][..., None] * bb_re[d][None] - pw_im[p_pow, d][..., None] * bb_im[d][None]
        pi = pw_re[p_pow, d][..., None] * bb_im[d][None] + pw_im[p_pow, d][..., None] * bb_re[d][None]
        pc = jnp.stack([pr, pi], axis=0)
        ps.append(jnp.transpose(sub(pc, 2), (2, 1, 3, 5, 0, 4)).reshape(S5_QUADS, S5_CHUNK * 128, 128))
        q_pow = (t_idx + 1) if d == 0 else (S5_CHUNK - t_idx)
        ct_re = jnp.transpose(c_re[d], (0, 2, 1))
        ct_im = jnp.transpose(c_im[d], (0, 2, 1))
        aq_re = jnp.transpose(pw_re[q_pow, d], (1, 2, 0))
        aq_im = jnp.transpose(pw_im[q_pow, d], (1, 2, 0))
        qr = ct_re[:, :, None, :] * aq_re[..., None] - ct_im[:, :, None, :] * aq_im[..., None]
        qi = ct_re[:, :, None, :] * aq_im[..., None] + ct_im[:, :, None, :] * aq_re[..., None]
        qc = jnp.stack([qr, -qi], axis=0)
        qs.append(jnp.transpose(sub(qc, 1), (1, 0, 3, 4, 2, 5)).reshape(S5_QUADS, 128, S5_CHUNK * 128))
    a16_re, a16_im = pw_re[S5_CHUNK], pw_im[S5_CHUNK]
    lanes = S5_GROUPS * 2 * S5_STATE
    return dict(
        klag=klag.astype(BF16),
        p=jnp.stack(ps).astype(BF16),
        q=jnp.stack(qs).astype(BF16),
        a_mul=jnp.concatenate([a16_re, a16_re], axis=-1).reshape(2, 1, lanes),
        a_swp=jnp.concatenate([-a16_im, a16_im], axis=-1).reshape(2, 1, lanes),
    )


def _s5_expand_p(pc):
    rep = jnp.concatenate([pc] * S5_GSUB, axis=1)
    row = lax.broadcasted_iota(jnp.int32, rep.shape, 0)
    col = lax.broadcasted_iota(jnp.int32, rep.shape, 1)
    return jnp.where((row // S5_GROUP) % S5_GSUB == col // 128, rep, jnp.zeros_like(rep))


def _s5_expand_q(qc):
    rep = jnp.concatenate([qc] * S5_GSUB, axis=0)
    row = lax.broadcasted_iota(jnp.int32, rep.shape, 0)
    col = lax.broadcasted_iota(jnp.int32, rep.shape, 1)
    return jnp.where(row // 128 == (col // S5_GROUP) % S5_GSUB, rep, jnp.zeros_like(rep))


def _s5_chunk_rows(u_ref):
    n = N_TOK // S5_CHUNK
    return jnp.concatenate([u_ref[pl.ds(s, n, stride=S5_CHUNK), :] for s in range(S5_CHUNK)], axis=-1).astype(BF16)


def _s5_z_kernel(u_ref, p_ref, z_ref):
    z_ref[...] = jnp.dot(_s5_chunk_rows(u_ref), _s5_expand_p(p_ref[...]), preferred_element_type=F32)


def _s5_scan_kernel(z_ref, amul_ref, aswp_ref, s_ref, zs_scr):
    d = pl.program_id(0)
    a_mul, a_swp = amul_ref[...], aswp_ref[...]
    lanes = s_ref.shape[1]
    z = z_ref[...]
    low_half = lax.broadcasted_iota(jnp.int32, z.shape, 1) % 128 < S5_STATE
    zs_scr[...] = jnp.where(low_half, pltpu.roll(z, lanes - S5_STATE, 1), pltpu.roll(z, S5_STATE, 1))
    nl, nc = SEQ // S5_CHUNK, S5_CTX_CHUNKS

    def segment(base, count, carry):
        def body(step, carry):
            k = jnp.where(d == 0, step, count - 1 - step)
            out = []
            for b in range(BATCH):
                s, w = carry[2 * b], carry[2 * b + 1]
                row = pl.ds(base + b * count + k, 1)
                s_ref[row, :] = s
                out += [a_mul * s + a_swp * w + z_ref[row, :], a_mul * w - a_swp * s + zs_scr[row, :]]
            return tuple(out)
        return lax.fori_loop(0, count, body, carry)

    zero = jnp.zeros((1, lanes), F32)
    carry = segment(BATCH * nl, nc, (zero,) * (2 * BATCH))
    segment(0, nl, carry)


def _s5_y_kernel(u_ref, klag_ref, s_ref, q_ref, dv_ref, y_ref, m_scr):
    d = pl.program_id(1)
    n = N_TOK // S5_CHUNK
    m_scr[...] = jnp.zeros_like(m_scr)
    row = lax.broadcasted_iota(jnp.int32, (128, 128), 0)
    col = lax.broadcasted_iota(jnp.int32, (128, 128), 1)
    same_group = row // S5_GROUP == col // S5_GROUP
    blocks = [jnp.where(same_group, jnp.concatenate([klag_ref[lag]] * S5_GSUB, axis=0), jnp.zeros((128, 128), BF16))
              for lag in range(S5_CHUNK)]
    for rev in range(2):
        @pl.when(d == rev)
        def _():
            for s in range(S5_CHUNK):
                for t in range(S5_CHUNK):
                    lag = (s - t) if rev else (t - s)
                    if lag >= 0:
                        m_scr[s * 128:(s + 1) * 128, t * 128:(t + 1) * 128] = blocks[lag]
    y = jnp.dot(_s5_chunk_rows(u_ref), m_scr[...], preferred_element_type=F32)
    y = y + jnp.dot(s_ref[...].astype(BF16), _s5_expand_q(q_ref[...]), preferred_element_type=F32)
    for t in range(S5_CHUNK):
        rows = pl.ds(t, n, stride=S5_CHUNK)
        yt = y[:, t * 128:(t + 1) * 128]

        @pl.when(d == 0)
        def _():
            y_ref[rows, :] = yt + u_ref[rows, :] * dv_ref[...]

        @pl.when(d == 1)
        def _():
            y_ref[rows, :] = y_ref[rows, :] + yt


def _s5(p, mats, layer, dskip):
    nstate = S5_GROUPS * 2 * S5_STATE
    qlanes = nstate // S5_QUADS
    slanes = 512
    ucol = (C_Q_RANK + C_KV_RANK) // 128
    z = pl.pallas_call(
        _s5_z_kernel,
        out_shape=jax.ShapeDtypeStruct((2, S5_ROWS, nstate), F32),
        grid=(2, S5_QUADS),
        in_specs=[
            pl.BlockSpec((N_TOK, 128), lambda d, q: (0, ucol + q)),
            pl.BlockSpec((None, None, None, S5_CHUNK * 128, 128), lambda d, q: (layer, d, q, 0, 0)),
        ],
        out_specs=pl.BlockSpec((None, S5_ROWS, qlanes), lambda d, q: (d, 0, q)),
        compiler_params=_cparams(("arbitrary", "arbitrary")),
        name="s5_chunk_state",
    )(p, mats["p"])
    s = pl.pallas_call(
        _s5_scan_kernel,
        out_shape=jax.ShapeDtypeStruct((2, S5_ROWS, nstate), F32),
        grid=(2, nstate // slanes),
        in_specs=[
            pl.BlockSpec((None, S5_ROWS, slanes), lambda d, q: (d, 0, q)),
            pl.BlockSpec((None, None, 1, slanes), lambda d, q: (layer, d, 0, q)),
            pl.BlockSpec((None, None, 1, slanes), lambda d, q: (layer, d, 0, q)),
        ],
        out_specs=pl.BlockSpec((None, S5_ROWS, slanes), lambda d, q: (d, 0, q)),
        scratch_shapes=[pltpu.VMEM((S5_ROWS, slanes), F32)],
        compiler_params=_cparams(("arbitrary", "arbitrary")),
        name="s5_scan",
    )(z, mats["a_mul"], mats["a_swp"])
    return pl.pallas_call(
        _s5_y_kernel,
        out_shape=jax.ShapeDtypeStruct((N_TOK, D_WIDTH), F32),
        grid=(S5_QUADS, 2),
        in_specs=[
            pl.BlockSpec((N_TOK, 128), lambda q, d: (0, ucol + q)),
            pl.BlockSpec((None, None, None, S5_CHUNK, S5_GROUP, 128), lambda q, d: (layer, d, q, 0, 0, 0)),
            pl.BlockSpec((None, S5_ROWS, qlanes), lambda q, d: (d, 0, q)),
            pl.BlockSpec((None, None, None, 128, S5_CHUNK * 128), lambda q, d: (layer, d, q, 0, 0)),
            pl.BlockSpec((None, 1, 128), lambda q, d: (layer, 0, q)),
        ],
        out_specs=pl.BlockSpec((N_TOK, 128), lambda q, d: (0, q)),
        scratch_shapes=[pltpu.VMEM((S5_CHUNK * 128, S5_CHUNK * 128), BF16)],
        compiler_params=pltpu.CompilerParams(dimension_semantics=("arbitrary", "arbitrary"),
                                             vmem_limit_bytes=56 * 1024 * 1024),
        name="s5_output",
    )(p, mats["klag"], s, mats["q"], dskip)


def _glu_kernel(y_ref, w_ref, o_ref):
    y = y_ref[...]
    z = y * (0.5 * (1.0 + jnp.tanh(math.sqrt(2.0 / math.pi) * (y + 0.044715 * (y * y * y)))))
    gate = jnp.dot(z.astype(BF16), w_ref[...], preferred_element_type=F32)
    o_ref[...] = (z * jax.nn.sigmoid(gate)).astype(BF16)


def _s5_glu(y, w_glu, layer):
    t = 512
    return pl.pallas_call(
        _glu_kernel,
        out_shape=jax.ShapeDtypeStruct((N_TOK, D_WIDTH), BF16),
        grid=(N_TOK // t,),
        in_specs=[pl.BlockSpec((t, D_WIDTH), lambda i: (i, 0)),
                  pl.BlockSpec((None, D_WIDTH, D_WIDTH), lambda i: (layer, 0, 0))],
        out_specs=pl.BlockSpec((t, D_WIDTH), lambda i: (i, 0)),
        compiler_params=_cparams(("arbitrary",)),
        name="s5_glu",
    )(y, w_glu)


LANE_CHUNKS = D_MODEL // 128


def _store_token_major(ref, val):
    rows = val.shape[0]
    for c in range(LANE_CHUNKS):
        ref[pl.ds(c, rows, stride=LANE_CHUNKS), :] = val[:, c * 128:(c + 1) * 128]


def _load_token_major(ref, rows):
    return jnp.concatenate([ref[pl.ds(c, rows, stride=LANE_CHUNKS), :] for c in range(LANE_CHUNKS)], axis=-1)


def _router_kernel(x_ref, g_ref, sh_ref, sc_ref, whi_ref, wlo_ref, h_ref, info_ref, gw_ref, cnt_ref, h_scr, carry):
    i = pl.program_id(0)

    @pl.when(i == 0)
    def _():
        carry[...] = jnp.zeros_like(carry)

    for r in range(0, TM, 256):
        h = _rms(x_ref[r:r + 256, :]) * g_ref[...]
        h_scr[r:r + 256, :] = h * (1.0 + sc_ref[...]) + sh_ref[...]
    _store_token_major(h_ref, h_scr[...])
    h = h_scr[...]
    h_hi = h.astype(BF16)
    h_lo = (h - h_hi.astype(F32)).astype(BF16)
    logits = (jnp.dot(h_hi, whi_ref[...], preferred_element_type=F32)
              + jnp.dot(h_lo, whi_ref[...], preferred_element_type=F32)
              + jnp.dot(h_hi, wlo_ref[...], preferred_element_type=F32))
    lane = lax.broadcasted_iota(jnp.int32, logits.shape, 1)
    neg = -jnp.inf
    big = jnp.int32(1 << 20)

    def first_argmax(v, vmax):
        return jnp.min(jnp.where(v == vmax, lane, big), axis=-1, keepdims=True)

    lg = jnp.where(lane < N_GROUPS, logits, neg)
    mg = jnp.max(lg, axis=-1, keepdims=True)
    g_w = 1.0 / jnp.sum(jnp.exp(lg - mg), axis=-1, keepdims=True)
    g_idx = first_argmax(lg, mg)
    lo = N_GROUPS + EXPERTS_PER_GROUP * g_idx
    le = jnp.where(jnp.logical_and(lane >= lo, lane < lo + EXPERTS_PER_GROUP), logits, neg)
    m1 = jnp.max(le, axis=-1, keepdims=True)
    i1 = first_argmax(le, m1)
    le2 = jnp.where(lane == i1, neg, le)
    m2 = jnp.max(le2, axis=-1, keepdims=True)
    i2 = first_argmax(le2, m2)
    r21 = jnp.exp(m2 - m1)
    w1 = g_w / (1.0 + r21)
    w2 = g_w * r21 / (1.0 + r21)
    oh = jnp.logical_or(lane == i1, lane == i2)
    ohb = jnp.where(oh, 1.0, 0.0).astype(BF16)
    rr = lax.broadcasted_iota(jnp.int32, (TM, TM), 0)
    cc = lax.broadcasted_iota(jnp.int32, (TM, TM), 1)
    lower = jnp.where(rr > cc, 1.0, 0.0).astype(BF16)
    before = jnp.dot(lower, ohb, preferred_element_type=F32) + carry[...]
    rank1 = jnp.sum(jnp.where(lane == i1, before, 0.0), axis=-1, keepdims=True).astype(jnp.int32)
    rank2 = jnp.sum(jnp.where(lane == i2, before, 0.0), axis=-1, keepdims=True).astype(jnp.int32)
    carry[...] = carry[...] + jnp.sum(ohb.astype(F32), axis=0, keepdims=True)
    cnt_ref[...] = jnp.broadcast_to(carry[...], cnt_ref.shape)
    info = jnp.where(lane == 0, i1 - N_GROUPS, jnp.where(lane == 1, i2 - N_GROUPS,
                     jnp.where(lane == 2, rank1, jnp.where(lane == 3, rank2, 0))))
    info_ref[...] = info
    gw_ref[...] = jnp.where(lane == 0, w1, jnp.where(lane == 1, w2, 0.0))


def _router(x, g, mod, w_router, layer):
    return pl.pallas_call(
        _router_kernel,
        out_shape=[jax.ShapeDtypeStruct((N_TOK * LANE_CHUNKS, 128), F32),
                   jax.ShapeDtypeStruct((N_TOK, 128), jnp.int32),
                   jax.ShapeDtypeStruct((N_TOK, 128), F32),
                   jax.ShapeDtypeStruct((8, 128), F32)],
        grid=(N_TOK // TM,),
        in_specs=[
            pl.BlockSpec((TM, D_MODEL), lambda i: (i, 0)),
            pl.BlockSpec((1, D_MODEL), lambda i: (0, 0)),
            pl.BlockSpec((None, None, 1, D_MODEL), lambda i: (_mod_row(i, TM), 3, 0, 0)),
            pl.BlockSpec((None, None, 1, D_MODEL), lambda i: (_mod_row(i, TM), 4, 0, 0)),
            pl.BlockSpec((None, D_MODEL, 128), lambda i: (layer, 0, 0)),
            pl.BlockSpec((None, D_MODEL, 128), lambda i: (layer, 0, 0)),
        ],
        out_specs=[pl.BlockSpec((TM * LANE_CHUNKS, 128), lambda i: (i, 0)),
                   pl.BlockSpec((TM, 128), lambda i: (i, 0)),
                   pl.BlockSpec((TM, 128), lambda i: (i, 0)),
                   pl.BlockSpec((8, 128), lambda i: (0, 0))],
        scratch_shapes=[pltpu.VMEM((TM, D_MODEL), F32), pltpu.VMEM((1, 128), F32)],
        compiler_params=_cparams(("arbitrary",)),
        name="moe_router",
    )(x, g, mod, mod, *w_router)


FFN_ISSUE_GROUPS = 8


def _ffn_kernel(pos_ref, te_ref, meta_ref, h_hbm, wg_ref, wu_ref, wd_ref, o_ref,
                src, xbuf, wg_b, wu_b, wd_b, sem):
    t = pl.program_id(0)
    nt = meta_ref[0]

    def row_copy(tile, slot, r):
        tok = src[tile * TE + r]
        return pltpu.make_async_copy(
            h_hbm.at[pl.ds(pl.multiple_of(tok * LANE_CHUNKS, LANE_CHUNKS), LANE_CHUNKS), :],
            xbuf.at[slot, pl.ds(pl.multiple_of(r * LANE_CHUNKS, LANE_CHUNKS), LANE_CHUNKS), :],
            sem.at[slot])

    def gather(tile, slot):
        def body(r, _):
            row_copy(tile, slot, r).start()
            return 0
        lax.fori_loop(0, TE, body, 0, unroll=8)

    def wait_tile(slot):
        pltpu.make_async_copy(xbuf.at[slot], xbuf.at[slot], sem.at[slot]).wait()

    @pl.when(t == 0)
    def _():
        for e in range(N_EXPERTS):
            def clear(i, _):
                src[i] = 0
                return 0
            lax.fori_loop(meta_ref[1 + e], meta_ref[1 + N_EXPERTS + e], clear, 0)

        def fill(tok, _):
            src[pos_ref[2 * tok]] = tok
            src[pos_ref[2 * tok + 1]] = tok
            return 0
        lax.fori_loop(0, N_TOK, fill, 0, unroll=8)
        gather(0, 0)

    @pl.when(t >= nt)
    def _():
        o_ref[...] = jnp.zeros_like(o_ref)

    @pl.when(t < nt)
    def _():
        slot = t % 2
        wait_tile(slot)

        @pl.when(jnp.logical_or(t == 0, te_ref[t] != te_ref[jnp.maximum(t - 1, 0)]))
        def _():
            wg_b[...] = wg_ref[...].astype(BF16)
            wu_b[...] = wu_ref[...].astype(BF16)
            wd_b[...] = wd_ref[...].astype(BF16)

        nxt = jnp.minimum(t + 1, nt - 1)
        per = TE // FFN_ISSUE_GROUPS

        def issue(g):
            for r in range(g * per, (g + 1) * per):
                row_copy(nxt, 1 - slot, r).start()

        x = _load_token_major(xbuf.at[slot], TE).astype(BF16)
        half = D_EXPERT // 2
        hg, hu = [], []
        for n in range(2):
            issue(n)
            hg.append(jnp.dot(x, wg_b[:, n * half:(n + 1) * half], preferred_element_type=F32))
        for n in range(2):
            issue(2 + n)
            hu.append(jnp.dot(x, wu_b[:, n * half:(n + 1) * half], preferred_element_type=F32))
        hg, hu = jnp.concatenate(hg, axis=1), jnp.concatenate(hu, axis=1)
        act = (hg * jax.nn.sigmoid(hg) * hu).astype(BF16)
        quarter = D_MODEL // 4
        for n in range(4):
            issue(4 + n)
            y = jnp.dot(act, wd_b[:, n * quarter:(n + 1) * quarter], preferred_element_type=F32)
            for c in range(quarter // 128):
                o_ref[pl.ds(n * (quarter // 128) + c, TE, stride=LANE_CHUNKS), :] = y[:, c * 128:(c + 1) * 128]

        @pl.when(t == nt - 1)
        def _():
            wait_tile(1 - slot)


def _expert_ffn(pos_flat, tile_expert, meta, h, w_gate, w_up, w_down, layer):
    wsel = lambda t, pos, te, meta: (layer, te[t], 0, 0)
    return pl.pallas_call(
        _ffn_kernel,
        out_shape=jax.ShapeDtypeStruct((N_SORT * LANE_CHUNKS, 128), F32),
        grid_spec=pltpu.PrefetchScalarGridSpec(
            num_scalar_prefetch=3,
            grid=(N_ETILES,),
            in_specs=[
                pl.BlockSpec(memory_space=pl.ANY),
                pl.BlockSpec((None, None, D_MODEL, D_EXPERT), wsel),
                pl.BlockSpec((None, None, D_MODEL, D_EXPERT), wsel),
                pl.BlockSpec((None, None, D_EXPERT, D_MODEL), wsel),
            ],
            out_specs=pl.BlockSpec((TE * LANE_CHUNKS, 128), lambda t, pos, te, meta: (t, 0)),
            scratch_shapes=[
                pltpu.SMEM((N_SORT,), jnp.int32),
                pltpu.VMEM((2, TE * LANE_CHUNKS, 128), F32),
                pltpu.VMEM((D_MODEL, D_EXPERT), BF16),
                pltpu.VMEM((D_MODEL, D_EXPERT), BF16),
                pltpu.VMEM((D_EXPERT, D_MODEL), BF16),
                pltpu.SemaphoreType.DMA((2,)),
            ],
        ),
        compiler_params=pltpu.CompilerParams(dimension_semantics=("arbitrary",),
                                             vmem_limit_bytes=56 * 1024 * 1024),
        name="moe_expert_ffn",
    )(pos_flat, tile_expert, meta, h, w_gate, w_up, w_down)


TC = 256


def _combine_kernel(pos_ref, x_ref, gate_ref, gw_ref, fn_ref, ys_hbm, o_ref, buf_a, buf_b, sem, *, final_norm):
    i = pl.program_id(0)
    slot = i % 2

    def slab(ref, row):
        return ref.at[pl.ds(pl.multiple_of(row * LANE_CHUNKS, LANE_CHUNKS), LANE_CHUNKS), :]

    def gather(tile, slot):
        def body(r, _):
            tok = tile * TC + r
            pltpu.make_async_copy(slab(ys_hbm, pos_ref[2 * tok]), slab(buf_a.at[slot], r), sem.at[slot, 0]).start()
            pltpu.make_async_copy(slab(ys_hbm, pos_ref[2 * tok + 1]), slab(buf_b.at[slot], r), sem.at[slot, 1]).start()
            return 0
        lax.fori_loop(0, TC, body, 0, unroll=8)

    @pl.when(i == 0)
    def _():
        gather(0, 0)

    @pl.when(i + 1 < pl.num_programs(0))
    def _():
        gather(i + 1, 1 - slot)

    pltpu.make_async_copy(buf_a.at[slot], buf_a.at[slot], sem.at[slot, 0]).wait()
    pltpu.make_async_copy(buf_b.at[slot], buf_b.at[slot], sem.at[slot, 1]).wait()
    w0 = gw_ref[:, 0:1]
    w1 = gw_ref[:, 1:2]
    y = x_ref[...] + gate_ref[...] * (w0 * _load_token_major(buf_a.at[slot], TC)
                                      + w1 * _load_token_major(buf_b.at[slot], TC))
    if final_norm:
        y = _rms(y) * fn_ref[...]
    o_ref[...] = y


def _combine(pos_flat, x, mod, gw, ys, final_g):
    final_norm = final_g is not None
    fn = final_g if final_norm else jnp.ones((1, D_MODEL), F32)
    rows = N_LAT if final_norm else N_TOK
    return pl.pallas_call(
        functools.partial(_combine_kernel, final_norm=final_norm),
        out_shape=jax.ShapeDtypeStruct((rows, D_MODEL), F32),
        grid_spec=pltpu.PrefetchScalarGridSpec(
            num_scalar_prefetch=1,
            grid=(rows // TC,),
            in_specs=[
                pl.BlockSpec((TC, D_MODEL), lambda i, pos: (i, 0)),
                pl.BlockSpec((None, None, 1, D_MODEL), lambda i, pos: (_mod_row(i, TC), 5, 0, 0)),
                pl.BlockSpec((TC, 128), lambda i, pos: (i, 0)),
                pl.BlockSpec((1, D_MODEL), lambda i, pos: (0, 0)),
                pl.BlockSpec(memory_space=pl.ANY),
            ],
            out_specs=pl.BlockSpec((TC, D_MODEL), lambda i, pos: (i, 0)),
            scratch_shapes=[pltpu.VMEM((2, TC * LANE_CHUNKS, 128), F32), pltpu.VMEM((2, TC * LANE_CHUNKS, 128), F32),
                            pltpu.SemaphoreType.DMA((2, 2))],
        ),
        compiler_params=_cparams(("arbitrary",)),
        name="moe_combine",
    )(pos_flat, x, mod, gw, fn, ys)


def _moe(x, g, mod, w_router, w_gate, w_up, w_down, layer, final_g):
    h, info, gw, cnt = _router(x, g, mod, w_router, layer)
    counts = cnt[0, N_GROUPS:N_GROUPS + N_EXPERTS].astype(jnp.int32)
    padded = ((counts + TE - 1) // TE) * TE
    ends = jnp.cumsum(padded)
    starts = ends - padded
    experts = jnp.arange(N_EXPERTS, dtype=jnp.int32)
    start_of = jnp.sum(jnp.where(info[:, 0:2, None] == experts, starts, 0), axis=-1)
    pos_flat = (start_of + info[:, 2:4]).reshape(-1)
    tile_ends = ends // TE
    num_tiles = tile_ends[-1]
    tiles = jnp.minimum(jnp.arange(N_ETILES, dtype=jnp.int32), num_tiles - 1)
    tile_expert = jnp.sum((tile_ends[None, :] <= tiles[:, None]).astype(jnp.int32), axis=-1)
    meta = jnp.concatenate([num_tiles[None], starts + counts, ends]).astype(jnp.int32)
    ys = _expert_ffn(pos_flat, tile_expert, meta, h, w_gate, w_up, w_down, layer)
    return _combine(pos_flat, x, mod, gw, ys, final_g)


def kernel(x, c, ctx, c_ctx, mod_w, mod_b, norm_mix, norm_ffn, ab_w_in, ab_q_norm, ab_k_norm, ab_conv_w, ab_w_out, cd_w_in, cd_q_norm, cd_kv_norm, cd_w_uq, cd_w_ukv, s5_lam_re, s5_lam_im, s5_log_dt, s5_b_re, s5_b_im, s5_c_re, s5_c_im, s5_d, s5_w_glu, cd_w_out, moe_w_group, moe_w_expert, moe_w_gate, moe_w_up, moe_w_down, final_norm):
    cc = jnp.concatenate([c, c_ctx[None, :], jnp.zeros((8 - BATCH - 1, D_MODEL), F32)], axis=0)
    mods = _modulation(cc, mod_w, mod_b).reshape(DEPTH, 8, N_MOD, 1, D_MODEL)
    xs = jnp.concatenate([x.reshape(N_LAT, D_MODEL), ctx.reshape(N_CTX, D_MODEL)], axis=0)
    cos_a, sin_a = _rope_tables(HEAD_DIM)
    cos_c, sin_c = _rope_tables(C_ROPE)

    ab_in_b, ab_out_b = ab_w_in.astype(BF16), ab_w_out.astype(BF16)
    cd_out_b, ukv_b, glu_b = cd_w_out.astype(BF16), cd_w_ukv.astype(BF16), s5_w_glu.astype(BF16)
    a, b_ = C_Q_RANK + C_KV_RANK, C_Q_RANK + C_KV_RANK + C_ROPE
    pad = jnp.zeros(cd_w_in.shape[:2] + (CD_IN_PAD - cd_w_in.shape[2],), F32)
    cd_in_b = jnp.concatenate([cd_w_in[..., :a], cd_w_in[..., b_:], cd_w_in[..., a:b_], pad], axis=-1).astype(BF16)
    w_uq = cd_w_uq.reshape(-1, C_Q_RANK, C_HEADS, C_NOPE + C_ROPE)
    uq_b = jnp.concatenate([w_uq[..., :C_NOPE].reshape(-1, C_Q_RANK, C_HEADS * C_NOPE),
                            w_uq[..., C_NOPE:].reshape(-1, C_Q_RANK, C_HEADS * C_ROPE)], axis=-1).astype(BF16)

    mats = jax.vmap(_s5_matrices)(s5_lam_re, s5_lam_im, s5_log_dt, s5_b_re, s5_b_im, s5_c_re, s5_c_im)
    w_router = jnp.concatenate(
        [moe_w_group, jnp.transpose(moe_w_expert, (0, 2, 1, 3)).reshape(DEPTH, D_MODEL, N_EXPERTS),
         jnp.zeros((DEPTH, D_MODEL, 128 - N_GROUPS - N_EXPERTS), F32)], axis=-1)
    w_router_hi = w_router.astype(BF16)
    w_router = (w_router_hi, (w_router - w_router_hi.astype(F32)).astype(BF16))

    for i in range(DEPTH):
        j = i // 2
        mod = mods[i]
        if i % 2 == 0:
            p = _norm_linear(xs, 0, D_MODEL, norm_mix[i][None, :], mod, 0, ab_in_b, j, 2048)
            k, vext = _gqa_kv(p, ab_k_norm[j][None, :], cos_a, sin_a)
            o_lat, o_ctx = _gqa_attention(p, k, vext, ab_q_norm[j][None, :], cos_a, sin_a)
            side = _gated_conv(p, ab_conv_w[j])
            w_out = ab_out_b
        else:
            p = _norm_linear(xs, 0, D_MODEL, norm_mix[i][None, :], mod, 0, cd_in_b, j, CD_IN_PAD)
            q = _norm_linear(p, 0, C_Q_RANK, cd_q_norm[j][None, :], None, 0, uq_b, j, uq_b.shape[2])
            kvu = _norm_linear(p, C_Q_RANK // C_KV_RANK, C_KV_RANK, cd_kv_norm[j][None, :], None, 0,
                               ukv_b, j, ukv_b.shape[2])
            kcat, vext = _mla_kv(kvu, p, cos_c, sin_c)
            o_lat, o_ctx = _mla_attention(q, kcat, vext, cos_c, sin_c)
            y = _s5(p, mats, j, s5_d.astype(F32)[:, None, :])
            side = _s5_glu(y, glu_b, j)
            w_out = cd_out_b
        xs = _out_linear(o_lat, o_ctx, side, w_out, j, xs, mod, 2)
        xs = _moe(xs, norm_ffn[i][None, :], mod, w_router, moe_w_gate, moe_w_up, moe_w_down, i,
                  final_norm[None, :] if i == DEPTH - 1 else None)
    return xs.reshape(BATCH, SEQ, D_MODEL)
```

```python
import functools
import math

import numpy as np
import jax
import jax.numpy as jnp
from jax import lax
from jax.experimental import pallas as pl
from jax.experimental.pallas import tpu as pltpu

F32 = jnp.float32
BF16 = jnp.bfloat16

D_MODEL = 2048
BATCH = 4
SEQ = 2048
DEPTH = 4
GRID_W = 64
CTX_LEN = 256
ROPE_THETA = 10000.0
EPS = 1e-6
N_MOD = 6
HEAD_DIM = 128
A_Q_HEADS = 12
A_KV_HEADS = 4
A_GROUP = A_Q_HEADS // A_KV_HEADS
A_WIDTH = A_Q_HEADS * HEAD_DIM
A_KV_WIDTH = A_KV_HEADS * HEAD_DIM
B_WIDTH = 512
AB_IN = A_WIDTH + 2 * A_KV_WIDTH + 3 * B_WIDTH
C_HEADS = 12
C_NOPE = 128
C_ROPE = 64
C_V = 128
C_Q_RANK = 512
C_KV_RANK = 256
C_WIDTH = C_HEADS * C_V
D_WIDTH = 512
S5_GROUP = 16
S5_GROUPS = D_WIDTH // S5_GROUP
S5_STATE = 64
N_GROUPS = 4
EXPERTS_PER_GROUP = 4
N_EXPERTS = N_GROUPS * EXPERTS_PER_GROUP
D_EXPERT = 512

N_LAT = BATCH * SEQ
N_CTX = BATCH * CTX_LEN
N_TOK = N_LAT + N_CTX
CTX_ROW = BATCH
CD_IN_PAD = 1536

TM = 512
TQ = 512
TE = 256
N_SORT = 2 * N_TOK + N_EXPERTS * TE
N_ETILES = N_SORT // TE
S5_CHUNK = 16
S5_CTX_CHUNKS = CTX_LEN // S5_CHUNK
S5_ROWS = N_TOK // S5_CHUNK
S5_GSUB = 128 // S5_GROUP
S5_QUADS = S5_GROUPS // S5_GSUB
VMEM_LIMIT = 48 * 1024 * 1024


def _cparams(sem):
    return pltpu.CompilerParams(dimension_semantics=sem, vmem_limit_bytes=VMEM_LIMIT)


def _mod_row(tile, tile_rows):
    r0 = tile * tile_rows
    return jnp.where(r0 >= N_LAT, CTX_ROW, r0 // SEQ)


def _rope_block(tile, tile_rows):
    r0 = tile * tile_rows
    return jnp.where(r0 >= N_LAT, SEQ // tile_rows, (r0 % SEQ) // tile_rows)


def _rms(x):
    return x * lax.rsqrt(jnp.mean(x * x, axis=-1, keepdims=True) + EPS)


def _mod_kernel(cc_ref, w_ref, b_ref, o_ref):
    cc = cc_ref[...]
    s = (cc * jax.nn.sigmoid(cc)).astype(BF16)
    o_ref[...] = jnp.dot(s, w_ref[...].astype(BF16), preferred_element_type=F32) + b_ref[...]


def _modulation(cc, mod_w, mod_b):
    tn = 1024
    nout = N_MOD * D_MODEL
    return pl.pallas_call(
        _mod_kernel,
        out_shape=jax.ShapeDtypeStruct((DEPTH, 8, nout), F32),
        grid=(DEPTH, nout // tn),
        in_specs=[
            pl.BlockSpec((8, D_MODEL), lambda l, j: (0, 0)),
            pl.BlockSpec((None, D_MODEL, tn), lambda l, j: (l, 0, j)),
            pl.BlockSpec((None, 1, tn), lambda l, j: (l, 0, j)),
        ],
        out_specs=pl.BlockSpec((None, 8, tn), lambda l, j: (l, 0, j)),
        compiler_params=_cparams(("arbitrary", "arbitrary")),
        name="modulation",
    )(cc, mod_w, mod_b.reshape(DEPTH, 1, nout))


def _norm_linear_kernel(x_ref, g_ref, sh_ref, sc_ref, w_ref, o_ref, h_scr, *, modulate, rows):
    @pl.when(pl.program_id(1) == 0)
    def _():
        for r in range(0, rows, 256):
            h = _rms(x_ref[r:r + 256, :]) * g_ref[...]
            if modulate:
                h = h * (1.0 + sc_ref[...]) + sh_ref[...]
            h_scr[r:r + 256, :] = h.astype(BF16)

    o_ref[...] = jnp.dot(h_scr[...], w_ref[...], preferred_element_type=F32)


def _norm_linear(x, xcol, kdim, g, mod, which, w, layer, tn):
    nout = w.shape[2]
    modulate = mod is not None
    if modulate:
        sh_spec = pl.BlockSpec((None, None, 1, kdim), lambda i, j: (_mod_row(i, TM), which, 0, 0))
        sc_spec = pl.BlockSpec((None, None, 1, kdim), lambda i, j: (_mod_row(i, TM), which + 1, 0, 0))
        sh = sc = mod
    else:
        sh_spec = sc_spec = pl.BlockSpec((1, kdim), lambda i, j: (0, 0))
        sh = sc = g
    return pl.pallas_call(
        functools.partial(_norm_linear_kernel, modulate=modulate, rows=TM),
        out_shape=jax.ShapeDtypeStruct((N_TOK, nout), F32),
        grid=(N_TOK // TM, nout // tn),
        in_specs=[
            pl.BlockSpec((TM, kdim), lambda i, j: (i, xcol)),
            pl.BlockSpec((1, kdim), lambda i, j: (0, 0)),
            sh_spec, sc_spec,
            pl.BlockSpec((None, kdim, tn), lambda i, j: (layer, 0, j)),
        ],
        out_specs=pl.BlockSpec((TM, tn), lambda i, j: (i, j)),
        scratch_shapes=[pltpu.VMEM((TM, kdim), BF16)],
        compiler_params=_cparams(("arbitrary", "arbitrary")),
        name="norm_linear",
    )(x, g, sh, sc, w)


def _out_linear_kernel(a1l_ref, a1c_ref, a2_ref, w1_ref, w2_ref, x_ref, gate_ref, o_ref):
    a1 = jnp.where(pl.program_id(0) >= N_LAT // TM, a1c_ref[...], a1l_ref[...])
    acc = jnp.dot(a1, w1_ref[...], preferred_element_type=F32)
    acc = acc + jnp.dot(a2_ref[...], w2_ref[...], preferred_element_type=F32)
    o_ref[...] = x_ref[...] + gate_ref[...] * acc


def _out_linear(a1_lat, a1_ctx, a2, w, layer, x, mod, which):
    tn = D_MODEL
    k1, k2 = a1_lat.shape[1], a2.shape[1]
    lat_tiles = N_LAT // TM
    return pl.pallas_call(
        _out_linear_kernel,
        out_shape=jax.ShapeDtypeStruct((N_TOK, D_MODEL), F32),
        grid=(N_TOK // TM, D_MODEL // tn),
        in_specs=[
            pl.BlockSpec((TM, k1), lambda i, j: (jnp.minimum(i, lat_tiles - 1), 0)),
            pl.BlockSpec((TM, k1), lambda i, j: (jnp.maximum(i - lat_tiles, 0), 0)),
            pl.BlockSpec((TM, k2), lambda i, j: (i, 0)),
            pl.BlockSpec((None, k1, tn), lambda i, j: (layer, 0, j)),
            pl.BlockSpec((None, k2, tn), lambda i, j: (layer, k1 // k2, j)),
            pl.BlockSpec((TM, tn), lambda i, j: (i, j)),
            pl.BlockSpec((None, None, 1, tn), lambda i, j: (_mod_row(i, TM), which, 0, j)),
        ],
        out_specs=pl.BlockSpec((TM, tn), lambda i, j: (i, j)),
        compiler_params=_cparams(("arbitrary", "arbitrary")),
        name="out_linear",
    )(a1_lat, a1_ctx, a2, w, w, x, mod)


def _rope_tables(rot_dim):
    rows = SEQ // GRID_W
    row_ids = np.repeat(np.arange(rows, dtype=np.float32), GRID_W)
    col_ids = np.tile(np.arange(GRID_W, dtype=np.float32), rows)
    d_axis = rot_dim // 2
    inv = (np.float32(ROPE_THETA) ** (-np.arange(0, d_axis, 2, dtype=np.float32) / np.float32(d_axis))).astype(np.float32)
    ang = np.concatenate([row_ids[:, None] * inv, col_ids[:, None] * inv], axis=-1).astype(np.float32)
    cos, sin = np.cos(ang).astype(np.float32), np.sin(ang).astype(np.float32)
    reps = 128 // rot_dim
    cos_f = np.tile(np.concatenate([cos, cos], axis=-1), (1, reps))
    sin_f = np.tile(np.concatenate([-sin, sin], axis=-1), (1, reps))
    cos_f = np.concatenate([cos_f, np.ones((512, 128), np.float32)], axis=0)
    sin_f = np.concatenate([sin_f, np.zeros((512, 128), np.float32)], axis=0)
    return jnp.asarray(cos_f), jnp.asarray(sin_f)


def _rope128(x, cos, sin):
    return x * cos + pltpu.roll(x, 64, 1) * sin


def _rope64(x, cos, sin):
    lane = lax.broadcasted_iota(jnp.int32, x.shape, 1)
    swapped = jnp.where((lane % 64) < 32, pltpu.roll(x, 96, 1), pltpu.roll(x, 32, 1))
    return x * cos + swapped * sin


def _ones_column(rows):
    lane = lax.broadcasted_iota(jnp.int32, (rows, 128), 1)
    return jnp.where(lane == 0, 1.0, 0.0).astype(BF16)


def _gqa_kv_kernel(k_ref, v_ref, kn_ref, cos_ref, sin_ref, ko_ref, vo_ref):
    cos, sin = cos_ref[...], sin_ref[...]
    ones = _ones_column(k_ref.shape[0])
    for h in range(A_KV_HEADS):
        k = _rms(k_ref[:, h * 128:(h + 1) * 128]) * kn_ref[...]
        ko_ref[:, h * 128:(h + 1) * 128] = _rope128(k, cos, sin).astype(BF16)
        vo_ref[:, h * 256:h * 256 + 128] = v_ref[:, h * 128:(h + 1) * 128].astype(BF16)
        vo_ref[:, h * 256 + 128:(h + 1) * 256] = ones


def _gqa_kv(p, k_norm, cos, sin):
    t = 512
    return pl.pallas_call(
        _gqa_kv_kernel,
        out_shape=[jax.ShapeDtypeStruct((N_TOK, A_KV_WIDTH), BF16),
                   jax.ShapeDtypeStruct((N_TOK, 2 * A_KV_WIDTH), BF16)],
        grid=(N_TOK // t,),
        in_specs=[
            pl.BlockSpec((t, A_KV_WIDTH), lambda i: (i, A_WIDTH // A_KV_WIDTH)),
            pl.BlockSpec((t, A_KV_WIDTH), lambda i: (i, A_WIDTH // A_KV_WIDTH + 1)),
            pl.BlockSpec((1, 128), lambda i: (0, 0)),
            pl.BlockSpec((t, 128), lambda i: (_rope_block(i, t), 0)),
            pl.BlockSpec((t, 128), lambda i: (_rope_block(i, t), 0)),
        ],
        out_specs=[pl.BlockSpec((t, A_KV_WIDTH), lambda i: (i, 0)),
                   pl.BlockSpec((t, 2 * A_KV_WIDTH), lambda i: (i, 0))],
        compiler_params=_cparams(("arbitrary",)),
        name="gqa_kv",
    )(p, p, k_norm, cos, sin)


LOG2E = math.log2(math.e)
KEY_PIECE = 1024


def _softmax_pv(q, keys, vals):
    dn = (((1,), (1,)), ((), ()))
    m = acc = None
    for k, v in zip(keys, vals):
        s = lax.dot_general(q, k, dn, preferred_element_type=F32)
        m_piece = jnp.max(s, axis=-1, keepdims=True)
        if m is None:
            m_new = m_piece
        else:
            m_new = jnp.maximum(m, m_piece)
            acc = acc * jnp.exp2(m - m_new)
        part = jnp.dot(jnp.exp2(s - m_new).astype(BF16), v, preferred_element_type=F32)
        acc = part if acc is None else acc + part
        m = m_new
    return acc[:, :128] / acc[:, 128:129]


def _attn_qrow(b, t):
    nq = SEQ // TQ
    return jnp.where(t == nq, (N_LAT + b * CTX_LEN) // TQ, b * nq + t)


def _ctx_half():
    return pl.multiple_of((pl.program_id(0) % (TQ // CTX_LEN)) * CTX_LEN, CTX_LEN)


def _gqa_attn_kernel(q_ref, qn_ref, cos_ref, sin_ref, kc_ref, vc_ref, kl_ref, vl_ref, ol_ref, oc_ref):
    scale = HEAD_DIM ** -0.5 * LOG2E
    is_ctx = pl.program_id(2) == SEQ // TQ

    def query(rows, g):
        q = _rms(q_ref[rows, g * 128:(g + 1) * 128]) * qn_ref[...]
        return (_rope128(q, cos_ref[rows, :], sin_ref[rows, :]) * scale).astype(BF16)

    @pl.when(jnp.logical_not(is_ctx))
    def _():
        pieces = [slice(j * KEY_PIECE, (j + 1) * KEY_PIECE) for j in range(SEQ // KEY_PIECE)]
        qs = [query(slice(None), g) for g in range(A_GROUP)]
        for g in range(A_GROUP):
            o = _softmax_pv(qs[g], [kc_ref[...]] + [kl_ref[r, :] for r in pieces],
                            [vc_ref[...]] + [vl_ref[r, :] for r in pieces])
            ol_ref[:, g * 128:(g + 1) * 128] = o.astype(BF16)

    @pl.when(is_ctx)
    def _():
        rows = pl.ds(_ctx_half(), CTX_LEN)
        for g in range(A_GROUP):
            oc_ref[:, g * 128:(g + 1) * 128] = _softmax_pv(query(rows, g), [kc_ref[...]], [vc_ref[...]]).astype(BF16)


def _gqa_attention(p, k, vext, q_norm, cos, sin):
    qw = A_GROUP * HEAD_DIM
    nq = SEQ // TQ
    ctx_blk = N_LAT // CTX_LEN
    rope = pl.BlockSpec((TQ, 128), lambda b, h, t: (_rope_block(_attn_qrow(b, t), TQ), 0))
    return pl.pallas_call(
        _gqa_attn_kernel,
        out_shape=[jax.ShapeDtypeStruct((N_LAT, A_WIDTH), BF16), jax.ShapeDtypeStruct((N_CTX, A_WIDTH), BF16)],
        grid=(BATCH, A_KV_HEADS, nq + 1),
        in_specs=[
            pl.BlockSpec((TQ, qw), lambda b, h, t: (_attn_qrow(b, t), h)),
            pl.BlockSpec((1, 128), lambda b, h, t: (0, 0)),
            rope, rope,
            pl.BlockSpec((CTX_LEN, 128), lambda b, h, t: (ctx_blk + b, h)),
            pl.BlockSpec((CTX_LEN, 256), lambda b, h, t: (ctx_blk + b, h)),
            pl.BlockSpec((SEQ, 128), lambda b, h, t: (b, h)),
            pl.BlockSpec((SEQ, 256), lambda b, h, t: (b, h)),
        ],
        out_specs=[pl.BlockSpec((TQ, qw), lambda b, h, t: (b * nq + jnp.minimum(t, nq - 1), h)),
                   pl.BlockSpec((CTX_LEN, qw), lambda b, h, t: (b, h))],
        compiler_params=_cparams(("arbitrary",) * 3),
        name="gqa_attn",
    )(p, q_norm, cos, sin, k, vext, k, vext)


def _conv_kernel(bg_ref, cg_ref, ug_ref, cgp_ref, ugp_ref, cgn_ref, ugn_ref, w_ref, o_ref, *, rows):
    i = pl.program_id(0)
    per_seq = SEQ // rows
    is_ctx = i >= N_LAT // rows
    is_start = jnp.logical_or(is_ctx, i % per_seq == 0)
    is_end = jnp.logical_or(is_ctx, i % per_seq == per_seq - 1)
    m = cg_ref[...] * ug_ref[...]
    m_prev = jnp.where(is_start, 0.0, cgp_ref[7:8, :] * ugp_ref[7:8, :])
    m_next = jnp.where(is_end, 0.0, cgn_ref[0:1, :] * ugn_ref[0:1, :])
    row = lax.broadcasted_iota(jnp.int32, m.shape, 0)
    down = jnp.where(row == 0, m_prev, pltpu.roll(m, 1, 0))
    up = jnp.where(row == rows - 1, m_next, pltpu.roll(m, rows - 1, 0))
    conv = down * w_ref[0:1, :] + m * w_ref[1:2, :] + up * w_ref[2:3, :]
    o_ref[...] = (bg_ref[...] * conv).astype(BF16)


def _gated_conv(p, conv_w):
    rows = CTX_LEN
    base = (A_WIDTH + 2 * A_KV_WIDTH) // B_WIDTH
    halo = rows // 8
    last = N_TOK // 8 - 1
    main = lambda c: pl.BlockSpec((rows, B_WIDTH), lambda i: (i, base + c))
    prev = lambda c: pl.BlockSpec((8, B_WIDTH), lambda i: (jnp.maximum(i * halo - 1, 0), base + c))
    nxt = lambda c: pl.BlockSpec((8, B_WIDTH), lambda i: (jnp.minimum((i + 1) * halo, last), base + c))
    return pl.pallas_call(
        functools.partial(_conv_kernel, rows=rows),
        out_shape=jax.ShapeDtypeStruct((N_TOK, B_WIDTH), BF16),
        grid=(N_TOK // rows,),
        in_specs=[main(0), main(1), main(2), prev(1), prev(2), nxt(1), nxt(2),
                  pl.BlockSpec((3, B_WIDTH), lambda i: (0, 0))],
        out_specs=pl.BlockSpec((rows, B_WIDTH), lambda i: (i, 0)),
        compiler_params=_cparams(("arbitrary",)),
        name="gated_conv",
    )(p, p, p, p, p, p, p, conv_w)


def _mla_kv_kernel(ckv_ref, g_ref, w_ref, kr_ref, cos_ref, sin_ref, ko_ref, vo_ref):
    ckv = (_rms(ckv_ref[...]) * g_ref[...]).astype(BF16)
    kv = jnp.dot(ckv, w_ref[...], preferred_element_type=F32)
    x = kr_ref[...]
    lane = lax.broadcasted_iota(jnp.int32, x.shape, 1)
    r = jnp.where(lane < C_ROPE, _rope64(x, cos_ref[...], sin_ref[...]), 0.0)
    kr_low = r.astype(BF16)
    kr_high = pltpu.roll(r, 64, 1).astype(BF16)
    ones = _ones_column(x.shape[0])
    for h in range(C_HEADS):
        ko_ref[:, h * 256:h * 256 + 128] = kv[:, h * 256:h * 256 + 128].astype(BF16)
        ko_ref[:, h * 256 + 128:(h + 1) * 256] = kr_low if h % 2 == 0 else kr_high
        vo_ref[:, h * 256:h * 256 + 128] = kv[:, h * 256 + 128:(h + 1) * 256].astype(BF16)
        vo_ref[:, h * 256 + 128:(h + 1) * 256] = ones


def _mla_kv(p, kv_norm, w_ukv, layer, cos, sin):
    t = 512
    col = (C_Q_RANK + C_KV_RANK + D_WIDTH) // 128
    width = C_HEADS * 256
    return pl.pallas_call(
        _mla_kv_kernel,
        out_shape=[jax.ShapeDtypeStruct((N_TOK, width), BF16)] * 2,
        grid=(N_TOK // t,),
        in_specs=[
            pl.BlockSpec((t, C_KV_RANK), lambda i: (i, C_Q_RANK // C_KV_RANK)),
            pl.BlockSpec((1, C_KV_RANK), lambda i: (0, 0)),
            pl.BlockSpec((None, C_KV_RANK, width), lambda i: (layer, 0, 0)),
            pl.BlockSpec((t, 128), lambda i: (i, col)),
            pl.BlockSpec((t, 128), lambda i: (_rope_block(i, t), 0)),
            pl.BlockSpec((t, 128), lambda i: (_rope_block(i, t), 0)),
        ],
        out_specs=[pl.BlockSpec((t, width), lambda i: (i, 0))] * 2,
        compiler_params=_cparams(("arbitrary",)),
        name="mla_kv",
    )(p, kv_norm, w_ukv, p, cos, sin)


def _mla_attn_kernel(qn_ref, qr_ref, cos_ref, sin_ref, kc_ref, vc_ref, kl_ref, vl_ref, ol_ref, oc_ref):
    scale = (C_NOPE + C_ROPE) ** -0.5 * LOG2E
    is_ctx = pl.program_id(2) == SEQ // TQ

    def queries(rows):
        qr = _rope64(qr_ref[rows, :], cos_ref[rows, :], sin_ref[rows, :])
        lane = lax.broadcasted_iota(jnp.int32, qr.shape, 1)
        out = []
        for hh in range(2):
            sel = (lane < 64) if hh == 0 else (lane >= 64)
            q = jnp.concatenate([qn_ref[rows, hh * 128:(hh + 1) * 128], jnp.where(sel, qr, 0.0)], axis=1)
            out.append((q * scale).astype(BF16))
        return out

    @pl.when(jnp.logical_not(is_ctx))
    def _():
        for hh, q in enumerate(queries(slice(None))):
            blk = slice(hh * 256, (hh + 1) * 256)
            pieces = [slice(j * KEY_PIECE, (j + 1) * KEY_PIECE) for j in range(SEQ // KEY_PIECE)]
            o = _softmax_pv(q, [kc_ref[:, blk]] + [kl_ref[r, blk] for r in pieces],
                            [vc_ref[:, blk]] + [vl_ref[r, blk] for r in pieces])
            ol_ref[:, hh * 128:(hh + 1) * 128] = o.astype(BF16)

    @pl.when(is_ctx)
    def _():
        for hh, q in enumerate(queries(pl.ds(_ctx_half(), CTX_LEN))):
            blk = slice(hh * 256, (hh + 1) * 256)
            oc_ref[:, hh * 128:(hh + 1) * 128] = _softmax_pv(q, [kc_ref[:, blk]], [vc_ref[:, blk]]).astype(BF16)


def _mla_attention(q, kcat, vext, cos, sin):
    nq = SEQ // TQ
    ctx_blk = N_LAT // CTX_LEN
    rope_col = C_HEADS * C_NOPE // 128
    qrow = _attn_qrow
    rope = pl.BlockSpec((TQ, 128), lambda b, h, t: (_rope_block(qrow(b, t), TQ), 0))
    ctx = pl.BlockSpec((CTX_LEN, 512), lambda b, h, t: (ctx_blk + b, h))
    lat = pl.BlockSpec((SEQ, 512), lambda b, h, t: (b, h))
    return pl.pallas_call(
        _mla_attn_kernel,
        out_shape=[jax.ShapeDtypeStruct((N_LAT, C_WIDTH), BF16), jax.ShapeDtypeStruct((N_CTX, C_WIDTH), BF16)],
        grid=(BATCH, C_HEADS // 2, nq + 1),
        in_specs=[
            pl.BlockSpec((TQ, 256), lambda b, h, t: (qrow(b, t), h)),
            pl.BlockSpec((TQ, 128), lambda b, h, t: (qrow(b, t), rope_col + h)),
            rope, rope, ctx, ctx, lat, lat,
        ],
        out_specs=[pl.BlockSpec((TQ, 256), lambda b, h, t: (b * nq + jnp.minimum(t, nq - 1), h)),
                   pl.BlockSpec((CTX_LEN, 256), lambda b, h, t: (b, h))],
        compiler_params=_cparams(("arbitrary",) * 3),
        name="mla_attn",
    )(q, q, cos, sin, kcat, vext, kcat, vext)


def _s5_matrices(lam_re, lam_im, log_dt, b_re, b_im, c_re, c_im):
    hi = lax.Precision.HIGHEST
    lam_re, lam_im = lam_re.astype(F32), lam_im.astype(F32)
    dt = jnp.exp(log_dt.astype(F32))[..., None]
    ks = jnp.arange(S5_CHUNK + 1, dtype=F32)[:, None, None, None]
    mag = jnp.exp(lam_re[None] * dt[None] * ks)
    ang = lam_im[None] * dt[None] * ks
    pw_re, pw_im = mag * jnp.cos(ang), mag * jnp.sin(ang)
    a_re, a_im = pw_re[1], pw_im[1]
    den = lam_re * lam_re + lam_im * lam_im
    f_re = ((a_re - 1.0) * lam_re + a_im * lam_im) / den
    f_im = (a_im * lam_re - (a_re - 1.0) * lam_im) / den
    b_re, b_im = b_re.astype(F32), b_im.astype(F32)
    bb_re = f_re[..., None] * b_re - f_im[..., None] * b_im
    bb_im = f_re[..., None] * b_im + f_im[..., None] * b_re
    c_re, c_im = c_re.astype(F32), c_im.astype(F32)

    ab_re = pw_re[:S5_CHUNK, ..., None] * bb_re[None] - pw_im[:S5_CHUNK, ..., None] * bb_im[None]
    ab_im = pw_re[:S5_CHUNK, ..., None] * bb_im[None] + pw_im[:S5_CHUNK, ..., None] * bb_re[None]
    kern = (jnp.einsum('dgcn,ldgne->ldgce', c_re, ab_re, precision=hi)
            - jnp.einsum('dgcn,ldgne->ldgce', c_im, ab_im, precision=hi))
    t_idx = np.arange(S5_CHUNK)
    sub = lambda m, ax: m.reshape(m.shape[:ax] + (S5_QUADS, S5_GSUB) + m.shape[ax + 1:])
    klag = jnp.transpose(sub(kern, 2), (1, 2, 0, 5, 3, 4)).reshape(2, S5_QUADS, S5_CHUNK, S5_GROUP, 128)
    bt_re = jnp.transpose(bb_re, (0, 1, 3, 2))
    bt_im = jnp.transpose(bb_im, (0, 1, 3, 2))
    lane_gc = lambda c: jnp.transpose(sub(c, 0), (0, 3, 1, 2)).reshape(S5_QUADS, S5_STATE, 128)
    ps, qs = [], []
    for d in range(2):
        p_pow = (S5_CHUNK - 1 - t_idx) if d == 0 else t_idx
        ar, ai = pw_re[p_pow, d][:, :, None, :], pw_im[p_pow, d][:, :, None, :]
        pc = jnp.concatenate([ar * bt_re[d][None] - ai * bt_im[d][None],
                              ar * bt_im[d][None] + ai * bt_re[d][None]], axis=-1)
        pc = pc.reshape(S5_CHUNK, S5_QUADS, 128, 128)
        ps.append(jnp.transpose(pc, (1, 0, 2, 3)).reshape(S5_QUADS, S5_CHUNK * 128, 128))
        q_pow = (t_idx + 1) if d == 0 else (S5_CHUNK - t_idx)
        cl_re, cl_im = lane_gc(c_re[d])[:, None], lane_gc(c_im[d])[:, None]
        rep = lambda a: jnp.repeat(jnp.transpose(sub(a, 1), (1, 0, 3, 2)), S5_GROUP, axis=-1)
        al_re, al_im = rep(pw_re[q_pow, d]), rep(pw_im[q_pow, d])
        qs.append(jnp.concatenate([cl_re * al_re - cl_im * al_im,
                                   -(cl_re * al_im + cl_im * al_re)], axis=2))
    a16_re, a16_im = pw_re[S5_CHUNK], pw_im[S5_CHUNK]
    lanes = S5_GROUPS * 2 * S5_STATE
    return dict(
        klag=klag.astype(BF16),
        p=jnp.stack(ps).astype(BF16),
        q=jnp.stack(qs).astype(BF16),
        a_mul=jnp.concatenate([a16_re, a16_re], axis=-1).reshape(2, 1, lanes),
        a_swp=jnp.concatenate([-a16_im, a16_im], axis=-1).reshape(2, 1, lanes),
    )


def _s5_expand_p(pc):
    rep = jnp.concatenate([pc] * S5_GSUB, axis=1)
    row = lax.broadcasted_iota(jnp.int32, rep.shape, 0)
    col = lax.broadcasted_iota(jnp.int32, rep.shape, 1)
    return jnp.where((row // S5_GROUP) % S5_GSUB == col // 128, rep, jnp.zeros_like(rep))


def _s5_expand_q(q_ref):
    row = lax.broadcasted_iota(jnp.int32, (S5_GSUB * 128, 128), 0)
    col = lax.broadcasted_iota(jnp.int32, (S5_GSUB * 128, 128), 1)
    keep = row // 128 == col // S5_GROUP
    blocks = [jnp.where(keep, jnp.concatenate([q_ref[t]] * S5_GSUB, axis=0), jnp.zeros((S5_GSUB * 128, 128), BF16))
              for t in range(S5_CHUNK)]
    return jnp.concatenate(blocks, axis=1)


def _s5_chunk_rows(u_ref):
    n = N_TOK // S5_CHUNK
    return jnp.concatenate([u_ref[pl.ds(s, n, stride=S5_CHUNK), :] for s in range(S5_CHUNK)], axis=-1).astype(BF16)


def _s5_z_kernel(u_ref, p_ref, z_ref):
    z_ref[...] = jnp.dot(_s5_chunk_rows(u_ref), _s5_expand_p(p_ref[...]), preferred_element_type=F32)


def _s5_scan_kernel(z_ref, amul_ref, aswp_ref, s_ref, zs_scr):
    d = pl.program_id(0)
    a_mul, a_swp = amul_ref[...], aswp_ref[...]
    lanes = s_ref.shape[1]
    z = z_ref[...]
    low_half = lax.broadcasted_iota(jnp.int32, z.shape, 1) % 128 < S5_STATE
    zs_scr[...] = jnp.where(low_half, pltpu.roll(z, lanes - S5_STATE, 1), pltpu.roll(z, S5_STATE, 1))
    nl, nc = SEQ // S5_CHUNK, S5_CTX_CHUNKS

    def segment(base, count, carry):
        def body(step, carry):
            k = jnp.where(d == 0, step, count - 1 - step)
            out = []
            for b in range(BATCH):
                s, w = carry[2 * b], carry[2 * b + 1]
                row = pl.ds(base + b * count + k, 1)
                s_ref[row, :] = s
                out += [a_mul * s + a_swp * w + z_ref[row, :], a_mul * w - a_swp * s + zs_scr[row, :]]
            return tuple(out)
        return lax.fori_loop(0, count, body, carry)

    zero = jnp.zeros((1, lanes), F32)
    carry = segment(BATCH * nl, nc, (zero,) * (2 * BATCH))
    segment(0, nl, carry)


def _s5_y_kernel(u_ref, klag_ref, s_ref, q_ref, dv_ref, y_ref, m_scr):
    d = pl.program_id(1)
    n = N_TOK // S5_CHUNK
    m_scr[...] = jnp.zeros_like(m_scr)
    row = lax.broadcasted_iota(jnp.int32, (128, 128), 0)
    col = lax.broadcasted_iota(jnp.int32, (128, 128), 1)
    same_group = row // S5_GROUP == col // S5_GROUP
    blocks = [jnp.where(same_group, jnp.concatenate([klag_ref[lag]] * S5_GSUB, axis=0), jnp.zeros((128, 128), BF16))
              for lag in range(S5_CHUNK)]
    for rev in range(2):
        @pl.when(d == rev)
        def _():
            for s in range(S5_CHUNK):
                for t in range(S5_CHUNK):
                    lag = (s - t) if rev else (t - s)
                    if lag >= 0:
                        m_scr[s * 128:(s + 1) * 128, t * 128:(t + 1) * 128] = blocks[lag]
    y = jnp.dot(_s5_chunk_rows(u_ref), m_scr[...], preferred_element_type=F32)
    y = y + jnp.dot(s_ref[...].astype(BF16), _s5_expand_q(q_ref), preferred_element_type=F32)
    for t in range(S5_CHUNK):
        rows = pl.ds(t, n, stride=S5_CHUNK)
        yt = y[:, t * 128:(t + 1) * 128]

        @pl.when(d == 0)
        def _():
            y_ref[rows, :] = yt + u_ref[rows, :] * dv_ref[...]

        @pl.when(d == 1)
        def _():
            y_ref[rows, :] = y_ref[rows, :] + yt


def _s5(p, mats, layer, dskip):
    nstate = S5_GROUPS * 2 * S5_STATE
    qlanes = nstate // S5_QUADS
    slanes = 512
    ucol = (C_Q_RANK + C_KV_RANK) // 128
    z = pl.pallas_call(
        _s5_z_kernel,
        out_shape=jax.ShapeDtypeStruct((2, S5_ROWS, nstate), F32),
        grid=(2, S5_QUADS),
        in_specs=[
            pl.BlockSpec((N_TOK, 128), lambda d, q: (0, ucol + q)),
            pl.BlockSpec((None, None, None, S5_CHUNK * 128, 128), lambda d, q: (layer, d, q, 0, 0)),
        ],
        out_specs=pl.BlockSpec((None, S5_ROWS, qlanes), lambda d, q: (d, 0, q)),
        compiler_params=_cparams(("arbitrary", "arbitrary")),
        name="s5_chunk_state",
    )(p, mats["p"])
    s = pl.pallas_call(
        _s5_scan_kernel,
        out_shape=jax.ShapeDtypeStruct((2, S5_ROWS, nstate), F32),
        grid=(2, nstate // slanes),
        in_specs=[
            pl.BlockSpec((None, S5_ROWS, slanes), lambda d, q: (d, 0, q)),
            pl.BlockSpec((None, None, 1, slanes), lambda d, q: (layer, d, 0, q)),
            pl.BlockSpec((None, None, 1, slanes), lambda d, q: (layer, d, 0, q)),
        ],
        out_specs=pl.BlockSpec((None, S5_ROWS, slanes), lambda d, q: (d, 0, q)),
        scratch_shapes=[pltpu.VMEM((S5_ROWS, slanes), F32)],
        compiler_params=_cparams(("arbitrary", "arbitrary")),
        name="s5_scan",
    )(z, mats["a_mul"], mats["a_swp"])
    return pl.pallas_call(
        _s5_y_kernel,
        out_shape=jax.ShapeDtypeStruct((N_TOK, D_WIDTH), F32),
        grid=(S5_QUADS, 2),
        in_specs=[
            pl.BlockSpec((N_TOK, 128), lambda q, d: (0, ucol + q)),
            pl.BlockSpec((None, None, None, S5_CHUNK, S5_GROUP, 128), lambda q, d: (layer, d, q, 0, 0, 0)),
            pl.BlockSpec((None, S5_ROWS, qlanes), lambda q, d: (d, 0, q)),
            pl.BlockSpec((None, None, None, S5_CHUNK, 128, 128), lambda q, d: (layer, d, q, 0, 0, 0)),
            pl.BlockSpec((None, 1, 128), lambda q, d: (layer, 0, q)),
        ],
        out_specs=pl.BlockSpec((N_TOK, 128), lambda q, d: (0, q)),
        scratch_shapes=[pltpu.VMEM((S5_CHUNK * 128, S5_CHUNK * 128), BF16)],
        compiler_params=pltpu.CompilerParams(dimension_semantics=("arbitrary", "arbitrary"),
                                             vmem_limit_bytes=56 * 1024 * 1024),
        name="s5_output",
    )(p, mats["klag"], s, mats["q"], dskip)


def _glu_kernel(y_ref, w_ref, o_ref):
    y = y_ref[...]
    z = y * (0.5 * (1.0 + jnp.tanh(math.sqrt(2.0 / math.pi) * (y + 0.044715 * (y * y * y)))))
    gate = jnp.dot(z.astype(BF16), w_ref[...], preferred_element_type=F32)
    o_ref[...] = (z * jax.nn.sigmoid(gate)).astype(BF16)


def _s5_glu(y, w_glu, layer):
    t = 512
    return pl.pallas_call(
        _glu_kernel,
        out_shape=jax.ShapeDtypeStruct((N_TOK, D_WIDTH), BF16),
        grid=(N_TOK // t,),
        in_specs=[pl.BlockSpec((t, D_WIDTH), lambda i: (i, 0)),
                  pl.BlockSpec((None, D_WIDTH, D_WIDTH), lambda i: (layer, 0, 0))],
        out_specs=pl.BlockSpec((t, D_WIDTH), lambda i: (i, 0)),
        compiler_params=_cparams(("arbitrary",)),
        name="s5_glu",
    )(y, w_glu)


LANE_CHUNKS = D_MODEL // 128


def _store_token_major(ref, val):
    rows = val.shape[0]
    for c in range(LANE_CHUNKS):
        ref[pl.ds(c, rows, stride=LANE_CHUNKS), :] = val[:, c * 128:(c + 1) * 128]


def _load_token_major(ref, rows):
    return jnp.concatenate([ref[pl.ds(c, rows, stride=LANE_CHUNKS), :] for c in range(LANE_CHUNKS)], axis=-1)


def _router_kernel(x_ref, g_ref, sh_ref, sc_ref, whi_ref, wlo_ref, h_ref, info_ref, gw_ref, cnt_ref, h_scr, carry):
    i = pl.program_id(0)

    @pl.when(i == 0)
    def _():
        carry[...] = jnp.zeros_like(carry)

    for r in range(0, TM, 256):
        h = _rms(x_ref[r:r + 256, :]) * g_ref[...]
        h_scr[r:r + 256, :] = h * (1.0 + sc_ref[...]) + sh_ref[...]
    _store_token_major(h_ref, h_scr[...])
    h = h_scr[...]
    h_hi = h.astype(BF16)
    h_lo = (h - h_hi.astype(F32)).astype(BF16)
    logits = (jnp.dot(h_hi, whi_ref[...], preferred_element_type=F32)
              + jnp.dot(h_lo, whi_ref[...], preferred_element_type=F32)
              + jnp.dot(h_hi, wlo_ref[...], preferred_element_type=F32))
    lane = lax.broadcasted_iota(jnp.int32, logits.shape, 1)
    neg = -jnp.inf
    big = jnp.int32(1 << 20)

    def first_argmax(v, vmax):
        return jnp.min(jnp.where(v == vmax, lane, big), axis=-1, keepdims=True)

    lg = jnp.where(lane < N_GROUPS, logits, neg)
    mg = jnp.max(lg, axis=-1, keepdims=True)
    g_w = 1.0 / jnp.sum(jnp.exp(lg - mg), axis=-1, keepdims=True)
    g_idx = first_argmax(lg, mg)
    lo = N_GROUPS + EXPERTS_PER_GROUP * g_idx
    le = jnp.where(jnp.logical_and(lane >= lo, lane < lo + EXPERTS_PER_GROUP), logits, neg)
    m1 = jnp.max(le, axis=-1, keepdims=True)
    i1 = first_argmax(le, m1)
    le2 = jnp.where(lane == i1, neg, le)
    m2 = jnp.max(le2, axis=-1, keepdims=True)
    i2 = first_argmax(le2, m2)
    r21 = jnp.exp(m2 - m1)
    w1 = g_w / (1.0 + r21)
    w2 = g_w * r21 / (1.0 + r21)
    oh = jnp.logical_or(lane == i1, lane == i2)
    ohb = jnp.where(oh, 1.0, 0.0).astype(BF16)
    rr = lax.broadcasted_iota(jnp.int32, (TM, TM), 0)
    cc = lax.broadcasted_iota(jnp.int32, (TM, TM), 1)
    lower = jnp.where(rr > cc, 1.0, 0.0).astype(BF16)
    before = jnp.dot(lower, ohb, preferred_element_type=F32) + carry[...]
    rank1 = jnp.sum(jnp.where(lane == i1, before, 0.0), axis=-1, keepdims=True).astype(jnp.int32)
    rank2 = jnp.sum(jnp.where(lane == i2, before, 0.0), axis=-1, keepdims=True).astype(jnp.int32)
    carry[...] = carry[...] + jnp.sum(ohb.astype(F32), axis=0, keepdims=True)
    cnt_ref[...] = jnp.broadcast_to(carry[...], cnt_ref.shape)
    info = jnp.where(lane == 0, i1 - N_GROUPS, jnp.where(lane == 1, i2 - N_GROUPS,
                     jnp.where(lane == 2, rank1, jnp.where(lane == 3, rank2, 0))))
    info_ref[...] = info
    gw_ref[...] = jnp.where(lane == 0, w1, jnp.where(lane == 1, w2, 0.0))


def _router(x, g, mod, w_router, layer):
    return pl.pallas_call(
        _router_kernel,
        out_shape=[jax.ShapeDtypeStruct((N_TOK * LANE_CHUNKS, 128), F32),
                   jax.ShapeDtypeStruct((N_TOK, 128), jnp.int32),
                   jax.ShapeDtypeStruct((N_TOK, 128), F32),
                   jax.ShapeDtypeStruct((8, 128), F32)],
        grid=(N_TOK // TM,),
        in_specs=[
            pl.BlockSpec((TM, D_MODEL), lambda i: (i, 0)),
            pl.BlockSpec((1, D_MODEL), lambda i: (0, 0)),
            pl.BlockSpec((None, None, 1, D_MODEL), lambda i: (_mod_row(i, TM), 3, 0, 0)),
            pl.BlockSpec((None, None, 1, D_MODEL), lambda i: (_mod_row(i, TM), 4, 0, 0)),
            pl.BlockSpec((None, D_MODEL, 128), lambda i: (layer, 0, 0)),
            pl.BlockSpec((None, D_MODEL, 128), lambda i: (layer, 0, 0)),
        ],
        out_specs=[pl.BlockSpec((TM * LANE_CHUNKS, 128), lambda i: (i, 0)),
                   pl.BlockSpec((TM, 128), lambda i: (i, 0)),
                   pl.BlockSpec((TM, 128), lambda i: (i, 0)),
                   pl.BlockSpec((8, 128), lambda i: (0, 0))],
        scratch_shapes=[pltpu.VMEM((TM, D_MODEL), F32), pltpu.VMEM((1, 128), F32)],
        compiler_params=_cparams(("arbitrary",)),
        name="moe_router",
    )(x, g, mod, mod, *w_router)


FFN_ISSUE_GROUPS = 8
ISSUE_UNROLL = 8


def _ffn_kernel(pos_ref, te_ref, meta_ref, h_hbm, wg_ref, wu_ref, wd_ref, o_ref,
                src, xbuf, wg_b, wu_b, wd_b, sem):
    t = pl.program_id(0)
    nt = meta_ref[0]

    def row_copy(tile, slot, r):
        tok = src[tile * TE + r]
        return pltpu.make_async_copy(
            h_hbm.at[pl.ds(pl.multiple_of(tok * LANE_CHUNKS, LANE_CHUNKS), LANE_CHUNKS), :],
            xbuf.at[slot, pl.ds(pl.multiple_of(r * LANE_CHUNKS, LANE_CHUNKS), LANE_CHUNKS), :],
            sem.at[slot])

    def gather(tile, slot):
        def body(g, _):
            for k in range(ISSUE_UNROLL):
                row_copy(tile, slot, g * ISSUE_UNROLL + k).start(priority=k % 2)
            return 0
        lax.fori_loop(0, TE // ISSUE_UNROLL, body, 0)

    def wait_tile(slot):
        pltpu.make_async_copy(xbuf.at[slot], xbuf.at[slot], sem.at[slot]).wait()

    @pl.when(t == 0)
    def _():
        for e in range(N_EXPERTS):
            def clear(i, _):
                src[i] = 0
                return 0
            lax.fori_loop(meta_ref[1 + e], meta_ref[1 + N_EXPERTS + e], clear, 0)

        def fill(tok, _):
            src[pos_ref[2 * tok]] = tok
            src[pos_ref[2 * tok + 1]] = tok
            return 0
        lax.fori_loop(0, N_TOK, fill, 0, unroll=8)
        gather(0, 0)

    @pl.when(t >= nt)
    def _():
        o_ref[...] = jnp.zeros_like(o_ref)

    @pl.when(t < nt)
    def _():
        slot = t % 2
        wait_tile(slot)

        @pl.when(jnp.logical_or(t == 0, te_ref[t] != te_ref[jnp.maximum(t - 1, 0)]))
        def _():
            wg_b[...] = wg_ref[...].astype(BF16)
            wu_b[...] = wu_ref[...].astype(BF16)
            wd_b[...] = wd_ref[...].astype(BF16)

        nxt = jnp.minimum(t + 1, nt - 1)
        per = TE // FFN_ISSUE_GROUPS

        def issue(g):
            for r in range(g * per, (g + 1) * per):
                row_copy(nxt, 1 - slot, r).start(priority=r % 2)

        x = _load_token_major(xbuf.at[slot], TE).astype(BF16)
        half = D_EXPERT // 2
        hg, hu = [], []
        for n in range(2):
            issue(n)
            hg.append(jnp.dot(x, wg_b[:, n * half:(n + 1) * half], preferred_element_type=F32))
        for n in range(2):
            issue(2 + n)
            hu.append(jnp.dot(x, wu_b[:, n * half:(n + 1) * half], preferred_element_type=F32))
        hg, hu = jnp.concatenate(hg, axis=1), jnp.concatenate(hu, axis=1)
        act = (hg * jax.nn.sigmoid(hg) * hu).astype(BF16)
        quarter = D_MODEL // 4
        for n in range(4):
            issue(4 + n)
            y = jnp.dot(act, wd_b[:, n * quarter:(n + 1) * quarter], preferred_element_type=F32)
            for c in range(quarter // 128):
                o_ref[pl.ds(n * (quarter // 128) + c, TE, stride=LANE_CHUNKS), :] = y[:, c * 128:(c + 1) * 128]

        @pl.when(t == nt - 1)
        def _():
            wait_tile(1 - slot)


def _expert_ffn(pos_flat, tile_expert, meta, h, w_gate, w_up, w_down, layer):
    wsel = lambda t, pos, te, meta: (layer, te[t], 0, 0)
    return pl.pallas_call(
        _ffn_kernel,
        out_shape=jax.ShapeDtypeStruct((N_SORT * LANE_CHUNKS, 128), F32),
        grid_spec=pltpu.PrefetchScalarGridSpec(
            num_scalar_prefetch=3,
            grid=(N_ETILES,),
            in_specs=[
                pl.BlockSpec(memory_space=pl.ANY),
                pl.BlockSpec((None, None, D_MODEL, D_EXPERT), wsel),
                pl.BlockSpec((None, None, D_MODEL, D_EXPERT), wsel),
                pl.BlockSpec((None, None, D_EXPERT, D_MODEL), wsel),
            ],
            out_specs=pl.BlockSpec((TE * LANE_CHUNKS, 128), lambda t, pos, te, meta: (t, 0)),
            scratch_shapes=[
                pltpu.SMEM((N_SORT,), jnp.int32),
                pltpu.VMEM((2, TE * LANE_CHUNKS, 128), F32),
                pltpu.VMEM((D_MODEL, D_EXPERT), BF16),
                pltpu.VMEM((D_MODEL, D_EXPERT), BF16),
                pltpu.VMEM((D_EXPERT, D_MODEL), BF16),
                pltpu.SemaphoreType.DMA((2,)),
            ],
        ),
        compiler_params=pltpu.CompilerParams(dimension_semantics=("arbitrary",),
                                             vmem_limit_bytes=56 * 1024 * 1024),
        name="moe_expert_ffn",
    )(pos_flat, tile_expert, meta, h, w_gate, w_up, w_down)


TC = 256


def _combine_kernel(pos_ref, x_ref, gate_ref, gw_ref, fn_ref, ys_hbm, o_ref, buf_a, buf_b, sem, *, final_norm):
    i = pl.program_id(0)
    slot = i % 2

    def slab(ref, row):
        return ref.at[pl.ds(pl.multiple_of(row * LANE_CHUNKS, LANE_CHUNKS), LANE_CHUNKS), :]

    def gather(tile, slot):
        def body(g, _):
            for k in range(ISSUE_UNROLL):
                r = g * ISSUE_UNROLL + k
                tok = tile * TC + r
                pltpu.make_async_copy(slab(ys_hbm, pos_ref[2 * tok]), slab(buf_a.at[slot], r),
                                      sem.at[slot, 0]).start(priority=0)
                pltpu.make_async_copy(slab(ys_hbm, pos_ref[2 * tok + 1]), slab(buf_b.at[slot], r),
                                      sem.at[slot, 1]).start(priority=1)
            return 0
        lax.fori_loop(0, TC // ISSUE_UNROLL, body, 0)

    @pl.when(i == 0)
    def _():
        gather(0, 0)

    @pl.when(i + 1 < pl.num_programs(0))
    def _():
        gather(i + 1, 1 - slot)

    pltpu.make_async_copy(buf_a.at[slot], buf_a.at[slot], sem.at[slot, 0]).wait()
    pltpu.make_async_copy(buf_b.at[slot], buf_b.at[slot], sem.at[slot, 1]).wait()
    w0 = gw_ref[:, 0:1]
    w1 = gw_ref[:, 1:2]
    y = x_ref[...] + gate_ref[...] * (w0 * _load_token_major(buf_a.at[slot], TC)
                                      + w1 * _load_token_major(buf_b.at[slot], TC))
    if final_norm:
        y = _rms(y) * fn_ref[...]
    o_ref[...] = y


def _combine(pos_flat, x, mod, gw, ys, final_g):
    final_norm = final_g is not None
    fn = final_g if final_norm else jnp.ones((1, D_MODEL), F32)
    rows = N_LAT if final_norm else N_TOK
    return pl.pallas_call(
        functools.partial(_combine_kernel, final_norm=final_norm),
        out_shape=jax.ShapeDtypeStruct((rows, D_MODEL), F32),
        grid_spec=pltpu.PrefetchScalarGridSpec(
            num_scalar_prefetch=1,
            grid=(rows // TC,),
            in_specs=[
                pl.BlockSpec((TC, D_MODEL), lambda i, pos: (i, 0)),
                pl.BlockSpec((None, None, 1, D_MODEL), lambda i, pos: (_mod_row(i, TC), 5, 0, 0)),
                pl.BlockSpec((TC, 128), lambda i, pos: (i, 0)),
                pl.BlockSpec((1, D_MODEL), lambda i, pos: (0, 0)),
                pl.BlockSpec(memory_space=pl.ANY),
            ],
            out_specs=pl.BlockSpec((TC, D_MODEL), lambda i, pos: (i, 0)),
            scratch_shapes=[pltpu.VMEM((2, TC * LANE_CHUNKS, 128), F32), pltpu.VMEM((2, TC * LANE_CHUNKS, 128), F32),
                            pltpu.SemaphoreType.DMA((2, 2))],
        ),
        compiler_params=_cparams(("arbitrary",)),
        name="moe_combine",
    )(pos_flat, x, mod, gw, fn, ys)


def _moe(x, g, mod, w_router, w_gate, w_up, w_down, layer, final_g):
    h, info, gw, cnt = _router(x, g, mod, w_router, layer)
    counts = cnt[0, N_GROUPS:N_GROUPS + N_EXPERTS].astype(jnp.int32)
    padded = ((counts + TE - 1) // TE) * TE
    ends = jnp.cumsum(padded)
    starts = ends - padded
    experts = jnp.arange(N_EXPERTS, dtype=jnp.int32)
    start_of = jnp.sum(jnp.where(info[:, 0:2, None] == experts, starts, 0), axis=-1)
    pos_flat = (start_of + info[:, 2:4]).reshape(-1)
    tile_ends = ends // TE
    num_tiles = tile_ends[-1]
    tiles = jnp.minimum(jnp.arange(N_ETILES, dtype=jnp.int32), num_tiles - 1)
    tile_expert = jnp.sum((tile_ends[None, :] <= tiles[:, None]).astype(jnp.int32), axis=-1)
    meta = jnp.concatenate([num_tiles[None], starts + counts, ends]).astype(jnp.int32)
    ys = _expert_ffn(pos_flat, tile_expert, meta, h, w_gate, w_up, w_down, layer)
    return _combine(pos_flat, x, mod, gw, ys, final_g)


def kernel(x, c, ctx, c_ctx, mod_w, mod_b, norm_mix, norm_ffn, ab_w_in, ab_q_norm, ab_k_norm, ab_conv_w, ab_w_out, cd_w_in, cd_q_norm, cd_kv_norm, cd_w_uq, cd_w_ukv, s5_lam_re, s5_lam_im, s5_log_dt, s5_b_re, s5_b_im, s5_c_re, s5_c_im, s5_d, s5_w_glu, cd_w_out, moe_w_group, moe_w_expert, moe_w_gate, moe_w_up, moe_w_down, final_norm):
    cc = jnp.concatenate([c, c_ctx[None, :], jnp.zeros((8 - BATCH - 1, D_MODEL), F32)], axis=0)
    mods = _modulation(cc, mod_w, mod_b).reshape(DEPTH, 8, N_MOD, 1, D_MODEL)
    xs = jnp.concatenate([x.reshape(N_LAT, D_MODEL), ctx.reshape(N_CTX, D_MODEL)], axis=0)
    cos_a, sin_a = _rope_tables(HEAD_DIM)
    cos_c, sin_c = _rope_tables(C_ROPE)

    ab_in_b, ab_out_b = ab_w_in.astype(BF16), ab_w_out.astype(BF16)
    cd_out_b, ukv_b, glu_b = cd_w_out.astype(BF16), cd_w_ukv.astype(BF16), s5_w_glu.astype(BF16)
    a, b_ = C_Q_RANK + C_KV_RANK, C_Q_RANK + C_KV_RANK + C_ROPE
    pad = jnp.zeros(cd_w_in.shape[:2] + (CD_IN_PAD - cd_w_in.shape[2],), F32)
    cd_in_b = jnp.concatenate([cd_w_in[..., :a], cd_w_in[..., b_:], cd_w_in[..., a:b_], pad], axis=-1).astype(BF16)
    w_uq = cd_w_uq.reshape(-1, C_Q_RANK, C_HEADS, C_NOPE + C_ROPE)
    uq_b = jnp.concatenate([w_uq[..., :C_NOPE].reshape(-1, C_Q_RANK, C_HEADS * C_NOPE),
                            w_uq[..., C_NOPE:].reshape(-1, C_Q_RANK, C_HEADS * C_ROPE)], axis=-1).astype(BF16)

    mats = jax.vmap(_s5_matrices)(s5_lam_re, s5_lam_im, s5_log_dt, s5_b_re, s5_b_im, s5_c_re, s5_c_im)
    w_router = jnp.concatenate(
        [moe_w_group, jnp.transpose(moe_w_expert, (0, 2, 1, 3)).reshape(DEPTH, D_MODEL, N_EXPERTS),
         jnp.zeros((DEPTH, D_MODEL, 128 - N_GROUPS - N_EXPERTS), F32)], axis=-1)
    w_router_hi = w_router.astype(BF16)
    w_router = (w_router_hi, (w_router - w_router_hi.astype(F32)).astype(BF16))

    for i in range(DEPTH):
        j = i // 2
        mod = mods[i]
        if i % 2 == 0:
            p = _norm_linear(xs, 0, D_MODEL, norm_mix[i][None, :], mod, 0, ab_in_b, j, 2048)
            k, vext = _gqa_kv(p, ab_k_norm[j][None, :], cos_a, sin_a)
            o_lat, o_ctx = _gqa_attention(p, k, vext, ab_q_norm[j][None, :], cos_a, sin_a)
            side = _gated_conv(p, ab_conv_w[j])
            w_out = ab_out_b
        else:
            p = _norm_linear(xs, 0, D_MODEL, norm_mix[i][None, :], mod, 0, cd_in_b, j, CD_IN_PAD)
            q = _norm_linear(p, 0, C_Q_RANK, cd_q_norm[j][None, :], None, 0, uq_b, j, uq_b.shape[2])
            kcat, vext = _mla_kv(p, cd_kv_norm[j][None, :], ukv_b, j, cos_c, sin_c)
            o_lat, o_ctx = _mla_attention(q, kcat, vext, cos_c, sin_c)
            y = _s5(p, mats, j, s5_d.astype(F32)[:, None, :])
            side = _s5_glu(y, glu_b, j)
            w_out = cd_out_b
        xs = _out_linear(o_lat, o_ctx, side, w_out, j, xs, mod, 2)
        xs = _moe(xs, norm_ffn[i][None, :], mod, w_router, moe_w_gate, moe_w_up, moe_w_down, i,
                  final_norm[None, :] if i == DEPTH - 1 else None)
    return xs.reshape(BATCH, SEQ, D_MODEL)
```

```python
import functools
import math

import numpy as np
import jax
import jax.numpy as jnp
from jax import lax
from jax.experimental import pallas as pl
from jax.experimental.pallas import tpu as pltpu

F32 = jnp.float32
BF16 = jnp.bfloat16

D_MODEL = 2048
BATCH = 4
SEQ = 2048
DEPTH = 4
GRID_W = 64
CTX_LEN = 256
ROPE_THETA = 10000.0
EPS = 1e-6
N_MOD = 6
HEAD_DIM = 128
A_Q_HEADS = 12
A_KV_HEADS = 4
A_GROUP = A_Q_HEADS // A_KV_HEADS
A_WIDTH = A_Q_HEADS * HEAD_DIM
A_KV_WIDTH = A_KV_HEADS * HEAD_DIM
B_WIDTH = 512
AB_IN = A_WIDTH + 2 * A_KV_WIDTH + 3 * B_WIDTH
C_HEADS = 12
C_NOPE = 128
C_ROPE = 64
C_V = 128
C_Q_RANK = 512
C_KV_RANK = 256
C_WIDTH = C_HEADS * C_V
D_WIDTH = 512
S5_GROUP = 16
S5_GROUPS = D_WIDTH // S5_GROUP
S5_STATE = 64
N_GROUPS = 4
EXPERTS_PER_GROUP = 4
N_EXPERTS = N_GROUPS * EXPERTS_PER_GROUP
D_EXPERT = 512

N_LAT = BATCH * SEQ
N_CTX = BATCH * CTX_LEN
N_TOK = N_LAT + N_CTX
CTX_ROW = BATCH
CD_IN_PAD = 1536

TM = 512
TQ = 512
TE = 256
N_SORT = 2 * N_TOK + N_EXPERTS * TE
N_ETILES = N_SORT // TE
S5_CHUNK = 16
S5_CTX_CHUNKS = CTX_LEN // S5_CHUNK
S5_ROWS = N_TOK // S5_CHUNK
S5_GSUB = 128 // S5_GROUP
S5_QUADS = S5_GROUPS // S5_GSUB
VMEM_LIMIT = 48 * 1024 * 1024


def _cparams(sem):
    return pltpu.CompilerParams(dimension_semantics=sem, vmem_limit_bytes=VMEM_LIMIT)


def _mod_row(tile, tile_rows):
    r0 = tile * tile_rows
    return jnp.where(r0 >= N_LAT, CTX_ROW, r0 // SEQ)


def _rope_block(tile, tile_rows):
    r0 = tile * tile_rows
    return jnp.where(r0 >= N_LAT, SEQ // tile_rows, (r0 % SEQ) // tile_rows)


def _rms(x):
    return x * lax.rsqrt(jnp.mean(x * x, axis=-1, keepdims=True) + EPS)


def _mod_kernel(cc_ref, w_ref, b_ref, o_ref):
    cc = cc_ref[...]
    s = (cc * jax.nn.sigmoid(cc)).astype(BF16)
    o_ref[...] = jnp.dot(s, w_ref[...].astype(BF16), preferred_element_type=F32) + b_ref[...]


def _modulation(cc, mod_w, mod_b):
    tn = 1024
    nout = N_MOD * D_MODEL
    return pl.pallas_call(
        _mod_kernel,
        out_shape=jax.ShapeDtypeStruct((DEPTH, 8, nout), F32),
        grid=(DEPTH, nout // tn),
        in_specs=[
            pl.BlockSpec((8, D_MODEL), lambda l, j: (0, 0)),
            pl.BlockSpec((None, D_MODEL, tn), lambda l, j: (l, 0, j)),
            pl.BlockSpec((None, 1, tn), lambda l, j: (l, 0, j)),
        ],
        out_specs=pl.BlockSpec((None, 8, tn), lambda l, j: (l, 0, j)),
        compiler_params=_cparams(("arbitrary", "arbitrary")),
        name="modulation",
    )(cc, mod_w, mod_b.reshape(DEPTH, 1, nout))


def _norm_linear_kernel(x_ref, g_ref, sh_ref, sc_ref, w_ref, o_ref, h_scr, *, modulate, rows):
    @pl.when(pl.program_id(1) == 0)
    def _():
        for r in range(0, rows, 256):
            h = _rms(x_ref[r:r + 256, :]) * g_ref[...]
            if modulate:
                h = h * (1.0 + sc_ref[...]) + sh_ref[...]
            h_scr[r:r + 256, :] = h.astype(BF16)

    o_ref[...] = jnp.dot(h_scr[...], w_ref[...], preferred_element_type=F32)


def _norm_linear(x, xcol, kdim, g, mod, which, w, layer, tn):
    nout = w.shape[2]
    modulate = mod is not None
    if modulate:
        sh_spec = pl.BlockSpec((None, None, 1, kdim), lambda i, j: (_mod_row(i, TM), which, 0, 0))
        sc_spec = pl.BlockSpec((None, None, 1, kdim), lambda i, j: (_mod_row(i, TM), which + 1, 0, 0))
        sh = sc = mod
    else:
        sh_spec = sc_spec = pl.BlockSpec((1, kdim), lambda i, j: (0, 0))
        sh = sc = g
    return pl.pallas_call(
        functools.partial(_norm_linear_kernel, modulate=modulate, rows=TM),
        out_shape=jax.ShapeDtypeStruct((N_TOK, nout), F32),
        grid=(N_TOK // TM, nout // tn),
        in_specs=[
            pl.BlockSpec((TM, kdim), lambda i, j: (i, xcol)),
            pl.BlockSpec((1, kdim), lambda i, j: (0, 0)),
            sh_spec, sc_spec,
            pl.BlockSpec((None, kdim, tn), lambda i, j: (layer, 0, j)),
        ],
        out_specs=pl.BlockSpec((TM, tn), lambda i, j: (i, j)),
        scratch_shapes=[pltpu.VMEM((TM, kdim), BF16)],
        compiler_params=_cparams(("arbitrary", "arbitrary")),
        name="norm_linear",
    )(x, g, sh, sc, w)


def _out_linear_kernel(a1l_ref, a1c_ref, a2_ref, w1_ref, w2_ref, x_ref, gate_ref, o_ref):
    a1 = jnp.where(pl.program_id(0) >= N_LAT // TM, a1c_ref[...], a1l_ref[...])
    acc = jnp.dot(a1, w1_ref[...], preferred_element_type=F32)
    acc = acc + jnp.dot(a2_ref[...], w2_ref[...], preferred_element_type=F32)
    o_ref[...] = x_ref[...] + gate_ref[...] * acc


def _out_linear(a1_lat, a1_ctx, a2, w, layer, x, mod, which):
    tn = D_MODEL
    k1, k2 = a1_lat.shape[1], a2.shape[1]
    lat_tiles = N_LAT // TM
    return pl.pallas_call(
        _out_linear_kernel,
        out_shape=jax.ShapeDtypeStruct((N_TOK, D_MODEL), F32),
        grid=(N_TOK // TM, D_MODEL // tn),
        in_specs=[
            pl.BlockSpec((TM, k1), lambda i, j: (jnp.minimum(i, lat_tiles - 1), 0)),
            pl.BlockSpec((TM, k1), lambda i, j: (jnp.maximum(i - lat_tiles, 0), 0)),
            pl.BlockSpec((TM, k2), lambda i, j: (i, 0)),
            pl.BlockSpec((None, k1, tn), lambda i, j: (layer, 0, j)),
            pl.BlockSpec((None, k2, tn), lambda i, j: (layer, k1 // k2, j)),
            pl.BlockSpec((TM, tn), lambda i, j: (i, j)),
            pl.BlockSpec((None, None, 1, tn), lambda i, j: (_mod_row(i, TM), which, 0, j)),
        ],
        out_specs=pl.BlockSpec((TM, tn), lambda i, j: (i, j)),
        compiler_params=_cparams(("arbitrary", "arbitrary")),
        name="out_linear",
    )(a1_lat, a1_ctx, a2, w, w, x, mod)


def _rope_tables(rot_dim):
    rows = SEQ // GRID_W
    row_ids = np.repeat(np.arange(rows, dtype=np.float32), GRID_W)
    col_ids = np.tile(np.arange(GRID_W, dtype=np.float32), rows)
    d_axis = rot_dim // 2
    inv = (np.float32(ROPE_THETA) ** (-np.arange(0, d_axis, 2, dtype=np.float32) / np.float32(d_axis))).astype(np.float32)
    ang = np.concatenate([row_ids[:, None] * inv, col_ids[:, None] * inv], axis=-1).astype(np.float32)
    cos, sin = np.cos(ang).astype(np.float32), np.sin(ang).astype(np.float32)
    reps = 128 // rot_dim
    cos_f = np.tile(np.concatenate([cos, cos], axis=-1), (1, reps))
    sin_f = np.tile(np.concatenate([-sin, sin], axis=-1), (1, reps))
    cos_f = np.concatenate([cos_f, np.ones((512, 128), np.float32)], axis=0)
    sin_f = np.concatenate([sin_f, np.zeros((512, 128), np.float32)], axis=0)
    return jnp.asarray(cos_f), jnp.asarray(sin_f)


def _rope128(x, cos, sin):
    return x * cos + pltpu.roll(x, 64, 1) * sin


def _rope64(x, cos, sin):
    lane = lax.broadcasted_iota(jnp.int32, x.shape, 1)
    swapped = jnp.where((lane % 64) < 32, pltpu.roll(x, 96, 1), pltpu.roll(x, 32, 1))
    return x * cos + swapped * sin


def _ones_column(rows):
    lane = lax.broadcasted_iota(jnp.int32, (rows, 128), 1)
    return jnp.where(lane == 0, 1.0, 0.0).astype(BF16)


def _gqa_kv_kernel(k_ref, v_ref, kn_ref, cos_ref, sin_ref, ko_ref, vo_ref):
    cos, sin = cos_ref[...], sin_ref[...]
    ones = _ones_column(k_ref.shape[0])
    for h in range(A_KV_HEADS):
        k = _rms(k_ref[:, h * 128:(h + 1) * 128]) * kn_ref[...]
        ko_ref[:, h * 128:(h + 1) * 128] = _rope128(k, cos, sin).astype(BF16)
        vo_ref[:, h * 256:h * 256 + 128] = v_ref[:, h * 128:(h + 1) * 128].astype(BF16)
        vo_ref[:, h * 256 + 128:(h + 1) * 256] = ones


def _gqa_kv(p, k_norm, cos, sin):
    t = 512
    return pl.pallas_call(
        _gqa_kv_kernel,
        out_shape=[jax.ShapeDtypeStruct((N_TOK, A_KV_WIDTH), BF16),
                   jax.ShapeDtypeStruct((N_TOK, 2 * A_KV_WIDTH), BF16)],
        grid=(N_TOK // t,),
        in_specs=[
            pl.BlockSpec((t, A_KV_WIDTH), lambda i: (i, A_WIDTH // A_KV_WIDTH)),
            pl.BlockSpec((t, A_KV_WIDTH), lambda i: (i, A_WIDTH // A_KV_WIDTH + 1)),
            pl.BlockSpec((1, 128), lambda i: (0, 0)),
            pl.BlockSpec((t, 128), lambda i: (_rope_block(i, t), 0)),
            pl.BlockSpec((t, 128), lambda i: (_rope_block(i, t), 0)),
        ],
        out_specs=[pl.BlockSpec((t, A_KV_WIDTH), lambda i: (i, 0)),
                   pl.BlockSpec((t, 2 * A_KV_WIDTH), lambda i: (i, 0))],
        compiler_params=_cparams(("arbitrary",)),
        name="gqa_kv",
    )(p, p, k_norm, cos, sin)


LOG2E = math.log2(math.e)
KEY_PIECE = 1024


def _softmax_pv(q, keys, vals):
    dn = (((1,), (1,)), ((), ()))
    m = acc = None
    for k, v in zip(keys, vals):
        s = lax.dot_general(q, k, dn, preferred_element_type=F32)
        m_piece = jnp.max(s, axis=-1, keepdims=True)
        if m is None:
            m_new = m_piece
        else:
            m_new = jnp.maximum(m, m_piece)
            acc = acc * jnp.exp2(m - m_new)
        part = jnp.dot(jnp.exp2(s - m_new).astype(BF16), v, preferred_element_type=F32)
        acc = part if acc is None else acc + part
        m = m_new
    return acc[:, :128] / acc[:, 128:129]


def _attn_qrow(b, t):
    nq = SEQ // TQ
    return jnp.where(t == nq, (N_LAT + b * CTX_LEN) // TQ, b * nq + t)


def _ctx_half():
    return pl.multiple_of((pl.program_id(0) % (TQ // CTX_LEN)) * CTX_LEN, CTX_LEN)


def _gqa_attn_kernel(q_ref, qn_ref, cos_ref, sin_ref, kc_ref, vc_ref, kl_ref, vl_ref, ol_ref, oc_ref):
    scale = HEAD_DIM ** -0.5 * LOG2E
    is_ctx = pl.program_id(2) == SEQ // TQ

    def query(rows, g):
        q = _rms(q_ref[rows, g * 128:(g + 1) * 128]) * qn_ref[...]
        return (_rope128(q, cos_ref[rows, :], sin_ref[rows, :]) * scale).astype(BF16)

    @pl.when(jnp.logical_not(is_ctx))
    def _():
        pieces = [slice(j * KEY_PIECE, (j + 1) * KEY_PIECE) for j in range(SEQ // KEY_PIECE)]
        qs = [query(slice(None), g) for g in range(A_GROUP)]
        for g in range(A_GROUP):
            o = _softmax_pv(qs[g], [kc_ref[...]] + [kl_ref[r, :] for r in pieces],
                            [vc_ref[...]] + [vl_ref[r, :] for r in pieces])
            ol_ref[:, g * 128:(g + 1) * 128] = o.astype(BF16)

    @pl.when(is_ctx)
    def _():
        rows = pl.ds(_ctx_half(), CTX_LEN)
        for g in range(A_GROUP):
            oc_ref[:, g * 128:(g + 1) * 128] = _softmax_pv(query(rows, g), [kc_ref[...]], [vc_ref[...]]).astype(BF16)


def _gqa_attention(p, k, vext, q_norm, cos, sin):
    qw = A_GROUP * HEAD_DIM
    nq = SEQ // TQ
    ctx_blk = N_LAT // CTX_LEN
    rope = pl.BlockSpec((TQ, 128), lambda b, h, t: (_rope_block(_attn_qrow(b, t), TQ), 0))
    return pl.pallas_call(
        _gqa_attn_kernel,
        out_shape=[jax.ShapeDtypeStruct((N_LAT, A_WIDTH), BF16), jax.ShapeDtypeStruct((N_CTX, A_WIDTH), BF16)],
        grid=(BATCH, A_KV_HEADS, nq + 1),
        in_specs=[
            pl.BlockSpec((TQ, qw), lambda b, h, t: (_attn_qrow(b, t), h)),
            pl.BlockSpec((1, 128), lambda b, h, t: (0, 0)),
            rope, rope,
            pl.BlockSpec((CTX_LEN, 128), lambda b, h, t: (ctx_blk + b, h)),
            pl.BlockSpec((CTX_LEN, 256), lambda b, h, t: (ctx_blk + b, h)),
            pl.BlockSpec((SEQ, 128), lambda b, h, t: (b, h)),
            pl.BlockSpec((SEQ, 256), lambda b, h, t: (b, h)),
        ],
        out_specs=[pl.BlockSpec((TQ, qw), lambda b, h, t: (b * nq + jnp.minimum(t, nq - 1), h)),
                   pl.BlockSpec((CTX_LEN, qw), lambda b, h, t: (b, h))],
        compiler_params=_cparams(("arbitrary",) * 3),
        name="gqa_attn",
    )(p, q_norm, cos, sin, k, vext, k, vext)


def _conv_kernel(bg_ref, cg_ref, ug_ref, cgp_ref, ugp_ref, cgn_ref, ugn_ref, w_ref, o_ref, *, rows):
    i = pl.program_id(0)
    per_seq = SEQ // rows
    is_ctx = i >= N_LAT // rows
    is_start = jnp.logical_or(is_ctx, i % per_seq == 0)
    is_end = jnp.logical_or(is_ctx, i % per_seq == per_seq - 1)
    m = cg_ref[...] * ug_ref[...]
    m_prev = jnp.where(is_start, 0.0, cgp_ref[7:8, :] * ugp_ref[7:8, :])
    m_next = jnp.where(is_end, 0.0, cgn_ref[0:1, :] * ugn_ref[0:1, :])
    row = lax.broadcasted_iota(jnp.int32, m.shape, 0)
    down = jnp.where(row == 0, m_prev, pltpu.roll(m, 1, 0))
    up = jnp.where(row == rows - 1, m_next, pltpu.roll(m, rows - 1, 0))
    conv = down * w_ref[0:1, :] + m * w_ref[1:2, :] + up * w_ref[2:3, :]
    o_ref[...] = (bg_ref[...] * conv).astype(BF16)


def _gated_conv(p, conv_w):
    rows = CTX_LEN
    base = (A_WIDTH + 2 * A_KV_WIDTH) // B_WIDTH
    halo = rows // 8
    last = N_TOK // 8 - 1
    main = lambda c: pl.BlockSpec((rows, B_WIDTH), lambda i: (i, base + c))
    prev = lambda c: pl.BlockSpec((8, B_WIDTH), lambda i: (jnp.maximum(i * halo - 1, 0), base + c))
    nxt = lambda c: pl.BlockSpec((8, B_WIDTH), lambda i: (jnp.minimum((i + 1) * halo, last), base + c))
    return pl.pallas_call(
        functools.partial(_conv_kernel, rows=rows),
        out_shape=jax.ShapeDtypeStruct((N_TOK, B_WIDTH), BF16),
        grid=(N_TOK // rows,),
        in_specs=[main(0), main(1), main(2), prev(1), prev(2), nxt(1), nxt(2),
                  pl.BlockSpec((3, B_WIDTH), lambda i: (0, 0))],
        out_specs=pl.BlockSpec((rows, B_WIDTH), lambda i: (i, 0)),
        compiler_params=_cparams(("arbitrary",)),
        name="gated_conv",
    )(p, p, p, p, p, p, p, conv_w)


def _mla_kv_kernel(ckv_ref, g_ref, w_ref, kr_ref, cos_ref, sin_ref, ko_ref, vo_ref):
    ckv = (_rms(ckv_ref[...]) * g_ref[...]).astype(BF16)
    kv = jnp.dot(ckv, w_ref[...], preferred_element_type=F32)
    x = kr_ref[...]
    lane = lax.broadcasted_iota(jnp.int32, x.shape, 1)
    r = jnp.where(lane < C_ROPE, _rope64(x, cos_ref[...], sin_ref[...]), 0.0)
    kr_low = r.astype(BF16)
    kr_high = pltpu.roll(r, 64, 1).astype(BF16)
    ones = _ones_column(x.shape[0])
    for h in range(C_HEADS):
        ko_ref[:, h * 256:h * 256 + 128] = kv[:, h * 256:h * 256 + 128].astype(BF16)
        ko_ref[:, h * 256 + 128:(h + 1) * 256] = kr_low if h % 2 == 0 else kr_high
        vo_ref[:, h * 256:h * 256 + 128] = kv[:, h * 256 + 128:(h + 1) * 256].astype(BF16)
        vo_ref[:, h * 256 + 128:(h + 1) * 256] = ones


def _mla_kv(p, kv_norm, w_ukv, layer, cos, sin):
    t = 512
    col = (C_Q_RANK + C_KV_RANK + D_WIDTH) // 128
    width = C_HEADS * 256
    return pl.pallas_call(
        _mla_kv_kernel,
        out_shape=[jax.ShapeDtypeStruct((N_TOK, width), BF16)] * 2,
        grid=(N_TOK // t,),
        in_specs=[
            pl.BlockSpec((t, C_KV_RANK), lambda i: (i, C_Q_RANK // C_KV_RANK)),
            pl.BlockSpec((1, C_KV_RANK), lambda i: (0, 0)),
            pl.BlockSpec((None, C_KV_RANK, width), lambda i: (layer, 0, 0)),
            pl.BlockSpec((t, 128), lambda i: (i, col)),
            pl.BlockSpec((t, 128), lambda i: (_rope_block(i, t), 0)),
            pl.BlockSpec((t, 128), lambda i: (_rope_block(i, t), 0)),
        ],
        out_specs=[pl.BlockSpec((t, width), lambda i: (i, 0))] * 2,
        compiler_params=_cparams(("arbitrary",)),
        name="mla_kv",
    )(p, kv_norm, w_ukv, p, cos, sin)


def _mla_attn_kernel(qn_ref, qr_ref, cos_ref, sin_ref, kc_ref, vc_ref, kl_ref, vl_ref, ol_ref, oc_ref):
    scale = (C_NOPE + C_ROPE) ** -0.5 * LOG2E
    is_ctx = pl.program_id(2) == SEQ // TQ

    def queries(rows):
        qr = _rope64(qr_ref[rows, :], cos_ref[rows, :], sin_ref[rows, :])
        lane = lax.broadcasted_iota(jnp.int32, qr.shape, 1)
        out = []
        for hh in range(2):
            sel = (lane < 64) if hh == 0 else (lane >= 64)
            q = jnp.concatenate([qn_ref[rows, hh * 128:(hh + 1) * 128], jnp.where(sel, qr, 0.0)], axis=1)
            out.append((q * scale).astype(BF16))
        return out

    @pl.when(jnp.logical_not(is_ctx))
    def _():
        for hh, q in enumerate(queries(slice(None))):
            blk = slice(hh * 256, (hh + 1) * 256)
            pieces = [slice(j * KEY_PIECE, (j + 1) * KEY_PIECE) for j in range(SEQ // KEY_PIECE)]
            o = _softmax_pv(q, [kc_ref[:, blk]] + [kl_ref[r, blk] for r in pieces],
                            [vc_ref[:, blk]] + [vl_ref[r, blk] for r in pieces])
            ol_ref[:, hh * 128:(hh + 1) * 128] = o.astype(BF16)

    @pl.when(is_ctx)
    def _():
        for hh, q in enumerate(queries(pl.ds(_ctx_half(), CTX_LEN))):
            blk = slice(hh * 256, (hh + 1) * 256)
            oc_ref[:, hh * 128:(hh + 1) * 128] = _softmax_pv(q, [kc_ref[:, blk]], [vc_ref[:, blk]]).astype(BF16)


def _mla_attention(q, kcat, vext, cos, sin):
    nq = SEQ // TQ
    ctx_blk = N_LAT // CTX_LEN
    rope_col = C_HEADS * C_NOPE // 128
    qrow = _attn_qrow
    rope = pl.BlockSpec((TQ, 128), lambda b, h, t: (_rope_block(qrow(b, t), TQ), 0))
    ctx = pl.BlockSpec((CTX_LEN, 512), lambda b, h, t: (ctx_blk + b, h))
    lat = pl.BlockSpec((SEQ, 512), lambda b, h, t: (b, h))
    return pl.pallas_call(
        _mla_attn_kernel,
        out_shape=[jax.ShapeDtypeStruct((N_LAT, C_WIDTH), BF16), jax.ShapeDtypeStruct((N_CTX, C_WIDTH), BF16)],
        grid=(BATCH, C_HEADS // 2, nq + 1),
        in_specs=[
            pl.BlockSpec((TQ, 256), lambda b, h, t: (qrow(b, t), h)),
            pl.BlockSpec((TQ, 128), lambda b, h, t: (qrow(b, t), rope_col + h)),
            rope, rope, ctx, ctx, lat, lat,
        ],
        out_specs=[pl.BlockSpec((TQ, 256), lambda b, h, t: (b * nq + jnp.minimum(t, nq - 1), h)),
                   pl.BlockSpec((CTX_LEN, 256), lambda b, h, t: (b, h))],
        compiler_params=_cparams(("arbitrary",) * 3),
        name="mla_attn",
    )(q, q, cos, sin, kcat, vext, kcat, vext)


def _s5_matrices(lam_re, lam_im, log_dt, b_re, b_im, c_re, c_im):
    hi = lax.Precision.HIGHEST
    lam_re, lam_im = lam_re.astype(F32), lam_im.astype(F32)
    dt = jnp.exp(log_dt.astype(F32))[..., None]
    ks = jnp.arange(S5_CHUNK + 1, dtype=F32)[:, None, None, None]
    mag = jnp.exp(lam_re[None] * dt[None] * ks)
    ang = lam_im[None] * dt[None] * ks
    pw_re, pw_im = mag * jnp.cos(ang), mag * jnp.sin(ang)
    a_re, a_im = pw_re[1], pw_im[1]
    den = lam_re * lam_re + lam_im * lam_im
    f_re = ((a_re - 1.0) * lam_re + a_im * lam_im) / den
    f_im = (a_im * lam_re - (a_re - 1.0) * lam_im) / den
    b_re, b_im = b_re.astype(F32), b_im.astype(F32)
    bb_re = f_re[..., None] * b_re - f_im[..., None] * b_im
    bb_im = f_re[..., None] * b_im + f_im[..., None] * b_re
    c_re, c_im = c_re.astype(F32), c_im.astype(F32)

    ab_re = pw_re[:S5_CHUNK, ..., None] * bb_re[None] - pw_im[:S5_CHUNK, ..., None] * bb_im[None]
    ab_im = pw_re[:S5_CHUNK, ..., None] * bb_im[None] + pw_im[:S5_CHUNK, ..., None] * bb_re[None]
    kern = (jnp.einsum('dgcn,ldgne->ldgce', c_re, ab_re, precision=hi)
            - jnp.einsum('dgcn,ldgne->ldgce', c_im, ab_im, precision=hi))
    t_idx = np.arange(S5_CHUNK)
    sub = lambda m, ax: m.reshape(m.shape[:ax] + (S5_QUADS, S5_GSUB) + m.shape[ax + 1:])
    klag = jnp.transpose(sub(kern, 2), (1, 2, 0, 5, 3, 4)).reshape(2, S5_QUADS, S5_CHUNK, S5_GROUP, 128)
    bt_re = jnp.transpose(bb_re, (0, 1, 3, 2))
    bt_im = jnp.transpose(bb_im, (0, 1, 3, 2))
    lane_gc = lambda c: jnp.transpose(sub(c, 0), (0, 3, 1, 2)).reshape(S5_QUADS, S5_STATE, 128)
    ps, qs = [], []
    for d in range(2):
        p_pow = (S5_CHUNK - 1 - t_idx) if d == 0 else t_idx
        ar, ai = pw_re[p_pow, d][:, :, None, :], pw_im[p_pow, d][:, :, None, :]
        pc = jnp.concatenate([ar * bt_re[d][None] - ai * bt_im[d][None],
                              ar * bt_im[d][None] + ai * bt_re[d][None]], axis=-1)
        pc = pc.reshape(S5_CHUNK, S5_QUADS, 128, 128)
        ps.append(jnp.transpose(pc, (1, 0, 2, 3)).reshape(S5_QUADS, S5_CHUNK * 128, 128))
        q_pow = (t_idx + 1) if d == 0 else (S5_CHUNK - t_idx)
        cl_re, cl_im = lane_gc(c_re[d])[:, None], lane_gc(c_im[d])[:, None]
        rep = lambda a: jnp.repeat(jnp.transpose(sub(a, 1), (1, 0, 3, 2)), S5_GROUP, axis=-1)
        al_re, al_im = rep(pw_re[q_pow, d]), rep(pw_im[q_pow, d])
        qs.append(jnp.concatenate([cl_re * al_re - cl_im * al_im,
                                   -(cl_re * al_im + cl_im * al_re)], axis=2))
    a16_re, a16_im = pw_re[S5_CHUNK], pw_im[S5_CHUNK]
    lanes = S5_GROUPS * 2 * S5_STATE
    return dict(
        klag=klag.astype(BF16),
        p=jnp.stack(ps).astype(BF16),
        q=jnp.stack(qs).astype(BF16),
        a_mul=jnp.concatenate([a16_re, a16_re], axis=-1).reshape(2, 1, lanes),
        a_swp=jnp.concatenate([-a16_im, a16_im], axis=-1).reshape(2, 1, lanes),
    )


def _s5_expand_p(pc):
    rep = jnp.concatenate([pc] * S5_GSUB, axis=1)
    row = lax.broadcasted_iota(jnp.int32, rep.shape, 0)
    col = lax.broadcasted_iota(jnp.int32, rep.shape, 1)
    return jnp.where((row // S5_GROUP) % S5_GSUB == col // 128, rep, jnp.zeros_like(rep))


def _s5_expand_q(q_ref):
    row = lax.broadcasted_iota(jnp.int32, (S5_GSUB * 128, 128), 0)
    col = lax.broadcasted_iota(jnp.int32, (S5_GSUB * 128, 128), 1)
    keep = row // 128 == col // S5_GROUP
    blocks = [jnp.where(keep, jnp.concatenate([q_ref[t]] * S5_GSUB, axis=0), jnp.zeros((S5_GSUB * 128, 128), BF16))
              for t in range(S5_CHUNK)]
    return jnp.concatenate(blocks, axis=1)


def _s5_chunk_rows(u_ref):
    n = N_TOK // S5_CHUNK
    return jnp.concatenate([u_ref[pl.ds(s, n, stride=S5_CHUNK), :] for s in range(S5_CHUNK)], axis=-1).astype(BF16)


def _s5_z_kernel(u_ref, p_ref, z_ref):
    z_ref[...] = jnp.dot(_s5_chunk_rows(u_ref), _s5_expand_p(p_ref[...]), preferred_element_type=F32)


def _s5_scan_kernel(z_ref, amul_ref, aswp_ref, s_ref, zs_scr):
    d = pl.program_id(0)
    a_mul, a_swp = amul_ref[...], aswp_ref[...]
    lanes = s_ref.shape[1]
    z = z_ref[...]
    low_half = lax.broadcasted_iota(jnp.int32, z.shape, 1) % 128 < S5_STATE
    zs_scr[...] = jnp.where(low_half, pltpu.roll(z, lanes - S5_STATE, 1), pltpu.roll(z, S5_STATE, 1))
    nl, nc = SEQ // S5_CHUNK, S5_CTX_CHUNKS

    def segment(base, count, carry):
        def body(step, carry):
            k = jnp.where(d == 0, step, count - 1 - step)
            out = []
            for b in range(BATCH):
                s, w = carry[2 * b], carry[2 * b + 1]
                row = pl.ds(base + b * count + k, 1)
                s_ref[row, :] = s
                out += [a_mul * s + a_swp * w + z_ref[row, :], a_mul * w - a_swp * s + zs_scr[row, :]]
            return tuple(out)
        return lax.fori_loop(0, count, body, carry)

    zero = jnp.zeros((1, lanes), F32)
    carry = segment(BATCH * nl, nc, (zero,) * (2 * BATCH))
    segment(0, nl, carry)


def _s5_y_kernel(u_ref, klag_ref, s_ref, q_ref, dv_ref, y_ref, m_scr):
    d = pl.program_id(1)
    n = N_TOK // S5_CHUNK
    m_scr[...] = jnp.zeros_like(m_scr)
    row = lax.broadcasted_iota(jnp.int32, (128, 128), 0)
    col = lax.broadcasted_iota(jnp.int32, (128, 128), 1)
    same_group = row // S5_GROUP == col // S5_GROUP
    blocks = [jnp.where(same_group, jnp.concatenate([klag_ref[lag]] * S5_GSUB, axis=0), jnp.zeros((128, 128), BF16))
              for lag in range(S5_CHUNK)]
    for rev in range(2):
        @pl.when(d == rev)
        def _():
            for s in range(S5_CHUNK):
                for t in range(S5_CHUNK):
                    lag = (s - t) if rev else (t - s)
                    if lag >= 0:
                        m_scr[s * 128:(s + 1) * 128, t * 128:(t + 1) * 128] = blocks[lag]
    y = jnp.dot(_s5_chunk_rows(u_ref), m_scr[...], preferred_element_type=F32)
    y = y + jnp.dot(s_ref[...].astype(BF16), _s5_expand_q(q_ref), preferred_element_type=F32)
    for t in range(S5_CHUNK):
        rows = pl.ds(t, n, stride=S5_CHUNK)
        yt = y[:, t * 128:(t + 1) * 128]

        @pl.when(d == 0)
        def _():
            y_ref[rows, :] = yt + u_ref[rows, :] * dv_ref[...]

        @pl.when(d == 1)
        def _():
            y_ref[rows, :] = y_ref[rows, :] + yt


def _s5(p, mats, layer, dskip):
    nstate = S5_GROUPS * 2 * S5_STATE
    qlanes = nstate // S5_QUADS
    slanes = 512
    ucol = (C_Q_RANK + C_KV_RANK) // 128
    z = pl.pallas_call(
        _s5_z_kernel,
        out_shape=jax.ShapeDtypeStruct((2, S5_ROWS, nstate), F32),
        grid=(2, S5_QUADS),
        in_specs=[
            pl.BlockSpec((N_TOK, 128), lambda d, q: (0, ucol + q)),
            pl.BlockSpec((None, None, None, S5_CHUNK * 128, 128), lambda d, q: (layer, d, q, 0, 0)),
        ],
        out_specs=pl.BlockSpec((None, S5_ROWS, qlanes), lambda d, q: (d, 0, q)),
        compiler_params=_cparams(("arbitrary", "arbitrary")),
        name="s5_chunk_state",
    )(p, mats["p"])
    s = pl.pallas_call(
        _s5_scan_kernel,
        out_shape=jax.ShapeDtypeStruct((2, S5_ROWS, nstate), F32),
        grid=(2, nstate // slanes),
        in_specs=[
            pl.BlockSpec((None, S5_ROWS, slanes), lambda d, q: (d, 0, q)),
            pl.BlockSpec((None, None, 1, slanes), lambda d, q: (layer, d, 0, q)),
            pl.BlockSpec((None, None, 1, slanes), lambda d, q: (layer, d, 0, q)),
        ],
        out_specs=pl.BlockSpec((None, S5_ROWS, slanes), lambda d, q: (d, 0, q)),
        scratch_shapes=[pltpu.VMEM((S5_ROWS, slanes), F32)],
        compiler_params=_cparams(("arbitrary", "arbitrary")),
        name="s5_scan",
    )(z, mats["a_mul"], mats["a_swp"])
    return pl.pallas_call(
        _s5_y_kernel,
        out_shape=jax.ShapeDtypeStruct((N_TOK, D_WIDTH), F32),
        grid=(S5_QUADS, 2),
        in_specs=[
            pl.BlockSpec((N_TOK, 128), lambda q, d: (0, ucol + q)),
            pl.BlockSpec((None, None, None, S5_CHUNK, S5_GROUP, 128), lambda q, d: (layer, d, q, 0, 0, 0)),
            pl.BlockSpec((None, S5_ROWS, qlanes), lambda q, d: (d, 0, q)),
            pl.BlockSpec((None, None, None, S5_CHUNK, 128, 128), lambda q, d: (layer, d, q, 0, 0, 0)),
            pl.BlockSpec((None, 1, 128), lambda q, d: (layer, 0, q)),
        ],
        out_specs=pl.BlockSpec((N_TOK, 128), lambda q, d: (0, q)),
        scratch_shapes=[pltpu.VMEM((S5_CHUNK * 128, S5_CHUNK * 128), BF16)],
        compiler_params=pltpu.CompilerParams(dimension_semantics=("arbitrary", "arbitrary"),
                                             vmem_limit_bytes=56 * 1024 * 1024),
        name="s5_output",
    )(p, mats["klag"], s, mats["q"], dskip)


def _glu_kernel(y_ref, w_ref, o_ref):
    y = y_ref[...]
    z = y * (0.5 * (1.0 + jnp.tanh(math.sqrt(2.0 / math.pi) * (y + 0.044715 * (y * y * y)))))
    gate = jnp.dot(z.astype(BF16), w_ref[...], preferred_element_type=F32)
    o_ref[...] = (z * jax.nn.sigmoid(gate)).astype(BF16)


def _s5_glu(y, w_glu, layer):
    t = 512
    return pl.pallas_call(
        _glu_kernel,
        out_shape=jax.ShapeDtypeStruct((N_TOK, D_WIDTH), BF16),
        grid=(N_TOK // t,),
        in_specs=[pl.BlockSpec((t, D_WIDTH), lambda i: (i, 0)),
                  pl.BlockSpec((None, D_WIDTH, D_WIDTH), lambda i: (layer, 0, 0))],
        out_specs=pl.BlockSpec((t, D_WIDTH), lambda i: (i, 0)),
        compiler_params=_cparams(("arbitrary",)),
        name="s5_glu",
    )(y, w_glu)


LANE_CHUNKS = D_MODEL // 128


def _store_token_major(ref, val):
    rows = val.shape[0]
    for c in range(LANE_CHUNKS):
        ref[pl.ds(c, rows, stride=LANE_CHUNKS), :] = val[:, c * 128:(c + 1) * 128]


def _load_token_major(ref, rows):
    return jnp.concatenate([ref[pl.ds(c, rows, stride=LANE_CHUNKS), :] for c in range(LANE_CHUNKS)], axis=-1)


def _router_kernel(x_ref, g_ref, sh_ref, sc_ref, whi_ref, wlo_ref, h_ref, info_ref, gw_ref, cnt_ref, h_scr, carry):
    i = pl.program_id(0)

    @pl.when(i == 0)
    def _():
        carry[...] = jnp.zeros_like(carry)

    for r in range(0, TM, 256):
        h = _rms(x_ref[r:r + 256, :]) * g_ref[...]
        h_scr[r:r + 256, :] = h * (1.0 + sc_ref[...]) + sh_ref[...]
    _store_token_major(h_ref, h_scr[...])
    h = h_scr[...]
    h_hi = h.astype(BF16)
    h_lo = (h - h_hi.astype(F32)).astype(BF16)
    logits = (jnp.dot(h_hi, whi_ref[...], preferred_element_type=F32)
              + jnp.dot(h_lo, whi_ref[...], preferred_element_type=F32)
              + jnp.dot(h_hi, wlo_ref[...], preferred_element_type=F32))
    lane = lax.broadcasted_iota(jnp.int32, logits.shape, 1)
    neg = -jnp.inf
    big = jnp.int32(1 << 20)

    def first_argmax(v, vmax):
        return jnp.min(jnp.where(v == vmax, lane, big), axis=-1, keepdims=True)

    lg = jnp.where(lane < N_GROUPS, logits, neg)
    mg = jnp.max(lg, axis=-1, keepdims=True)
    g_w = 1.0 / jnp.sum(jnp.exp(lg - mg), axis=-1, keepdims=True)
    g_idx = first_argmax(lg, mg)
    lo = N_GROUPS + EXPERTS_PER_GROUP * g_idx
    le = jnp.where(jnp.logical_and(lane >= lo, lane < lo + EXPERTS_PER_GROUP), logits, neg)
    m1 = jnp.max(le, axis=-1, keepdims=True)
    i1 = first_argmax(le, m1)
    le2 = jnp.where(lane == i1, neg, le)
    m2 = jnp.max(le2, axis=-1, keepdims=True)
    i2 = first_argmax(le2, m2)
    r21 = jnp.exp(m2 - m1)
    w1 = g_w / (1.0 + r21)
    w2 = g_w * r21 / (1.0 + r21)
    oh = jnp.logical_or(lane == i1, lane == i2)
    ohb = jnp.where(oh, 1.0, 0.0).astype(BF16)
    rr = lax.broadcasted_iota(jnp.int32, (TM, TM), 0)
    cc = lax.broadcasted_iota(jnp.int32, (TM, TM), 1)
    lower = jnp.where(rr > cc, 1.0, 0.0).astype(BF16)
    before = jnp.dot(lower, ohb, preferred_element_type=F32) + carry[...]
    rank1 = jnp.sum(jnp.where(lane == i1, before, 0.0), axis=-1, keepdims=True).astype(jnp.int32)
    rank2 = jnp.sum(jnp.where(lane == i2, before, 0.0), axis=-1, keepdims=True).astype(jnp.int32)
    carry[...] = carry[...] + jnp.sum(ohb.astype(F32), axis=0, keepdims=True)
    cnt_ref[...] = jnp.broadcast_to(carry[...], cnt_ref.shape)
    info = jnp.where(lane == 0, i1 - N_GROUPS, jnp.where(lane == 1, i2 - N_GROUPS,
                     jnp.where(lane == 2, rank1, jnp.where(lane == 3, rank2, 0))))
    info_ref[...] = info
    gw_ref[...] = jnp.where(lane == 0, w1, jnp.where(lane == 1, w2, 0.0))


def _router(x, g, mod, w_router, layer):
    return pl.pallas_call(
        _router_kernel,
        out_shape=[jax.ShapeDtypeStruct((N_TOK * LANE_CHUNKS, 128), F32),
                   jax.ShapeDtypeStruct((N_TOK, 128), jnp.int32),
                   jax.ShapeDtypeStruct((N_TOK, 128), F32),
                   jax.ShapeDtypeStruct((8, 128), F32)],
        grid=(N_TOK // TM,),
        in_specs=[
            pl.BlockSpec((TM, D_MODEL), lambda i: (i, 0)),
            pl.BlockSpec((1, D_MODEL), lambda i: (0, 0)),
            pl.BlockSpec((None, None, 1, D_MODEL), lambda i: (_mod_row(i, TM), 3, 0, 0)),
            pl.BlockSpec((None, None, 1, D_MODEL), lambda i: (_mod_row(i, TM), 4, 0, 0)),
            pl.BlockSpec((None, D_MODEL, 128), lambda i: (layer, 0, 0)),
            pl.BlockSpec((None, D_MODEL, 128), lambda i: (layer, 0, 0)),
        ],
        out_specs=[pl.BlockSpec((TM * LANE_CHUNKS, 128), lambda i: (i, 0)),
                   pl.BlockSpec((TM, 128), lambda i: (i, 0)),
                   pl.BlockSpec((TM, 128), lambda i: (i, 0)),
                   pl.BlockSpec((8, 128), lambda i: (0, 0))],
        scratch_shapes=[pltpu.VMEM((TM, D_MODEL), F32), pltpu.VMEM((1, 128), F32)],
        compiler_params=_cparams(("arbitrary",)),
        name="moe_router",
    )(x, g, mod, mod, *w_router)


FFN_ISSUE_GROUPS = 8
ISSUE_UNROLL = 8


def _ffn_kernel(pos_ref, te_ref, meta_ref, h_hbm, wg_hbm, wu_hbm, wd_hbm, o_ref,
                src, xbuf, wg_f, wu_f, wd_f, wg_b, wu_b, wd_b, sem, wsem, *, layer):
    t = pl.program_id(0)
    nt = meta_ref[0]
    rank_of = lambda e: meta_ref[1 + 2 * N_EXPERTS + e]
    active = lambda k: meta_ref[1 + 3 * N_EXPERTS + k]
    n_active = meta_ref[1 + 4 * N_EXPERTS]

    def weight_copies(e, wslot):
        return [pltpu.make_async_copy(w.at[layer, e], buf.at[wslot], wsem.at[wslot])
                for w, buf in ((wg_hbm, wg_f), (wu_hbm, wu_f), (wd_hbm, wd_f))]

    def row_copy(tile, slot, r):
        tok = src[tile * TE + r]
        return pltpu.make_async_copy(
            h_hbm.at[pl.ds(pl.multiple_of(tok * LANE_CHUNKS, LANE_CHUNKS), LANE_CHUNKS), :],
            xbuf.at[slot, pl.ds(pl.multiple_of(r * LANE_CHUNKS, LANE_CHUNKS), LANE_CHUNKS), :],
            sem.at[slot])

    def gather(tile, slot):
        def body(g, _):
            for k in range(ISSUE_UNROLL):
                row_copy(tile, slot, g * ISSUE_UNROLL + k).start(priority=k % 2)
            return 0
        lax.fori_loop(0, TE // ISSUE_UNROLL, body, 0)

    def wait_tile(slot):
        pltpu.make_async_copy(xbuf.at[slot], xbuf.at[slot], sem.at[slot]).wait()

    @pl.when(t == 0)
    def _():
        for e in range(N_EXPERTS):
            def clear(i, _):
                src[i] = 0
                return 0
            lax.fori_loop(meta_ref[1 + e], meta_ref[1 + N_EXPERTS + e], clear, 0)

        def fill(tok, _):
            src[pos_ref[2 * tok]] = tok
            src[pos_ref[2 * tok + 1]] = tok
            return 0
        lax.fori_loop(0, N_TOK, fill, 0, unroll=8)
        gather(0, 0)
        for cp in weight_copies(te_ref[0], 0):
            cp.start()

    @pl.when(t >= nt)
    def _():
        o_ref[...] = jnp.zeros_like(o_ref)

    @pl.when(t < nt)
    def _():
        slot = t % 2
        wait_tile(slot)

        @pl.when(jnp.logical_or(t == 0, te_ref[t] != te_ref[jnp.maximum(t - 1, 0)]))
        def _():
            k = rank_of(te_ref[t])
            wslot = k % 2
            for cp in weight_copies(te_ref[t], wslot):
                cp.wait()

            @pl.when(k + 1 < n_active)
            def _():
                for cp in weight_copies(active(k + 1), 1 - wslot):
                    cp.start()

            wg_b[...] = wg_f[wslot].astype(BF16)
            wu_b[...] = wu_f[wslot].astype(BF16)
            wd_b[...] = wd_f[wslot].astype(BF16)

        nxt = jnp.minimum(t + 1, nt - 1)
        per = TE // FFN_ISSUE_GROUPS

        def issue(g):
            for r in range(g * per, (g + 1) * per):
                row_copy(nxt, 1 - slot, r).start(priority=r % 2)

        x = _load_token_major(xbuf.at[slot], TE).astype(BF16)
        half = D_EXPERT // 2
        hg, hu = [], []
        for n in range(2):
            issue(n)
            hg.append(jnp.dot(x, wg_b[:, n * half:(n + 1) * half], preferred_element_type=F32))
        for n in range(2):
            issue(2 + n)
            hu.append(jnp.dot(x, wu_b[:, n * half:(n + 1) * half], preferred_element_type=F32))
        hg, hu = jnp.concatenate(hg, axis=1), jnp.concatenate(hu, axis=1)
        act = (hg * jax.nn.sigmoid(hg) * hu).astype(BF16)
        quarter = D_MODEL // 4
        for n in range(4):
            issue(4 + n)
            y = jnp.dot(act, wd_b[:, n * quarter:(n + 1) * quarter], preferred_element_type=F32)
            for c in range(quarter // 128):
                o_ref[pl.ds(n * (quarter // 128) + c, TE, stride=LANE_CHUNKS), :] = y[:, c * 128:(c + 1) * 128]

        @pl.when(t == nt - 1)
        def _():
            wait_tile(1 - slot)


def _expert_ffn(pos_flat, tile_expert, meta, h, w_gate, w_up, w_down, layer):
    hbm = pl.BlockSpec(memory_space=pl.ANY)
    return pl.pallas_call(
        functools.partial(_ffn_kernel, layer=layer),
        out_shape=jax.ShapeDtypeStruct((N_SORT * LANE_CHUNKS, 128), F32),
        grid_spec=pltpu.PrefetchScalarGridSpec(
            num_scalar_prefetch=3,
            grid=(N_ETILES,),
            in_specs=[hbm, hbm, hbm, hbm],
            out_specs=pl.BlockSpec((TE * LANE_CHUNKS, 128), lambda t, pos, te, meta: (t, 0)),
            scratch_shapes=[
                pltpu.SMEM((N_SORT,), jnp.int32),
                pltpu.VMEM((2, TE * LANE_CHUNKS, 128), F32),
                pltpu.VMEM((2, D_MODEL, D_EXPERT), F32),
                pltpu.VMEM((2, D_MODEL, D_EXPERT), F32),
                pltpu.VMEM((2, D_EXPERT, D_MODEL), F32),
                pltpu.VMEM((D_MODEL, D_EXPERT), BF16),
                pltpu.VMEM((D_MODEL, D_EXPERT), BF16),
                pltpu.VMEM((D_EXPERT, D_MODEL), BF16),
                pltpu.SemaphoreType.DMA((2,)),
                pltpu.SemaphoreType.DMA((2,)),
            ],
        ),
        compiler_params=pltpu.CompilerParams(dimension_semantics=("arbitrary",),
                                             vmem_limit_bytes=56 * 1024 * 1024),
        name="moe_expert_ffn",
    )(pos_flat, tile_expert, meta, h, w_gate, w_up, w_down)


TC = 256


def _combine_kernel(pos_ref, x_ref, gate_ref, gw_ref, fn_ref, ys_hbm, o_ref, buf_a, buf_b, sem, *, final_norm):
    i = pl.program_id(0)
    slot = i % 2

    def slab(ref, row):
        return ref.at[pl.ds(pl.multiple_of(row * LANE_CHUNKS, LANE_CHUNKS), LANE_CHUNKS), :]

    def gather(tile, slot):
        def body(g, _):
            for k in range(ISSUE_UNROLL):
                r = g * ISSUE_UNROLL + k
                tok = tile * TC + r
                pltpu.make_async_copy(slab(ys_hbm, pos_ref[2 * tok]), slab(buf_a.at[slot], r),
                                      sem.at[slot, 0]).start(priority=0)
                pltpu.make_async_copy(slab(ys_hbm, pos_ref[2 * tok + 1]), slab(buf_b.at[slot], r),
                                      sem.at[slot, 1]).start(priority=1)
            return 0
        lax.fori_loop(0, TC // ISSUE_UNROLL, body, 0)

    @pl.when(i == 0)
    def _():
        gather(0, 0)

    @pl.when(i + 1 < pl.num_programs(0))
    def _():
        gather(i + 1, 1 - slot)

    pltpu.make_async_copy(buf_a.at[slot], buf_a.at[slot], sem.at[slot, 0]).wait()
    pltpu.make_async_copy(buf_b.at[slot], buf_b.at[slot], sem.at[slot, 1]).wait()
    w0 = jnp.broadcast_to(gw_ref[:, 0:1], (TC, 128))
    w1 = jnp.broadcast_to(gw_ref[:, 1:2], (TC, 128))
    sq = jnp.zeros((TC, 128), F32)
    for c in range(LANE_CHUNKS):
        cols = slice(c * 128, (c + 1) * 128)
        rows = pl.ds(c, TC, stride=LANE_CHUNKS)
        y = x_ref[:, cols] + gate_ref[:, cols] * (w0 * buf_a[slot, rows, :] + w1 * buf_b[slot, rows, :])
        o_ref[:, cols] = y
        if final_norm:
            sq = sq + y * y
    if final_norm:
        inv = lax.rsqrt(jnp.sum(sq, axis=-1, keepdims=True) / D_MODEL + EPS)
        for c in range(LANE_CHUNKS):
            cols = slice(c * 128, (c + 1) * 128)
            o_ref[:, cols] = o_ref[:, cols] * inv * fn_ref[:, cols]


def _combine(pos_flat, x, mod, gw, ys, final_g):
    final_norm = final_g is not None
    fn = final_g if final_norm else jnp.ones((1, D_MODEL), F32)
    rows = N_LAT if final_norm else N_TOK
    return pl.pallas_call(
        functools.partial(_combine_kernel, final_norm=final_norm),
        out_shape=jax.ShapeDtypeStruct((rows, D_MODEL), F32),
        grid_spec=pltpu.PrefetchScalarGridSpec(
            num_scalar_prefetch=1,
            grid=(rows // TC,),
            in_specs=[
                pl.BlockSpec((TC, D_MODEL), lambda i, pos: (i, 0)),
                pl.BlockSpec((None, None, 1, D_MODEL), lambda i, pos: (_mod_row(i, TC), 5, 0, 0)),
                pl.BlockSpec((TC, 128), lambda i, pos: (i, 0)),
                pl.BlockSpec((1, D_MODEL), lambda i, pos: (0, 0)),
                pl.BlockSpec(memory_space=pl.ANY),
            ],
            out_specs=pl.BlockSpec((TC, D_MODEL), lambda i, pos: (i, 0)),
            scratch_shapes=[pltpu.VMEM((2, TC * LANE_CHUNKS, 128), F32), pltpu.VMEM((2, TC * LANE_CHUNKS, 128), F32),
                            pltpu.SemaphoreType.DMA((2, 2))],
        ),
        compiler_params=_cparams(("arbitrary",)),
        name="moe_combine",
    )(pos_flat, x, mod, gw, fn, ys)


def _moe(x, g, mod, w_router, w_gate, w_up, w_down, layer, final_g):
    h, info, gw, cnt = _router(x, g, mod, w_router, layer)
    counts = cnt[0, N_GROUPS:N_GROUPS + N_EXPERTS].astype(jnp.int32)
    padded = ((counts + TE - 1) // TE) * TE
    ends = jnp.cumsum(padded)
    starts = ends - padded
    experts = jnp.arange(N_EXPERTS, dtype=jnp.int32)
    start_of = jnp.sum(jnp.where(info[:, 0:2, None] == experts, starts, 0), axis=-1)
    pos_flat = (start_of + info[:, 2:4]).reshape(-1)
    tile_ends = ends // TE
    num_tiles = tile_ends[-1]
    tiles = jnp.minimum(jnp.arange(N_ETILES, dtype=jnp.int32), num_tiles - 1)
    tile_expert = jnp.sum((tile_ends[None, :] <= tiles[:, None]).astype(jnp.int32), axis=-1)
    owns = padded > 0
    rank = jnp.cumsum(owns.astype(jnp.int32)) - 1
    active = jnp.sum(jnp.where(jnp.logical_and(owns[None, :], rank[None, :] == experts[:, None]), experts[None, :], 0),
                     axis=-1)
    meta = jnp.concatenate([num_tiles[None], starts + counts, ends, rank, active,
                            jnp.sum(owns.astype(jnp.int32))[None]]).astype(jnp.int32)
    ys = _expert_ffn(pos_flat, tile_expert, meta, h, w_gate, w_up, w_down, layer)
    return _combine(pos_flat, x, mod, gw, ys, final_g)


def kernel(x, c, ctx, c_ctx, mod_w, mod_b, norm_mix, norm_ffn, ab_w_in, ab_q_norm, ab_k_norm, ab_conv_w, ab_w_out, cd_w_in, cd_q_norm, cd_kv_norm, cd_w_uq, cd_w_ukv, s5_lam_re, s5_lam_im, s5_log_dt, s5_b_re, s5_b_im, s5_c_re, s5_c_im, s5_d, s5_w_glu, cd_w_out, moe_w_group, moe_w_expert, moe_w_gate, moe_w_up, moe_w_down, final_norm):
    cc = jnp.concatenate([c, c_ctx[None, :], jnp.zeros((8 - BATCH - 1, D_MODEL), F32)], axis=0)
    mods = _modulation(cc, mod_w, mod_b).reshape(DEPTH, 8, N_MOD, 1, D_MODEL)
    xs = jnp.concatenate([x.reshape(N_LAT, D_MODEL), ctx.reshape(N_CTX, D_MODEL)], axis=0)
    cos_a, sin_a = _rope_tables(HEAD_DIM)
    cos_c, sin_c = _rope_tables(C_ROPE)

    ab_in_b, ab_out_b = ab_w_in.astype(BF16), ab_w_out.astype(BF16)
    cd_out_b, ukv_b, glu_b = cd_w_out.astype(BF16), cd_w_ukv.astype(BF16), s5_w_glu.astype(BF16)
    a, b_ = C_Q_RANK + C_KV_RANK, C_Q_RANK + C_KV_RANK + C_ROPE
    pad = jnp.zeros(cd_w_in.shape[:2] + (CD_IN_PAD - cd_w_in.shape[2],), F32)
    cd_in_b = jnp.concatenate([cd_w_in[..., :a], cd_w_in[..., b_:], cd_w_in[..., a:b_], pad], axis=-1).astype(BF16)
    w_uq = cd_w_uq.reshape(-1, C_Q_RANK, C_HEADS, C_NOPE + C_ROPE)
    uq_b = jnp.concatenate([w_uq[..., :C_NOPE].reshape(-1, C_Q_RANK, C_HEADS * C_NOPE),
                            w_uq[..., C_NOPE:].reshape(-1, C_Q_RANK, C_HEADS * C_ROPE)], axis=-1).astype(BF16)

    mats = jax.vmap(_s5_matrices)(s5_lam_re, s5_lam_im, s5_log_dt, s5_b_re, s5_b_im, s5_c_re, s5_c_im)
    w_router = jnp.concatenate(
        [moe_w_group, jnp.transpose(moe_w_expert, (0, 2, 1, 3)).reshape(DEPTH, D_MODEL, N_EXPERTS),
         jnp.zeros((DEPTH, D_MODEL, 128 - N_GROUPS - N_EXPERTS), F32)], axis=-1)
    w_router_hi = w_router.astype(BF16)
    w_router = (w_router_hi, (w_router - w_router_hi.astype(F32)).astype(BF16))

    for i in range(DEPTH):
        j = i // 2
        mod = mods[i]
        if i % 2 == 0:
            p = _norm_linear(xs, 0, D_MODEL, norm_mix[i][None, :], mod, 0, ab_in_b, j, 2048)
            k, vext = _gqa_kv(p, ab_k_norm[j][None, :], cos_a, sin_a)
            o_lat, o_ctx = _gqa_attention(p, k, vext, ab_q_norm[j][None, :], cos_a, sin_a)
            side = _gated_conv(p, ab_conv_w[j])
            w_out = ab_out_b
        else:
            p = _norm_linear(xs, 0, D_MODEL, norm_mix[i][None, :], mod, 0, cd_in_b, j, CD_IN_PAD)
            q = _norm_linear(p, 0, C_Q_RANK, cd_q_norm[j][None, :], None, 0, uq_b, j, uq_b.shape[2])
            kcat, vext = _mla_kv(p, cd_kv_norm[j][None, :], ukv_b, j, cos_c, sin_c)
            o_lat, o_ctx = _mla_attention(q, kcat, vext, cos_c, sin_c)
            y = _s5(p, mats, j, s5_d.astype(F32)[:, None, :])
            side = _s5_glu(y, glu_b, j)
            w_out = cd_out_b
        xs = _out_linear(o_lat, o_ctx, side, w_out, j, xs, mod, 2)
        xs = _moe(xs, norm_ffn[i][None, :], mod, w_router, moe_w_gate, moe_w_up, moe_w_down, i,
                  final_norm[None, :] if i == DEPTH - 1 else None)
    return xs.reshape(BATCH, SEQ, D_MODEL)
```

```python
import functools
import math

import numpy as np
import jax
import jax.numpy as jnp
from jax import lax
from jax.experimental import pallas as pl
from jax.experimental.pallas import tpu as pltpu

F32 = jnp.float32
BF16 = jnp.bfloat16

D_MODEL = 2048
BATCH = 4
SEQ = 2048
DEPTH = 4
GRID_W = 64
CTX_LEN = 256
ROPE_THETA = 10000.0
EPS = 1e-6
N_MOD = 6
HEAD_DIM = 128
A_Q_HEADS = 12
A_KV_HEADS = 4
A_GROUP = A_Q_HEADS // A_KV_HEADS
A_WIDTH = A_Q_HEADS * HEAD_DIM
A_KV_WIDTH = A_KV_HEADS * HEAD_DIM
B_WIDTH = 512
AB_IN = A_WIDTH + 2 * A_KV_WIDTH + 3 * B_WIDTH
C_HEADS = 12
C_NOPE = 128
C_ROPE = 64
C_V = 128
C_Q_RANK = 512
C_KV_RANK = 256
C_WIDTH = C_HEADS * C_V
D_WIDTH = 512
S5_GROUP = 16
S5_GROUPS = D_WIDTH // S5_GROUP
S5_STATE = 64
N_GROUPS = 4
EXPERTS_PER_GROUP = 4
N_EXPERTS = N_GROUPS * EXPERTS_PER_GROUP
D_EXPERT = 512

N_LAT = BATCH * SEQ
N_CTX = BATCH * CTX_LEN
N_TOK = N_LAT + N_CTX
CTX_ROW = BATCH
CD_IN_PAD = 1536

TM = 512
TQ = 512
TE = 256
N_SORT = 2 * N_TOK + N_EXPERTS * TE
N_ETILES = N_SORT // TE
S5_CHUNK = 16
S5_CTX_CHUNKS = CTX_LEN // S5_CHUNK
S5_ROWS = N_TOK // S5_CHUNK
S5_GSUB = 128 // S5_GROUP
S5_QUADS = S5_GROUPS // S5_GSUB
VMEM_LIMIT = 48 * 1024 * 1024


def _cparams(sem):
    return pltpu.CompilerParams(dimension_semantics=sem, vmem_limit_bytes=VMEM_LIMIT)


def _mod_row(tile, tile_rows):
    r0 = tile * tile_rows
    return jnp.where(r0 >= N_LAT, CTX_ROW, r0 // SEQ)


def _rope_block(tile, tile_rows):
    r0 = tile * tile_rows
    return jnp.where(r0 >= N_LAT, SEQ // tile_rows, (r0 % SEQ) // tile_rows)


def _rms(x):
    return x * lax.rsqrt(jnp.mean(x * x, axis=-1, keepdims=True) + EPS)


def _mod_kernel(cc_ref, w_ref, b_ref, o_ref):
    cc = cc_ref[...]
    s = (cc * jax.nn.sigmoid(cc)).astype(BF16)
    o_ref[...] = jnp.dot(s, w_ref[...].astype(BF16), preferred_element_type=F32) + b_ref[...]


def _modulation(cc, mod_w, mod_b):
    tn = 1024
    nout = N_MOD * D_MODEL
    return pl.pallas_call(
        _mod_kernel,
        out_shape=jax.ShapeDtypeStruct((DEPTH, 8, nout), F32),
        grid=(DEPTH, nout // tn),
        in_specs=[
            pl.BlockSpec((8, D_MODEL), lambda l, j: (0, 0)),
            pl.BlockSpec((None, D_MODEL, tn), lambda l, j: (l, 0, j)),
            pl.BlockSpec((None, 1, tn), lambda l, j: (l, 0, j)),
        ],
        out_specs=pl.BlockSpec((None, 8, tn), lambda l, j: (l, 0, j)),
        compiler_params=_cparams(("arbitrary", "arbitrary")),
        name="modulation",
    )(cc, mod_w, mod_b.reshape(DEPTH, 1, nout))


def _modulated_norm(x, g, sh, sc):
    return _rms(x) * g * (1.0 + sc) + sh


def _embed_kernel(xl_ref, xc_ref, g_ref, sh_ref, sc_ref, xs_ref, h_ref):
    is_ctx = pl.program_id(0) >= N_LAT // TM
    for r in range(0, TM, 256):
        x = jnp.where(is_ctx, xc_ref[r:r + 256, :], xl_ref[r:r + 256, :])
        xs_ref[r:r + 256, :] = x
        h_ref[r:r + 256, :] = _modulated_norm(x, g_ref[...], sh_ref[...], sc_ref[...]).astype(BF16)


def _embed(x_lat, x_ctx, g, mod):
    lat_tiles = N_LAT // TM
    return pl.pallas_call(
        _embed_kernel,
        out_shape=[jax.ShapeDtypeStruct((N_TOK, D_MODEL), F32), jax.ShapeDtypeStruct((N_TOK, D_MODEL), BF16)],
        grid=(N_TOK // TM,),
        in_specs=[
            pl.BlockSpec((TM, D_MODEL), lambda i: (jnp.minimum(i, lat_tiles - 1), 0)),
            pl.BlockSpec((TM, D_MODEL), lambda i: (jnp.maximum(i - lat_tiles, 0), 0)),
            pl.BlockSpec((1, D_MODEL), lambda i: (0, 0)),
            pl.BlockSpec((None, None, 1, D_MODEL), lambda i: (_mod_row(i, TM), 0, 0, 0)),
            pl.BlockSpec((None, None, 1, D_MODEL), lambda i: (_mod_row(i, TM), 1, 0, 0)),
        ],
        out_specs=[pl.BlockSpec((TM, D_MODEL), lambda i: (i, 0))] * 2,
        compiler_params=_cparams(("arbitrary",)),
        name="embed_norm",
    )(x_lat, x_ctx, g, mod, mod)


def _linear_kernel(h_ref, w_ref, o_ref):
    o_ref[...] = jnp.dot(h_ref[...], w_ref[...], preferred_element_type=F32)


def _linear(h, w, layer, tn):
    kdim, nout = w.shape[1], w.shape[2]
    return pl.pallas_call(
        _linear_kernel,
        out_shape=jax.ShapeDtypeStruct((N_TOK, nout), F32),
        grid=(N_TOK // TM, nout // tn),
        in_specs=[
            pl.BlockSpec((TM, kdim), lambda i, j: (i, 0)),
            pl.BlockSpec((None, kdim, tn), lambda i, j: (layer, 0, j)),
        ],
        out_specs=pl.BlockSpec((TM, tn), lambda i, j: (i, j)),
        compiler_params=_cparams(("arbitrary", "arbitrary")),
        name="linear",
    )(h, w)


def _norm_linear_kernel(x_ref, g_ref, w_ref, o_ref):
    h = (_rms(x_ref[...]) * g_ref[...]).astype(BF16)
    o_ref[...] = jnp.dot(h, w_ref[...], preferred_element_type=F32)


def _norm_linear(x, xcol, kdim, g, w, layer):
    nout = w.shape[2]
    return pl.pallas_call(
        _norm_linear_kernel,
        out_shape=jax.ShapeDtypeStruct((N_TOK, nout), F32),
        grid=(N_TOK // TM,),
        in_specs=[
            pl.BlockSpec((TM, kdim), lambda i: (i, xcol)),
            pl.BlockSpec((1, kdim), lambda i: (0, 0)),
            pl.BlockSpec((None, kdim, nout), lambda i: (layer, 0, 0)),
        ],
        out_specs=pl.BlockSpec((TM, nout), lambda i: (i, 0)),
        compiler_params=_cparams(("arbitrary",)),
        name="norm_linear",
    )(x, g, w)


def _out_linear_kernel(a1l_ref, a1c_ref, a2_ref, w1_ref, w2_ref, x_ref, gate_ref, o_ref):
    a1 = jnp.where(pl.program_id(0) >= N_LAT // TM, a1c_ref[...], a1l_ref[...])
    acc = jnp.dot(a1, w1_ref[...], preferred_element_type=F32)
    acc = acc + jnp.dot(a2_ref[...], w2_ref[...], preferred_element_type=F32)
    o_ref[...] = x_ref[...] + gate_ref[...] * acc


def _out_linear(a1_lat, a1_ctx, a2, w, layer, x, mod, which):
    tn = D_MODEL
    k1, k2 = a1_lat.shape[1], a2.shape[1]
    lat_tiles = N_LAT // TM
    return pl.pallas_call(
        _out_linear_kernel,
        out_shape=jax.ShapeDtypeStruct((N_TOK, D_MODEL), F32),
        grid=(N_TOK // TM, D_MODEL // tn),
        in_specs=[
            pl.BlockSpec((TM, k1), lambda i, j: (jnp.minimum(i, lat_tiles - 1), 0)),
            pl.BlockSpec((TM, k1), lambda i, j: (jnp.maximum(i - lat_tiles, 0), 0)),
            pl.BlockSpec((TM, k2), lambda i, j: (i, 0)),
            pl.BlockSpec((None, k1, tn), lambda i, j: (layer, 0, j)),
            pl.BlockSpec((None, k2, tn), lambda i, j: (layer, k1 // k2, j)),
            pl.BlockSpec((TM, tn), lambda i, j: (i, j)),
            pl.BlockSpec((None, None, 1, tn), lambda i, j: (_mod_row(i, TM), which, 0, j)),
        ],
        out_specs=pl.BlockSpec((TM, tn), lambda i, j: (i, j)),
        compiler_params=_cparams(("arbitrary", "arbitrary")),
        name="out_linear",
    )(a1_lat, a1_ctx, a2, w, w, x, mod)


def _rope_tables(rot_dim):
    rows = SEQ // GRID_W
    row_ids = np.repeat(np.arange(rows, dtype=np.float32), GRID_W)
    col_ids = np.tile(np.arange(GRID_W, dtype=np.float32), rows)
    d_axis = rot_dim // 2
    inv = (np.float32(ROPE_THETA) ** (-np.arange(0, d_axis, 2, dtype=np.float32) / np.float32(d_axis))).astype(np.float32)
    ang = np.concatenate([row_ids[:, None] * inv, col_ids[:, None] * inv], axis=-1).astype(np.float32)
    cos, sin = np.cos(ang).astype(np.float32), np.sin(ang).astype(np.float32)
    reps = 128 // rot_dim
    cos_f = np.tile(np.concatenate([cos, cos], axis=-1), (1, reps))
    sin_f = np.tile(np.concatenate([-sin, sin], axis=-1), (1, reps))
    cos_f = np.concatenate([cos_f, np.ones((512, 128), np.float32)], axis=0)
    sin_f = np.concatenate([sin_f, np.zeros((512, 128), np.float32)], axis=0)
    return jnp.asarray(cos_f), jnp.asarray(sin_f)


def _rope128(x, cos, sin):
    return x * cos + pltpu.roll(x, 64, 1) * sin


def _rope64(x, cos, sin):
    lane = lax.broadcasted_iota(jnp.int32, x.shape, 1)
    swapped = jnp.where((lane % 64) < 32, pltpu.roll(x, 96, 1), pltpu.roll(x, 32, 1))
    return x * cos + swapped * sin


def _ones_column(rows):
    lane = lax.broadcasted_iota(jnp.int32, (rows, 128), 1)
    return jnp.where(lane == 0, 1.0, 0.0).astype(BF16)


def _gqa_kv_kernel(k_ref, v_ref, kn_ref, cos_ref, sin_ref, ko_ref, vo_ref):
    cos, sin = cos_ref[...], sin_ref[...]
    ones = _ones_column(k_ref.shape[0])
    for h in range(A_KV_HEADS):
        k = _rms(k_ref[:, h * 128:(h + 1) * 128]) * kn_ref[...]
        ko_ref[:, h * 128:(h + 1) * 128] = _rope128(k, cos, sin).astype(BF16)
        vo_ref[:, h * 256:h * 256 + 128] = v_ref[:, h * 128:(h + 1) * 128].astype(BF16)
        vo_ref[:, h * 256 + 128:(h + 1) * 256] = ones


def _gqa_kv(p, k_norm, cos, sin):
    t = 512
    return pl.pallas_call(
        _gqa_kv_kernel,
        out_shape=[jax.ShapeDtypeStruct((N_TOK, A_KV_WIDTH), BF16),
                   jax.ShapeDtypeStruct((N_TOK, 2 * A_KV_WIDTH), BF16)],
        grid=(N_TOK // t,),
        in_specs=[
            pl.BlockSpec((t, A_KV_WIDTH), lambda i: (i, A_WIDTH // A_KV_WIDTH)),
            pl.BlockSpec((t, A_KV_WIDTH), lambda i: (i, A_WIDTH // A_KV_WIDTH + 1)),
            pl.BlockSpec((1, 128), lambda i: (0, 0)),
            pl.BlockSpec((t, 128), lambda i: (_rope_block(i, t), 0)),
            pl.BlockSpec((t, 128), lambda i: (_rope_block(i, t), 0)),
        ],
        out_specs=[pl.BlockSpec((t, A_KV_WIDTH), lambda i: (i, 0)),
                   pl.BlockSpec((t, 2 * A_KV_WIDTH), lambda i: (i, 0))],
        compiler_params=_cparams(("arbitrary",)),
        name="gqa_kv",
    )(p, p, k_norm, cos, sin)


LOG2E = math.log2(math.e)
KEY_PIECE = 1024


def _softmax_pv(q, keys, vals):
    dn = (((1,), (1,)), ((), ()))
    m = acc = None
    for k, v in zip(keys, vals):
        s = lax.dot_general(q, k, dn, preferred_element_type=F32)
        m_piece = jnp.max(s, axis=-1, keepdims=True)
        if m is None:
            m_new = m_piece
        else:
            m_new = jnp.maximum(m, m_piece)
            acc = acc * jnp.exp2(m - m_new)
        part = jnp.dot(jnp.exp2(s - m_new).astype(BF16), v, preferred_element_type=F32)
        acc = part if acc is None else acc + part
        m = m_new
    return acc[:, :128] / acc[:, 128:129]


def _attn_qrow(b, t):
    nq = SEQ // TQ
    return jnp.where(t == nq, (N_LAT + b * CTX_LEN) // TQ, b * nq + t)


def _ctx_half():
    return pl.multiple_of((pl.program_id(0) % (TQ // CTX_LEN)) * CTX_LEN, CTX_LEN)


def _gqa_attn_kernel(q_ref, qn_ref, cos_ref, sin_ref, kc_ref, vc_ref, kl_ref, vl_ref, ol_ref, oc_ref):
    scale = HEAD_DIM ** -0.5 * LOG2E
    is_ctx = pl.program_id(2) == SEQ // TQ

    def query(rows, g):
        q = _rms(q_ref[rows, g * 128:(g + 1) * 128]) * qn_ref[...]
        return (_rope128(q, cos_ref[rows, :], sin_ref[rows, :]) * scale).astype(BF16)

    @pl.when(jnp.logical_not(is_ctx))
    def _():
        pieces = [slice(j * KEY_PIECE, (j + 1) * KEY_PIECE) for j in range(SEQ // KEY_PIECE)]
        qs = [query(slice(None), g) for g in range(A_GROUP)]
        for g in range(A_GROUP):
            o = _softmax_pv(qs[g], [kc_ref[...]] + [kl_ref[r, :] for r in pieces],
                            [vc_ref[...]] + [vl_ref[r, :] for r in pieces])
            ol_ref[:, g * 128:(g + 1) * 128] = o.astype(BF16)

    @pl.when(is_ctx)
    def _():
        rows = pl.ds(_ctx_half(), CTX_LEN)
        for g in range(A_GROUP):
            oc_ref[:, g * 128:(g + 1) * 128] = _softmax_pv(query(rows, g), [kc_ref[...]], [vc_ref[...]]).astype(BF16)


def _gqa_attention(p, k, vext, q_norm, cos, sin):
    qw = A_GROUP * HEAD_DIM
    nq = SEQ // TQ
    ctx_blk = N_LAT // CTX_LEN
    rope = pl.BlockSpec((TQ, 128), lambda b, h, t: (_rope_block(_attn_qrow(b, t), TQ), 0))
    return pl.pallas_call(
        _gqa_attn_kernel,
        out_shape=[jax.ShapeDtypeStruct((N_LAT, A_WIDTH), BF16), jax.ShapeDtypeStruct((N_CTX, A_WIDTH), BF16)],
        grid=(BATCH, A_KV_HEADS, nq + 1),
        in_specs=[
            pl.BlockSpec((TQ, qw), lambda b, h, t: (_attn_qrow(b, t), h)),
            pl.BlockSpec((1, 128), lambda b, h, t: (0, 0)),
            rope, rope,
            pl.BlockSpec((CTX_LEN, 128), lambda b, h, t: (ctx_blk + b, h)),
            pl.BlockSpec((CTX_LEN, 256), lambda b, h, t: (ctx_blk + b, h)),
            pl.BlockSpec((SEQ, 128), lambda b, h, t: (b, h)),
            pl.BlockSpec((SEQ, 256), lambda b, h, t: (b, h)),
        ],
        out_specs=[pl.BlockSpec((TQ, qw), lambda b, h, t: (b * nq + jnp.minimum(t, nq - 1), h)),
                   pl.BlockSpec((CTX_LEN, qw), lambda b, h, t: (b, h))],
        compiler_params=_cparams(("arbitrary",) * 3),
        name="gqa_attn",
    )(p, q_norm, cos, sin, k, vext, k, vext)


def _conv_kernel(bg_ref, cg_ref, ug_ref, cgp_ref, ugp_ref, cgn_ref, ugn_ref, w_ref, o_ref, *, rows):
    i = pl.program_id(0)
    per_seq = SEQ // rows
    is_ctx = i >= N_LAT // rows
    is_start = jnp.logical_or(is_ctx, i % per_seq == 0)
    is_end = jnp.logical_or(is_ctx, i % per_seq == per_seq - 1)
    m = cg_ref[...] * ug_ref[...]
    m_prev = jnp.where(is_start, 0.0, cgp_ref[7:8, :] * ugp_ref[7:8, :])
    m_next = jnp.where(is_end, 0.0, cgn_ref[0:1, :] * ugn_ref[0:1, :])
    row = lax.broadcasted_iota(jnp.int32, m.shape, 0)
    down = jnp.where(row == 0, m_prev, pltpu.roll(m, 1, 0))
    up = jnp.where(row == rows - 1, m_next, pltpu.roll(m, rows - 1, 0))
    conv = down * w_ref[0:1, :] + m * w_ref[1:2, :] + up * w_ref[2:3, :]
    o_ref[...] = (bg_ref[...] * conv).astype(BF16)


def _gated_conv(p, conv_w):
    rows = CTX_LEN
    base = (A_WIDTH + 2 * A_KV_WIDTH) // B_WIDTH
    halo = rows // 8
    last = N_TOK // 8 - 1
    main = lambda c: pl.BlockSpec((rows, B_WIDTH), lambda i: (i, base + c))
    prev = lambda c: pl.BlockSpec((8, B_WIDTH), lambda i: (jnp.maximum(i * halo - 1, 0), base + c))
    nxt = lambda c: pl.BlockSpec((8, B_WIDTH), lambda i: (jnp.minimum((i + 1) * halo, last), base + c))
    return pl.pallas_call(
        functools.partial(_conv_kernel, rows=rows),
        out_shape=jax.ShapeDtypeStruct((N_TOK, B_WIDTH), BF16),
        grid=(N_TOK // rows,),
        in_specs=[main(0), main(1), main(2), prev(1), prev(2), nxt(1), nxt(2),
                  pl.BlockSpec((3, B_WIDTH), lambda i: (0, 0))],
        out_specs=pl.BlockSpec((rows, B_WIDTH), lambda i: (i, 0)),
        compiler_params=_cparams(("arbitrary",)),
        name="gated_conv",
    )(p, p, p, p, p, p, p, conv_w)


def _mla_kv_kernel(ckv_ref, g_ref, w_ref, kr_ref, cos_ref, sin_ref, ko_ref, vo_ref):
    ckv = (_rms(ckv_ref[...]) * g_ref[...]).astype(BF16)
    kv = jnp.dot(ckv, w_ref[...], preferred_element_type=F32)
    x = kr_ref[...]
    lane = lax.broadcasted_iota(jnp.int32, x.shape, 1)
    r = jnp.where(lane < C_ROPE, _rope64(x, cos_ref[...], sin_ref[...]), 0.0)
    kr_low = r.astype(BF16)
    kr_high = pltpu.roll(r, 64, 1).astype(BF16)
    ones = _ones_column(x.shape[0])
    for h in range(C_HEADS):
        ko_ref[:, h * 256:h * 256 + 128] = kv[:, h * 256:h * 256 + 128].astype(BF16)
        ko_ref[:, h * 256 + 128:(h + 1) * 256] = kr_low if h % 2 == 0 else kr_high
        vo_ref[:, h * 256:h * 256 + 128] = kv[:, h * 256 + 128:(h + 1) * 256].astype(BF16)
        vo_ref[:, h * 256 + 128:(h + 1) * 256] = ones


def _mla_kv(p, kv_norm, w_ukv, layer, cos, sin):
    t = 512
    col = (C_Q_RANK + C_KV_RANK + D_WIDTH) // 128
    width = C_HEADS * 256
    return pl.pallas_call(
        _mla_kv_kernel,
        out_shape=[jax.ShapeDtypeStruct((N_TOK, width), BF16)] * 2,
        grid=(N_TOK // t,),
        in_specs=[
            pl.BlockSpec((t, C_KV_RANK), lambda i: (i, C_Q_RANK // C_KV_RANK)),
            pl.BlockSpec((1, C_KV_RANK), lambda i: (0, 0)),
            pl.BlockSpec((None, C_KV_RANK, width), lambda i: (layer, 0, 0)),
            pl.BlockSpec((t, 128), lambda i: (i, col)),
            pl.BlockSpec((t, 128), lambda i: (_rope_block(i, t), 0)),
            pl.BlockSpec((t, 128), lambda i: (_rope_block(i, t), 0)),
        ],
        out_specs=[pl.BlockSpec((t, width), lambda i: (i, 0))] * 2,
        compiler_params=_cparams(("arbitrary",)),
        name="mla_kv",
    )(p, kv_norm, w_ukv, p, cos, sin)


def _mla_attn_kernel(qn_ref, qr_ref, cos_ref, sin_ref, kc_ref, vc_ref, kl_ref, vl_ref, ol_ref, oc_ref):
    scale = (C_NOPE + C_ROPE) ** -0.5 * LOG2E
    is_ctx = pl.program_id(2) == SEQ // TQ

    def queries(rows):
        qr = _rope64(qr_ref[rows, :], cos_ref[rows, :], sin_ref[rows, :])
        lane = lax.broadcasted_iota(jnp.int32, qr.shape, 1)
        out = []
        for hh in range(2):
            sel = (lane < 64) if hh == 0 else (lane >= 64)
            q = jnp.concatenate([qn_ref[rows, hh * 128:(hh + 1) * 128], jnp.where(sel, qr, 0.0)], axis=1)
            out.append((q * scale).astype(BF16))
        return out

    @pl.when(jnp.logical_not(is_ctx))
    def _():
        for hh, q in enumerate(queries(slice(None))):
            blk = slice(hh * 256, (hh + 1) * 256)
            pieces = [slice(j * KEY_PIECE, (j + 1) * KEY_PIECE) for j in range(SEQ // KEY_PIECE)]
            o = _softmax_pv(q, [kc_ref[:, blk]] + [kl_ref[r, blk] for r in pieces],
                            [vc_ref[:, blk]] + [vl_ref[r, blk] for r in pieces])
            ol_ref[:, hh * 128:(hh + 1) * 128] = o.astype(BF16)

    @pl.when(is_ctx)
    def _():
        for hh, q in enumerate(queries(pl.ds(_ctx_half(), CTX_LEN))):
            blk = slice(hh * 256, (hh + 1) * 256)
            oc_ref[:, hh * 128:(hh + 1) * 128] = _softmax_pv(q, [kc_ref[:, blk]], [vc_ref[:, blk]]).astype(BF16)


def _mla_attention(q, kcat, vext, cos, sin):
    nq = SEQ // TQ
    ctx_blk = N_LAT // CTX_LEN
    rope_col = C_HEADS * C_NOPE // 128
    qrow = _attn_qrow
    rope = pl.BlockSpec((TQ, 128), lambda b, h, t: (_rope_block(qrow(b, t), TQ), 0))
    ctx = pl.BlockSpec((CTX_LEN, 512), lambda b, h, t: (ctx_blk + b, h))
    lat = pl.BlockSpec((SEQ, 512), lambda b, h, t: (b, h))
    return pl.pallas_call(
        _mla_attn_kernel,
        out_shape=[jax.ShapeDtypeStruct((N_LAT, C_WIDTH), BF16), jax.ShapeDtypeStruct((N_CTX, C_WIDTH), BF16)],
        grid=(BATCH, C_HEADS // 2, nq + 1),
        in_specs=[
            pl.BlockSpec((TQ, 256), lambda b, h, t: (qrow(b, t), h)),
            pl.BlockSpec((TQ, 128), lambda b, h, t: (qrow(b, t), rope_col + h)),
            rope, rope, ctx, ctx, lat, lat,
        ],
        out_specs=[pl.BlockSpec((TQ, 256), lambda b, h, t: (b * nq + jnp.minimum(t, nq - 1), h)),
                   pl.BlockSpec((CTX_LEN, 256), lambda b, h, t: (b, h))],
        compiler_params=_cparams(("arbitrary",) * 3),
        name="mla_attn",
    )(q, q, cos, sin, kcat, vext, kcat, vext)


def _s5_matrices(lam_re, lam_im, log_dt, b_re, b_im, c_re, c_im):
    hi = lax.Precision.HIGHEST
    lam_re, lam_im = lam_re.astype(F32), lam_im.astype(F32)
    dt = jnp.exp(log_dt.astype(F32))[..., None]
    ks = jnp.arange(S5_CHUNK + 1, dtype=F32)[:, None, None, None]
    mag = jnp.exp(lam_re[None] * dt[None] * ks)
    ang = lam_im[None] * dt[None] * ks
    pw_re, pw_im = mag * jnp.cos(ang), mag * jnp.sin(ang)
    a_re, a_im = pw_re[1], pw_im[1]
    den = lam_re * lam_re + lam_im * lam_im
    f_re = ((a_re - 1.0) * lam_re + a_im * lam_im) / den
    f_im = (a_im * lam_re - (a_re - 1.0) * lam_im) / den
    b_re, b_im = b_re.astype(F32), b_im.astype(F32)
    bb_re = f_re[..., None] * b_re - f_im[..., None] * b_im
    bb_im = f_re[..., None] * b_im + f_im[..., None] * b_re
    c_re, c_im = c_re.astype(F32), c_im.astype(F32)

    ab_re = pw_re[:S5_CHUNK, ..., None] * bb_re[None] - pw_im[:S5_CHUNK, ..., None] * bb_im[None]
    ab_im = pw_re[:S5_CHUNK, ..., None] * bb_im[None] + pw_im[:S5_CHUNK, ..., None] * bb_re[None]
    kern = (jnp.einsum('dgcn,ldgne->ldgce', c_re, ab_re, precision=hi)
            - jnp.einsum('dgcn,ldgne->ldgce', c_im, ab_im, precision=hi))
    t_idx = np.arange(S5_CHUNK)
    sub = lambda m, ax: m.reshape(m.shape[:ax] + (S5_QUADS, S5_GSUB) + m.shape[ax + 1:])
    klag = jnp.transpose(sub(kern, 2), (1, 2, 0, 5, 3, 4)).reshape(2, S5_QUADS, S5_CHUNK, S5_GROUP, 128)
    bt_re = jnp.transpose(bb_re, (0, 1, 3, 2))
    bt_im = jnp.transpose(bb_im, (0, 1, 3, 2))
    lane_gc = lambda c: jnp.transpose(sub(c, 0), (0, 3, 1, 2)).reshape(S5_QUADS, S5_STATE, 128)
    ps, qs = [], []
    for d in range(2):
        p_pow = (S5_CHUNK - 1 - t_idx) if d == 0 else t_idx
        ar, ai = pw_re[p_pow, d][:, :, None, :], pw_im[p_pow, d][:, :, None, :]
        pc = jnp.concatenate([ar * bt_re[d][None] - ai * bt_im[d][None],
                              ar * bt_im[d][None] + ai * bt_re[d][None]], axis=-1)
        pc = pc.reshape(S5_CHUNK, S5_QUADS, 128, 128)
        ps.append(jnp.transpose(pc, (1, 0, 2, 3)).reshape(S5_QUADS, S5_CHUNK * 128, 128))
        q_pow = (t_idx + 1) if d == 0 else (S5_CHUNK - t_idx)
        cl_re, cl_im = lane_gc(c_re[d])[:, None], lane_gc(c_im[d])[:, None]
        rep = lambda a: jnp.repeat(jnp.transpose(sub(a, 1), (1, 0, 3, 2)), S5_GROUP, axis=-1)
        al_re, al_im = rep(pw_re[q_pow, d]), rep(pw_im[q_pow, d])
        qs.append(jnp.concatenate([cl_re * al_re - cl_im * al_im,
                                   -(cl_re * al_im + cl_im * al_re)], axis=2))
    a16_re, a16_im = pw_re[S5_CHUNK], pw_im[S5_CHUNK]
    lanes = S5_GROUPS * 2 * S5_STATE
    return dict(
        klag=klag.astype(BF16),
        p=jnp.stack(ps).astype(BF16),
        q=jnp.stack(qs).astype(BF16),
        a_mul=jnp.concatenate([a16_re, a16_re], axis=-1).reshape(2, 1, lanes),
        a_swp=jnp.concatenate([-a16_im, a16_im], axis=-1).reshape(2, 1, lanes),
    )


def _s5_expand_p(pc):
    rep = jnp.concatenate([pc] * S5_GSUB, axis=1)
    row = lax.broadcasted_iota(jnp.int32, rep.shape, 0)
    col = lax.broadcasted_iota(jnp.int32, rep.shape, 1)
    return jnp.where((row // S5_GROUP) % S5_GSUB == col // 128, rep, jnp.zeros_like(rep))


def _s5_expand_q(q_ref):
    row = lax.broadcasted_iota(jnp.int32, (S5_GSUB * 128, 128), 0)
    col = lax.broadcasted_iota(jnp.int32, (S5_GSUB * 128, 128), 1)
    keep = row // 128 == col // S5_GROUP
    blocks = [jnp.where(keep, jnp.concatenate([q_ref[t]] * S5_GSUB, axis=0), jnp.zeros((S5_GSUB * 128, 128), BF16))
              for t in range(S5_CHUNK)]
    return jnp.concatenate(blocks, axis=1)


def _s5_chunk_rows(u_ref):
    n = N_TOK // S5_CHUNK
    return jnp.concatenate([u_ref[pl.ds(s, n, stride=S5_CHUNK), :] for s in range(S5_CHUNK)], axis=-1).astype(BF16)


def _s5_z_kernel(u_ref, p_ref, z_ref):
    z_ref[...] = jnp.dot(_s5_chunk_rows(u_ref), _s5_expand_p(p_ref[...]), preferred_element_type=F32)


def _s5_scan_kernel(z_ref, amul_ref, aswp_ref, s_ref, zs_scr):
    d = pl.program_id(0)
    a_mul, a_swp = amul_ref[...], aswp_ref[...]
    lanes = s_ref.shape[1]
    z = z_ref[...]
    low_half = lax.broadcasted_iota(jnp.int32, z.shape, 1) % 128 < S5_STATE
    zs_scr[...] = jnp.where(low_half, pltpu.roll(z, lanes - S5_STATE, 1), pltpu.roll(z, S5_STATE, 1))
    nl, nc = SEQ // S5_CHUNK, S5_CTX_CHUNKS

    def segment(base, count, carry):
        def body(step, carry):
            k = jnp.where(d == 0, step, count - 1 - step)
            out = []
            for b in range(BATCH):
                s, w = carry[2 * b], carry[2 * b + 1]
                row = pl.ds(base + b * count + k, 1)
                s_ref[row, :] = s
                out += [a_mul * s + a_swp * w + z_ref[row, :], a_mul * w - a_swp * s + zs_scr[row, :]]
            return tuple(out)
        return lax.fori_loop(0, count, body, carry)

    zero = jnp.zeros((1, lanes), F32)
    carry = segment(BATCH * nl, nc, (zero,) * (2 * BATCH))
    segment(0, nl, carry)


def _s5_y_kernel(u_ref, klag_ref, s_ref, q_ref, dv_ref, y_ref, m_scr):
    d = pl.program_id(1)
    n = N_TOK // S5_CHUNK
    m_scr[...] = jnp.zeros_like(m_scr)
    row = lax.broadcasted_iota(jnp.int32, (128, 128), 0)
    col = lax.broadcasted_iota(jnp.int32, (128, 128), 1)
    same_group = row // S5_GROUP == col // S5_GROUP
    blocks = [jnp.where(same_group, jnp.concatenate([klag_ref[lag]] * S5_GSUB, axis=0), jnp.zeros((128, 128), BF16))
              for lag in range(S5_CHUNK)]
    for rev in range(2):
        @pl.when(d == rev)
        def _():
            for s in range(S5_CHUNK):
                for t in range(S5_CHUNK):
                    lag = (s - t) if rev else (t - s)
                    if lag >= 0:
                        m_scr[s * 128:(s + 1) * 128, t * 128:(t + 1) * 128] = blocks[lag]
    y = jnp.dot(_s5_chunk_rows(u_ref), m_scr[...], preferred_element_type=F32)
    y = y + jnp.dot(s_ref[...].astype(BF16), _s5_expand_q(q_ref), preferred_element_type=F32)
    for t in range(S5_CHUNK):
        rows = pl.ds(t, n, stride=S5_CHUNK)
        yt = y[:, t * 128:(t + 1) * 128]

        @pl.when(d == 0)
        def _():
            y_ref[rows, :] = yt + u_ref[rows, :] * dv_ref[...]

        @pl.when(d == 1)
        def _():
            y_ref[rows, :] = y_ref[rows, :] + yt


def _s5(p, mats, layer, dskip):
    nstate = S5_GROUPS * 2 * S5_STATE
    qlanes = nstate // S5_QUADS
    slanes = 512
    ucol = (C_Q_RANK + C_KV_RANK) // 128
    z = pl.pallas_call(
        _s5_z_kernel,
        out_shape=jax.ShapeDtypeStruct((2, S5_ROWS, nstate), F32),
        grid=(2, S5_QUADS),
        in_specs=[
            pl.BlockSpec((N_TOK, 128), lambda d, q: (0, ucol + q)),
            pl.BlockSpec((None, None, None, S5_CHUNK * 128, 128), lambda d, q: (layer, d, q, 0, 0)),
        ],
        out_specs=pl.BlockSpec((None, S5_ROWS, qlanes), lambda d, q: (d, 0, q)),
        compiler_params=_cparams(("arbitrary", "arbitrary")),
        name="s5_chunk_state",
    )(p, mats["p"])
    s = pl.pallas_call(
        _s5_scan_kernel,
        out_shape=jax.ShapeDtypeStruct((2, S5_ROWS, nstate), F32),
        grid=(2, nstate // slanes),
        in_specs=[
            pl.BlockSpec((None, S5_ROWS, slanes), lambda d, q: (d, 0, q)),
            pl.BlockSpec((None, None, 1, slanes), lambda d, q: (layer, d, 0, q)),
            pl.BlockSpec((None, None, 1, slanes), lambda d, q: (layer, d, 0, q)),
        ],
        out_specs=pl.BlockSpec((None, S5_ROWS, slanes), lambda d, q: (d, 0, q)),
        scratch_shapes=[pltpu.VMEM((S5_ROWS, slanes), F32)],
        compiler_params=_cparams(("arbitrary", "arbitrary")),
        name="s5_scan",
    )(z, mats["a_mul"], mats["a_swp"])
    return pl.pallas_call(
        _s5_y_kernel,
        out_shape=jax.ShapeDtypeStruct((N_TOK, D_WIDTH), F32),
        grid=(S5_QUADS, 2),
        in_specs=[
            pl.BlockSpec((N_TOK, 128), lambda q, d: (0, ucol + q)),
            pl.BlockSpec((None, None, None, S5_CHUNK, S5_GROUP, 128), lambda q, d: (layer, d, q, 0, 0, 0)),
            pl.BlockSpec((None, S5_ROWS, qlanes), lambda q, d: (d, 0, q)),
            pl.BlockSpec((None, None, None, S5_CHUNK, 128, 128), lambda q, d: (layer, d, q, 0, 0, 0)),
            pl.BlockSpec((None, 1, 128), lambda q, d: (layer, 0, q)),
        ],
        out_specs=pl.BlockSpec((N_TOK, 128), lambda q, d: (0, q)),
        scratch_shapes=[pltpu.VMEM((S5_CHUNK * 128, S5_CHUNK * 128), BF16)],
        compiler_params=pltpu.CompilerParams(dimension_semantics=("arbitrary", "arbitrary"),
                                             vmem_limit_bytes=56 * 1024 * 1024),
        name="s5_output",
    )(p, mats["klag"], s, mats["q"], dskip)


def _glu_kernel(y_ref, w_ref, o_ref):
    y = y_ref[...]
    z = y * (0.5 * (1.0 + jnp.tanh(math.sqrt(2.0 / math.pi) * (y + 0.044715 * (y * y * y)))))
    gate = jnp.dot(z.astype(BF16), w_ref[...], preferred_element_type=F32)
    o_ref[...] = (z * jax.nn.sigmoid(gate)).astype(BF16)


def _s5_glu(y, w_glu, layer):
    t = 512
    return pl.pallas_call(
        _glu_kernel,
        out_shape=jax.ShapeDtypeStruct((N_TOK, D_WIDTH), BF16),
        grid=(N_TOK // t,),
        in_specs=[pl.BlockSpec((t, D_WIDTH), lambda i: (i, 0)),
                  pl.BlockSpec((None, D_WIDTH, D_WIDTH), lambda i: (layer, 0, 0))],
        out_specs=pl.BlockSpec((t, D_WIDTH), lambda i: (i, 0)),
        compiler_params=_cparams(("arbitrary",)),
        name="s5_glu",
    )(y, w_glu)


LANE_CHUNKS = D_MODEL // 128


def _store_token_major(ref, val):
    rows = val.shape[0]
    for c in range(LANE_CHUNKS):
        ref[pl.ds(c, rows, stride=LANE_CHUNKS), :] = val[:, c * 128:(c + 1) * 128]


def _load_token_major(ref, rows):
    return jnp.concatenate([ref[pl.ds(c, rows, stride=LANE_CHUNKS), :] for c in range(LANE_CHUNKS)], axis=-1)


def _router_kernel(x_ref, g_ref, sh_ref, sc_ref, whi_ref, wlo_ref, h_ref, info_ref, gw_ref, cnt_ref, h_scr, carry):
    i = pl.program_id(0)

    @pl.when(i == 0)
    def _():
        carry[...] = jnp.zeros_like(carry)

    for r in range(0, TM, 256):
        h = _rms(x_ref[r:r + 256, :]) * g_ref[...]
        h_scr[r:r + 256, :] = h * (1.0 + sc_ref[...]) + sh_ref[...]
    _store_token_major(h_ref, h_scr[...])
    h = h_scr[...]
    h_hi = h.astype(BF16)
    h_lo = (h - h_hi.astype(F32)).astype(BF16)
    logits = (jnp.dot(h_hi, whi_ref[...], preferred_element_type=F32)
              + jnp.dot(h_lo, whi_ref[...], preferred_element_type=F32)
              + jnp.dot(h_hi, wlo_ref[...], preferred_element_type=F32))
    lane = lax.broadcasted_iota(jnp.int32, logits.shape, 1)
    neg = -jnp.inf
    big = jnp.int32(1 << 20)

    def first_argmax(v, vmax):
        return jnp.min(jnp.where(v == vmax, lane, big), axis=-1, keepdims=True)

    lg = jnp.where(lane < N_GROUPS, logits, neg)
    mg = jnp.max(lg, axis=-1, keepdims=True)
    g_w = 1.0 / jnp.sum(jnp.exp(lg - mg), axis=-1, keepdims=True)
    g_idx = first_argmax(lg, mg)
    lo = N_GROUPS + EXPERTS_PER_GROUP * g_idx
    le = jnp.where(jnp.logical_and(lane >= lo, lane < lo + EXPERTS_PER_GROUP), logits, neg)
    m1 = jnp.max(le, axis=-1, keepdims=True)
    i1 = first_argmax(le, m1)
    le2 = jnp.where(lane == i1, neg, le)
    m2 = jnp.max(le2, axis=-1, keepdims=True)
    i2 = first_argmax(le2, m2)
    r21 = jnp.exp(m2 - m1)
    w1 = g_w / (1.0 + r21)
    w2 = g_w * r21 / (1.0 + r21)
    oh = jnp.logical_or(lane == i1, lane == i2)
    ohb = jnp.where(oh, 1.0, 0.0).astype(BF16)
    rr = lax.broadcasted_iota(jnp.int32, (TM, TM), 0)
    cc = lax.broadcasted_iota(jnp.int32, (TM, TM), 1)
    lower = jnp.where(rr > cc, 1.0, 0.0).astype(BF16)
    before = jnp.dot(lower, ohb, preferred_element_type=F32) + carry[...]
    rank1 = jnp.sum(jnp.where(lane == i1, before, 0.0), axis=-1, keepdims=True).astype(jnp.int32)
    rank2 = jnp.sum(jnp.where(lane == i2, before, 0.0), axis=-1, keepdims=True).astype(jnp.int32)
    carry[...] = carry[...] + jnp.sum(ohb.astype(F32), axis=0, keepdims=True)
    cnt_ref[...] = jnp.broadcast_to(carry[...], cnt_ref.shape)
    info = jnp.where(lane == 0, i1 - N_GROUPS, jnp.where(lane == 1, i2 - N_GROUPS,
                     jnp.where(lane == 2, rank1, jnp.where(lane == 3, rank2, 0))))
    info_ref[...] = info
    gw_ref[...] = jnp.where(lane == 0, w1, jnp.where(lane == 1, w2, 0.0))


def _router(x, g, mod, w_router, layer):
    return pl.pallas_call(
        _router_kernel,
        out_shape=[jax.ShapeDtypeStruct((N_TOK * LANE_CHUNKS, 128), F32),
                   jax.ShapeDtypeStruct((N_TOK, 128), jnp.int32),
                   jax.ShapeDtypeStruct((N_TOK, 128), F32),
                   jax.ShapeDtypeStruct((8, 128), F32)],
        grid=(N_TOK // TM,),
        in_specs=[
            pl.BlockSpec((TM, D_MODEL), lambda i: (i, 0)),
            pl.BlockSpec((1, D_MODEL), lambda i: (0, 0)),
            pl.BlockSpec((None, None, 1, D_MODEL), lambda i: (_mod_row(i, TM), 3, 0, 0)),
            pl.BlockSpec((None, None, 1, D_MODEL), lambda i: (_mod_row(i, TM), 4, 0, 0)),
            pl.BlockSpec((None, D_MODEL, 128), lambda i: (layer, 0, 0)),
            pl.BlockSpec((None, D_MODEL, 128), lambda i: (layer, 0, 0)),
        ],
        out_specs=[pl.BlockSpec((TM * LANE_CHUNKS, 128), lambda i: (i, 0)),
                   pl.BlockSpec((TM, 128), lambda i: (i, 0)),
                   pl.BlockSpec((TM, 128), lambda i: (i, 0)),
                   pl.BlockSpec((8, 128), lambda i: (0, 0))],
        scratch_shapes=[pltpu.VMEM((TM, D_MODEL), F32), pltpu.VMEM((1, 128), F32)],
        compiler_params=_cparams(("arbitrary",)),
        name="moe_router",
    )(x, g, mod, mod, *w_router)


FFN_ISSUE_GROUPS = 8
ISSUE_UNROLL = 8


def _ffn_kernel(pos_ref, te_ref, meta_ref, h_hbm, wg_ref, wu_ref, wd_ref, o_ref,
                src, xbuf, wg_b, wu_b, wd_b, sem):
    t = pl.program_id(0)
    nt = meta_ref[0]

    def row_copy(tile, slot, r):
        tok = src[tile * TE + r]
        return pltpu.make_async_copy(
            h_hbm.at[pl.ds(pl.multiple_of(tok * LANE_CHUNKS, LANE_CHUNKS), LANE_CHUNKS), :],
            xbuf.at[slot, pl.ds(pl.multiple_of(r * LANE_CHUNKS, LANE_CHUNKS), LANE_CHUNKS), :],
            sem.at[slot])

    def gather(tile, slot):
        def body(g, _):
            for k in range(ISSUE_UNROLL):
                row_copy(tile, slot, g * ISSUE_UNROLL + k).start(priority=k % 2)
            return 0
        lax.fori_loop(0, TE // ISSUE_UNROLL, body, 0)

    def wait_tile(slot):
        pltpu.make_async_copy(xbuf.at[slot], xbuf.at[slot], sem.at[slot]).wait()

    @pl.when(t == 0)
    def _():
        for e in range(N_EXPERTS):
            def clear(i, _):
                src[i] = 0
                return 0
            lax.fori_loop(meta_ref[1 + e], meta_ref[1 + N_EXPERTS + e], clear, 0)

        def fill(tok, _):
            src[pos_ref[2 * tok]] = tok
            src[pos_ref[2 * tok + 1]] = tok
            return 0
        lax.fori_loop(0, N_TOK, fill, 0, unroll=8)
        gather(0, 0)

    @pl.when(t >= nt)
    def _():
        o_ref[...] = jnp.zeros_like(o_ref)

    @pl.when(t < nt)
    def _():
        slot = t % 2
        wait_tile(slot)

        @pl.when(jnp.logical_or(t == 0, te_ref[t] != te_ref[jnp.maximum(t - 1, 0)]))
        def _():
            wg_b[...] = wg_ref[...].astype(BF16)
            wu_b[...] = wu_ref[...].astype(BF16)
            wd_b[...] = wd_ref[...].astype(BF16)

        nxt = jnp.minimum(t + 1, nt - 1)
        per = TE // FFN_ISSUE_GROUPS

        def issue(g):
            for r in range(g * per, (g + 1) * per):
                row_copy(nxt, 1 - slot, r).start(priority=r % 2)

        x = _load_token_major(xbuf.at[slot], TE).astype(BF16)
        half = D_EXPERT // 2
        hg, hu = [], []
        for n in range(2):
            issue(n)
            hg.append(jnp.dot(x, wg_b[:, n * half:(n + 1) * half], preferred_element_type=F32))
        for n in range(2):
            issue(2 + n)
            hu.append(jnp.dot(x, wu_b[:, n * half:(n + 1) * half], preferred_element_type=F32))
        hg, hu = jnp.concatenate(hg, axis=1), jnp.concatenate(hu, axis=1)
        act = (hg * jax.nn.sigmoid(hg) * hu).astype(BF16)
        quarter = D_MODEL // 4
        for n in range(4):
            issue(4 + n)
            y = jnp.dot(act, wd_b[:, n * quarter:(n + 1) * quarter], preferred_element_type=F32)
            for c in range(quarter // 128):
                o_ref[pl.ds(n * (quarter // 128) + c, TE, stride=LANE_CHUNKS), :] = y[:, c * 128:(c + 1) * 128]

        @pl.when(t == nt - 1)
        def _():
            wait_tile(1 - slot)


def _expert_ffn(pos_flat, tile_expert, meta, h, w_gate, w_up, w_down, layer):
    wsel = lambda t, pos, te, meta: (layer, te[t], 0, 0)
    return pl.pallas_call(
        _ffn_kernel,
        out_shape=jax.ShapeDtypeStruct((N_SORT * LANE_CHUNKS, 128), F32),
        grid_spec=pltpu.PrefetchScalarGridSpec(
            num_scalar_prefetch=3,
            grid=(N_ETILES,),
            in_specs=[
                pl.BlockSpec(memory_space=pl.ANY),
                pl.BlockSpec((None, None, D_MODEL, D_EXPERT), wsel),
                pl.BlockSpec((None, None, D_MODEL, D_EXPERT), wsel),
                pl.BlockSpec((None, None, D_EXPERT, D_MODEL), wsel),
            ],
            out_specs=pl.BlockSpec((TE * LANE_CHUNKS, 128), lambda t, pos, te, meta: (t, 0)),
            scratch_shapes=[
                pltpu.SMEM((N_SORT,), jnp.int32),
                pltpu.VMEM((2, TE * LANE_CHUNKS, 128), F32),
                pltpu.VMEM((D_MODEL, D_EXPERT), BF16),
                pltpu.VMEM((D_MODEL, D_EXPERT), BF16),
                pltpu.VMEM((D_EXPERT, D_MODEL), BF16),
                pltpu.SemaphoreType.DMA((2,)),
            ],
        ),
        compiler_params=pltpu.CompilerParams(dimension_semantics=("arbitrary",),
                                             vmem_limit_bytes=56 * 1024 * 1024),
        name="moe_expert_ffn",
    )(pos_flat, tile_expert, meta, h, w_gate, w_up, w_down)


TC = 256


def _combine_kernel(pos_ref, x_ref, gate_ref, gw_ref, g_ref, sh_ref, sc_ref, ys_hbm, *refs, final_norm):
    if final_norm:
        o_ref, buf_a, buf_b, sem = refs
    else:
        o_ref, h_ref, buf_a, buf_b, sem = refs
    i = pl.program_id(0)
    slot = i % 2

    def slab(ref, row):
        return ref.at[pl.ds(pl.multiple_of(row * LANE_CHUNKS, LANE_CHUNKS), LANE_CHUNKS), :]

    def gather(tile, slot):
        def body(g, _):
            for k in range(ISSUE_UNROLL):
                r = g * ISSUE_UNROLL + k
                tok = tile * TC + r
                pltpu.make_async_copy(slab(ys_hbm, pos_ref[2 * tok]), slab(buf_a.at[slot], r),
                                      sem.at[slot, 0]).start(priority=0)
                pltpu.make_async_copy(slab(ys_hbm, pos_ref[2 * tok + 1]), slab(buf_b.at[slot], r),
                                      sem.at[slot, 1]).start(priority=1)
            return 0
        lax.fori_loop(0, TC // ISSUE_UNROLL, body, 0)

    @pl.when(i == 0)
    def _():
        gather(0, 0)

    @pl.when(i + 1 < pl.num_programs(0))
    def _():
        gather(i + 1, 1 - slot)

    pltpu.make_async_copy(buf_a.at[slot], buf_a.at[slot], sem.at[slot, 0]).wait()
    pltpu.make_async_copy(buf_b.at[slot], buf_b.at[slot], sem.at[slot, 1]).wait()
    w0 = jnp.broadcast_to(gw_ref[:, 0:1], (TC, 128))
    w1 = jnp.broadcast_to(gw_ref[:, 1:2], (TC, 128))
    sq = jnp.zeros((TC, 128), F32)
    for c in range(LANE_CHUNKS):
        cols = slice(c * 128, (c + 1) * 128)
        rows = pl.ds(c, TC, stride=LANE_CHUNKS)
        y = x_ref[:, cols] + gate_ref[:, cols] * (w0 * buf_a[slot, rows, :] + w1 * buf_b[slot, rows, :])
        o_ref[:, cols] = y
        sq = sq + y * y
    inv = lax.rsqrt(jnp.sum(sq, axis=-1, keepdims=True) / D_MODEL + EPS)
    for c in range(LANE_CHUNKS):
        cols = slice(c * 128, (c + 1) * 128)
        normed = o_ref[:, cols] * inv * g_ref[:, cols]
        if final_norm:
            o_ref[:, cols] = normed
        else:
            h_ref[:, cols] = (normed * (1.0 + sc_ref[:, cols]) + sh_ref[:, cols]).astype(BF16)


def _combine(pos_flat, x, mod, gw, ys, g, mod_next):
    final_norm = mod_next is None
    rows = N_LAT if final_norm else N_TOK
    tile = pl.BlockSpec((TC, D_MODEL), lambda i, pos: (i, 0))
    mod_row = lambda which: pl.BlockSpec((None, None, 1, D_MODEL), lambda i, pos: (_mod_row(i, TC), which, 0, 0))
    next_tab = mod if final_norm else mod_next
    out_shape = [jax.ShapeDtypeStruct((rows, D_MODEL), F32)]
    if not final_norm:
        out_shape.append(jax.ShapeDtypeStruct((rows, D_MODEL), BF16))
    return pl.pallas_call(
        functools.partial(_combine_kernel, final_norm=final_norm),
        out_shape=out_shape,
        grid_spec=pltpu.PrefetchScalarGridSpec(
            num_scalar_prefetch=1,
            grid=(rows // TC,),
            in_specs=[
                tile,
                mod_row(5),
                pl.BlockSpec((TC, 128), lambda i, pos: (i, 0)),
                pl.BlockSpec((1, D_MODEL), lambda i, pos: (0, 0)),
                mod_row(0), mod_row(1),
                pl.BlockSpec(memory_space=pl.ANY),
            ],
            out_specs=[tile] * len(out_shape),
            scratch_shapes=[pltpu.VMEM((2, TC * LANE_CHUNKS, 128), F32), pltpu.VMEM((2, TC * LANE_CHUNKS, 128), F32),
                            pltpu.SemaphoreType.DMA((2, 2))],
        ),
        compiler_params=_cparams(("arbitrary",)),
        name="moe_combine",
    )(pos_flat, x, mod, gw, g, next_tab, next_tab, ys)


def _moe(x, g, mod, w_router, w_gate, w_up, w_down, layer, g_next, mod_next):
    h, info, gw, cnt = _router(x, g, mod, w_router, layer)
    counts = cnt[0, N_GROUPS:N_GROUPS + N_EXPERTS].astype(jnp.int32)
    padded = ((counts + TE - 1) // TE) * TE
    ends = jnp.cumsum(padded)
    starts = ends - padded
    experts = jnp.arange(N_EXPERTS, dtype=jnp.int32)
    start_of = jnp.sum(jnp.where(info[:, 0:2, None] == experts, starts, 0), axis=-1)
    pos_flat = (start_of + info[:, 2:4]).reshape(-1)
    tile_ends = ends // TE
    num_tiles = tile_ends[-1]
    tiles = jnp.minimum(jnp.arange(N_ETILES, dtype=jnp.int32), num_tiles - 1)
    tile_expert = jnp.sum((tile_ends[None, :] <= tiles[:, None]).astype(jnp.int32), axis=-1)
    meta = jnp.concatenate([num_tiles[None], starts + counts, ends]).astype(jnp.int32)
    ys = _expert_ffn(pos_flat, tile_expert, meta, h, w_gate, w_up, w_down, layer)
    return _combine(pos_flat, x, mod, gw, ys, g_next, mod_next)


def kernel(x, c, ctx, c_ctx, mod_w, mod_b, norm_mix, norm_ffn, ab_w_in, ab_q_norm, ab_k_norm, ab_conv_w, ab_w_out, cd_w_in, cd_q_norm, cd_kv_norm, cd_w_uq, cd_w_ukv, s5_lam_re, s5_lam_im, s5_log_dt, s5_b_re, s5_b_im, s5_c_re, s5_c_im, s5_d, s5_w_glu, cd_w_out, moe_w_group, moe_w_expert, moe_w_gate, moe_w_up, moe_w_down, final_norm):
    cc = jnp.concatenate([c, c_ctx[None, :], jnp.zeros((8 - BATCH - 1, D_MODEL), F32)], axis=0)
    mods = _modulation(cc, mod_w, mod_b).reshape(DEPTH, 8, N_MOD, 1, D_MODEL)
    xs, h = _embed(x.reshape(N_LAT, D_MODEL), ctx.reshape(N_CTX, D_MODEL), norm_mix[0][None, :], mods[0])
    cos_a, sin_a = _rope_tables(HEAD_DIM)
    cos_c, sin_c = _rope_tables(C_ROPE)

    ab_in_b, ab_out_b = ab_w_in.astype(BF16), ab_w_out.astype(BF16)
    cd_out_b, ukv_b, glu_b = cd_w_out.astype(BF16), cd_w_ukv.astype(BF16), s5_w_glu.astype(BF16)
    a, b_ = C_Q_RANK + C_KV_RANK, C_Q_RANK + C_KV_RANK + C_ROPE
    pad = jnp.zeros(cd_w_in.shape[:2] + (CD_IN_PAD - cd_w_in.shape[2],), F32)
    cd_in_b = jnp.concatenate([cd_w_in[..., :a], cd_w_in[..., b_:], cd_w_in[..., a:b_], pad], axis=-1).astype(BF16)
    w_uq = cd_w_uq.reshape(-1, C_Q_RANK, C_HEADS, C_NOPE + C_ROPE)
    uq_b = jnp.concatenate([w_uq[..., :C_NOPE].reshape(-1, C_Q_RANK, C_HEADS * C_NOPE),
                            w_uq[..., C_NOPE:].reshape(-1, C_Q_RANK, C_HEADS * C_ROPE)], axis=-1).astype(BF16)

    mats = jax.vmap(_s5_matrices)(s5_lam_re, s5_lam_im, s5_log_dt, s5_b_re, s5_b_im, s5_c_re, s5_c_im)
    w_router = jnp.concatenate(
        [moe_w_group, jnp.transpose(moe_w_expert, (0, 2, 1, 3)).reshape(DEPTH, D_MODEL, N_EXPERTS),
         jnp.zeros((DEPTH, D_MODEL, 128 - N_GROUPS - N_EXPERTS), F32)], axis=-1)
    w_router_hi = w_router.astype(BF16)
    w_router = (w_router_hi, (w_router - w_router_hi.astype(F32)).astype(BF16))

    for i in range(DEPTH):
        j = i // 2
        mod = mods[i]
        if i % 2 == 0:
            p = _linear(h, ab_in_b, j, 2048)
            k, vext = _gqa_kv(p, ab_k_norm[j][None, :], cos_a, sin_a)
            o_lat, o_ctx = _gqa_attention(p, k, vext, ab_q_norm[j][None, :], cos_a, sin_a)
            side = _gated_conv(p, ab_conv_w[j])
            w_out = ab_out_b
        else:
            p = _linear(h, cd_in_b, j, CD_IN_PAD)
            q = _norm_linear(p, 0, C_Q_RANK, cd_q_norm[j][None, :], uq_b, j)
            kcat, vext = _mla_kv(p, cd_kv_norm[j][None, :], ukv_b, j, cos_c, sin_c)
            o_lat, o_ctx = _mla_attention(q, kcat, vext, cos_c, sin_c)
            y = _s5(p, mats, j, s5_d.astype(F32)[:, None, :])
            side = _s5_glu(y, glu_b, j)
            w_out = cd_out_b
        xs = _out_linear(o_lat, o_ctx, side, w_out, j, xs, mod, 2)
        if i < DEPTH - 1:
            xs, h = _moe(xs, norm_ffn[i][None, :], mod, w_router, moe_w_gate, moe_w_up, moe_w_down, i,
                         norm_mix[i + 1][None, :], mods[i + 1])
        else:
            out, = _moe(xs, norm_ffn[i][None, :], mod, w_router, moe_w_gate, moe_w_up, moe_w_down, i,
                        final_norm[None, :], None)
    return out.reshape(BATCH, SEQ, D_MODEL)
```

```python
import functools
import math

import numpy as np
import jax
import jax.numpy as jnp
from jax import lax
from jax.experimental import pallas as pl
from jax.experimental.pallas import tpu as pltpu

F32 = jnp.float32
BF16 = jnp.bfloat16

D_MODEL = 2048
BATCH = 4
SEQ = 2048
DEPTH = 4
GRID_W = 64
CTX_LEN = 256
ROPE_THETA = 10000.0
EPS = 1e-6
N_MOD = 6
HEAD_DIM = 128
A_Q_HEADS = 12
A_KV_HEADS = 4
A_GROUP = A_Q_HEADS // A_KV_HEADS
A_WIDTH = A_Q_HEADS * HEAD_DIM
A_KV_WIDTH = A_KV_HEADS * HEAD_DIM
B_WIDTH = 512
AB_IN = A_WIDTH + 2 * A_KV_WIDTH + 3 * B_WIDTH
C_HEADS = 12
C_NOPE = 128
C_ROPE = 64
C_V = 128
C_Q_RANK = 512
C_KV_RANK = 256
C_WIDTH = C_HEADS * C_V
D_WIDTH = 512
S5_GROUP = 16
S5_GROUPS = D_WIDTH // S5_GROUP
S5_STATE = 64
N_GROUPS = 4
EXPERTS_PER_GROUP = 4
N_EXPERTS = N_GROUPS * EXPERTS_PER_GROUP
D_EXPERT = 512

N_LAT = BATCH * SEQ
N_CTX = BATCH * CTX_LEN
N_TOK = N_LAT + N_CTX
CTX_ROW = BATCH
CD_IN_PAD = 1536

TM = 512
TQ = 1024
TE = 256
ROPE_IDENTITY_ROWS = max(TM, TQ)
N_SORT = 2 * N_TOK + N_EXPERTS * TE
N_ETILES = N_SORT // TE
S5_CHUNK = 16
S5_CTX_CHUNKS = CTX_LEN // S5_CHUNK
S5_ROWS = N_TOK // S5_CHUNK
S5_GSUB = 128 // S5_GROUP
S5_QUADS = S5_GROUPS // S5_GSUB
VMEM_LIMIT = 48 * 1024 * 1024


def _cparams(sem):
    return pltpu.CompilerParams(dimension_semantics=sem, vmem_limit_bytes=VMEM_LIMIT)


def _mod_row(tile, tile_rows):
    r0 = tile * tile_rows
    return jnp.where(r0 >= N_LAT, CTX_ROW, r0 // SEQ)


def _rope_block(tile, tile_rows):
    r0 = tile * tile_rows
    return jnp.where(r0 >= N_LAT, SEQ // tile_rows, (r0 % SEQ) // tile_rows)


def _rms(x):
    return x * lax.rsqrt(jnp.mean(x * x, axis=-1, keepdims=True) + EPS)


def _mod_kernel(cc_ref, w_ref, b_ref, o_ref):
    cc = cc_ref[...]
    s = (cc * jax.nn.sigmoid(cc)).astype(BF16)
    o_ref[...] = jnp.dot(s, w_ref[...].astype(BF16), preferred_element_type=F32) + b_ref[...]


def _modulation(cc, mod_w, mod_b):
    tn = 1024
    nout = N_MOD * D_MODEL
    return pl.pallas_call(
        _mod_kernel,
        out_shape=jax.ShapeDtypeStruct((DEPTH, 8, nout), F32),
        grid=(DEPTH, nout // tn),
        in_specs=[
            pl.BlockSpec((8, D_MODEL), lambda l, j: (0, 0)),
            pl.BlockSpec((None, D_MODEL, tn), lambda l, j: (l, 0, j)),
            pl.BlockSpec((None, 1, tn), lambda l, j: (l, 0, j)),
        ],
        out_specs=pl.BlockSpec((None, 8, tn), lambda l, j: (l, 0, j)),
        compiler_params=_cparams(("arbitrary", "arbitrary")),
        name="modulation",
    )(cc, mod_w, mod_b.reshape(DEPTH, 1, nout))


def _modulated_norm(x, g, sh, sc):
    return _rms(x) * g * (1.0 + sc) + sh


def _embed_kernel(xl_ref, xc_ref, g_ref, sh_ref, sc_ref, xs_ref, h_ref):
    is_ctx = pl.program_id(0) >= N_LAT // TM
    for r in range(0, TM, 256):
        x = jnp.where(is_ctx, xc_ref[r:r + 256, :], xl_ref[r:r + 256, :])
        xs_ref[r:r + 256, :] = x
        h_ref[r:r + 256, :] = _modulated_norm(x, g_ref[...], sh_ref[...], sc_ref[...]).astype(BF16)


def _embed(x_lat, x_ctx, g, mod):
    lat_tiles = N_LAT // TM
    return pl.pallas_call(
        _embed_kernel,
        out_shape=[jax.ShapeDtypeStruct((N_TOK, D_MODEL), F32), jax.ShapeDtypeStruct((N_TOK, D_MODEL), BF16)],
        grid=(N_TOK // TM,),
        in_specs=[
            pl.BlockSpec((TM, D_MODEL), lambda i: (jnp.minimum(i, lat_tiles - 1), 0)),
            pl.BlockSpec((TM, D_MODEL), lambda i: (jnp.maximum(i - lat_tiles, 0), 0)),
            pl.BlockSpec((1, D_MODEL), lambda i: (0, 0)),
            pl.BlockSpec((None, None, 1, D_MODEL), lambda i: (_mod_row(i, TM), 0, 0, 0)),
            pl.BlockSpec((None, None, 1, D_MODEL), lambda i: (_mod_row(i, TM), 1, 0, 0)),
        ],
        out_specs=[pl.BlockSpec((TM, D_MODEL), lambda i: (i, 0))] * 2,
        compiler_params=_cparams(("arbitrary",)),
        name="embed_norm",
    )(x_lat, x_ctx, g, mod, mod)


def _linear_kernel(h_ref, w_ref, o_ref):
    o_ref[...] = jnp.dot(h_ref[...], w_ref[...], preferred_element_type=F32)


def _linear(h, w, layer, tn):
    kdim, nout = w.shape[1], w.shape[2]
    return pl.pallas_call(
        _linear_kernel,
        out_shape=jax.ShapeDtypeStruct((N_TOK, nout), F32),
        grid=(N_TOK // TM, nout // tn),
        in_specs=[
            pl.BlockSpec((TM, kdim), lambda i, j: (i, 0)),
            pl.BlockSpec((None, kdim, tn), lambda i, j: (layer, 0, j)),
        ],
        out_specs=pl.BlockSpec((TM, tn), lambda i, j: (i, j)),
        compiler_params=_cparams(("arbitrary", "arbitrary")),
        name="linear",
    )(h, w)


def _norm_linear_kernel(x_ref, g_ref, w_ref, o_ref):
    h = (_rms(x_ref[...]) * g_ref[...]).astype(BF16)
    o_ref[...] = jnp.dot(h, w_ref[...], preferred_element_type=F32)


def _norm_linear(x, xcol, kdim, g, w, layer):
    nout = w.shape[2]
    return pl.pallas_call(
        _norm_linear_kernel,
        out_shape=jax.ShapeDtypeStruct((N_TOK, nout), F32),
        grid=(N_TOK // TM,),
        in_specs=[
            pl.BlockSpec((TM, kdim), lambda i: (i, xcol)),
            pl.BlockSpec((1, kdim), lambda i: (0, 0)),
            pl.BlockSpec((None, kdim, nout), lambda i: (layer, 0, 0)),
        ],
        out_specs=pl.BlockSpec((TM, nout), lambda i: (i, 0)),
        compiler_params=_cparams(("arbitrary",)),
        name="norm_linear",
    )(x, g, w)


def _out_linear_kernel(a1l_ref, a1c_ref, a2_ref, w1_ref, w2_ref, x_ref, gate_ref, o_ref):
    a1 = jnp.where(pl.program_id(0) >= N_LAT // TM, a1c_ref[...], a1l_ref[...])
    acc = jnp.dot(a1, w1_ref[...], preferred_element_type=F32)
    acc = acc + jnp.dot(a2_ref[...], w2_ref[...], preferred_element_type=F32)
    o_ref[...] = x_ref[...] + gate_ref[...] * acc


def _out_linear(a1_lat, a1_ctx, a2, w, layer, x, mod, which):
    tn = D_MODEL
    k1, k2 = a1_lat.shape[1], a2.shape[1]
    lat_tiles = N_LAT // TM
    return pl.pallas_call(
        _out_linear_kernel,
        out_shape=jax.ShapeDtypeStruct((N_TOK, D_MODEL), F32),
        grid=(N_TOK // TM, D_MODEL // tn),
        in_specs=[
            pl.BlockSpec((TM, k1), lambda i, j: (jnp.minimum(i, lat_tiles - 1), 0)),
            pl.BlockSpec((TM, k1), lambda i, j: (jnp.maximum(i - lat_tiles, 0), 0)),
            pl.BlockSpec((TM, k2), lambda i, j: (i, 0)),
            pl.BlockSpec((None, k1, tn), lambda i, j: (layer, 0, j)),
            pl.BlockSpec((None, k2, tn), lambda i, j: (layer, k1 // k2, j)),
            pl.BlockSpec((TM, tn), lambda i, j: (i, j)),
            pl.BlockSpec((None, None, 1, tn), lambda i, j: (_mod_row(i, TM), which, 0, j)),
        ],
        out_specs=pl.BlockSpec((TM, tn), lambda i, j: (i, j)),
        compiler_params=_cparams(("arbitrary", "arbitrary")),
        name="out_linear",
    )(a1_lat, a1_ctx, a2, w, w, x, mod)


def _rope_tables(rot_dim):
    rows = SEQ // GRID_W
    row_ids = np.repeat(np.arange(rows, dtype=np.float32), GRID_W)
    col_ids = np.tile(np.arange(GRID_W, dtype=np.float32), rows)
    d_axis = rot_dim // 2
    inv = (np.float32(ROPE_THETA) ** (-np.arange(0, d_axis, 2, dtype=np.float32) / np.float32(d_axis))).astype(np.float32)
    ang = np.concatenate([row_ids[:, None] * inv, col_ids[:, None] * inv], axis=-1).astype(np.float32)
    cos, sin = np.cos(ang).astype(np.float32), np.sin(ang).astype(np.float32)
    reps = 128 // rot_dim
    cos_f = np.tile(np.concatenate([cos, cos], axis=-1), (1, reps))
    sin_f = np.tile(np.concatenate([-sin, sin], axis=-1), (1, reps))
    cos_f = np.concatenate([cos_f, np.ones((ROPE_IDENTITY_ROWS, 128), np.float32)], axis=0)
    sin_f = np.concatenate([sin_f, np.zeros((ROPE_IDENTITY_ROWS, 128), np.float32)], axis=0)
    return jnp.asarray(cos_f), jnp.asarray(sin_f)


def _rope128(x, cos, sin):
    return x * cos + pltpu.roll(x, 64, 1) * sin


def _rope64(x, cos, sin):
    lane = lax.broadcasted_iota(jnp.int32, x.shape, 1)
    swapped = jnp.where((lane % 64) < 32, pltpu.roll(x, 96, 1), pltpu.roll(x, 32, 1))
    return x * cos + swapped * sin


def _ones_column(rows):
    lane = lax.broadcasted_iota(jnp.int32, (rows, 128), 1)
    return jnp.where(lane == 0, 1.0, 0.0).astype(BF16)


def _gqa_kv_kernel(k_ref, v_ref, kn_ref, cos_ref, sin_ref, ko_ref, vo_ref):
    cos, sin = cos_ref[...], sin_ref[...]
    ones = _ones_column(k_ref.shape[0])
    for h in range(A_KV_HEADS):
        k = _rms(k_ref[:, h * 128:(h + 1) * 128]) * kn_ref[...]
        ko_ref[:, h * 128:(h + 1) * 128] = _rope128(k, cos, sin).astype(BF16)
        vo_ref[:, h * 256:h * 256 + 128] = v_ref[:, h * 128:(h + 1) * 128].astype(BF16)
        vo_ref[:, h * 256 + 128:(h + 1) * 256] = ones


def _gqa_kv(p, k_norm, cos, sin):
    t = 512
    return pl.pallas_call(
        _gqa_kv_kernel,
        out_shape=[jax.ShapeDtypeStruct((N_TOK, A_KV_WIDTH), BF16),
                   jax.ShapeDtypeStruct((N_TOK, 2 * A_KV_WIDTH), BF16)],
        grid=(N_TOK // t,),
        in_specs=[
            pl.BlockSpec((t, A_KV_WIDTH), lambda i: (i, A_WIDTH // A_KV_WIDTH)),
            pl.BlockSpec((t, A_KV_WIDTH), lambda i: (i, A_WIDTH // A_KV_WIDTH + 1)),
            pl.BlockSpec((1, 128), lambda i: (0, 0)),
            pl.BlockSpec((t, 128), lambda i: (_rope_block(i, t), 0)),
            pl.BlockSpec((t, 128), lambda i: (_rope_block(i, t), 0)),
        ],
        out_specs=[pl.BlockSpec((t, A_KV_WIDTH), lambda i: (i, 0)),
                   pl.BlockSpec((t, 2 * A_KV_WIDTH), lambda i: (i, 0))],
        compiler_params=_cparams(("arbitrary",)),
        name="gqa_kv",
    )(p, p, k_norm, cos, sin)


LOG2E = math.log2(math.e)
KEY_PIECE = 1024


def _softmax_pv(q, keys, vals):
    dn = (((1,), (1,)), ((), ()))
    m = acc = None
    for k, v in zip(keys, vals):
        s = lax.dot_general(q, k, dn, preferred_element_type=F32)
        m_piece = jnp.max(s, axis=-1, keepdims=True)
        if m is None:
            m_new = m_piece
        else:
            m_new = jnp.maximum(m, m_piece)
            acc = acc * jnp.exp2(m - m_new)
        part = jnp.dot(jnp.exp2(s - m_new).astype(BF16), v, preferred_element_type=F32)
        acc = part if acc is None else acc + part
        m = m_new
    return acc[:, :128] / acc[:, 128:129]


def _attn_qrow(b, t):
    nq = SEQ // TQ
    return jnp.where(t == nq, (N_LAT + b * CTX_LEN) // TQ, b * nq + t)


def _ctx_half():
    return pl.multiple_of((pl.program_id(0) % (TQ // CTX_LEN)) * CTX_LEN, CTX_LEN)


def _gqa_attn_kernel(q_ref, qn_ref, cos_ref, sin_ref, kc_ref, vc_ref, kl_ref, vl_ref, ol_ref, oc_ref):
    scale = HEAD_DIM ** -0.5 * LOG2E
    is_ctx = pl.program_id(2) == SEQ // TQ

    def query(rows, g):
        q = _rms(q_ref[rows, g * 128:(g + 1) * 128]) * qn_ref[...]
        return (_rope128(q, cos_ref[rows, :], sin_ref[rows, :]) * scale).astype(BF16)

    @pl.when(jnp.logical_not(is_ctx))
    def _():
        pieces = [slice(j * KEY_PIECE, (j + 1) * KEY_PIECE) for j in range(SEQ // KEY_PIECE)]
        qs = [query(slice(None), g) for g in range(A_GROUP)]
        for g in range(A_GROUP):
            o = _softmax_pv(qs[g], [kc_ref[...]] + [kl_ref[r, :] for r in pieces],
                            [vc_ref[...]] + [vl_ref[r, :] for r in pieces])
            ol_ref[:, g * 128:(g + 1) * 128] = o.astype(BF16)

    @pl.when(is_ctx)
    def _():
        rows = pl.ds(_ctx_half(), CTX_LEN)
        for g in range(A_GROUP):
            oc_ref[:, g * 128:(g + 1) * 128] = _softmax_pv(query(rows, g), [kc_ref[...]], [vc_ref[...]]).astype(BF16)


def _gqa_attention(p, k, vext, q_norm, cos, sin):
    qw = A_GROUP * HEAD_DIM
    nq = SEQ // TQ
    ctx_blk = N_LAT // CTX_LEN
    rope = pl.BlockSpec((TQ, 128), lambda b, h, t: (_rope_block(_attn_qrow(b, t), TQ), 0))
    return pl.pallas_call(
        _gqa_attn_kernel,
        out_shape=[jax.ShapeDtypeStruct((N_LAT, A_WIDTH), BF16), jax.ShapeDtypeStruct((N_CTX, A_WIDTH), BF16)],
        grid=(BATCH, A_KV_HEADS, nq + 1),
        in_specs=[
            pl.BlockSpec((TQ, qw), lambda b, h, t: (_attn_qrow(b, t), h)),
            pl.BlockSpec((1, 128), lambda b, h, t: (0, 0)),
            rope, rope,
            pl.BlockSpec((CTX_LEN, 128), lambda b, h, t: (ctx_blk + b, h)),
            pl.BlockSpec((CTX_LEN, 256), lambda b, h, t: (ctx_blk + b, h)),
            pl.BlockSpec((SEQ, 128), lambda b, h, t: (b, h)),
            pl.BlockSpec((SEQ, 256), lambda b, h, t: (b, h)),
        ],
        out_specs=[pl.BlockSpec((TQ, qw), lambda b, h, t: (b * nq + jnp.minimum(t, nq - 1), h)),
                   pl.BlockSpec((CTX_LEN, qw), lambda b, h, t: (b, h))],
        compiler_params=_cparams(("arbitrary",) * 3),
        name="gqa_attn",
    )(p, q_norm, cos, sin, k, vext, k, vext)


def _conv_kernel(bg_ref, cg_ref, ug_ref, cgp_ref, ugp_ref, cgn_ref, ugn_ref, w_ref, o_ref, *, rows):
    i = pl.program_id(0)
    per_seq = SEQ // rows
    is_ctx = i >= N_LAT // rows
    is_start = jnp.logical_or(is_ctx, i % per_seq == 0)
    is_end = jnp.logical_or(is_ctx, i % per_seq == per_seq - 1)
    m = cg_ref[...] * ug_ref[...]
    m_prev = jnp.where(is_start, 0.0, cgp_ref[7:8, :] * ugp_ref[7:8, :])
    m_next = jnp.where(is_end, 0.0, cgn_ref[0:1, :] * ugn_ref[0:1, :])
    row = lax.broadcasted_iota(jnp.int32, m.shape, 0)
    down = jnp.where(row == 0, m_prev, pltpu.roll(m, 1, 0))
    up = jnp.where(row == rows - 1, m_next, pltpu.roll(m, rows - 1, 0))
    conv = down * w_ref[0:1, :] + m * w_ref[1:2, :] + up * w_ref[2:3, :]
    o_ref[...] = (bg_ref[...] * conv).astype(BF16)


def _gated_conv(p, conv_w):
    rows = CTX_LEN
    base = (A_WIDTH + 2 * A_KV_WIDTH) // B_WIDTH
    halo = rows // 8
    last = N_TOK // 8 - 1
    main = lambda c: pl.BlockSpec((rows, B_WIDTH), lambda i: (i, base + c))
    prev = lambda c: pl.BlockSpec((8, B_WIDTH), lambda i: (jnp.maximum(i * halo - 1, 0), base + c))
    nxt = lambda c: pl.BlockSpec((8, B_WIDTH), lambda i: (jnp.minimum((i + 1) * halo, last), base + c))
    return pl.pallas_call(
        functools.partial(_conv_kernel, rows=rows),
        out_shape=jax.ShapeDtypeStruct((N_TOK, B_WIDTH), BF16),
        grid=(N_TOK // rows,),
        in_specs=[main(0), main(1), main(2), prev(1), prev(2), nxt(1), nxt(2),
                  pl.BlockSpec((3, B_WIDTH), lambda i: (0, 0))],
        out_specs=pl.BlockSpec((rows, B_WIDTH), lambda i: (i, 0)),
        compiler_params=_cparams(("arbitrary",)),
        name="gated_conv",
    )(p, p, p, p, p, p, p, conv_w)


def _mla_kv_kernel(ckv_ref, g_ref, w_ref, kr_ref, cos_ref, sin_ref, ko_ref, vo_ref):
    ckv = (_rms(ckv_ref[...]) * g_ref[...]).astype(BF16)
    kv = jnp.dot(ckv, w_ref[...], preferred_element_type=F32)
    x = kr_ref[...]
    lane = lax.broadcasted_iota(jnp.int32, x.shape, 1)
    r = jnp.where(lane < C_ROPE, _rope64(x, cos_ref[...], sin_ref[...]), 0.0)
    kr_low = r.astype(BF16)
    kr_high = pltpu.roll(r, 64, 1).astype(BF16)
    ones = _ones_column(x.shape[0])
    for h in range(C_HEADS):
        ko_ref[:, h * 256:h * 256 + 128] = kv[:, h * 256:h * 256 + 128].astype(BF16)
        ko_ref[:, h * 256 + 128:(h + 1) * 256] = kr_low if h % 2 == 0 else kr_high
        vo_ref[:, h * 256:h * 256 + 128] = kv[:, h * 256 + 128:(h + 1) * 256].astype(BF16)
        vo_ref[:, h * 256 + 128:(h + 1) * 256] = ones


def _mla_kv(p, kv_norm, w_ukv, layer, cos, sin):
    t = 512
    col = (C_Q_RANK + C_KV_RANK + D_WIDTH) // 128
    width = C_HEADS * 256
    return pl.pallas_call(
        _mla_kv_kernel,
        out_shape=[jax.ShapeDtypeStruct((N_TOK, width), BF16)] * 2,
        grid=(N_TOK // t,),
        in_specs=[
            pl.BlockSpec((t, C_KV_RANK), lambda i: (i, C_Q_RANK // C_KV_RANK)),
            pl.BlockSpec((1, C_KV_RANK), lambda i: (0, 0)),
            pl.BlockSpec((None, C_KV_RANK, width), lambda i: (layer, 0, 0)),
            pl.BlockSpec((t, 128), lambda i: (i, col)),
            pl.BlockSpec((t, 128), lambda i: (_rope_block(i, t), 0)),
            pl.BlockSpec((t, 128), lambda i: (_rope_block(i, t), 0)),
        ],
        out_specs=[pl.BlockSpec((t, width), lambda i: (i, 0))] * 2,
        compiler_params=_cparams(("arbitrary",)),
        name="mla_kv",
    )(p, kv_norm, w_ukv, p, cos, sin)


def _mla_attn_kernel(qn_ref, qr_ref, cos_ref, sin_ref, kc_ref, vc_ref, kl_ref, vl_ref, ol_ref, oc_ref):
    scale = (C_NOPE + C_ROPE) ** -0.5 * LOG2E
    is_ctx = pl.program_id(2) == SEQ // TQ

    def queries(rows):
        qr = _rope64(qr_ref[rows, :], cos_ref[rows, :], sin_ref[rows, :])
        lane = lax.broadcasted_iota(jnp.int32, qr.shape, 1)
        out = []
        for hh in range(2):
            sel = (lane < 64) if hh == 0 else (lane >= 64)
            q = jnp.concatenate([qn_ref[rows, hh * 128:(hh + 1) * 128], jnp.where(sel, qr, 0.0)], axis=1)
            out.append((q * scale).astype(BF16))
        return out

    @pl.when(jnp.logical_not(is_ctx))
    def _():
        for hh, q in enumerate(queries(slice(None))):
            blk = slice(hh * 256, (hh + 1) * 256)
            pieces = [slice(j * KEY_PIECE, (j + 1) * KEY_PIECE) for j in range(SEQ // KEY_PIECE)]
            o = _softmax_pv(q, [kc_ref[:, blk]] + [kl_ref[r, blk] for r in pieces],
                            [vc_ref[:, blk]] + [vl_ref[r, blk] for r in pieces])
            ol_ref[:, hh * 128:(hh + 1) * 128] = o.astype(BF16)

    @pl.when(is_ctx)
    def _():
        for hh, q in enumerate(queries(pl.ds(_ctx_half(), CTX_LEN))):
            blk = slice(hh * 256, (hh + 1) * 256)
            oc_ref[:, hh * 128:(hh + 1) * 128] = _softmax_pv(q, [kc_ref[:, blk]], [vc_ref[:, blk]]).astype(BF16)


def _mla_attention(q, kcat, vext, cos, sin):
    nq = SEQ // TQ
    ctx_blk = N_LAT // CTX_LEN
    rope_col = C_HEADS * C_NOPE // 128
    qrow = _attn_qrow
    rope = pl.BlockSpec((TQ, 128), lambda b, h, t: (_rope_block(qrow(b, t), TQ), 0))
    ctx = pl.BlockSpec((CTX_LEN, 512), lambda b, h, t: (ctx_blk + b, h))
    lat = pl.BlockSpec((SEQ, 512), lambda b, h, t: (b, h))
    return pl.pallas_call(
        _mla_attn_kernel,
        out_shape=[jax.ShapeDtypeStruct((N_LAT, C_WIDTH), BF16), jax.ShapeDtypeStruct((N_CTX, C_WIDTH), BF16)],
        grid=(BATCH, C_HEADS // 2, nq + 1),
        in_specs=[
            pl.BlockSpec((TQ, 256), lambda b, h, t: (qrow(b, t), h)),
            pl.BlockSpec((TQ, 128), lambda b, h, t: (qrow(b, t), rope_col + h)),
            rope, rope, ctx, ctx, lat, lat,
        ],
        out_specs=[pl.BlockSpec((TQ, 256), lambda b, h, t: (b * nq + jnp.minimum(t, nq - 1), h)),
                   pl.BlockSpec((CTX_LEN, 256), lambda b, h, t: (b, h))],
        compiler_params=_cparams(("arbitrary",) * 3),
        name="mla_attn",
    )(q, q, cos, sin, kcat, vext, kcat, vext)


def _s5_matrices(lam_re, lam_im, log_dt, b_re, b_im, c_re, c_im):
    hi = lax.Precision.HIGHEST
    lam_re, lam_im = lam_re.astype(F32), lam_im.astype(F32)
    dt = jnp.exp(log_dt.astype(F32))[..., None]
    ks = jnp.arange(S5_CHUNK + 1, dtype=F32)[:, None, None, None]
    mag = jnp.exp(lam_re[None] * dt[None] * ks)
    ang = lam_im[None] * dt[None] * ks
    pw_re, pw_im = mag * jnp.cos(ang), mag * jnp.sin(ang)
    a_re, a_im = pw_re[1], pw_im[1]
    den = lam_re * lam_re + lam_im * lam_im
    f_re = ((a_re - 1.0) * lam_re + a_im * lam_im) / den
    f_im = (a_im * lam_re - (a_re - 1.0) * lam_im) / den
    b_re, b_im = b_re.astype(F32), b_im.astype(F32)
    bb_re = f_re[..., None] * b_re - f_im[..., None] * b_im
    bb_im = f_re[..., None] * b_im + f_im[..., None] * b_re
    c_re, c_im = c_re.astype(F32), c_im.astype(F32)

    ab_re = pw_re[:S5_CHUNK, ..., None] * bb_re[None] - pw_im[:S5_CHUNK, ..., None] * bb_im[None]
    ab_im = pw_re[:S5_CHUNK, ..., None] * bb_im[None] + pw_im[:S5_CHUNK, ..., None] * bb_re[None]
    kern = (jnp.einsum('dgcn,ldgne->ldgce', c_re, ab_re, precision=hi)
            - jnp.einsum('dgcn,ldgne->ldgce', c_im, ab_im, precision=hi))
    t_idx = np.arange(S5_CHUNK)
    sub = lambda m, ax: m.reshape(m.shape[:ax] + (S5_QUADS, S5_GSUB) + m.shape[ax + 1:])
    klag = jnp.transpose(sub(kern, 2), (1, 2, 0, 5, 3, 4)).reshape(2, S5_QUADS, S5_CHUNK, S5_GROUP, 128)
    bt_re = jnp.transpose(bb_re, (0, 1, 3, 2))
    bt_im = jnp.transpose(bb_im, (0, 1, 3, 2))
    lane_gc = lambda c: jnp.transpose(sub(c, 0), (0, 3, 1, 2)).reshape(S5_QUADS, S5_STATE, 128)
    ps, qs = [], []
    for d in range(2):
        p_pow = (S5_CHUNK - 1 - t_idx) if d == 0 else t_idx
        ar, ai = pw_re[p_pow, d][:, :, None, :], pw_im[p_pow, d][:, :, None, :]
        pc = jnp.concatenate([ar * bt_re[d][None] - ai * bt_im[d][None],
                              ar * bt_im[d][None] + ai * bt_re[d][None]], axis=-1)
        pc = pc.reshape(S5_CHUNK, S5_QUADS, 128, 128)
        ps.append(jnp.transpose(pc, (1, 0, 2, 3)).reshape(S5_QUADS, S5_CHUNK * 128, 128))
        q_pow = (t_idx + 1) if d == 0 else (S5_CHUNK - t_idx)
        cl_re, cl_im = lane_gc(c_re[d])[:, None], lane_gc(c_im[d])[:, None]
        rep = lambda a: jnp.repeat(jnp.transpose(sub(a, 1), (1, 0, 3, 2)), S5_GROUP, axis=-1)
        al_re, al_im = rep(pw_re[q_pow, d]), rep(pw_im[q_pow, d])
        qs.append(jnp.concatenate([cl_re * al_re - cl_im * al_im,
                                   -(cl_re * al_im + cl_im * al_re)], axis=2))
    a16_re, a16_im = pw_re[S5_CHUNK], pw_im[S5_CHUNK]
    lanes = S5_GROUPS * 2 * S5_STATE
    return dict(
        klag=klag.astype(BF16),
        p=jnp.stack(ps).astype(BF16),
        q=jnp.stack(qs).astype(BF16),
        a_mul=jnp.concatenate([a16_re, a16_re], axis=-1).reshape(2, 1, lanes),
        a_swp=jnp.concatenate([-a16_im, a16_im], axis=-1).reshape(2, 1, lanes),
    )


def _s5_expand_p(pc):
    rep = jnp.concatenate([pc] * S5_GSUB, axis=1)
    row = lax.broadcasted_iota(jnp.int32, rep.shape, 0)
    col = lax.broadcasted_iota(jnp.int32, rep.shape, 1)
    return jnp.where((row // S5_GROUP) % S5_GSUB == col // 128, rep, jnp.zeros_like(rep))


def _s5_expand_q(q_ref):
    row = lax.broadcasted_iota(jnp.int32, (S5_GSUB * 128, 128), 0)
    col = lax.broadcasted_iota(jnp.int32, (S5_GSUB * 128, 128), 1)
    keep = row // 128 == col // S5_GROUP
    blocks = [jnp.where(keep, jnp.concatenate([q_ref[t]] * S5_GSUB, axis=0), jnp.zeros((S5_GSUB * 128, 128), BF16))
              for t in range(S5_CHUNK)]
    return jnp.concatenate(blocks, axis=1)


def _s5_chunk_rows(u_ref):
    n = N_TOK // S5_CHUNK
    return jnp.concatenate([u_ref[pl.ds(s, n, stride=S5_CHUNK), :] for s in range(S5_CHUNK)], axis=-1).astype(BF16)


def _s5_z_kernel(u_ref, p_ref, z_ref):
    z_ref[...] = jnp.dot(_s5_chunk_rows(u_ref), _s5_expand_p(p_ref[...]), preferred_element_type=F32)


def _s5_scan_kernel(z_ref, amul_ref, aswp_ref, s_ref, zs_scr):
    d = pl.program_id(0)
    a_mul, a_swp = amul_ref[...], aswp_ref[...]
    lanes = s_ref.shape[1]
    z = z_ref[...]
    low_half = lax.broadcasted_iota(jnp.int32, z.shape, 1) % 128 < S5_STATE
    zs_scr[...] = jnp.where(low_half, pltpu.roll(z, lanes - S5_STATE, 1), pltpu.roll(z, S5_STATE, 1))
    nl, nc = SEQ // S5_CHUNK, S5_CTX_CHUNKS

    def segment(base, count, carry):
        def body(step, carry):
            k = jnp.where(d == 0, step, count - 1 - step)
            out = []
            for b in range(BATCH):
                s, w = carry[2 * b], carry[2 * b + 1]
                row = pl.ds(base + b * count + k, 1)
                s_ref[row, :] = s
                out += [a_mul * s + a_swp * w + z_ref[row, :], a_mul * w - a_swp * s + zs_scr[row, :]]
            return tuple(out)
        return lax.fori_loop(0, count, body, carry)

    zero = jnp.zeros((1, lanes), F32)
    carry = segment(BATCH * nl, nc, (zero,) * (2 * BATCH))
    segment(0, nl, carry)


def _s5_y_kernel(u_ref, klag_ref, s_ref, q_ref, dv_ref, y_ref, m_scr):
    d = pl.program_id(1)
    n = N_TOK // S5_CHUNK
    m_scr[...] = jnp.zeros_like(m_scr)
    row = lax.broadcasted_iota(jnp.int32, (128, 128), 0)
    col = lax.broadcasted_iota(jnp.int32, (128, 128), 1)
    same_group = row // S5_GROUP == col // S5_GROUP
    blocks = [jnp.where(same_group, jnp.concatenate([klag_ref[lag]] * S5_GSUB, axis=0), jnp.zeros((128, 128), BF16))
              for lag in range(S5_CHUNK)]
    for rev in range(2):
        @pl.when(d == rev)
        def _():
            for s in range(S5_CHUNK):
                for t in range(S5_CHUNK):
                    lag = (s - t) if rev else (t - s)
                    if lag >= 0:
                        m_scr[s * 128:(s + 1) * 128, t * 128:(t + 1) * 128] = blocks[lag]
    y = jnp.dot(_s5_chunk_rows(u_ref), m_scr[...], preferred_element_type=F32)
    y = y + jnp.dot(s_ref[...].astype(BF16), _s5_expand_q(q_ref), preferred_element_type=F32)
    for t in range(S5_CHUNK):
        rows = pl.ds(t, n, stride=S5_CHUNK)
        yt = y[:, t * 128:(t + 1) * 128]

        @pl.when(d == 0)
        def _():
            y_ref[rows, :] = yt + u_ref[rows, :] * dv_ref[...]

        @pl.when(d == 1)
        def _():
            y_ref[rows, :] = y_ref[rows, :] + yt


def _s5(p, mats, layer, dskip):
    nstate = S5_GROUPS * 2 * S5_STATE
    qlanes = nstate // S5_QUADS
    slanes = 512
    ucol = (C_Q_RANK + C_KV_RANK) // 128
    z = pl.pallas_call(
        _s5_z_kernel,
        out_shape=jax.ShapeDtypeStruct((2, S5_ROWS, nstate), F32),
        grid=(2, S5_QUADS),
        in_specs=[
            pl.BlockSpec((N_TOK, 128), lambda d, q: (0, ucol + q)),
            pl.BlockSpec((None, None, None, S5_CHUNK * 128, 128), lambda d, q: (layer, d, q, 0, 0)),
        ],
        out_specs=pl.BlockSpec((None, S5_ROWS, qlanes), lambda d, q: (d, 0, q)),
        compiler_params=_cparams(("arbitrary", "arbitrary")),
        name="s5_chunk_state",
    )(p, mats["p"])
    s = pl.pallas_call(
        _s5_scan_kernel,
        out_shape=jax.ShapeDtypeStruct((2, S5_ROWS, nstate), F32),
        grid=(2, nstate // slanes),
        in_specs=[
            pl.BlockSpec((None, S5_ROWS, slanes), lambda d, q: (d, 0, q)),
            pl.BlockSpec((None, None, 1, slanes), lambda d, q: (layer, d, 0, q)),
            pl.BlockSpec((None, None, 1, slanes), lambda d, q: (layer, d, 0, q)),
        ],
        out_specs=pl.BlockSpec((None, S5_ROWS, slanes), lambda d, q: (d, 0, q)),
        scratch_shapes=[pltpu.VMEM((S5_ROWS, slanes), F32)],
        compiler_params=_cparams(("arbitrary", "arbitrary")),
        name="s5_scan",
    )(z, mats["a_mul"], mats["a_swp"])
    return pl.pallas_call(
        _s5_y_kernel,
        out_shape=jax.ShapeDtypeStruct((N_TOK, D_WIDTH), F32),
        grid=(S5_QUADS, 2),
        in_specs=[
            pl.BlockSpec((N_TOK, 128), lambda q, d: (0, ucol + q)),
            pl.BlockSpec((None, None, None, S5_CHUNK, S5_GROUP, 128), lambda q, d: (layer, d, q, 0, 0, 0)),
            pl.BlockSpec((None, S5_ROWS, qlanes), lambda q, d: (d, 0, q)),
            pl.BlockSpec((None, None, None, S5_CHUNK, 128, 128), lambda q, d: (layer, d, q, 0, 0, 0)),
            pl.BlockSpec((None, 1, 128), lambda q, d: (layer, 0, q)),
        ],
        out_specs=pl.BlockSpec((N_TOK, 128), lambda q, d: (0, q)),
        scratch_shapes=[pltpu.VMEM((S5_CHUNK * 128, S5_CHUNK * 128), BF16)],
        compiler_params=pltpu.CompilerParams(dimension_semantics=("arbitrary", "arbitrary"),
                                             vmem_limit_bytes=56 * 1024 * 1024),
        name="s5_output",
    )(p, mats["klag"], s, mats["q"], dskip)


def _glu_kernel(y_ref, w_ref, o_ref):
    y = y_ref[...]
    z = y * (0.5 * (1.0 + jnp.tanh(math.sqrt(2.0 / math.pi) * (y + 0.044715 * (y * y * y)))))
    gate = jnp.dot(z.astype(BF16), w_ref[...], preferred_element_type=F32)
    o_ref[...] = (z * jax.nn.sigmoid(gate)).astype(BF16)


def _s5_glu(y, w_glu, layer):
    t = 512
    return pl.pallas_call(
        _glu_kernel,
        out_shape=jax.ShapeDtypeStruct((N_TOK, D_WIDTH), BF16),
        grid=(N_TOK // t,),
        in_specs=[pl.BlockSpec((t, D_WIDTH), lambda i: (i, 0)),
                  pl.BlockSpec((None, D_WIDTH, D_WIDTH), lambda i: (layer, 0, 0))],
        out_specs=pl.BlockSpec((t, D_WIDTH), lambda i: (i, 0)),
        compiler_params=_cparams(("arbitrary",)),
        name="s5_glu",
    )(y, w_glu)


LANE_CHUNKS = D_MODEL // 128


def _store_token_major(ref, val):
    rows = val.shape[0]
    for c in range(LANE_CHUNKS):
        ref[pl.ds(c, rows, stride=LANE_CHUNKS), :] = val[:, c * 128:(c + 1) * 128]


def _load_token_major(ref, rows):
    return jnp.concatenate([ref[pl.ds(c, rows, stride=LANE_CHUNKS), :] for c in range(LANE_CHUNKS)], axis=-1)


def _router_kernel(x_ref, g_ref, sh_ref, sc_ref, whi_ref, wlo_ref, h_ref, info_ref, gw_ref, cnt_ref, h_scr, carry):
    i = pl.program_id(0)

    @pl.when(i == 0)
    def _():
        carry[...] = jnp.zeros_like(carry)

    for r in range(0, TM, 256):
        h = _rms(x_ref[r:r + 256, :]) * g_ref[...]
        h_scr[r:r + 256, :] = h * (1.0 + sc_ref[...]) + sh_ref[...]
    _store_token_major(h_ref, h_scr[...])
    h = h_scr[...]
    h_hi = h.astype(BF16)
    h_lo = (h - h_hi.astype(F32)).astype(BF16)
    logits = (jnp.dot(h_hi, whi_ref[...], preferred_element_type=F32)
              + jnp.dot(h_lo, whi_ref[...], preferred_element_type=F32)
              + jnp.dot(h_hi, wlo_ref[...], preferred_element_type=F32))
    lane = lax.broadcasted_iota(jnp.int32, logits.shape, 1)
    neg = -jnp.inf
    big = jnp.int32(1 << 20)

    def first_argmax(v, vmax):
        return jnp.min(jnp.where(v == vmax, lane, big), axis=-1, keepdims=True)

    lg = jnp.where(lane < N_GROUPS, logits, neg)
    mg = jnp.max(lg, axis=-1, keepdims=True)
    g_w = 1.0 / jnp.sum(jnp.exp(lg - mg), axis=-1, keepdims=True)
    g_idx = first_argmax(lg, mg)
    lo = N_GROUPS + EXPERTS_PER_GROUP * g_idx
    le = jnp.where(jnp.logical_and(lane >= lo, lane < lo + EXPERTS_PER_GROUP), logits, neg)
    m1 = jnp.max(le, axis=-1, keepdims=True)
    i1 = first_argmax(le, m1)
    le2 = jnp.where(lane == i1, neg, le)
    m2 = jnp.max(le2, axis=-1, keepdims=True)
    i2 = first_argmax(le2, m2)
    r21 = jnp.exp(m2 - m1)
    w1 = g_w / (1.0 + r21)
    w2 = g_w * r21 / (1.0 + r21)
    oh = jnp.logical_or(lane == i1, lane == i2)
    ohb = jnp.where(oh, 1.0, 0.0).astype(BF16)
    rr = lax.broadcasted_iota(jnp.int32, (TM, TM), 0)
    cc = lax.broadcasted_iota(jnp.int32, (TM, TM), 1)
    lower = jnp.where(rr > cc, 1.0, 0.0).astype(BF16)
    before = jnp.dot(lower, ohb, preferred_element_type=F32) + carry[...]
    rank1 = jnp.sum(jnp.where(lane == i1, before, 0.0), axis=-1, keepdims=True).astype(jnp.int32)
    rank2 = jnp.sum(jnp.where(lane == i2, before, 0.0), axis=-1, keepdims=True).astype(jnp.int32)
    carry[...] = carry[...] + jnp.sum(ohb.astype(F32), axis=0, keepdims=True)
    cnt_ref[...] = jnp.broadcast_to(carry[...], cnt_ref.shape)
    info = jnp.where(lane == 0, i1 - N_GROUPS, jnp.where(lane == 1, i2 - N_GROUPS,
                     jnp.where(lane == 2, rank1, jnp.where(lane == 3, rank2, 0))))
    info_ref[...] = info
    gw_ref[...] = jnp.where(lane == 0, w1, jnp.where(lane == 1, w2, 0.0))


def _router(x, g, mod, w_router, layer):
    return pl.pallas_call(
        _router_kernel,
        out_shape=[jax.ShapeDtypeStruct((N_TOK * LANE_CHUNKS, 128), F32),
                   jax.ShapeDtypeStruct((N_TOK, 128), jnp.int32),
                   jax.ShapeDtypeStruct((N_TOK, 128), F32),
                   jax.ShapeDtypeStruct((8, 128), F32)],
        grid=(N_TOK // TM,),
        in_specs=[
            pl.BlockSpec((TM, D_MODEL), lambda i: (i, 0)),
            pl.BlockSpec((1, D_MODEL), lambda i: (0, 0)),
            pl.BlockSpec((None, None, 1, D_MODEL), lambda i: (_mod_row(i, TM), 3, 0, 0)),
            pl.BlockSpec((None, None, 1, D_MODEL), lambda i: (_mod_row(i, TM), 4, 0, 0)),
            pl.BlockSpec((None, D_MODEL, 128), lambda i: (layer, 0, 0)),
            pl.BlockSpec((None, D_MODEL, 128), lambda i: (layer, 0, 0)),
        ],
        out_specs=[pl.BlockSpec((TM * LANE_CHUNKS, 128), lambda i: (i, 0)),
                   pl.BlockSpec((TM, 128), lambda i: (i, 0)),
                   pl.BlockSpec((TM, 128), lambda i: (i, 0)),
                   pl.BlockSpec((8, 128), lambda i: (0, 0))],
        scratch_shapes=[pltpu.VMEM((TM, D_MODEL), F32), pltpu.VMEM((1, 128), F32)],
        compiler_params=_cparams(("arbitrary",)),
        name="moe_router",
    )(x, g, mod, mod, *w_router)


FFN_ISSUE_GROUPS = 8
ISSUE_UNROLL = 8


def _ffn_kernel(pos_ref, te_ref, meta_ref, h_hbm, wg_ref, wu_ref, wd_ref, o_ref,
                src, xbuf, wg_b, wu_b, wd_b, sem):
    t = pl.program_id(0)
    nt = meta_ref[0]

    def row_copy(tile, slot, r):
        tok = src[tile * TE + r]
        return pltpu.make_async_copy(
            h_hbm.at[pl.ds(pl.multiple_of(tok * LANE_CHUNKS, LANE_CHUNKS), LANE_CHUNKS), :],
            xbuf.at[slot, pl.ds(pl.multiple_of(r * LANE_CHUNKS, LANE_CHUNKS), LANE_CHUNKS), :],
            sem.at[slot])

    def gather(tile, slot):
        def body(g, _):
            for k in range(ISSUE_UNROLL):
                row_copy(tile, slot, g * ISSUE_UNROLL + k).start(priority=k % 2)
            return 0
        lax.fori_loop(0, TE // ISSUE_UNROLL, body, 0)

    def wait_tile(slot):
        pltpu.make_async_copy(xbuf.at[slot], xbuf.at[slot], sem.at[slot]).wait()

    @pl.when(t == 0)
    def _():
        for e in range(N_EXPERTS):
            def clear(i, _):
                src[i] = 0
                return 0
            lax.fori_loop(meta_ref[1 + e], meta_ref[1 + N_EXPERTS + e], clear, 0)

        def fill(tok, _):
            src[pos_ref[2 * tok]] = tok
            src[pos_ref[2 * tok + 1]] = tok
            return 0
        lax.fori_loop(0, N_TOK, fill, 0, unroll=8)
        gather(0, 0)

    @pl.when(t >= nt)
    def _():
        o_ref[...] = jnp.zeros_like(o_ref)

    @pl.when(t < nt)
    def _():
        slot = t % 2
        wait_tile(slot)

        @pl.when(jnp.logical_or(t == 0, te_ref[t] != te_ref[jnp.maximum(t - 1, 0)]))
        def _():
            wg_b[...] = wg_ref[...].astype(BF16)
            wu_b[...] = wu_ref[...].astype(BF16)
            wd_b[...] = wd_ref[...].astype(BF16)

        nxt = jnp.minimum(t + 1, nt - 1)
        per = TE // FFN_ISSUE_GROUPS

        def issue(g):
            for r in range(g * per, (g + 1) * per):
                row_copy(nxt, 1 - slot, r).start(priority=r % 2)

        x = _load_token_major(xbuf.at[slot], TE).astype(BF16)
        half = D_EXPERT // 2
        hg, hu = [], []
        for n in range(2):
            issue(n)
            hg.append(jnp.dot(x, wg_b[:, n * half:(n + 1) * half], preferred_element_type=F32))
        for n in range(2):
            issue(2 + n)
            hu.append(jnp.dot(x, wu_b[:, n * half:(n + 1) * half], preferred_element_type=F32))
        hg, hu = jnp.concatenate(hg, axis=1), jnp.concatenate(hu, axis=1)
        act = (hg * jax.nn.sigmoid(hg) * hu).astype(BF16)
        quarter = D_MODEL // 4
        for n in range(4):
            issue(4 + n)
            y = jnp.dot(act, wd_b[:, n * quarter:(n + 1) * quarter], preferred_element_type=F32)
            for c in range(quarter // 128):
                o_ref[pl.ds(n * (quarter // 128) + c, TE, stride=LANE_CHUNKS), :] = y[:, c * 128:(c + 1) * 128]

        @pl.when(t == nt - 1)
        def _():
            wait_tile(1 - slot)


def _expert_ffn(pos_flat, tile_expert, meta, h, w_gate, w_up, w_down, layer):
    wsel = lambda t, pos, te, meta: (layer, te[t], 0, 0)
    return pl.pallas_call(
        _ffn_kernel,
        out_shape=jax.ShapeDtypeStruct((N_SORT * LANE_CHUNKS, 128), F32),
        grid_spec=pltpu.PrefetchScalarGridSpec(
            num_scalar_prefetch=3,
            grid=(N_ETILES,),
            in_specs=[
                pl.BlockSpec(memory_space=pl.ANY),
                pl.BlockSpec((None, None, D_MODEL, D_EXPERT), wsel),
                pl.BlockSpec((None, None, D_MODEL, D_EXPERT), wsel),
                pl.BlockSpec((None, None, D_EXPERT, D_MODEL), wsel),
            ],
            out_specs=pl.BlockSpec((TE * LANE_CHUNKS, 128), lambda t, pos, te, meta: (t, 0)),
            scratch_shapes=[
                pltpu.SMEM((N_SORT,), jnp.int32),
                pltpu.VMEM((2, TE * LANE_CHUNKS, 128), F32),
                pltpu.VMEM((D_MODEL, D_EXPERT), BF16),
                pltpu.VMEM((D_MODEL, D_EXPERT), BF16),
                pltpu.VMEM((D_EXPERT, D_MODEL), BF16),
                pltpu.SemaphoreType.DMA((2,)),
            ],
        ),
        compiler_params=pltpu.CompilerParams(dimension_semantics=("arbitrary",),
                                             vmem_limit_bytes=56 * 1024 * 1024),
        name="moe_expert_ffn",
    )(pos_flat, tile_expert, meta, h, w_gate, w_up, w_down)


TC = 256


def _combine_kernel(pos_ref, x_ref, gate_ref, gw_ref, g_ref, sh_ref, sc_ref, ys_hbm, *refs, final_norm):
    if final_norm:
        o_ref, buf_a, buf_b, sem = refs
    else:
        o_ref, h_ref, buf_a, buf_b, sem = refs
    i = pl.program_id(0)
    slot = i % 2

    def slab(ref, row):
        return ref.at[pl.ds(pl.multiple_of(row * LANE_CHUNKS, LANE_CHUNKS), LANE_CHUNKS), :]

    def gather(tile, slot):
        def body(g, _):
            for k in range(ISSUE_UNROLL):
                r = g * ISSUE_UNROLL + k
                tok = tile * TC + r
                pltpu.make_async_copy(slab(ys_hbm, pos_ref[2 * tok]), slab(buf_a.at[slot], r),
                                      sem.at[slot, 0]).start(priority=0)
                pltpu.make_async_copy(slab(ys_hbm, pos_ref[2 * tok + 1]), slab(buf_b.at[slot], r),
                                      sem.at[slot, 1]).start(priority=1)
            return 0
        lax.fori_loop(0, TC // ISSUE_UNROLL, body, 0)

    @pl.when(i == 0)
    def _():
        gather(0, 0)

    @pl.when(i + 1 < pl.num_programs(0))
    def _():
        gather(i + 1, 1 - slot)

    pltpu.make_async_copy(buf_a.at[slot], buf_a.at[slot], sem.at[slot, 0]).wait()
    pltpu.make_async_copy(buf_b.at[slot], buf_b.at[slot], sem.at[slot, 1]).wait()
    w0 = jnp.broadcast_to(gw_ref[:, 0:1], (TC, 128))
    w1 = jnp.broadcast_to(gw_ref[:, 1:2], (TC, 128))
    sq = jnp.zeros((TC, 128), F32)
    for c in range(LANE_CHUNKS):
        cols = slice(c * 128, (c + 1) * 128)
        rows = pl.ds(c, TC, stride=LANE_CHUNKS)
        y = x_ref[:, cols] + gate_ref[:, cols] * (w0 * buf_a[slot, rows, :] + w1 * buf_b[slot, rows, :])
        o_ref[:, cols] = y
        sq = sq + y * y
    inv = lax.rsqrt(jnp.sum(sq, axis=-1, keepdims=True) / D_MODEL + EPS)
    for c in range(LANE_CHUNKS):
        cols = slice(c * 128, (c + 1) * 128)
        normed = o_ref[:, cols] * inv * g_ref[:, cols]
        if final_norm:
            o_ref[:, cols] = normed
        else:
            h_ref[:, cols] = (normed * (1.0 + sc_ref[:, cols]) + sh_ref[:, cols]).astype(BF16)


def _combine(pos_flat, x, mod, gw, ys, g, mod_next):
    final_norm = mod_next is None
    rows = N_LAT if final_norm else N_TOK
    tile = pl.BlockSpec((TC, D_MODEL), lambda i, pos: (i, 0))
    mod_row = lambda which: pl.BlockSpec((None, None, 1, D_MODEL), lambda i, pos: (_mod_row(i, TC), which, 0, 0))
    next_tab = mod if final_norm else mod_next
    out_shape = [jax.ShapeDtypeStruct((rows, D_MODEL), F32)]
    if not final_norm:
        out_shape.append(jax.ShapeDtypeStruct((rows, D_MODEL), BF16))
    return pl.pallas_call(
        functools.partial(_combine_kernel, final_norm=final_norm),
        out_shape=out_shape,
        grid_spec=pltpu.PrefetchScalarGridSpec(
            num_scalar_prefetch=1,
            grid=(rows // TC,),
            in_specs=[
                tile,
                mod_row(5),
                pl.BlockSpec((TC, 128), lambda i, pos: (i, 0)),
                pl.BlockSpec((1, D_MODEL), lambda i, pos: (0, 0)),
                mod_row(0), mod_row(1),
                pl.BlockSpec(memory_space=pl.ANY),
            ],
            out_specs=[tile] * len(out_shape),
            scratch_shapes=[pltpu.VMEM((2, TC * LANE_CHUNKS, 128), F32), pltpu.VMEM((2, TC * LANE_CHUNKS, 128), F32),
                            pltpu.SemaphoreType.DMA((2, 2))],
        ),
        compiler_params=_cparams(("arbitrary",)),
        name="moe_combine",
    )(pos_flat, x, mod, gw, g, next_tab, next_tab, ys)


def _moe(x, g, mod, w_router, w_gate, w_up, w_down, layer, g_next, mod_next):
    h, info, gw, cnt = _router(x, g, mod, w_router, layer)
    counts = cnt[0, N_GROUPS:N_GROUPS + N_EXPERTS].astype(jnp.int32)
    padded = ((counts + TE - 1) // TE) * TE
    ends = jnp.cumsum(padded)
    starts = ends - padded
    experts = jnp.arange(N_EXPERTS, dtype=jnp.int32)
    start_of = jnp.sum(jnp.where(info[:, 0:2, None] == experts, starts, 0), axis=-1)
    pos_flat = (start_of + info[:, 2:4]).reshape(-1)
    tile_ends = ends // TE
    num_tiles = tile_ends[-1]
    tiles = jnp.minimum(jnp.arange(N_ETILES, dtype=jnp.int32), num_tiles - 1)
    tile_expert = jnp.sum((tile_ends[None, :] <= tiles[:, None]).astype(jnp.int32), axis=-1)
    meta = jnp.concatenate([num_tiles[None], starts + counts, ends]).astype(jnp.int32)
    ys = _expert_ffn(pos_flat, tile_expert, meta, h, w_gate, w_up, w_down, layer)
    return _combine(pos_flat, x, mod, gw, ys, g_next, mod_next)


def kernel(x, c, ctx, c_ctx, mod_w, mod_b, norm_mix, norm_ffn, ab_w_in, ab_q_norm, ab_k_norm, ab_conv_w, ab_w_out, cd_w_in, cd_q_norm, cd_kv_norm, cd_w_uq, cd_w_ukv, s5_lam_re, s5_lam_im, s5_log_dt, s5_b_re, s5_b_im, s5_c_re, s5_c_im, s5_d, s5_w_glu, cd_w_out, moe_w_group, moe_w_expert, moe_w_gate, moe_w_up, moe_w_down, final_norm):
    cc = jnp.concatenate([c, c_ctx[None, :], jnp.zeros((8 - BATCH - 1, D_MODEL), F32)], axis=0)
    mods = _modulation(cc, mod_w, mod_b).reshape(DEPTH, 8, N_MOD, 1, D_MODEL)
    xs, h = _embed(x.reshape(N_LAT, D_MODEL), ctx.reshape(N_CTX, D_MODEL), norm_mix[0][None, :], mods[0])
    cos_a, sin_a = _rope_tables(HEAD_DIM)
    cos_c, sin_c = _rope_tables(C_ROPE)

    ab_in_b, ab_out_b = ab_w_in.astype(BF16), ab_w_out.astype(BF16)
    cd_out_b, ukv_b, glu_b = cd_w_out.astype(BF16), cd_w_ukv.astype(BF16), s5_w_glu.astype(BF16)
    a, b_ = C_Q_RANK + C_KV_RANK, C_Q_RANK + C_KV_RANK + C_ROPE
    pad = jnp.zeros(cd_w_in.shape[:2] + (CD_IN_PAD - cd_w_in.shape[2],), F32)
    cd_in_b = jnp.concatenate([cd_w_in[..., :a], cd_w_in[..., b_:], cd_w_in[..., a:b_], pad], axis=-1).astype(BF16)
    w_uq = cd_w_uq.reshape(-1, C_Q_RANK, C_HEADS, C_NOPE + C_ROPE)
    uq_b = jnp.concatenate([w_uq[..., :C_NOPE].reshape(-1, C_Q_RANK, C_HEADS * C_NOPE),
                            w_uq[..., C_NOPE:].reshape(-1, C_Q_RANK, C_HEADS * C_ROPE)], axis=-1).astype(BF16)

    mats = jax.vmap(_s5_matrices)(s5_lam_re, s5_lam_im, s5_log_dt, s5_b_re, s5_b_im, s5_c_re, s5_c_im)
    w_router = jnp.concatenate(
        [moe_w_group, jnp.transpose(moe_w_expert, (0, 2, 1, 3)).reshape(DEPTH, D_MODEL, N_EXPERTS),
         jnp.zeros((DEPTH, D_MODEL, 128 - N_GROUPS - N_EXPERTS), F32)], axis=-1)
    w_router_hi = w_router.astype(BF16)
    w_router = (w_router_hi, (w_router - w_router_hi.astype(F32)).astype(BF16))

    for i in range(DEPTH):
        j = i // 2
        mod = mods[i]
        if i % 2 == 0:
            p = _linear(h, ab_in_b, j, 2048)
            k, vext = _gqa_kv(p, ab_k_norm[j][None, :], cos_a, sin_a)
            o_lat, o_ctx = _gqa_attention(p, k, vext, ab_q_norm[j][None, :], cos_a, sin_a)
            side = _gated_conv(p, ab_conv_w[j])
            w_out = ab_out_b
        else:
            p = _linear(h, cd_in_b, j, CD_IN_PAD)
            q = _norm_linear(p, 0, C_Q_RANK, cd_q_norm[j][None, :], uq_b, j)
            kcat, vext = _mla_kv(p, cd_kv_norm[j][None, :], ukv_b, j, cos_c, sin_c)
            o_lat, o_ctx = _mla_attention(q, kcat, vext, cos_c, sin_c)
            y = _s5(p, mats, j, s5_d.astype(F32)[:, None, :])
            side = _s5_glu(y, glu_b, j)
            w_out = cd_out_b
        xs = _out_linear(o_lat, o_ctx, side, w_out, j, xs, mod, 2)
        if i < DEPTH - 1:
            xs, h = _moe(xs, norm_ffn[i][None, :], mod, w_router, moe_w_gate, moe_w_up, moe_w_down, i,
                         norm_mix[i + 1][None, :], mods[i + 1])
        else:
            out, = _moe(xs, norm_ffn[i][None, :], mod, w_router, moe_w_gate, moe_w_up, moe_w_down, i,
                        final_norm[None, :], None)
    return out.reshape(BATCH, SEQ, D_MODEL)
```

```python
import functools
import math

import numpy as np
import jax
import jax.numpy as jnp
from jax import lax
from jax.experimental import pallas as pl
from jax.experimental.pallas import tpu as pltpu

F32 = jnp.float32
BF16 = jnp.bfloat16

D_MODEL = 2048
BATCH = 4
SEQ = 2048
DEPTH = 4
GRID_W = 64
CTX_LEN = 256
ROPE_THETA = 10000.0
EPS = 1e-6
N_MOD = 6
HEAD_DIM = 128
A_Q_HEADS = 12
A_KV_HEADS = 4
A_GROUP = A_Q_HEADS // A_KV_HEADS
A_WIDTH = A_Q_HEADS * HEAD_DIM
A_KV_WIDTH = A_KV_HEADS * HEAD_DIM
B_WIDTH = 512
AB_IN = A_WIDTH + 2 * A_KV_WIDTH + 3 * B_WIDTH
C_HEADS = 12
C_NOPE = 128
C_ROPE = 64
C_V = 128
C_Q_RANK = 512
C_KV_RANK = 256
C_WIDTH = C_HEADS * C_V
D_WIDTH = 512
S5_GROUP = 16
S5_GROUPS = D_WIDTH // S5_GROUP
S5_STATE = 64
N_GROUPS = 4
EXPERTS_PER_GROUP = 4
N_EXPERTS = N_GROUPS * EXPERTS_PER_GROUP
D_EXPERT = 512

N_LAT = BATCH * SEQ
N_CTX = BATCH * CTX_LEN
N_TOK = N_LAT + N_CTX
CTX_ROW = BATCH
CD_IN_PAD = 1536

TM = 512
TQ = 1024
TE = 256
ROPE_IDENTITY_ROWS = max(TM, TQ)
N_SORT = 2 * N_TOK + N_EXPERTS * TE
N_ETILES = N_SORT // TE
S5_CHUNK = 16
S5_CTX_CHUNKS = CTX_LEN // S5_CHUNK
S5_ROWS = N_TOK // S5_CHUNK
S5_GSUB = 128 // S5_GROUP
S5_QUADS = S5_GROUPS // S5_GSUB
VMEM_LIMIT = 48 * 1024 * 1024
VMEM_LIMIT_LARGE = 56 * 1024 * 1024
ROW_CHUNK = 256


def _cparams(sem, limit=VMEM_LIMIT):
    return pltpu.CompilerParams(dimension_semantics=sem, vmem_limit_bytes=limit)


def _mod_row(tile, tile_rows):
    r0 = tile * tile_rows
    return jnp.where(r0 >= N_LAT, CTX_ROW, r0 // SEQ)


def _rope_block(tile, tile_rows):
    r0 = tile * tile_rows
    return jnp.where(r0 >= N_LAT, SEQ // tile_rows, (r0 % SEQ) // tile_rows)


def _rms(x):
    return x * lax.rsqrt(jnp.mean(x * x, axis=-1, keepdims=True) + EPS)


def _mod_kernel(cc_ref, w_ref, b_ref, o_ref):
    cc = cc_ref[...]
    s = (cc * jax.nn.sigmoid(cc)).astype(BF16)
    o_ref[...] = jnp.dot(s, w_ref[...].astype(BF16), preferred_element_type=F32) + b_ref[...]


def _modulation(cc, mod_w, mod_b):
    tn = 1024
    nout = N_MOD * D_MODEL
    return pl.pallas_call(
        _mod_kernel,
        out_shape=jax.ShapeDtypeStruct((DEPTH, 8, nout), F32),
        grid=(DEPTH, nout // tn),
        in_specs=[
            pl.BlockSpec((8, D_MODEL), lambda l, j: (0, 0)),
            pl.BlockSpec((None, D_MODEL, tn), lambda l, j: (l, 0, j)),
            pl.BlockSpec((None, 1, tn), lambda l, j: (l, 0, j)),
        ],
        out_specs=pl.BlockSpec((None, 8, tn), lambda l, j: (l, 0, j)),
        compiler_params=_cparams(("arbitrary", "arbitrary")),
        name="modulation",
    )(cc, mod_w, mod_b.reshape(DEPTH, 1, nout))


def _modulated_norm(x, g, sh, sc):
    return _rms(x) * g * (1.0 + sc) + sh


def _embed_kernel(xl_ref, xc_ref, g_ref, sh_ref, sc_ref, xs_ref, h_ref):
    is_ctx = pl.program_id(0) >= N_LAT // TM
    for r in range(0, TM, ROW_CHUNK):
        rows = slice(r, r + ROW_CHUNK)
        x = jnp.where(is_ctx, xc_ref[rows, :], xl_ref[rows, :])
        xs_ref[rows, :] = x
        h_ref[rows, :] = _modulated_norm(x, g_ref[...], sh_ref[...], sc_ref[...]).astype(BF16)


def _embed(x_lat, x_ctx, g, mod):
    lat_tiles = N_LAT // TM
    return pl.pallas_call(
        _embed_kernel,
        out_shape=[jax.ShapeDtypeStruct((N_TOK, D_MODEL), F32), jax.ShapeDtypeStruct((N_TOK, D_MODEL), BF16)],
        grid=(N_TOK // TM,),
        in_specs=[
            pl.BlockSpec((TM, D_MODEL), lambda i: (jnp.minimum(i, lat_tiles - 1), 0)),
            pl.BlockSpec((TM, D_MODEL), lambda i: (jnp.maximum(i - lat_tiles, 0), 0)),
            pl.BlockSpec((1, D_MODEL), lambda i: (0, 0)),
            pl.BlockSpec((None, None, 1, D_MODEL), lambda i: (_mod_row(i, TM), 0, 0, 0)),
            pl.BlockSpec((None, None, 1, D_MODEL), lambda i: (_mod_row(i, TM), 1, 0, 0)),
        ],
        out_specs=[pl.BlockSpec((TM, D_MODEL), lambda i: (i, 0))] * 2,
        compiler_params=_cparams(("arbitrary",)),
        name="embed_norm",
    )(x_lat, x_ctx, g, mod, mod)


def _linear_kernel(h_ref, w_ref, o_ref):
    o_ref[...] = jnp.dot(h_ref[...], w_ref[...], preferred_element_type=F32)


def _linear(h, w, layer, tn):
    kdim, nout = w.shape[1], w.shape[2]
    return pl.pallas_call(
        _linear_kernel,
        out_shape=jax.ShapeDtypeStruct((N_TOK, nout), F32),
        grid=(N_TOK // TM, nout // tn),
        in_specs=[
            pl.BlockSpec((TM, kdim), lambda i, j: (i, 0)),
            pl.BlockSpec((None, kdim, tn), lambda i, j: (layer, 0, j)),
        ],
        out_specs=pl.BlockSpec((TM, tn), lambda i, j: (i, j)),
        compiler_params=_cparams(("arbitrary", "arbitrary")),
        name="linear",
    )(h, w)


def _norm_linear_kernel(x_ref, g_ref, w_ref, o_ref):
    h = (_rms(x_ref[...]) * g_ref[...]).astype(BF16)
    o_ref[...] = jnp.dot(h, w_ref[...], preferred_element_type=F32)


def _norm_linear(x, xcol, kdim, g, w, layer):
    nout = w.shape[2]
    return pl.pallas_call(
        _norm_linear_kernel,
        out_shape=jax.ShapeDtypeStruct((N_TOK, nout), F32),
        grid=(N_TOK // TM,),
        in_specs=[
            pl.BlockSpec((TM, kdim), lambda i: (i, xcol)),
            pl.BlockSpec((1, kdim), lambda i: (0, 0)),
            pl.BlockSpec((None, kdim, nout), lambda i: (layer, 0, 0)),
        ],
        out_specs=pl.BlockSpec((TM, nout), lambda i: (i, 0)),
        compiler_params=_cparams(("arbitrary",)),
        name="norm_linear",
    )(x, g, w)


def _out_linear_kernel(a1l_ref, a1c_ref, a2_ref, w1_ref, w2_ref, x_ref, gate_ref, o_ref):
    a1 = jnp.where(pl.program_id(0) >= N_LAT // TM, a1c_ref[...], a1l_ref[...])
    acc = jnp.dot(a1, w1_ref[...], preferred_element_type=F32)
    acc = acc + jnp.dot(a2_ref[...], w2_ref[...], preferred_element_type=F32)
    o_ref[...] = x_ref[...] + gate_ref[...] * acc


def _out_linear(a1_lat, a1_ctx, a2, w, layer, x, mod, which):
    tn = D_MODEL
    k1, k2 = a1_lat.shape[1], a2.shape[1]
    lat_tiles = N_LAT // TM
    return pl.pallas_call(
        _out_linear_kernel,
        out_shape=jax.ShapeDtypeStruct((N_TOK, D_MODEL), F32),
        grid=(N_TOK // TM, D_MODEL // tn),
        in_specs=[
            pl.BlockSpec((TM, k1), lambda i, j: (jnp.minimum(i, lat_tiles - 1), 0)),
            pl.BlockSpec((TM, k1), lambda i, j: (jnp.maximum(i - lat_tiles, 0), 0)),
            pl.BlockSpec((TM, k2), lambda i, j: (i, 0)),
            pl.BlockSpec((None, k1, tn), lambda i, j: (layer, 0, j)),
            pl.BlockSpec((None, k2, tn), lambda i, j: (layer, k1 // k2, j)),
            pl.BlockSpec((TM, tn), lambda i, j: (i, j)),
            pl.BlockSpec((None, None, 1, tn), lambda i, j: (_mod_row(i, TM), which, 0, j)),
        ],
        out_specs=pl.BlockSpec((TM, tn), lambda i, j: (i, j)),
        compiler_params=_cparams(("arbitrary", "arbitrary")),
        name="out_linear",
    )(a1_lat, a1_ctx, a2, w, w, x, mod)


def _rope_tables(rot_dim):
    rows = SEQ // GRID_W
    row_ids = np.repeat(np.arange(rows, dtype=np.float32), GRID_W)
    col_ids = np.tile(np.arange(GRID_W, dtype=np.float32), rows)
    d_axis = rot_dim // 2
    inv = (np.float32(ROPE_THETA) ** (-np.arange(0, d_axis, 2, dtype=np.float32) / np.float32(d_axis))).astype(np.float32)
    ang = np.concatenate([row_ids[:, None] * inv, col_ids[:, None] * inv], axis=-1).astype(np.float32)
    cos, sin = np.cos(ang).astype(np.float32), np.sin(ang).astype(np.float32)
    reps = 128 // rot_dim
    cos_f = np.tile(np.concatenate([cos, cos], axis=-1), (1, reps))
    sin_f = np.tile(np.concatenate([-sin, sin], axis=-1), (1, reps))
    cos_f = np.concatenate([cos_f, np.ones((ROPE_IDENTITY_ROWS, 128), np.float32)], axis=0)
    sin_f = np.concatenate([sin_f, np.zeros((ROPE_IDENTITY_ROWS, 128), np.float32)], axis=0)
    return jnp.asarray(cos_f), jnp.asarray(sin_f)


def _rope128(x, cos, sin):
    return x * cos + pltpu.roll(x, 64, 1) * sin


def _rope64(x, cos, sin):
    lane = lax.broadcasted_iota(jnp.int32, x.shape, 1)
    swapped = jnp.where((lane % 64) < 32, pltpu.roll(x, 96, 1), pltpu.roll(x, 32, 1))
    return x * cos + swapped * sin


def _ones_column(rows):
    lane = lax.broadcasted_iota(jnp.int32, (rows, 128), 1)
    return jnp.where(lane == 0, 1.0, 0.0).astype(BF16)


def _gqa_kv_kernel(k_ref, v_ref, kn_ref, cos_ref, sin_ref, ko_ref, vo_ref):
    cos, sin = cos_ref[...], sin_ref[...]
    ones = _ones_column(k_ref.shape[0])
    for h in range(A_KV_HEADS):
        k = _rms(k_ref[:, h * 128:(h + 1) * 128]) * kn_ref[...]
        ko_ref[:, h * 128:(h + 1) * 128] = _rope128(k, cos, sin).astype(BF16)
        vo_ref[:, h * 256:h * 256 + 128] = v_ref[:, h * 128:(h + 1) * 128].astype(BF16)
        vo_ref[:, h * 256 + 128:(h + 1) * 256] = ones


def _gqa_kv(p, k_norm, cos, sin):
    t = 512
    return pl.pallas_call(
        _gqa_kv_kernel,
        out_shape=[jax.ShapeDtypeStruct((N_TOK, A_KV_WIDTH), BF16),
                   jax.ShapeDtypeStruct((N_TOK, 2 * A_KV_WIDTH), BF16)],
        grid=(N_TOK // t,),
        in_specs=[
            pl.BlockSpec((t, A_KV_WIDTH), lambda i: (i, A_WIDTH // A_KV_WIDTH)),
            pl.BlockSpec((t, A_KV_WIDTH), lambda i: (i, A_WIDTH // A_KV_WIDTH + 1)),
            pl.BlockSpec((1, 128), lambda i: (0, 0)),
            pl.BlockSpec((t, 128), lambda i: (_rope_block(i, t), 0)),
            pl.BlockSpec((t, 128), lambda i: (_rope_block(i, t), 0)),
        ],
        out_specs=[pl.BlockSpec((t, A_KV_WIDTH), lambda i: (i, 0)),
                   pl.BlockSpec((t, 2 * A_KV_WIDTH), lambda i: (i, 0))],
        compiler_params=_cparams(("arbitrary",)),
        name="gqa_kv",
    )(p, p, k_norm, cos, sin)


LOG2E = math.log2(math.e)
KEY_PIECE = 1024


def _softmax_pv(q, keys, vals):
    dn = (((1,), (1,)), ((), ()))
    m = acc = None
    for k, v in zip(keys, vals):
        s = lax.dot_general(q, k, dn, preferred_element_type=F32)
        m_piece = jnp.max(s, axis=-1, keepdims=True)
        if m is None:
            m_new = m_piece
        else:
            m_new = jnp.maximum(m, m_piece)
            acc = acc * jnp.exp2(m - m_new)
        part = jnp.dot(jnp.exp2(s - m_new).astype(BF16), v, preferred_element_type=F32)
        acc = part if acc is None else acc + part
        m = m_new
    return acc[:, :128] / acc[:, 128:129]


def _attn_qrow(b, t):
    nq = SEQ // TQ
    return jnp.where(t == nq, (N_LAT + b * CTX_LEN) // TQ, b * nq + t)


def _ctx_half():
    return pl.multiple_of((pl.program_id(0) % (TQ // CTX_LEN)) * CTX_LEN, CTX_LEN)


def _gqa_attn_kernel(q_ref, qn_ref, cos_ref, sin_ref, kc_ref, vc_ref, kl_ref, vl_ref, ol_ref, oc_ref):
    scale = HEAD_DIM ** -0.5 * LOG2E
    is_ctx = pl.program_id(2) == SEQ // TQ

    def query(rows, g):
        q = _rms(q_ref[rows, g * 128:(g + 1) * 128]) * qn_ref[...]
        return (_rope128(q, cos_ref[rows, :], sin_ref[rows, :]) * scale).astype(BF16)

    @pl.when(jnp.logical_not(is_ctx))
    def _():
        pieces = [slice(j * KEY_PIECE, (j + 1) * KEY_PIECE) for j in range(SEQ // KEY_PIECE)]
        qs = [query(slice(None), g) for g in range(A_GROUP)]
        for g in range(A_GROUP):
            o = _softmax_pv(qs[g], [kc_ref[...]] + [kl_ref[r, :] for r in pieces],
                            [vc_ref[...]] + [vl_ref[r, :] for r in pieces])
            ol_ref[:, g * 128:(g + 1) * 128] = o.astype(BF16)

    @pl.when(is_ctx)
    def _():
        rows = pl.ds(_ctx_half(), CTX_LEN)
        for g in range(A_GROUP):
            oc_ref[:, g * 128:(g + 1) * 128] = _softmax_pv(query(rows, g), [kc_ref[...]], [vc_ref[...]]).astype(BF16)


def _gqa_attention(p, k, vext, q_norm, cos, sin):
    qw = A_GROUP * HEAD_DIM
    nq = SEQ // TQ
    ctx_blk = N_LAT // CTX_LEN
    rope = pl.BlockSpec((TQ, 128), lambda b, h, t: (_rope_block(_attn_qrow(b, t), TQ), 0))
    return pl.pallas_call(
        _gqa_attn_kernel,
        out_shape=[jax.ShapeDtypeStruct((N_LAT, A_WIDTH), BF16), jax.ShapeDtypeStruct((N_CTX, A_WIDTH), BF16)],
        grid=(BATCH, A_KV_HEADS, nq + 1),
        in_specs=[
            pl.BlockSpec((TQ, qw), lambda b, h, t: (_attn_qrow(b, t), h)),
            pl.BlockSpec((1, 128), lambda b, h, t: (0, 0)),
            rope, rope,
            pl.BlockSpec((CTX_LEN, 128), lambda b, h, t: (ctx_blk + b, h)),
            pl.BlockSpec((CTX_LEN, 256), lambda b, h, t: (ctx_blk + b, h)),
            pl.BlockSpec((SEQ, 128), lambda b, h, t: (b, h)),
            pl.BlockSpec((SEQ, 256), lambda b, h, t: (b, h)),
        ],
        out_specs=[pl.BlockSpec((TQ, qw), lambda b, h, t: (b * nq + jnp.minimum(t, nq - 1), h)),
                   pl.BlockSpec((CTX_LEN, qw), lambda b, h, t: (b, h))],
        compiler_params=_cparams(("arbitrary",) * 3),
        name="gqa_attn",
    )(p, q_norm, cos, sin, k, vext, k, vext)


def _conv_kernel(bg_ref, cg_ref, ug_ref, cgp_ref, ugp_ref, cgn_ref, ugn_ref, w_ref, o_ref, *, rows):
    i = pl.program_id(0)
    per_seq = SEQ // rows
    is_ctx = i >= N_LAT // rows
    is_start = jnp.logical_or(is_ctx, i % per_seq == 0)
    is_end = jnp.logical_or(is_ctx, i % per_seq == per_seq - 1)
    m = cg_ref[...] * ug_ref[...]
    m_prev = jnp.where(is_start, 0.0, cgp_ref[7:8, :] * ugp_ref[7:8, :])
    m_next = jnp.where(is_end, 0.0, cgn_ref[0:1, :] * ugn_ref[0:1, :])
    row = lax.broadcasted_iota(jnp.int32, m.shape, 0)
    down = jnp.where(row == 0, m_prev, pltpu.roll(m, 1, 0))
    up = jnp.where(row == rows - 1, m_next, pltpu.roll(m, rows - 1, 0))
    conv = down * w_ref[0:1, :] + m * w_ref[1:2, :] + up * w_ref[2:3, :]
    o_ref[...] = (bg_ref[...] * conv).astype(BF16)


def _gated_conv(p, conv_w):
    rows = CTX_LEN
    base = (A_WIDTH + 2 * A_KV_WIDTH) // B_WIDTH
    halo = rows // 8
    last = N_TOK // 8 - 1
    main = lambda c: pl.BlockSpec((rows, B_WIDTH), lambda i: (i, base + c))
    prev = lambda c: pl.BlockSpec((8, B_WIDTH), lambda i: (jnp.maximum(i * halo - 1, 0), base + c))
    nxt = lambda c: pl.BlockSpec((8, B_WIDTH), lambda i: (jnp.minimum((i + 1) * halo, last), base + c))
    return pl.pallas_call(
        functools.partial(_conv_kernel, rows=rows),
        out_shape=jax.ShapeDtypeStruct((N_TOK, B_WIDTH), BF16),
        grid=(N_TOK // rows,),
        in_specs=[main(0), main(1), main(2), prev(1), prev(2), nxt(1), nxt(2),
                  pl.BlockSpec((3, B_WIDTH), lambda i: (0, 0))],
        out_specs=pl.BlockSpec((rows, B_WIDTH), lambda i: (i, 0)),
        compiler_params=_cparams(("arbitrary",)),
        name="gated_conv",
    )(p, p, p, p, p, p, p, conv_w)


def _mla_kv_kernel(ckv_ref, g_ref, w_ref, kr_ref, cos_ref, sin_ref, ko_ref, vo_ref):
    ckv = (_rms(ckv_ref[...]) * g_ref[...]).astype(BF16)
    kv = jnp.dot(ckv, w_ref[...], preferred_element_type=F32)
    x = kr_ref[...]
    lane = lax.broadcasted_iota(jnp.int32, x.shape, 1)
    r = jnp.where(lane < C_ROPE, _rope64(x, cos_ref[...], sin_ref[...]), 0.0)
    kr_low = r.astype(BF16)
    kr_high = pltpu.roll(r, 64, 1).astype(BF16)
    ones = _ones_column(x.shape[0])
    for h in range(C_HEADS):
        ko_ref[:, h * 256:h * 256 + 128] = kv[:, h * 256:h * 256 + 128].astype(BF16)
        ko_ref[:, h * 256 + 128:(h + 1) * 256] = kr_low if h % 2 == 0 else kr_high
        vo_ref[:, h * 256:h * 256 + 128] = kv[:, h * 256 + 128:(h + 1) * 256].astype(BF16)
        vo_ref[:, h * 256 + 128:(h + 1) * 256] = ones


def _mla_kv(p, kv_norm, w_ukv, layer, cos, sin):
    t = 512
    col = (C_Q_RANK + C_KV_RANK + D_WIDTH) // 128
    width = C_HEADS * 256
    return pl.pallas_call(
        _mla_kv_kernel,
        out_shape=[jax.ShapeDtypeStruct((N_TOK, width), BF16)] * 2,
        grid=(N_TOK // t,),
        in_specs=[
            pl.BlockSpec((t, C_KV_RANK), lambda i: (i, C_Q_RANK // C_KV_RANK)),
            pl.BlockSpec((1, C_KV_RANK), lambda i: (0, 0)),
            pl.BlockSpec((None, C_KV_RANK, width), lambda i: (layer, 0, 0)),
            pl.BlockSpec((t, 128), lambda i: (i, col)),
            pl.BlockSpec((t, 128), lambda i: (_rope_block(i, t), 0)),
            pl.BlockSpec((t, 128), lambda i: (_rope_block(i, t), 0)),
        ],
        out_specs=[pl.BlockSpec((t, width), lambda i: (i, 0))] * 2,
        compiler_params=_cparams(("arbitrary",)),
        name="mla_kv",
    )(p, kv_norm, w_ukv, p, cos, sin)


def _mla_attn_kernel(qn_ref, qr_ref, cos_ref, sin_ref, kc_ref, vc_ref, kl_ref, vl_ref, ol_ref, oc_ref):
    scale = (C_NOPE + C_ROPE) ** -0.5 * LOG2E
    is_ctx = pl.program_id(2) == SEQ // TQ

    def queries(rows):
        qr = _rope64(qr_ref[rows, :], cos_ref[rows, :], sin_ref[rows, :])
        lane = lax.broadcasted_iota(jnp.int32, qr.shape, 1)
        out = []
        for hh in range(2):
            sel = (lane < 64) if hh == 0 else (lane >= 64)
            q = jnp.concatenate([qn_ref[rows, hh * 128:(hh + 1) * 128], jnp.where(sel, qr, 0.0)], axis=1)
            out.append((q * scale).astype(BF16))
        return out

    @pl.when(jnp.logical_not(is_ctx))
    def _():
        for hh, q in enumerate(queries(slice(None))):
            blk = slice(hh * 256, (hh + 1) * 256)
            pieces = [slice(j * KEY_PIECE, (j + 1) * KEY_PIECE) for j in range(SEQ // KEY_PIECE)]
            o = _softmax_pv(q, [kc_ref[:, blk]] + [kl_ref[r, blk] for r in pieces],
                            [vc_ref[:, blk]] + [vl_ref[r, blk] for r in pieces])
            ol_ref[:, hh * 128:(hh + 1) * 128] = o.astype(BF16)

    @pl.when(is_ctx)
    def _():
        for hh, q in enumerate(queries(pl.ds(_ctx_half(), CTX_LEN))):
            blk = slice(hh * 256, (hh + 1) * 256)
            oc_ref[:, hh * 128:(hh + 1) * 128] = _softmax_pv(q, [kc_ref[:, blk]], [vc_ref[:, blk]]).astype(BF16)


def _mla_attention(q, kcat, vext, cos, sin):
    nq = SEQ // TQ
    ctx_blk = N_LAT // CTX_LEN
    rope_col = C_HEADS * C_NOPE // 128
    qrow = _attn_qrow
    rope = pl.BlockSpec((TQ, 128), lambda b, h, t: (_rope_block(qrow(b, t), TQ), 0))
    ctx = pl.BlockSpec((CTX_LEN, 512), lambda b, h, t: (ctx_blk + b, h))
    lat = pl.BlockSpec((SEQ, 512), lambda b, h, t: (b, h))
    return pl.pallas_call(
        _mla_attn_kernel,
        out_shape=[jax.ShapeDtypeStruct((N_LAT, C_WIDTH), BF16), jax.ShapeDtypeStruct((N_CTX, C_WIDTH), BF16)],
        grid=(BATCH, C_HEADS // 2, nq + 1),
        in_specs=[
            pl.BlockSpec((TQ, 256), lambda b, h, t: (qrow(b, t), h)),
            pl.BlockSpec((TQ, 128), lambda b, h, t: (qrow(b, t), rope_col + h)),
            rope, rope, ctx, ctx, lat, lat,
        ],
        out_specs=[pl.BlockSpec((TQ, 256), lambda b, h, t: (b * nq + jnp.minimum(t, nq - 1), h)),
                   pl.BlockSpec((CTX_LEN, 256), lambda b, h, t: (b, h))],
        compiler_params=_cparams(("arbitrary",) * 3),
        name="mla_attn",
    )(q, q, cos, sin, kcat, vext, kcat, vext)


def _s5_matrices(lam_re, lam_im, log_dt, b_re, b_im, c_re, c_im):
    hi = lax.Precision.HIGHEST
    lam_re, lam_im = lam_re.astype(F32), lam_im.astype(F32)
    dt = jnp.exp(log_dt.astype(F32))[..., None]
    ks = jnp.arange(S5_CHUNK + 1, dtype=F32)[:, None, None, None]
    mag = jnp.exp(lam_re[None] * dt[None] * ks)
    ang = lam_im[None] * dt[None] * ks
    pw_re, pw_im = mag * jnp.cos(ang), mag * jnp.sin(ang)
    a_re, a_im = pw_re[1], pw_im[1]
    den = lam_re * lam_re + lam_im * lam_im
    f_re = ((a_re - 1.0) * lam_re + a_im * lam_im) / den
    f_im = (a_im * lam_re - (a_re - 1.0) * lam_im) / den
    b_re, b_im = b_re.astype(F32), b_im.astype(F32)
    bb_re = f_re[..., None] * b_re - f_im[..., None] * b_im
    bb_im = f_re[..., None] * b_im + f_im[..., None] * b_re
    c_re, c_im = c_re.astype(F32), c_im.astype(F32)

    ab_re = pw_re[:S5_CHUNK, ..., None] * bb_re[None] - pw_im[:S5_CHUNK, ..., None] * bb_im[None]
    ab_im = pw_re[:S5_CHUNK, ..., None] * bb_im[None] + pw_im[:S5_CHUNK, ..., None] * bb_re[None]
    kern = (jnp.einsum('dgcn,ldgne->ldgce', c_re, ab_re, precision=hi)
            - jnp.einsum('dgcn,ldgne->ldgce', c_im, ab_im, precision=hi))
    t_idx = np.arange(S5_CHUNK)
    sub = lambda m, ax: m.reshape(m.shape[:ax] + (S5_QUADS, S5_GSUB) + m.shape[ax + 1:])
    klag = jnp.transpose(sub(kern, 2), (1, 2, 0, 5, 3, 4)).reshape(2, S5_QUADS, S5_CHUNK, S5_GROUP, 128)
    bt_re = jnp.transpose(bb_re, (0, 1, 3, 2))
    bt_im = jnp.transpose(bb_im, (0, 1, 3, 2))
    lane_gc = lambda c: jnp.transpose(sub(c, 0), (0, 3, 1, 2)).reshape(S5_QUADS, S5_STATE, 128)
    ps, qs = [], []
    for d in range(2):
        p_pow = (S5_CHUNK - 1 - t_idx) if d == 0 else t_idx
        ar, ai = pw_re[p_pow, d][:, :, None, :], pw_im[p_pow, d][:, :, None, :]
        pc = jnp.concatenate([ar * bt_re[d][None] - ai * bt_im[d][None],
                              ar * bt_im[d][None] + ai * bt_re[d][None]], axis=-1)
        pc = pc.reshape(S5_CHUNK, S5_QUADS, 128, 128)
        ps.append(jnp.transpose(pc, (1, 0, 2, 3)).reshape(S5_QUADS, S5_CHUNK * 128, 128))
        q_pow = (t_idx + 1) if d == 0 else (S5_CHUNK - t_idx)
        cl_re, cl_im = lane_gc(c_re[d])[:, None], lane_gc(c_im[d])[:, None]
        rep = lambda a: jnp.repeat(jnp.transpose(sub(a, 1), (1, 0, 3, 2)), S5_GROUP, axis=-1)
        al_re, al_im = rep(pw_re[q_pow, d]), rep(pw_im[q_pow, d])
        qs.append(jnp.concatenate([cl_re * al_re - cl_im * al_im,
                                   -(cl_re * al_im + cl_im * al_re)], axis=2))
    a16_re, a16_im = pw_re[S5_CHUNK], pw_im[S5_CHUNK]
    lanes = S5_GROUPS * 2 * S5_STATE
    return dict(
        klag=klag.astype(BF16),
        p=jnp.stack(ps).astype(BF16),
        q=jnp.stack(qs).astype(BF16),
        a_mul=jnp.concatenate([a16_re, a16_re], axis=-1).reshape(2, 1, lanes),
        a_swp=jnp.concatenate([-a16_im, a16_im], axis=-1).reshape(2, 1, lanes),
    )


def _s5_expand_p(pc):
    rep = jnp.concatenate([pc] * S5_GSUB, axis=1)
    row = lax.broadcasted_iota(jnp.int32, rep.shape, 0)
    col = lax.broadcasted_iota(jnp.int32, rep.shape, 1)
    return jnp.where((row // S5_GROUP) % S5_GSUB == col // 128, rep, jnp.zeros_like(rep))


def _s5_expand_q(q_ref):
    row = lax.broadcasted_iota(jnp.int32, (S5_GSUB * 128, 128), 0)
    col = lax.broadcasted_iota(jnp.int32, (S5_GSUB * 128, 128), 1)
    keep = row // 128 == col // S5_GROUP
    blocks = [jnp.where(keep, jnp.concatenate([q_ref[t]] * S5_GSUB, axis=0), jnp.zeros((S5_GSUB * 128, 128), BF16))
              for t in range(S5_CHUNK)]
    return jnp.concatenate(blocks, axis=1)


def _s5_chunk_rows(u_ref):
    n = N_TOK // S5_CHUNK
    return jnp.concatenate([u_ref[pl.ds(s, n, stride=S5_CHUNK), :] for s in range(S5_CHUNK)], axis=-1).astype(BF16)


def _s5_z_kernel(u_ref, p_ref, z_ref):
    z_ref[...] = jnp.dot(_s5_chunk_rows(u_ref), _s5_expand_p(p_ref[...]), preferred_element_type=F32)


def _s5_scan_kernel(z_ref, amul_ref, aswp_ref, s_ref, zs_scr):
    d = pl.program_id(0)
    a_mul, a_swp = amul_ref[...], aswp_ref[...]
    lanes = s_ref.shape[1]
    z = z_ref[...]
    low_half = lax.broadcasted_iota(jnp.int32, z.shape, 1) % 128 < S5_STATE
    zs_scr[...] = jnp.where(low_half, pltpu.roll(z, lanes - S5_STATE, 1), pltpu.roll(z, S5_STATE, 1))
    nl, nc = SEQ // S5_CHUNK, S5_CTX_CHUNKS

    def segment(base, count, carry):
        def body(step, carry):
            s, w = carry
            k = jnp.where(d == 0, step, count - 1 - step)
            rows = [pl.ds(base + b * count + k, 1) for b in range(BATCH)]
            for b in range(BATCH):
                s_ref[rows[b], :] = s[b:b + 1, :]
            z = jnp.concatenate([z_ref[r, :] for r in rows], axis=0)
            zs = jnp.concatenate([zs_scr[r, :] for r in rows], axis=0)
            return a_mul * s + a_swp * w + z, a_mul * w - a_swp * s + zs
        return lax.fori_loop(0, count, body, carry)

    zero = jnp.zeros((BATCH, lanes), F32)
    carry = segment(BATCH * nl, nc, (zero, zero))
    segment(0, nl, carry)


def _s5_y_kernel(u_ref, klag_ref, s_ref, q_ref, dv_ref, y_ref, m_scr):
    d = pl.program_id(1)
    n = N_TOK // S5_CHUNK
    m_scr[...] = jnp.zeros_like(m_scr)
    row = lax.broadcasted_iota(jnp.int32, (128, 128), 0)
    col = lax.broadcasted_iota(jnp.int32, (128, 128), 1)
    same_group = row // S5_GROUP == col // S5_GROUP
    blocks = [jnp.where(same_group, jnp.concatenate([klag_ref[lag]] * S5_GSUB, axis=0), jnp.zeros((128, 128), BF16))
              for lag in range(S5_CHUNK)]
    for rev in range(2):
        @pl.when(d == rev)
        def _():
            for s in range(S5_CHUNK):
                for t in range(S5_CHUNK):
                    lag = (s - t) if rev else (t - s)
                    if lag >= 0:
                        m_scr[s * 128:(s + 1) * 128, t * 128:(t + 1) * 128] = blocks[lag]
    y = jnp.dot(_s5_chunk_rows(u_ref), m_scr[...], preferred_element_type=F32)
    y = y + jnp.dot(s_ref[...].astype(BF16), _s5_expand_q(q_ref), preferred_element_type=F32)
    for t in range(S5_CHUNK):
        rows = pl.ds(t, n, stride=S5_CHUNK)
        yt = y[:, t * 128:(t + 1) * 128]

        @pl.when(d == 0)
        def _():
            y_ref[rows, :] = yt + u_ref[rows, :] * dv_ref[...]

        @pl.when(d == 1)
        def _():
            y_ref[rows, :] = y_ref[rows, :] + yt


def _s5(p, mats, layer, dskip):
    nstate = S5_GROUPS * 2 * S5_STATE
    qlanes = nstate // S5_QUADS
    slanes = qlanes
    ucol = (C_Q_RANK + C_KV_RANK) // 128
    z = pl.pallas_call(
        _s5_z_kernel,
        out_shape=jax.ShapeDtypeStruct((2, S5_ROWS, nstate), F32),
        grid=(2, S5_QUADS),
        in_specs=[
            pl.BlockSpec((N_TOK, 128), lambda d, q: (0, ucol + q)),
            pl.BlockSpec((None, None, None, S5_CHUNK * 128, 128), lambda d, q: (layer, d, q, 0, 0)),
        ],
        out_specs=pl.BlockSpec((None, S5_ROWS, qlanes), lambda d, q: (d, 0, q)),
        compiler_params=_cparams(("arbitrary", "arbitrary")),
        name="s5_chunk_state",
    )(p, mats["p"])
    s = pl.pallas_call(
        _s5_scan_kernel,
        out_shape=jax.ShapeDtypeStruct((2, S5_ROWS, nstate), F32),
        grid=(2, nstate // slanes),
        in_specs=[
            pl.BlockSpec((None, S5_ROWS, slanes), lambda d, q: (d, 0, q)),
            pl.BlockSpec((None, None, 1, slanes), lambda d, q: (layer, d, 0, q)),
            pl.BlockSpec((None, None, 1, slanes), lambda d, q: (layer, d, 0, q)),
        ],
        out_specs=pl.BlockSpec((None, S5_ROWS, slanes), lambda d, q: (d, 0, q)),
        scratch_shapes=[pltpu.VMEM((S5_ROWS, slanes), F32)],
        compiler_params=_cparams(("arbitrary", "arbitrary")),
        name="s5_scan",
    )(z, mats["a_mul"], mats["a_swp"])
    return pl.pallas_call(
        _s5_y_kernel,
        out_shape=jax.ShapeDtypeStruct((N_TOK, D_WIDTH), F32),
        grid=(S5_QUADS, 2),
        in_specs=[
            pl.BlockSpec((N_TOK, 128), lambda q, d: (0, ucol + q)),
            pl.BlockSpec((None, None, None, S5_CHUNK, S5_GROUP, 128), lambda q, d: (layer, d, q, 0, 0, 0)),
            pl.BlockSpec((None, S5_ROWS, qlanes), lambda q, d: (d, 0, q)),
            pl.BlockSpec((None, None, None, S5_CHUNK, 128, 128), lambda q, d: (layer, d, q, 0, 0, 0)),
            pl.BlockSpec((None, 1, 128), lambda q, d: (layer, 0, q)),
        ],
        out_specs=pl.BlockSpec((N_TOK, 128), lambda q, d: (0, q)),
        scratch_shapes=[pltpu.VMEM((S5_CHUNK * 128, S5_CHUNK * 128), BF16)],
        compiler_params=_cparams(("arbitrary", "arbitrary"), VMEM_LIMIT_LARGE),
        name="s5_output",
    )(p, mats["klag"], s, mats["q"], dskip)


def _glu_kernel(y_ref, w_ref, o_ref):
    y = y_ref[...]
    z = y * (0.5 * (1.0 + jnp.tanh(math.sqrt(2.0 / math.pi) * (y + 0.044715 * (y * y * y)))))
    gate = jnp.dot(z.astype(BF16), w_ref[...], preferred_element_type=F32)
    o_ref[...] = (z * jax.nn.sigmoid(gate)).astype(BF16)


def _s5_glu(y, w_glu, layer):
    t = 512
    return pl.pallas_call(
        _glu_kernel,
        out_shape=jax.ShapeDtypeStruct((N_TOK, D_WIDTH), BF16),
        grid=(N_TOK // t,),
        in_specs=[pl.BlockSpec((t, D_WIDTH), lambda i: (i, 0)),
                  pl.BlockSpec((None, D_WIDTH, D_WIDTH), lambda i: (layer, 0, 0))],
        out_specs=pl.BlockSpec((t, D_WIDTH), lambda i: (i, 0)),
        compiler_params=_cparams(("arbitrary",)),
        name="s5_glu",
    )(y, w_glu)


LANE_CHUNKS = D_MODEL // 128


def _store_token_major(ref, val):
    rows = val.shape[0]
    for c in range(LANE_CHUNKS):
        ref[pl.ds(c, rows, stride=LANE_CHUNKS), :] = val[:, c * 128:(c + 1) * 128]


def _load_token_major(ref, rows):
    return jnp.concatenate([ref[pl.ds(c, rows, stride=LANE_CHUNKS), :] for c in range(LANE_CHUNKS)], axis=-1)


def _router_kernel(x_ref, g_ref, sh_ref, sc_ref, whi_ref, wlo_ref, h_ref, info_ref, gw_ref, cnt_ref, h_scr, carry):
    i = pl.program_id(0)

    @pl.when(i == 0)
    def _():
        carry[...] = jnp.zeros_like(carry)

    for r in range(0, TM, ROW_CHUNK):
        rows = slice(r, r + ROW_CHUNK)
        h_scr[rows, :] = _modulated_norm(x_ref[rows, :], g_ref[...], sh_ref[...], sc_ref[...])
    _store_token_major(h_ref, h_scr[...])
    h = h_scr[...]
    h_hi = h.astype(BF16)
    h_lo = (h - h_hi.astype(F32)).astype(BF16)
    logits = (jnp.dot(h_hi, whi_ref[...], preferred_element_type=F32)
              + jnp.dot(h_lo, whi_ref[...], preferred_element_type=F32)
              + jnp.dot(h_hi, wlo_ref[...], preferred_element_type=F32))
    lane = lax.broadcasted_iota(jnp.int32, logits.shape, 1)
    neg = -jnp.inf
    big = jnp.int32(1 << 20)

    def first_argmax(v, vmax):
        return jnp.min(jnp.where(v == vmax, lane, big), axis=-1, keepdims=True)

    lg = jnp.where(lane < N_GROUPS, logits, neg)
    mg = jnp.max(lg, axis=-1, keepdims=True)
    g_w = 1.0 / jnp.sum(jnp.exp(lg - mg), axis=-1, keepdims=True)
    g_idx = first_argmax(lg, mg)
    lo = N_GROUPS + EXPERTS_PER_GROUP * g_idx
    le = jnp.where(jnp.logical_and(lane >= lo, lane < lo + EXPERTS_PER_GROUP), logits, neg)
    m1 = jnp.max(le, axis=-1, keepdims=True)
    i1 = first_argmax(le, m1)
    le2 = jnp.where(lane == i1, neg, le)
    m2 = jnp.max(le2, axis=-1, keepdims=True)
    i2 = first_argmax(le2, m2)
    r21 = jnp.exp(m2 - m1)
    w1 = g_w / (1.0 + r21)
    w2 = g_w * r21 / (1.0 + r21)
    oh = jnp.logical_or(lane == i1, lane == i2)
    ohb = jnp.where(oh, 1.0, 0.0).astype(BF16)
    rr = lax.broadcasted_iota(jnp.int32, (TM, TM), 0)
    cc = lax.broadcasted_iota(jnp.int32, (TM, TM), 1)
    lower = jnp.where(rr > cc, 1.0, 0.0).astype(BF16)
    before = jnp.dot(lower, ohb, preferred_element_type=F32) + carry[...]
    rank1 = jnp.sum(jnp.where(lane == i1, before, 0.0), axis=-1, keepdims=True).astype(jnp.int32)
    rank2 = jnp.sum(jnp.where(lane == i2, before, 0.0), axis=-1, keepdims=True).astype(jnp.int32)
    carry[...] = carry[...] + jnp.sum(ohb.astype(F32), axis=0, keepdims=True)
    cnt_ref[...] = jnp.broadcast_to(carry[...], cnt_ref.shape)
    info = jnp.where(lane == 0, i1 - N_GROUPS, jnp.where(lane == 1, i2 - N_GROUPS,
                     jnp.where(lane == 2, rank1, jnp.where(lane == 3, rank2, 0))))
    info_ref[...] = info
    gw_ref[...] = jnp.where(lane == 0, w1, jnp.where(lane == 1, w2, 0.0))


def _router(x, g, mod, w_router, layer):
    return pl.pallas_call(
        _router_kernel,
        out_shape=[jax.ShapeDtypeStruct((N_TOK * LANE_CHUNKS, 128), F32),
                   jax.ShapeDtypeStruct((N_TOK, 128), jnp.int32),
                   jax.ShapeDtypeStruct((N_TOK, 128), F32),
                   jax.ShapeDtypeStruct((8, 128), F32)],
        grid=(N_TOK // TM,),
        in_specs=[
            pl.BlockSpec((TM, D_MODEL), lambda i: (i, 0)),
            pl.BlockSpec((1, D_MODEL), lambda i: (0, 0)),
            pl.BlockSpec((None, None, 1, D_MODEL), lambda i: (_mod_row(i, TM), 3, 0, 0)),
            pl.BlockSpec((None, None, 1, D_MODEL), lambda i: (_mod_row(i, TM), 4, 0, 0)),
            pl.BlockSpec((None, D_MODEL, 128), lambda i: (layer, 0, 0)),
            pl.BlockSpec((None, D_MODEL, 128), lambda i: (layer, 0, 0)),
        ],
        out_specs=[pl.BlockSpec((TM * LANE_CHUNKS, 128), lambda i: (i, 0)),
                   pl.BlockSpec((TM, 128), lambda i: (i, 0)),
                   pl.BlockSpec((TM, 128), lambda i: (i, 0)),
                   pl.BlockSpec((8, 128), lambda i: (0, 0))],
        scratch_shapes=[pltpu.VMEM((TM, D_MODEL), F32), pltpu.VMEM((1, 128), F32)],
        compiler_params=_cparams(("arbitrary",)),
        name="moe_router",
    )(x, g, mod, mod, *w_router)


FFN_ISSUE_GROUPS = 8
ISSUE_UNROLL = 8


def _ffn_kernel(pos_ref, te_ref, meta_ref, h_hbm, wg_ref, wu_ref, wd_ref, o_ref,
                src, xbuf, wg_b, wu_b, wd_b, sem):
    t = pl.program_id(0)
    nt = meta_ref[0]

    def row_copy(tile, slot, r):
        tok = src[tile * TE + r]
        return pltpu.make_async_copy(
            h_hbm.at[pl.ds(pl.multiple_of(tok * LANE_CHUNKS, LANE_CHUNKS), LANE_CHUNKS), :],
            xbuf.at[slot, pl.ds(pl.multiple_of(r * LANE_CHUNKS, LANE_CHUNKS), LANE_CHUNKS), :],
            sem.at[slot])

    def gather(tile, slot):
        def body(r, _):
            row_copy(tile, slot, r).start()
            return 0
        lax.fori_loop(0, TE, body, 0, unroll=ISSUE_UNROLL)

    def wait_tile(slot):
        pltpu.make_async_copy(xbuf.at[slot], xbuf.at[slot], sem.at[slot]).wait()

    @pl.when(t == 0)
    def _():
        for e in range(N_EXPERTS):
            def clear(i, _):
                src[i] = 0
                return 0
            lax.fori_loop(meta_ref[1 + e], meta_ref[1 + N_EXPERTS + e], clear, 0)

        def fill(tok, _):
            src[pos_ref[2 * tok]] = tok
            src[pos_ref[2 * tok + 1]] = tok
            return 0
        lax.fori_loop(0, N_TOK, fill, 0, unroll=8)
        gather(0, 0)

    @pl.when(t >= nt)
    def _():
        o_ref[...] = jnp.zeros_like(o_ref)

    @pl.when(t < nt)
    def _():
        slot = t % 2
        wait_tile(slot)

        @pl.when(jnp.logical_or(t == 0, te_ref[t] != te_ref[jnp.maximum(t - 1, 0)]))
        def _():
            wg_b[...] = wg_ref[...].astype(BF16)
            wu_b[...] = wu_ref[...].astype(BF16)
            wd_b[...] = wd_ref[...].astype(BF16)

        nxt = jnp.minimum(t + 1, nt - 1)
        per = TE // FFN_ISSUE_GROUPS

        def issue(g):
            for r in range(g * per, (g + 1) * per):
                row_copy(nxt, 1 - slot, r).start()

        x = _load_token_major(xbuf.at[slot], TE).astype(BF16)
        half = D_EXPERT // 2
        hg, hu = [], []
        for n in range(2):
            issue(n)
            hg.append(jnp.dot(x, wg_b[:, n * half:(n + 1) * half], preferred_element_type=F32))
        for n in range(2):
            issue(2 + n)
            hu.append(jnp.dot(x, wu_b[:, n * half:(n + 1) * half], preferred_element_type=F32))
        hg, hu = jnp.concatenate(hg, axis=1), jnp.concatenate(hu, axis=1)
        act = (hg * jax.nn.sigmoid(hg) * hu).astype(BF16)
        quarter = D_MODEL // 4
        for n in range(4):
            issue(4 + n)
            y = jnp.dot(act, wd_b[:, n * quarter:(n + 1) * quarter], preferred_element_type=F32)
            for c in range(quarter // 128):
                o_ref[pl.ds(n * (quarter // 128) + c, TE, stride=LANE_CHUNKS), :] = y[:, c * 128:(c + 1) * 128]

        @pl.when(t == nt - 1)
        def _():
            wait_tile(1 - slot)


def _expert_ffn(pos_flat, tile_expert, meta, h, w_gate, w_up, w_down, layer):
    wsel = lambda t, pos, te, meta: (layer, te[t], 0, 0)
    return pl.pallas_call(
        _ffn_kernel,
        out_shape=jax.ShapeDtypeStruct((N_SORT * LANE_CHUNKS, 128), F32),
        grid_spec=pltpu.PrefetchScalarGridSpec(
            num_scalar_prefetch=3,
            grid=(N_ETILES,),
            in_specs=[
                pl.BlockSpec(memory_space=pl.ANY),
                pl.BlockSpec((None, None, D_MODEL, D_EXPERT), wsel),
                pl.BlockSpec((None, None, D_MODEL, D_EXPERT), wsel),
                pl.BlockSpec((None, None, D_EXPERT, D_MODEL), wsel),
            ],
            out_specs=pl.BlockSpec((TE * LANE_CHUNKS, 128), lambda t, pos, te, meta: (t, 0)),
            scratch_shapes=[
                pltpu.SMEM((N_SORT,), jnp.int32),
                pltpu.VMEM((2, TE * LANE_CHUNKS, 128), F32),
                pltpu.VMEM((D_MODEL, D_EXPERT), BF16),
                pltpu.VMEM((D_MODEL, D_EXPERT), BF16),
                pltpu.VMEM((D_EXPERT, D_MODEL), BF16),
                pltpu.SemaphoreType.DMA((2,)),
            ],
        ),
        compiler_params=_cparams(("arbitrary",), VMEM_LIMIT_LARGE),
        name="moe_expert_ffn",
    )(pos_flat, tile_expert, meta, h, w_gate, w_up, w_down)


TC = 256


def _combine_kernel(pos_ref, x_ref, gate_ref, gw_ref, g_ref, sh_ref, sc_ref, ys_hbm, *refs, final_norm):
    if final_norm:
        o_ref, buf_a, buf_b, sem = refs
    else:
        o_ref, h_ref, buf_a, buf_b, sem = refs
    i = pl.program_id(0)
    slot = i % 2

    def slab(ref, row):
        return ref.at[pl.ds(pl.multiple_of(row * LANE_CHUNKS, LANE_CHUNKS), LANE_CHUNKS), :]

    def gather(tile, slot):
        def body(r, _):
            tok = tile * TC + r
            pltpu.make_async_copy(slab(ys_hbm, pos_ref[2 * tok]), slab(buf_a.at[slot], r), sem.at[slot, 0]).start()
            pltpu.make_async_copy(slab(ys_hbm, pos_ref[2 * tok + 1]), slab(buf_b.at[slot], r), sem.at[slot, 1]).start()
            return 0
        lax.fori_loop(0, TC, body, 0, unroll=ISSUE_UNROLL)

    @pl.when(i == 0)
    def _():
        gather(0, 0)

    @pl.when(i + 1 < pl.num_programs(0))
    def _():
        gather(i + 1, 1 - slot)

    pltpu.make_async_copy(buf_a.at[slot], buf_a.at[slot], sem.at[slot, 0]).wait()
    pltpu.make_async_copy(buf_b.at[slot], buf_b.at[slot], sem.at[slot, 1]).wait()
    w0 = jnp.broadcast_to(gw_ref[:, 0:1], (TC, 128))
    w1 = jnp.broadcast_to(gw_ref[:, 1:2], (TC, 128))
    sq = jnp.zeros((TC, 128), F32)
    for c in range(LANE_CHUNKS):
        cols = slice(c * 128, (c + 1) * 128)
        rows = pl.ds(c, TC, stride=LANE_CHUNKS)
        y = x_ref[:, cols] + gate_ref[:, cols] * (w0 * buf_a[slot, rows, :] + w1 * buf_b[slot, rows, :])
        o_ref[:, cols] = y
        sq = sq + y * y
    inv = lax.rsqrt(jnp.sum(sq, axis=-1, keepdims=True) / D_MODEL + EPS)
    for c in range(LANE_CHUNKS):
        cols = slice(c * 128, (c + 1) * 128)
        normed = o_ref[:, cols] * inv * g_ref[:, cols]
        if final_norm:
            o_ref[:, cols] = normed
        else:
            h_ref[:, cols] = (normed * (1.0 + sc_ref[:, cols]) + sh_ref[:, cols]).astype(BF16)


def _combine(pos_flat, x, mod, gw, ys, g, mod_next):
    final_norm = mod_next is None
    rows = N_LAT if final_norm else N_TOK
    tile = pl.BlockSpec((TC, D_MODEL), lambda i, pos: (i, 0))
    mod_row = lambda which: pl.BlockSpec((None, None, 1, D_MODEL), lambda i, pos: (_mod_row(i, TC), which, 0, 0))
    next_tab = mod if final_norm else mod_next
    out_shape = [jax.ShapeDtypeStruct((rows, D_MODEL), F32)]
    if not final_norm:
        out_shape.append(jax.ShapeDtypeStruct((rows, D_MODEL), BF16))
    return pl.pallas_call(
        functools.partial(_combine_kernel, final_norm=final_norm),
        out_shape=out_shape,
        grid_spec=pltpu.PrefetchScalarGridSpec(
            num_scalar_prefetch=1,
            grid=(rows // TC,),
            in_specs=[
                tile,
                mod_row(5),
                pl.BlockSpec((TC, 128), lambda i, pos: (i, 0)),
                pl.BlockSpec((1, D_MODEL), lambda i, pos: (0, 0)),
                mod_row(0), mod_row(1),
                pl.BlockSpec(memory_space=pl.ANY),
            ],
            out_specs=[tile] * len(out_shape),
            scratch_shapes=[pltpu.VMEM((2, TC * LANE_CHUNKS, 128), F32), pltpu.VMEM((2, TC * LANE_CHUNKS, 128), F32),
                            pltpu.SemaphoreType.DMA((2, 2))],
        ),
        compiler_params=_cparams(("arbitrary",)),
        name="moe_combine",
    )(pos_flat, x, mod, gw, g, next_tab, next_tab, ys)


def _moe(x, g, mod, w_router, w_gate, w_up, w_down, layer, g_next, mod_next):
    h, info, gw, cnt = _router(x, g, mod, w_router, layer)
    counts = cnt[0, N_GROUPS:N_GROUPS + N_EXPERTS].astype(jnp.int32)
    padded = ((counts + TE - 1) // TE) * TE
    ends = jnp.cumsum(padded)
    starts = ends - padded
    experts = jnp.arange(N_EXPERTS, dtype=jnp.int32)
    start_of = jnp.sum(jnp.where(info[:, 0:2, None] == experts, starts, 0), axis=-1)
    pos_flat = (start_of + info[:, 2:4]).reshape(-1)
    tile_ends = ends // TE
    num_tiles = tile_ends[-1]
    tiles = jnp.minimum(jnp.arange(N_ETILES, dtype=jnp.int32), num_tiles - 1)
    tile_expert = jnp.sum((tile_ends[None, :] <= tiles[:, None]).astype(jnp.int32), axis=-1)
    meta = jnp.concatenate([num_tiles[None], starts + counts, ends]).astype(jnp.int32)
    ys = _expert_ffn(pos_flat, tile_expert, meta, h, w_gate, w_up, w_down, layer)
    return _combine(pos_flat, x, mod, gw, ys, g_next, mod_next)


def kernel(x, c, ctx, c_ctx, mod_w, mod_b, norm_mix, norm_ffn, ab_w_in, ab_q_norm, ab_k_norm, ab_conv_w, ab_w_out, cd_w_in, cd_q_norm, cd_kv_norm, cd_w_uq, cd_w_ukv, s5_lam_re, s5_lam_im, s5_log_dt, s5_b_re, s5_b_im, s5_c_re, s5_c_im, s5_d, s5_w_glu, cd_w_out, moe_w_group, moe_w_expert, moe_w_gate, moe_w_up, moe_w_down, final_norm):
    cc = jnp.concatenate([c, c_ctx[None, :], jnp.zeros((8 - BATCH - 1, D_MODEL), F32)], axis=0)
    mods = _modulation(cc, mod_w, mod_b).reshape(DEPTH, 8, N_MOD, 1, D_MODEL)
    xs, h = _embed(x.reshape(N_LAT, D_MODEL), ctx.reshape(N_CTX, D_MODEL), norm_mix[0][None, :], mods[0])
    cos_a, sin_a = _rope_tables(HEAD_DIM)
    cos_c, sin_c = _rope_tables(C_ROPE)

    ab_in_b, ab_out_b = ab_w_in.astype(BF16), ab_w_out.astype(BF16)
    cd_out_b, ukv_b, glu_b = cd_w_out.astype(BF16), cd_w_ukv.astype(BF16), s5_w_glu.astype(BF16)
    a, b_ = C_Q_RANK + C_KV_RANK, C_Q_RANK + C_KV_RANK + C_ROPE
    pad = jnp.zeros(cd_w_in.shape[:2] + (CD_IN_PAD - cd_w_in.shape[2],), F32)
    cd_in_b = jnp.concatenate([cd_w_in[..., :a], cd_w_in[..., b_:], cd_w_in[..., a:b_], pad], axis=-1).astype(BF16)
    w_uq = cd_w_uq.reshape(-1, C_Q_RANK, C_HEADS, C_NOPE + C_ROPE)
    uq_b = jnp.concatenate([w_uq[..., :C_NOPE].reshape(-1, C_Q_RANK, C_HEADS * C_NOPE),
                            w_uq[..., C_NOPE:].reshape(-1, C_Q_RANK, C_HEADS * C_ROPE)], axis=-1).astype(BF16)

    mats = jax.vmap(_s5_matrices)(s5_lam_re, s5_lam_im, s5_log_dt, s5_b_re, s5_b_im, s5_c_re, s5_c_im)
    w_router = jnp.concatenate(
        [moe_w_group, jnp.transpose(moe_w_expert, (0, 2, 1, 3)).reshape(DEPTH, D_MODEL, N_EXPERTS),
         jnp.zeros((DEPTH, D_MODEL, 128 - N_GROUPS - N_EXPERTS), F32)], axis=-1)
    w_router_hi = w_router.astype(BF16)
    w_router = (w_router_hi, (w_router - w_router_hi.astype(F32)).astype(BF16))

    for i in range(DEPTH):
        j = i // 2
        mod = mods[i]
        if i % 2 == 0:
            p = _linear(h, ab_in_b, j, 2048)
            k, vext = _gqa_kv(p, ab_k_norm[j][None, :], cos_a, sin_a)
            o_lat, o_ctx = _gqa_attention(p, k, vext, ab_q_norm[j][None, :], cos_a, sin_a)
            side = _gated_conv(p, ab_conv_w[j])
            w_out = ab_out_b
        else:
            p = _linear(h, cd_in_b, j, CD_IN_PAD)
            q = _norm_linear(p, 0, C_Q_RANK, cd_q_norm[j][None, :], uq_b, j)
            kcat, vext = _mla_kv(p, cd_kv_norm[j][None, :], ukv_b, j, cos_c, sin_c)
            o_lat, o_ctx = _mla_attention(q, kcat, vext, cos_c, sin_c)
            y = _s5(p, mats, j, s5_d.astype(F32)[:, None, :])
            side = _s5_glu(y, glu_b, j)
            w_out = cd_out_b
        xs = _out_linear(o_lat, o_ctx, side, w_out, j, xs, mod, 2)
        if i < DEPTH - 1:
            xs, h = _moe(xs, norm_ffn[i][None, :], mod, w_router, moe_w_gate, moe_w_up, moe_w_down, i,
                         norm_mix[i + 1][None, :], mods[i + 1])
        else:
            out, = _moe(xs, norm_ffn[i][None, :], mod, w_router, moe_w_gate, moe_w_up, moe_w_down, i,
                        final_norm[None, :], None)
    return out.reshape(BATCH, SEQ, D_MODEL)
```

```python
import functools
import math

import numpy as np
import jax
import jax.numpy as jnp
from jax import lax
from jax.experimental import pallas as pl
from jax.experimental.pallas import tpu as pltpu

F32 = jnp.float32
BF16 = jnp.bfloat16

D_MODEL = 2048
BATCH = 4
SEQ = 2048
DEPTH = 4
GRID_W = 64
CTX_LEN = 256
ROPE_THETA = 10000.0
EPS = 1e-6
N_MOD = 6
HEAD_DIM = 128
A_Q_HEADS = 12
A_KV_HEADS = 4
A_GROUP = A_Q_HEADS // A_KV_HEADS
A_WIDTH = A_Q_HEADS * HEAD_DIM
A_KV_WIDTH = A_KV_HEADS * HEAD_DIM
B_WIDTH = 512
AB_IN = A_WIDTH + 2 * A_KV_WIDTH + 3 * B_WIDTH
C_HEADS = 12
C_NOPE = 128
C_ROPE = 64
C_V = 128
C_Q_RANK = 512
C_KV_RANK = 256
C_WIDTH = C_HEADS * C_V
D_WIDTH = 512
S5_GROUP = 16
S5_GROUPS = D_WIDTH // S5_GROUP
S5_STATE = 64
N_GROUPS = 4
EXPERTS_PER_GROUP = 4
N_EXPERTS = N_GROUPS * EXPERTS_PER_GROUP
D_EXPERT = 512

N_LAT = BATCH * SEQ
N_CTX = BATCH * CTX_LEN
N_TOK = N_LAT + N_CTX
CTX_ROW = BATCH
CD_IN_PAD = 1536

TM = 512
TQ = 1024
TE = 256
ROPE_IDENTITY_ROWS = max(TM, TQ)
N_SORT = 2 * N_TOK + N_EXPERTS * TE
N_ETILES = N_SORT // TE
S5_CHUNK = 16
S5_CTX_CHUNKS = CTX_LEN // S5_CHUNK
S5_ROWS = N_TOK // S5_CHUNK
S5_GSUB = 128 // S5_GROUP
S5_QUADS = S5_GROUPS // S5_GSUB
VMEM_LIMIT = 48 * 1024 * 1024
VMEM_LIMIT_LARGE = 56 * 1024 * 1024
ROW_CHUNK = 256


def _cparams(sem, limit=VMEM_LIMIT):
    return pltpu.CompilerParams(dimension_semantics=sem, vmem_limit_bytes=limit)


def _mod_row(tile, tile_rows):
    r0 = tile * tile_rows
    return jnp.where(r0 >= N_LAT, CTX_ROW, r0 // SEQ)


def _rope_block(tile, tile_rows):
    r0 = tile * tile_rows
    return jnp.where(r0 >= N_LAT, SEQ // tile_rows, (r0 % SEQ) // tile_rows)


def _rms(x):
    return x * lax.rsqrt(jnp.mean(x * x, axis=-1, keepdims=True) + EPS)


def _mod_kernel(cc_ref, w_ref, b_ref, o_ref):
    cc = cc_ref[...]
    s = (cc * jax.nn.sigmoid(cc)).astype(BF16)
    o_ref[...] = jnp.dot(s, w_ref[...].astype(BF16), preferred_element_type=F32) + b_ref[...]


def _modulation(cc, mod_w, mod_b):
    tn = 1024
    nout = N_MOD * D_MODEL
    return pl.pallas_call(
        _mod_kernel,
        out_shape=jax.ShapeDtypeStruct((DEPTH, 8, nout), F32),
        grid=(DEPTH, nout // tn),
        in_specs=[
            pl.BlockSpec((8, D_MODEL), lambda l, j: (0, 0)),
            pl.BlockSpec((None, D_MODEL, tn), lambda l, j: (l, 0, j)),
            pl.BlockSpec((None, 1, tn), lambda l, j: (l, 0, j)),
        ],
        out_specs=pl.BlockSpec((None, 8, tn), lambda l, j: (l, 0, j)),
        compiler_params=_cparams(("arbitrary", "arbitrary")),
        name="modulation",
    )(cc, mod_w, mod_b.reshape(DEPTH, 1, nout))


def _modulated_norm(x, g, sh, sc):
    return _rms(x) * g * (1.0 + sc) + sh


def _embed_kernel(xl_ref, xc_ref, g_ref, sh_ref, sc_ref, xs_ref, h_ref):
    is_ctx = pl.program_id(0) >= N_LAT // TM
    for r in range(0, TM, ROW_CHUNK):
        rows = slice(r, r + ROW_CHUNK)
        x = jnp.where(is_ctx, xc_ref[rows, :], xl_ref[rows, :])
        xs_ref[rows, :] = x
        h_ref[rows, :] = _modulated_norm(x, g_ref[...], sh_ref[...], sc_ref[...]).astype(BF16)


def _embed(x_lat, x_ctx, g, mod):
    lat_tiles = N_LAT // TM
    return pl.pallas_call(
        _embed_kernel,
        out_shape=[jax.ShapeDtypeStruct((N_TOK, D_MODEL), F32), jax.ShapeDtypeStruct((N_TOK, D_MODEL), BF16)],
        grid=(N_TOK // TM,),
        in_specs=[
            pl.BlockSpec((TM, D_MODEL), lambda i: (jnp.minimum(i, lat_tiles - 1), 0)),
            pl.BlockSpec((TM, D_MODEL), lambda i: (jnp.maximum(i - lat_tiles, 0), 0)),
            pl.BlockSpec((1, D_MODEL), lambda i: (0, 0)),
            pl.BlockSpec((None, None, 1, D_MODEL), lambda i: (_mod_row(i, TM), 0, 0, 0)),
            pl.BlockSpec((None, None, 1, D_MODEL), lambda i: (_mod_row(i, TM), 1, 0, 0)),
        ],
        out_specs=[pl.BlockSpec((TM, D_MODEL), lambda i: (i, 0))] * 2,
        compiler_params=_cparams(("arbitrary",)),
        name="embed_norm",
    )(x_lat, x_ctx, g, mod, mod)


def _linear_kernel(h_ref, w_ref, o_ref):
    o_ref[...] = jnp.dot(h_ref[...], w_ref[...], preferred_element_type=F32)


def _linear(h, w, layer, tn):
    kdim, nout = w.shape[1], w.shape[2]
    return pl.pallas_call(
        _linear_kernel,
        out_shape=jax.ShapeDtypeStruct((N_TOK, nout), F32),
        grid=(N_TOK // TM, nout // tn),
        in_specs=[
            pl.BlockSpec((TM, kdim), lambda i, j: (i, 0)),
            pl.BlockSpec((None, kdim, tn), lambda i, j: (layer, 0, j)),
        ],
        out_specs=pl.BlockSpec((TM, tn), lambda i, j: (i, j)),
        compiler_params=_cparams(("arbitrary", "arbitrary")),
        name="linear",
    )(h, w)


def _norm_linear_kernel(x_ref, g_ref, w_ref, o_ref):
    h = (_rms(x_ref[...]) * g_ref[...]).astype(BF16)
    o_ref[...] = jnp.dot(h, w_ref[...], preferred_element_type=F32)


def _norm_linear(x, xcol, kdim, g, w, layer):
    nout = w.shape[2]
    return pl.pallas_call(
        _norm_linear_kernel,
        out_shape=jax.ShapeDtypeStruct((N_TOK, nout), F32),
        grid=(N_TOK // TM,),
        in_specs=[
            pl.BlockSpec((TM, kdim), lambda i: (i, xcol)),
            pl.BlockSpec((1, kdim), lambda i: (0, 0)),
            pl.BlockSpec((None, kdim, nout), lambda i: (layer, 0, 0)),
        ],
        out_specs=pl.BlockSpec((TM, nout), lambda i: (i, 0)),
        compiler_params=_cparams(("arbitrary",)),
        name="norm_linear",
    )(x, g, w)


def _out_linear_kernel(a1l_ref, a1c_ref, a2_ref, w1_ref, w2_ref, x_ref, gate_ref, o_ref):
    a1 = jnp.where(pl.program_id(0) >= N_LAT // TM, a1c_ref[...], a1l_ref[...])
    acc = jnp.dot(a1, w1_ref[...], preferred_element_type=F32)
    acc = acc + jnp.dot(a2_ref[...], w2_ref[...], preferred_element_type=F32)
    o_ref[...] = x_ref[...] + gate_ref[...] * acc


def _out_linear(a1_lat, a1_ctx, a2, w, layer, x, mod, which):
    tn = D_MODEL
    k1, k2 = a1_lat.shape[1], a2.shape[1]
    lat_tiles = N_LAT // TM
    return pl.pallas_call(
        _out_linear_kernel,
        out_shape=jax.ShapeDtypeStruct((N_TOK, D_MODEL), F32),
        grid=(N_TOK // TM, D_MODEL // tn),
        in_specs=[
            pl.BlockSpec((TM, k1), lambda i, j: (jnp.minimum(i, lat_tiles - 1), 0)),
            pl.BlockSpec((TM, k1), lambda i, j: (jnp.maximum(i - lat_tiles, 0), 0)),
            pl.BlockSpec((TM, k2), lambda i, j: (i, 0)),
            pl.BlockSpec((None, k1, tn), lambda i, j: (layer, 0, j)),
            pl.BlockSpec((None, k2, tn), lambda i, j: (layer, k1 // k2, j)),
            pl.BlockSpec((TM, tn), lambda i, j: (i, j)),
            pl.BlockSpec((None, None, 1, tn), lambda i, j: (_mod_row(i, TM), which, 0, j)),
        ],
        out_specs=pl.BlockSpec((TM, tn), lambda i, j: (i, j)),
        compiler_params=_cparams(("arbitrary", "arbitrary")),
        name="out_linear",
    )(a1_lat, a1_ctx, a2, w, w, x, mod)


def _rope_tables(rot_dim):
    rows = SEQ // GRID_W
    row_ids = np.repeat(np.arange(rows, dtype=np.float32), GRID_W)
    col_ids = np.tile(np.arange(GRID_W, dtype=np.float32), rows)
    d_axis = rot_dim // 2
    inv = (np.float32(ROPE_THETA) ** (-np.arange(0, d_axis, 2, dtype=np.float32) / np.float32(d_axis))).astype(np.float32)
    ang = np.concatenate([row_ids[:, None] * inv, col_ids[:, None] * inv], axis=-1).astype(np.float32)
    cos, sin = np.cos(ang).astype(np.float32), np.sin(ang).astype(np.float32)
    reps = 128 // rot_dim
    cos_f = np.tile(np.concatenate([cos, cos], axis=-1), (1, reps))
    sin_f = np.tile(np.concatenate([-sin, sin], axis=-1), (1, reps))
    cos_f = np.concatenate([cos_f, np.ones((ROPE_IDENTITY_ROWS, 128), np.float32)], axis=0)
    sin_f = np.concatenate([sin_f, np.zeros((ROPE_IDENTITY_ROWS, 128), np.float32)], axis=0)
    return jnp.asarray(cos_f), jnp.asarray(sin_f)


def _rope128(x, cos, sin):
    return x * cos + pltpu.roll(x, 64, 1) * sin


def _rope64(x, cos, sin):
    lane = lax.broadcasted_iota(jnp.int32, x.shape, 1)
    swapped = jnp.where((lane % 64) < 32, pltpu.roll(x, 96, 1), pltpu.roll(x, 32, 1))
    return x * cos + swapped * sin


def _ones_column(rows):
    lane = lax.broadcasted_iota(jnp.int32, (rows, 128), 1)
    return jnp.where(lane == 0, 1.0, 0.0).astype(BF16)


def _gqa_kv_kernel(k_ref, v_ref, kn_ref, cos_ref, sin_ref, ko_ref, vo_ref):
    cos, sin = cos_ref[...], sin_ref[...]
    ones = _ones_column(k_ref.shape[0])
    for h in range(A_KV_HEADS):
        k = _rms(k_ref[:, h * 128:(h + 1) * 128]) * kn_ref[...]
        ko_ref[:, h * 128:(h + 1) * 128] = _rope128(k, cos, sin).astype(BF16)
        vo_ref[:, h * 256:h * 256 + 128] = v_ref[:, h * 128:(h + 1) * 128].astype(BF16)
        vo_ref[:, h * 256 + 128:(h + 1) * 256] = ones


def _gqa_kv(p, k_norm, cos, sin):
    t = 512
    return pl.pallas_call(
        _gqa_kv_kernel,
        out_shape=[jax.ShapeDtypeStruct((N_TOK, A_KV_WIDTH), BF16),
                   jax.ShapeDtypeStruct((N_TOK, 2 * A_KV_WIDTH), BF16)],
        grid=(N_TOK // t,),
        in_specs=[
            pl.BlockSpec((t, A_KV_WIDTH), lambda i: (i, A_WIDTH // A_KV_WIDTH)),
            pl.BlockSpec((t, A_KV_WIDTH), lambda i: (i, A_WIDTH // A_KV_WIDTH + 1)),
            pl.BlockSpec((1, 128), lambda i: (0, 0)),
            pl.BlockSpec((t, 128), lambda i: (_rope_block(i, t), 0)),
            pl.BlockSpec((t, 128), lambda i: (_rope_block(i, t), 0)),
        ],
        out_specs=[pl.BlockSpec((t, A_KV_WIDTH), lambda i: (i, 0)),
                   pl.BlockSpec((t, 2 * A_KV_WIDTH), lambda i: (i, 0))],
        compiler_params=_cparams(("arbitrary",)),
        name="gqa_kv",
    )(p, p, k_norm, cos, sin)


LOG2E = math.log2(math.e)
KEY_PIECE = 1024


def _softmax_pv(q, keys, vals):
    dn = (((1,), (1,)), ((), ()))
    m = acc = None
    for k, v in zip(keys, vals):
        s = lax.dot_general(q, k, dn, preferred_element_type=F32)
        m_piece = jnp.max(s, axis=-1, keepdims=True)
        if m is None:
            m_new = m_piece
        else:
            m_new = jnp.maximum(m, m_piece)
            acc = acc * jnp.exp2(m - m_new)
        part = jnp.dot(jnp.exp2(s - m_new).astype(BF16), v, preferred_element_type=F32)
        acc = part if acc is None else acc + part
        m = m_new
    return acc[:, :128] / acc[:, 128:129]


def _attn_qrow(b, t):
    nq = SEQ // TQ
    return jnp.where(t == nq, (N_LAT + b * CTX_LEN) // TQ, b * nq + t)


def _ctx_half():
    return pl.multiple_of((pl.program_id(0) % (TQ // CTX_LEN)) * CTX_LEN, CTX_LEN)


def _gqa_attn_kernel(q_ref, qn_ref, cos_ref, sin_ref, kc_ref, vc_ref, kl_ref, vl_ref, ol_ref, oc_ref):
    scale = HEAD_DIM ** -0.5 * LOG2E
    is_ctx = pl.program_id(2) == SEQ // TQ

    def query(rows, g):
        q = _rms(q_ref[rows, g * 128:(g + 1) * 128]) * qn_ref[...]
        return (_rope128(q, cos_ref[rows, :], sin_ref[rows, :]) * scale).astype(BF16)

    @pl.when(jnp.logical_not(is_ctx))
    def _():
        pieces = [slice(j * KEY_PIECE, (j + 1) * KEY_PIECE) for j in range(SEQ // KEY_PIECE)]
        qs = [query(slice(None), g) for g in range(A_GROUP)]
        for g in range(A_GROUP):
            o = _softmax_pv(qs[g], [kc_ref[...]] + [kl_ref[r, :] for r in pieces],
                            [vc_ref[...]] + [vl_ref[r, :] for r in pieces])
            ol_ref[:, g * 128:(g + 1) * 128] = o.astype(BF16)

    @pl.when(is_ctx)
    def _():
        rows = pl.ds(_ctx_half(), CTX_LEN)
        for g in range(A_GROUP):
            oc_ref[:, g * 128:(g + 1) * 128] = _softmax_pv(query(rows, g), [kc_ref[...]], [vc_ref[...]]).astype(BF16)


def _gqa_attention(p, k, vext, q_norm, cos, sin):
    qw = A_GROUP * HEAD_DIM
    nq = SEQ // TQ
    ctx_blk = N_LAT // CTX_LEN
    rope = pl.BlockSpec((TQ, 128), lambda b, h, t: (_rope_block(_attn_qrow(b, t), TQ), 0))
    return pl.pallas_call(
        _gqa_attn_kernel,
        out_shape=[jax.ShapeDtypeStruct((N_LAT, A_WIDTH), BF16), jax.ShapeDtypeStruct((N_CTX, A_WIDTH), BF16)],
        grid=(BATCH, A_KV_HEADS, nq + 1),
        in_specs=[
            pl.BlockSpec((TQ, qw), lambda b, h, t: (_attn_qrow(b, t), h)),
            pl.BlockSpec((1, 128), lambda b, h, t: (0, 0)),
            rope, rope,
            pl.BlockSpec((CTX_LEN, 128), lambda b, h, t: (ctx_blk + b, h)),
            pl.BlockSpec((CTX_LEN, 256), lambda b, h, t: (ctx_blk + b, h)),
            pl.BlockSpec((SEQ, 128), lambda b, h, t: (b, h)),
            pl.BlockSpec((SEQ, 256), lambda b, h, t: (b, h)),
        ],
        out_specs=[pl.BlockSpec((TQ, qw), lambda b, h, t: (b * nq + jnp.minimum(t, nq - 1), h)),
                   pl.BlockSpec((CTX_LEN, qw), lambda b, h, t: (b, h))],
        compiler_params=_cparams(("arbitrary",) * 3),
        name="gqa_attn",
    )(p, q_norm, cos, sin, k, vext, k, vext)


def _conv_kernel(bg_ref, cg_ref, ug_ref, cgp_ref, ugp_ref, cgn_ref, ugn_ref, w_ref, o_ref, *, rows):
    i = pl.program_id(0)
    per_seq = SEQ // rows
    is_ctx = i >= N_LAT // rows
    is_start = jnp.logical_or(is_ctx, i % per_seq == 0)
    is_end = jnp.logical_or(is_ctx, i % per_seq == per_seq - 1)
    m = cg_ref[...] * ug_ref[...]
    m_prev = jnp.where(is_start, 0.0, cgp_ref[7:8, :] * ugp_ref[7:8, :])
    m_next = jnp.where(is_end, 0.0, cgn_ref[0:1, :] * ugn_ref[0:1, :])
    row = lax.broadcasted_iota(jnp.int32, m.shape, 0)
    down = jnp.where(row == 0, m_prev, pltpu.roll(m, 1, 0))
    up = jnp.where(row == rows - 1, m_next, pltpu.roll(m, rows - 1, 0))
    conv = down * w_ref[0:1, :] + m * w_ref[1:2, :] + up * w_ref[2:3, :]
    o_ref[...] = (bg_ref[...] * conv).astype(BF16)


def _gated_conv(p, conv_w):
    rows = CTX_LEN
    base = (A_WIDTH + 2 * A_KV_WIDTH) // B_WIDTH
    halo = rows // 8
    last = N_TOK // 8 - 1
    main = lambda c: pl.BlockSpec((rows, B_WIDTH), lambda i: (i, base + c))
    prev = lambda c: pl.BlockSpec((8, B_WIDTH), lambda i: (jnp.maximum(i * halo - 1, 0), base + c))
    nxt = lambda c: pl.BlockSpec((8, B_WIDTH), lambda i: (jnp.minimum((i + 1) * halo, last), base + c))
    return pl.pallas_call(
        functools.partial(_conv_kernel, rows=rows),
        out_shape=jax.ShapeDtypeStruct((N_TOK, B_WIDTH), BF16),
        grid=(N_TOK // rows,),
        in_specs=[main(0), main(1), main(2), prev(1), prev(2), nxt(1), nxt(2),
                  pl.BlockSpec((3, B_WIDTH), lambda i: (0, 0))],
        out_specs=pl.BlockSpec((rows, B_WIDTH), lambda i: (i, 0)),
        compiler_params=_cparams(("arbitrary",)),
        name="gated_conv",
    )(p, p, p, p, p, p, p, conv_w)


def _mla_kv_kernel(ckv_ref, g_ref, w_ref, kr_ref, cos_ref, sin_ref, ko_ref, vo_ref):
    ckv = (_rms(ckv_ref[...]) * g_ref[...]).astype(BF16)
    kv = jnp.dot(ckv, w_ref[...], preferred_element_type=F32)
    x = kr_ref[...]
    lane = lax.broadcasted_iota(jnp.int32, x.shape, 1)
    r = jnp.where(lane < C_ROPE, _rope64(x, cos_ref[...], sin_ref[...]), 0.0)
    kr_low = r.astype(BF16)
    kr_high = pltpu.roll(r, 64, 1).astype(BF16)
    ones = _ones_column(x.shape[0])
    for h in range(C_HEADS):
        ko_ref[:, h * 256:h * 256 + 128] = kv[:, h * 256:h * 256 + 128].astype(BF16)
        ko_ref[:, h * 256 + 128:(h + 1) * 256] = kr_low if h % 2 == 0 else kr_high
        vo_ref[:, h * 256:h * 256 + 128] = kv[:, h * 256 + 128:(h + 1) * 256].astype(BF16)
        vo_ref[:, h * 256 + 128:(h + 1) * 256] = ones


def _mla_kv(p, kv_norm, w_ukv, layer, cos, sin):
    t = 512
    col = (C_Q_RANK + C_KV_RANK + D_WIDTH) // 128
    width = C_HEADS * 256
    return pl.pallas_call(
        _mla_kv_kernel,
        out_shape=[jax.ShapeDtypeStruct((N_TOK, width), BF16)] * 2,
        grid=(N_TOK // t,),
        in_specs=[
            pl.BlockSpec((t, C_KV_RANK), lambda i: (i, C_Q_RANK // C_KV_RANK)),
            pl.BlockSpec((1, C_KV_RANK), lambda i: (0, 0)),
            pl.BlockSpec((None, C_KV_RANK, width), lambda i: (layer, 0, 0)),
            pl.BlockSpec((t, 128), lambda i: (i, col)),
            pl.BlockSpec((t, 128), lambda i: (_rope_block(i, t), 0)),
            pl.BlockSpec((t, 128), lambda i: (_rope_block(i, t), 0)),
        ],
        out_specs=[pl.BlockSpec((t, width), lambda i: (i, 0))] * 2,
        compiler_params=_cparams(("arbitrary",)),
        name="mla_kv",
    )(p, kv_norm, w_ukv, p, cos, sin)


def _mla_attn_kernel(qn_ref, qr_ref, cos_ref, sin_ref, kc_ref, vc_ref, kl_ref, vl_ref, ol_ref, oc_ref):
    scale = (C_NOPE + C_ROPE) ** -0.5 * LOG2E
    is_ctx = pl.program_id(2) == SEQ // TQ

    def queries(rows):
        qr = _rope64(qr_ref[rows, :], cos_ref[rows, :], sin_ref[rows, :])
        lane = lax.broadcasted_iota(jnp.int32, qr.shape, 1)
        out = []
        for hh in range(2):
            sel = (lane < 64) if hh == 0 else (lane >= 64)
            q = jnp.concatenate([qn_ref[rows, hh * 128:(hh + 1) * 128], jnp.where(sel, qr, 0.0)], axis=1)
            out.append((q * scale).astype(BF16))
        return out

    @pl.when(jnp.logical_not(is_ctx))
    def _():
        for hh, q in enumerate(queries(slice(None))):
            blk = slice(hh * 256, (hh + 1) * 256)
            pieces = [slice(j * KEY_PIECE, (j + 1) * KEY_PIECE) for j in range(SEQ // KEY_PIECE)]
            o = _softmax_pv(q, [kc_ref[:, blk]] + [kl_ref[r, blk] for r in pieces],
                            [vc_ref[:, blk]] + [vl_ref[r, blk] for r in pieces])
            ol_ref[:, hh * 128:(hh + 1) * 128] = o.astype(BF16)

    @pl.when(is_ctx)
    def _():
        for hh, q in enumerate(queries(pl.ds(_ctx_half(), CTX_LEN))):
            blk = slice(hh * 256, (hh + 1) * 256)
            oc_ref[:, hh * 128:(hh + 1) * 128] = _softmax_pv(q, [kc_ref[:, blk]], [vc_ref[:, blk]]).astype(BF16)


def _mla_attention(q, kcat, vext, cos, sin):
    nq = SEQ // TQ
    ctx_blk = N_LAT // CTX_LEN
    rope_col = C_HEADS * C_NOPE // 128
    qrow = _attn_qrow
    rope = pl.BlockSpec((TQ, 128), lambda b, h, t: (_rope_block(qrow(b, t), TQ), 0))
    ctx = pl.BlockSpec((CTX_LEN, 512), lambda b, h, t: (ctx_blk + b, h))
    lat = pl.BlockSpec((SEQ, 512), lambda b, h, t: (b, h))
    return pl.pallas_call(
        _mla_attn_kernel,
        out_shape=[jax.ShapeDtypeStruct((N_LAT, C_WIDTH), BF16), jax.ShapeDtypeStruct((N_CTX, C_WIDTH), BF16)],
        grid=(BATCH, C_HEADS // 2, nq + 1),
        in_specs=[
            pl.BlockSpec((TQ, 256), lambda b, h, t: (qrow(b, t), h)),
            pl.BlockSpec((TQ, 128), lambda b, h, t: (qrow(b, t), rope_col + h)),
            rope, rope, ctx, ctx, lat, lat,
        ],
        out_specs=[pl.BlockSpec((TQ, 256), lambda b, h, t: (b * nq + jnp.minimum(t, nq - 1), h)),
                   pl.BlockSpec((CTX_LEN, 256), lambda b, h, t: (b, h))],
        compiler_params=_cparams(("arbitrary",) * 3),
        name="mla_attn",
    )(q, q, cos, sin, kcat, vext, kcat, vext)


def _s5_matrices(lam_re, lam_im, log_dt, b_re, b_im, c_re, c_im):
    hi = lax.Precision.HIGHEST
    lam_re, lam_im = lam_re.astype(F32), lam_im.astype(F32)
    dt = jnp.exp(log_dt.astype(F32))[..., None]
    ks = jnp.arange(S5_CHUNK + 1, dtype=F32)[:, None, None, None]
    mag = jnp.exp(lam_re[None] * dt[None] * ks)
    ang = lam_im[None] * dt[None] * ks
    pw_re, pw_im = mag * jnp.cos(ang), mag * jnp.sin(ang)
    a_re, a_im = pw_re[1], pw_im[1]
    den = lam_re * lam_re + lam_im * lam_im
    f_re = ((a_re - 1.0) * lam_re + a_im * lam_im) / den
    f_im = (a_im * lam_re - (a_re - 1.0) * lam_im) / den
    b_re, b_im = b_re.astype(F32), b_im.astype(F32)
    bb_re = f_re[..., None] * b_re - f_im[..., None] * b_im
    bb_im = f_re[..., None] * b_im + f_im[..., None] * b_re
    c_re, c_im = c_re.astype(F32), c_im.astype(F32)

    ab_re = pw_re[:S5_CHUNK, ..., None] * bb_re[None] - pw_im[:S5_CHUNK, ..., None] * bb_im[None]
    ab_im = pw_re[:S5_CHUNK, ..., None] * bb_im[None] + pw_im[:S5_CHUNK, ..., None] * bb_re[None]
    kern = (jnp.einsum('dgcn,ldgne->ldgce', c_re, ab_re, precision=hi)
            - jnp.einsum('dgcn,ldgne->ldgce', c_im, ab_im, precision=hi))
    t_idx = np.arange(S5_CHUNK)
    sub = lambda m, ax: m.reshape(m.shape[:ax] + (S5_QUADS, S5_GSUB) + m.shape[ax + 1:])
    klag = jnp.transpose(sub(kern, 2), (1, 2, 0, 5, 3, 4)).reshape(2, S5_QUADS, S5_CHUNK, S5_GROUP, 128)
    bt_re = jnp.transpose(bb_re, (0, 1, 3, 2))
    bt_im = jnp.transpose(bb_im, (0, 1, 3, 2))
    lane_gc = lambda c: jnp.transpose(sub(c, 0), (0, 3, 1, 2)).reshape(S5_QUADS, S5_STATE, 128)
    ps, qs = [], []
    for d in range(2):
        p_pow = (S5_CHUNK - 1 - t_idx) if d == 0 else t_idx
        ar, ai = pw_re[p_pow, d][:, :, None, :], pw_im[p_pow, d][:, :, None, :]
        pc = jnp.concatenate([ar * bt_re[d][None] - ai * bt_im[d][None],
                              ar * bt_im[d][None] + ai * bt_re[d][None]], axis=-1)
        pc = pc.reshape(S5_CHUNK, S5_QUADS, 128, 128)
        ps.append(jnp.transpose(pc, (1, 0, 2, 3)).reshape(S5_QUADS, S5_CHUNK * 128, 128))
        q_pow = (t_idx + 1) if d == 0 else (S5_CHUNK - t_idx)
        cl_re, cl_im = lane_gc(c_re[d])[:, None], lane_gc(c_im[d])[:, None]
        rep = lambda a: jnp.repeat(jnp.transpose(sub(a, 1), (1, 0, 3, 2)), S5_GROUP, axis=-1)
        al_re, al_im = rep(pw_re[q_pow, d]), rep(pw_im[q_pow, d])
        qs.append(jnp.concatenate([cl_re * al_re - cl_im * al_im,
                                   -(cl_re * al_im + cl_im * al_re)], axis=2))
    a16_re, a16_im = pw_re[S5_CHUNK], pw_im[S5_CHUNK]
    lanes = S5_GROUPS * 2 * S5_STATE
    return dict(
        klag=klag.astype(BF16),
        p=jnp.stack(ps).astype(BF16),
        q=jnp.stack(qs).astype(BF16),
        a_mul=jnp.concatenate([a16_re, a16_re], axis=-1).reshape(2, 1, lanes),
        a_swp=jnp.concatenate([-a16_im, a16_im], axis=-1).reshape(2, 1, lanes),
    )


def _s5_expand_p(pc):
    rep = jnp.concatenate([pc] * S5_GSUB, axis=1)
    row = lax.broadcasted_iota(jnp.int32, rep.shape, 0)
    col = lax.broadcasted_iota(jnp.int32, rep.shape, 1)
    return jnp.where((row // S5_GROUP) % S5_GSUB == col // 128, rep, jnp.zeros_like(rep))


def _s5_expand_q(q_ref):
    row = lax.broadcasted_iota(jnp.int32, (S5_GSUB * 128, 128), 0)
    col = lax.broadcasted_iota(jnp.int32, (S5_GSUB * 128, 128), 1)
    keep = row // 128 == col // S5_GROUP
    blocks = [jnp.where(keep, jnp.concatenate([q_ref[t]] * S5_GSUB, axis=0), jnp.zeros((S5_GSUB * 128, 128), BF16))
              for t in range(S5_CHUNK)]
    return jnp.concatenate(blocks, axis=1)


def _s5_chunk_rows(u_ref):
    n = N_TOK // S5_CHUNK
    return jnp.concatenate([u_ref[pl.ds(s, n, stride=S5_CHUNK), :] for s in range(S5_CHUNK)], axis=-1).astype(BF16)


def _s5_z_kernel(u_ref, p_ref, z_ref):
    z_ref[...] = jnp.dot(_s5_chunk_rows(u_ref), _s5_expand_p(p_ref[...]), preferred_element_type=F32)


def _s5_scan_kernel(z_ref, amul_ref, aswp_ref, s_ref, zs_scr):
    d = pl.program_id(0)
    a_mul, a_swp = amul_ref[...], aswp_ref[...]
    lanes = s_ref.shape[1]
    z = z_ref[...]
    low_half = lax.broadcasted_iota(jnp.int32, z.shape, 1) % 128 < S5_STATE
    zs_scr[...] = jnp.where(low_half, pltpu.roll(z, lanes - S5_STATE, 1), pltpu.roll(z, S5_STATE, 1))
    nl, nc = SEQ // S5_CHUNK, S5_CTX_CHUNKS

    def segment(base, count, carry):
        def body(step, carry):
            s, w = carry
            k = jnp.where(d == 0, step, count - 1 - step)
            rows = [pl.ds(base + b * count + k, 1) for b in range(BATCH)]
            for b in range(BATCH):
                s_ref[rows[b], :] = s[b:b + 1, :]
            z = jnp.concatenate([z_ref[r, :] for r in rows], axis=0)
            zs = jnp.concatenate([zs_scr[r, :] for r in rows], axis=0)
            return a_mul * s + a_swp * w + z, a_mul * w - a_swp * s + zs
        return lax.fori_loop(0, count, body, carry)

    zero = jnp.zeros((BATCH, lanes), F32)
    carry = segment(BATCH * nl, nc, (zero, zero))
    segment(0, nl, carry)


def _s5_y_kernel(u_ref, klag_ref, s_ref, q_ref, dv_ref, y_ref, m_scr):
    d = pl.program_id(1)
    n = N_TOK // S5_CHUNK
    row = lax.broadcasted_iota(jnp.int32, (128, 128), 0)
    col = lax.broadcasted_iota(jnp.int32, (128, 128), 1)
    same_group = row // S5_GROUP == col // S5_GROUP
    blocks = [jnp.where(same_group, jnp.concatenate([klag_ref[lag]] * S5_GSUB, axis=0), jnp.zeros((128, 128), BF16))
              for lag in range(S5_CHUNK)]
    for rev in range(2):
        @pl.when(d == rev)
        def _():
            for s in range(S5_CHUNK):
                for t in range(S5_CHUNK):
                    lag = (s - t) if rev else (t - s)
                    blk = blocks[lag] if lag >= 0 else jnp.zeros((128, 128), BF16)
                    m_scr[s * 128:(s + 1) * 128, t * 128:(t + 1) * 128] = blk
    y = jnp.dot(_s5_chunk_rows(u_ref), m_scr[...], preferred_element_type=F32)
    y = y + jnp.dot(s_ref[...].astype(BF16), _s5_expand_q(q_ref), preferred_element_type=F32)
    for t in range(S5_CHUNK):
        rows = pl.ds(t, n, stride=S5_CHUNK)
        yt = y[:, t * 128:(t + 1) * 128]

        @pl.when(d == 0)
        def _():
            y_ref[rows, :] = yt + u_ref[rows, :] * dv_ref[...]

        @pl.when(d == 1)
        def _():
            y_ref[rows, :] = y_ref[rows, :] + yt


def _s5(p, mats, layer, dskip):
    nstate = S5_GROUPS * 2 * S5_STATE
    qlanes = nstate // S5_QUADS
    slanes = 2 * qlanes
    ucol = (C_Q_RANK + C_KV_RANK) // 128
    z = pl.pallas_call(
        _s5_z_kernel,
        out_shape=jax.ShapeDtypeStruct((2, S5_ROWS, nstate), F32),
        grid=(2, S5_QUADS),
        in_specs=[
            pl.BlockSpec((N_TOK, 128), lambda d, q: (0, ucol + q)),
            pl.BlockSpec((None, None, None, S5_CHUNK * 128, 128), lambda d, q: (layer, d, q, 0, 0)),
        ],
        out_specs=pl.BlockSpec((None, S5_ROWS, qlanes), lambda d, q: (d, 0, q)),
        compiler_params=_cparams(("arbitrary", "arbitrary")),
        name="s5_chunk_state",
    )(p, mats["p"])
    s = pl.pallas_call(
        _s5_scan_kernel,
        out_shape=jax.ShapeDtypeStruct((2, S5_ROWS, nstate), F32),
        grid=(2, nstate // slanes),
        in_specs=[
            pl.BlockSpec((None, S5_ROWS, slanes), lambda d, q: (d, 0, q)),
            pl.BlockSpec((None, None, 1, slanes), lambda d, q: (layer, d, 0, q)),
            pl.BlockSpec((None, None, 1, slanes), lambda d, q: (layer, d, 0, q)),
        ],
        out_specs=pl.BlockSpec((None, S5_ROWS, slanes), lambda d, q: (d, 0, q)),
        scratch_shapes=[pltpu.VMEM((S5_ROWS, slanes), F32)],
        compiler_params=_cparams(("arbitrary", "arbitrary")),
        name="s5_scan",
    )(z, mats["a_mul"], mats["a_swp"])
    return pl.pallas_call(
        _s5_y_kernel,
        out_shape=jax.ShapeDtypeStruct((N_TOK, D_WIDTH), F32),
        grid=(S5_QUADS, 2),
        in_specs=[
            pl.BlockSpec((N_TOK, 128), lambda q, d: (0, ucol + q)),
            pl.BlockSpec((None, None, None, S5_CHUNK, S5_GROUP, 128), lambda q, d: (layer, d, q, 0, 0, 0)),
            pl.BlockSpec((None, S5_ROWS, qlanes), lambda q, d: (d, 0, q)),
            pl.BlockSpec((None, None, None, S5_CHUNK, 128, 128), lambda q, d: (layer, d, q, 0, 0, 0)),
            pl.BlockSpec((None, 1, 128), lambda q, d: (layer, 0, q)),
        ],
        out_specs=pl.BlockSpec((N_TOK, 128), lambda q, d: (0, q)),
        scratch_shapes=[pltpu.VMEM((S5_CHUNK * 128, S5_CHUNK * 128), BF16)],
        compiler_params=_cparams(("arbitrary", "arbitrary"), VMEM_LIMIT_LARGE),
        name="s5_output",
    )(p, mats["klag"], s, mats["q"], dskip)


def _glu_kernel(y_ref, w_ref, o_ref):
    y = y_ref[...]
    z = y * (0.5 * (1.0 + jnp.tanh(math.sqrt(2.0 / math.pi) * (y + 0.044715 * (y * y * y)))))
    gate = jnp.dot(z.astype(BF16), w_ref[...], preferred_element_type=F32)
    o_ref[...] = (z * jax.nn.sigmoid(gate)).astype(BF16)


def _s5_glu(y, w_glu, layer):
    t = 512
    return pl.pallas_call(
        _glu_kernel,
        out_shape=jax.ShapeDtypeStruct((N_TOK, D_WIDTH), BF16),
        grid=(N_TOK // t,),
        in_specs=[pl.BlockSpec((t, D_WIDTH), lambda i: (i, 0)),
                  pl.BlockSpec((None, D_WIDTH, D_WIDTH), lambda i: (layer, 0, 0))],
        out_specs=pl.BlockSpec((t, D_WIDTH), lambda i: (i, 0)),
        compiler_params=_cparams(("arbitrary",)),
        name="s5_glu",
    )(y, w_glu)


LANE_CHUNKS = D_MODEL // 128


def _store_token_major(ref, val):
    rows = val.shape[0]
    for c in range(LANE_CHUNKS):
        ref[pl.ds(c, rows, stride=LANE_CHUNKS), :] = val[:, c * 128:(c + 1) * 128]


def _load_token_major(ref, rows):
    return jnp.concatenate([ref[pl.ds(c, rows, stride=LANE_CHUNKS), :] for c in range(LANE_CHUNKS)], axis=-1)


def _router_kernel(x_ref, g_ref, sh_ref, sc_ref, w_ref, h_ref, info_ref, gw_ref, cnt_ref, h_scr, carry):
    i = pl.program_id(0)

    @pl.when(i == 0)
    def _():
        carry[...] = jnp.zeros_like(carry)

    for r in range(0, TM, ROW_CHUNK):
        rows = slice(r, r + ROW_CHUNK)
        h_scr[rows, :] = _modulated_norm(x_ref[rows, :], g_ref[...], sh_ref[...], sc_ref[...])
    _store_token_major(h_ref, h_scr[...])
    h = h_scr[...]
    h_hi = h.astype(BF16)
    h_lo = (h - h_hi.astype(F32)).astype(BF16)
    both = jnp.dot(h_hi, w_ref[...], preferred_element_type=F32)
    logits = both[:, :128] + both[:, 128:] + jnp.dot(h_lo, w_ref[:, :128], preferred_element_type=F32)
    lane = lax.broadcasted_iota(jnp.int32, logits.shape, 1)
    neg = -jnp.inf
    big = jnp.int32(1 << 20)

    def first_argmax(v, vmax):
        return jnp.min(jnp.where(v == vmax, lane, big), axis=-1, keepdims=True)

    lg = jnp.where(lane < N_GROUPS, logits, neg)
    mg = jnp.max(lg, axis=-1, keepdims=True)
    g_w = 1.0 / jnp.sum(jnp.exp(lg - mg), axis=-1, keepdims=True)
    g_idx = first_argmax(lg, mg)
    lo = N_GROUPS + EXPERTS_PER_GROUP * g_idx
    le = jnp.where(jnp.logical_and(lane >= lo, lane < lo + EXPERTS_PER_GROUP), logits, neg)
    m1 = jnp.max(le, axis=-1, keepdims=True)
    i1 = first_argmax(le, m1)
    le2 = jnp.where(lane == i1, neg, le)
    m2 = jnp.max(le2, axis=-1, keepdims=True)
    i2 = first_argmax(le2, m2)
    r21 = jnp.exp(m2 - m1)
    w1 = g_w / (1.0 + r21)
    w2 = g_w * r21 / (1.0 + r21)
    oh = jnp.logical_or(lane == i1, lane == i2)
    ohb = jnp.where(oh, 1.0, 0.0).astype(BF16)
    rr = lax.broadcasted_iota(jnp.int32, (TM, TM), 0)
    cc = lax.broadcasted_iota(jnp.int32, (TM, TM), 1)
    lower = jnp.where(rr > cc, 1.0, 0.0).astype(BF16)
    before = jnp.dot(lower, ohb, preferred_element_type=F32) + carry[...]
    rank1 = jnp.sum(jnp.where(lane == i1, before, 0.0), axis=-1, keepdims=True).astype(jnp.int32)
    rank2 = jnp.sum(jnp.where(lane == i2, before, 0.0), axis=-1, keepdims=True).astype(jnp.int32)
    carry[...] = carry[...] + jnp.sum(ohb.astype(F32), axis=0, keepdims=True)
    cnt_ref[...] = jnp.broadcast_to(carry[...], cnt_ref.shape)
    info = jnp.where(lane == 0, i1 - N_GROUPS, jnp.where(lane == 1, i2 - N_GROUPS,
                     jnp.where(lane == 2, rank1, jnp.where(lane == 3, rank2, 0))))
    info_ref[...] = info
    gw_ref[...] = jnp.where(lane == 0, w1, jnp.where(lane == 1, w2, 0.0))


def _router(x, g, mod, w_router, layer):
    return pl.pallas_call(
        _router_kernel,
        out_shape=[jax.ShapeDtypeStruct((N_TOK * LANE_CHUNKS, 128), F32),
                   jax.ShapeDtypeStruct((N_TOK, 128), jnp.int32),
                   jax.ShapeDtypeStruct((N_TOK, 128), F32),
                   jax.ShapeDtypeStruct((8, 128), F32)],
        grid=(N_TOK // TM,),
        in_specs=[
            pl.BlockSpec((TM, D_MODEL), lambda i: (i, 0)),
            pl.BlockSpec((1, D_MODEL), lambda i: (0, 0)),
            pl.BlockSpec((None, None, 1, D_MODEL), lambda i: (_mod_row(i, TM), 3, 0, 0)),
            pl.BlockSpec((None, None, 1, D_MODEL), lambda i: (_mod_row(i, TM), 4, 0, 0)),
            pl.BlockSpec((None, D_MODEL, 256), lambda i: (layer, 0, 0)),
        ],
        out_specs=[pl.BlockSpec((TM * LANE_CHUNKS, 128), lambda i: (i, 0)),
                   pl.BlockSpec((TM, 128), lambda i: (i, 0)),
                   pl.BlockSpec((TM, 128), lambda i: (i, 0)),
                   pl.BlockSpec((8, 128), lambda i: (0, 0))],
        scratch_shapes=[pltpu.VMEM((TM, D_MODEL), F32), pltpu.VMEM((1, 128), F32)],
        compiler_params=_cparams(("arbitrary",)),
        name="moe_router",
    )(x, g, mod, mod, w_router)


FFN_ISSUE_GROUPS = 8
ISSUE_UNROLL = 8


def _ffn_kernel(pos_ref, te_ref, meta_ref, h_hbm, wg_ref, wu_ref, wd_ref, o_ref,
                src, xbuf, wg_b, wu_b, wd_b, sem):
    t = pl.program_id(0)
    nt = meta_ref[0]

    def row_copy(tile, slot, r):
        tok = src[tile * TE + r]
        return pltpu.make_async_copy(
            h_hbm.at[pl.ds(pl.multiple_of(tok * LANE_CHUNKS, LANE_CHUNKS), LANE_CHUNKS), :],
            xbuf.at[slot, pl.ds(pl.multiple_of(r * LANE_CHUNKS, LANE_CHUNKS), LANE_CHUNKS), :],
            sem.at[slot])

    def gather(tile, slot):
        def body(r, _):
            row_copy(tile, slot, r).start()
            return 0
        lax.fori_loop(0, TE, body, 0, unroll=ISSUE_UNROLL)

    def wait_tile(slot):
        pltpu.make_async_copy(xbuf.at[slot], xbuf.at[slot], sem.at[slot]).wait()

    @pl.when(t == 0)
    def _():
        for e in range(N_EXPERTS):
            def clear(i, _):
                src[i] = 0
                return 0
            lax.fori_loop(meta_ref[1 + e], meta_ref[1 + N_EXPERTS + e], clear, 0)

        def fill(tok, _):
            src[pos_ref[2 * tok]] = tok
            src[pos_ref[2 * tok + 1]] = tok
            return 0
        lax.fori_loop(0, N_TOK, fill, 0, unroll=8)
        gather(0, 0)

    @pl.when(t >= nt)
    def _():
        o_ref[...] = jnp.zeros_like(o_ref)

    @pl.when(t < nt)
    def _():
        slot = t % 2
        wait_tile(slot)

        @pl.when(jnp.logical_or(t == 0, te_ref[t] != te_ref[jnp.maximum(t - 1, 0)]))
        def _():
            wg_b[...] = wg_ref[...].astype(BF16)
            wu_b[...] = wu_ref[...].astype(BF16)
            wd_b[...] = wd_ref[...].astype(BF16)

        nxt = jnp.minimum(t + 1, nt - 1)
        per = TE // FFN_ISSUE_GROUPS

        def issue(g):
            for r in range(g * per, (g + 1) * per):
                row_copy(nxt, 1 - slot, r).start()

        x = _load_token_major(xbuf.at[slot], TE).astype(BF16)
        half = D_EXPERT // 2
        hg, hu = [], []
        for n in range(2):
            issue(n)
            hg.append(jnp.dot(x, wg_b[:, n * half:(n + 1) * half], preferred_element_type=F32))
        for n in range(2):
            issue(2 + n)
            hu.append(jnp.dot(x, wu_b[:, n * half:(n + 1) * half], preferred_element_type=F32))
        hg, hu = jnp.concatenate(hg, axis=1), jnp.concatenate(hu, axis=1)
        act = (hg * jax.nn.sigmoid(hg) * hu).astype(BF16)
        quarter = D_MODEL // 4
        for n in range(4):
            issue(4 + n)
            y = jnp.dot(act, wd_b[:, n * quarter:(n + 1) * quarter], preferred_element_type=F32)
            for c in range(quarter // 128):
                o_ref[pl.ds(n * (quarter // 128) + c, TE, stride=LANE_CHUNKS), :] = y[:, c * 128:(c + 1) * 128]

        @pl.when(t == nt - 1)
        def _():
            wait_tile(1 - slot)


def _expert_ffn(pos_flat, tile_expert, meta, h, w_gate, w_up, w_down, layer):
    wsel = lambda t, pos, te, meta: (layer, te[t], 0, 0)
    return pl.pallas_call(
        _ffn_kernel,
        out_shape=jax.ShapeDtypeStruct((N_SORT * LANE_CHUNKS, 128), F32),
        grid_spec=pltpu.PrefetchScalarGridSpec(
            num_scalar_prefetch=3,
            grid=(N_ETILES,),
            in_specs=[
                pl.BlockSpec(memory_space=pl.ANY),
                pl.BlockSpec((None, None, D_MODEL, D_EXPERT), wsel),
                pl.BlockSpec((None, None, D_MODEL, D_EXPERT), wsel),
                pl.BlockSpec((None, None, D_EXPERT, D_MODEL), wsel),
            ],
            out_specs=pl.BlockSpec((TE * LANE_CHUNKS, 128), lambda t, pos, te, meta: (t, 0)),
            scratch_shapes=[
                pltpu.SMEM((N_SORT,), jnp.int32),
                pltpu.VMEM((2, TE * LANE_CHUNKS, 128), F32),
                pltpu.VMEM((D_MODEL, D_EXPERT), BF16),
                pltpu.VMEM((D_MODEL, D_EXPERT), BF16),
                pltpu.VMEM((D_EXPERT, D_MODEL), BF16),
                pltpu.SemaphoreType.DMA((2,)),
            ],
        ),
        compiler_params=_cparams(("arbitrary",), VMEM_LIMIT_LARGE),
        name="moe_expert_ffn",
    )(pos_flat, tile_expert, meta, h, w_gate, w_up, w_down)


TC = 256


def _combine_kernel(pos_ref, x_ref, gate_ref, gw_ref, g_ref, sh_ref, sc_ref, ys_hbm, *refs, final_norm):
    if final_norm:
        o_ref, buf_a, buf_b, sem = refs
    else:
        o_ref, h_ref, buf_a, buf_b, sem = refs
    i = pl.program_id(0)
    slot = i % 2

    def slab(ref, row):
        return ref.at[pl.ds(pl.multiple_of(row * LANE_CHUNKS, LANE_CHUNKS), LANE_CHUNKS), :]

    def gather(tile, slot):
        def body(r, _):
            tok = tile * TC + r
            pltpu.make_async_copy(slab(ys_hbm, pos_ref[2 * tok]), slab(buf_a.at[slot], r), sem.at[slot, 0]).start()
            pltpu.make_async_copy(slab(ys_hbm, pos_ref[2 * tok + 1]), slab(buf_b.at[slot], r), sem.at[slot, 1]).start()
            return 0
        lax.fori_loop(0, TC, body, 0, unroll=ISSUE_UNROLL)

    @pl.when(i == 0)
    def _():
        gather(0, 0)

    @pl.when(i + 1 < pl.num_programs(0))
    def _():
        gather(i + 1, 1 - slot)

    pltpu.make_async_copy(buf_a.at[slot], buf_a.at[slot], sem.at[slot, 0]).wait()
    pltpu.make_async_copy(buf_b.at[slot], buf_b.at[slot], sem.at[slot, 1]).wait()
    w0 = jnp.broadcast_to(gw_ref[:, 0:1], (TC, 128))
    w1 = jnp.broadcast_to(gw_ref[:, 1:2], (TC, 128))
    sq = jnp.zeros((TC, 128), F32)
    for c in range(LANE_CHUNKS):
        cols = slice(c * 128, (c + 1) * 128)
        rows = pl.ds(c, TC, stride=LANE_CHUNKS)
        y = x_ref[:, cols] + gate_ref[:, cols] * (w0 * buf_a[slot, rows, :] + w1 * buf_b[slot, rows, :])
        o_ref[:, cols] = y
        sq = sq + y * y
    inv = lax.rsqrt(jnp.sum(sq, axis=-1, keepdims=True) / D_MODEL + EPS)
    for c in range(LANE_CHUNKS):
        cols = slice(c * 128, (c + 1) * 128)
        normed = o_ref[:, cols] * inv * g_ref[:, cols]
        if final_norm:
            o_ref[:, cols] = normed
        else:
            h_ref[:, cols] = (normed * (1.0 + sc_ref[:, cols]) + sh_ref[:, cols]).astype(BF16)


def _combine(pos_flat, x, mod, gw, ys, g, mod_next):
    final_norm = mod_next is None
    rows = N_LAT if final_norm else N_TOK
    tile = pl.BlockSpec((TC, D_MODEL), lambda i, pos: (i, 0))
    mod_row = lambda which: pl.BlockSpec((None, None, 1, D_MODEL), lambda i, pos: (_mod_row(i, TC), which, 0, 0))
    next_tab = mod if final_norm else mod_next
    out_shape = [jax.ShapeDtypeStruct((rows, D_MODEL), F32)]
    if not final_norm:
        out_shape.append(jax.ShapeDtypeStruct((rows, D_MODEL), BF16))
    return pl.pallas_call(
        functools.partial(_combine_kernel, final_norm=final_norm),
        out_shape=out_shape,
        grid_spec=pltpu.PrefetchScalarGridSpec(
            num_scalar_prefetch=1,
            grid=(rows // TC,),
            in_specs=[
                tile,
                mod_row(5),
                pl.BlockSpec((TC, 128), lambda i, pos: (i, 0)),
                pl.BlockSpec((1, D_MODEL), lambda i, pos: (0, 0)),
                mod_row(0), mod_row(1),
                pl.BlockSpec(memory_space=pl.ANY),
            ],
            out_specs=[tile] * len(out_shape),
            scratch_shapes=[pltpu.VMEM((2, TC * LANE_CHUNKS, 128), F32), pltpu.VMEM((2, TC * LANE_CHUNKS, 128), F32),
                            pltpu.SemaphoreType.DMA((2, 2))],
        ),
        compiler_params=_cparams(("arbitrary",)),
        name="moe_combine",
    )(pos_flat, x, mod, gw, g, next_tab, next_tab, ys)


def _moe(x, g, mod, w_router, w_gate, w_up, w_down, layer, g_next, mod_next):
    h, info, gw, cnt = _router(x, g, mod, w_router, layer)
    counts = cnt[0, N_GROUPS:N_GROUPS + N_EXPERTS].astype(jnp.int32)
    padded = ((counts + TE - 1) // TE) * TE
    ends = jnp.cumsum(padded)
    starts = ends - padded
    experts = jnp.arange(N_EXPERTS, dtype=jnp.int32)
    start_of = jnp.sum(jnp.where(info[:, 0:2, None] == experts, starts, 0), axis=-1)
    pos_flat = (start_of + info[:, 2:4]).reshape(-1)
    tile_ends = ends // TE
    num_tiles = tile_ends[-1]
    tiles = jnp.minimum(jnp.arange(N_ETILES, dtype=jnp.int32), num_tiles - 1)
    tile_expert = jnp.sum((tile_ends[None, :] <= tiles[:, None]).astype(jnp.int32), axis=-1)
    meta = jnp.concatenate([num_tiles[None], starts + counts, ends]).astype(jnp.int32)
    ys = _expert_ffn(pos_flat, tile_expert, meta, h, w_gate, w_up, w_down, layer)
    return _combine(pos_flat, x, mod, gw, ys, g_next, mod_next)


def kernel(x, c, ctx, c_ctx, mod_w, mod_b, norm_mix, norm_ffn, ab_w_in, ab_q_norm, ab_k_norm, ab_conv_w, ab_w_out, cd_w_in, cd_q_norm, cd_kv_norm, cd_w_uq, cd_w_ukv, s5_lam_re, s5_lam_im, s5_log_dt, s5_b_re, s5_b_im, s5_c_re, s5_c_im, s5_d, s5_w_glu, cd_w_out, moe_w_group, moe_w_expert, moe_w_gate, moe_w_up, moe_w_down, final_norm):
    cc = jnp.concatenate([c, c_ctx[None, :], jnp.zeros((8 - BATCH - 1, D_MODEL), F32)], axis=0)
    mods = _modulation(cc, mod_w, mod_b).reshape(DEPTH, 8, N_MOD, 1, D_MODEL)
    xs, h = _embed(x.reshape(N_LAT, D_MODEL), ctx.reshape(N_CTX, D_MODEL), norm_mix[0][None, :], mods[0])
    cos_a, sin_a = _rope_tables(HEAD_DIM)
    cos_c, sin_c = _rope_tables(C_ROPE)

    ab_in_b, ab_out_b = ab_w_in.astype(BF16), ab_w_out.astype(BF16)
    cd_out_b, ukv_b, glu_b = cd_w_out.astype(BF16), cd_w_ukv.astype(BF16), s5_w_glu.astype(BF16)
    a, b_ = C_Q_RANK + C_KV_RANK, C_Q_RANK + C_KV_RANK + C_ROPE
    pad = jnp.zeros(cd_w_in.shape[:2] + (CD_IN_PAD - cd_w_in.shape[2],), F32)
    cd_in_b = jnp.concatenate([cd_w_in[..., :a], cd_w_in[..., b_:], cd_w_in[..., a:b_], pad], axis=-1).astype(BF16)
    w_uq = cd_w_uq.reshape(-1, C_Q_RANK, C_HEADS, C_NOPE + C_ROPE)
    uq_b = jnp.concatenate([w_uq[..., :C_NOPE].reshape(-1, C_Q_RANK, C_HEADS * C_NOPE),
                            w_uq[..., C_NOPE:].reshape(-1, C_Q_RANK, C_HEADS * C_ROPE)], axis=-1).astype(BF16)

    mats = jax.vmap(_s5_matrices)(s5_lam_re, s5_lam_im, s5_log_dt, s5_b_re, s5_b_im, s5_c_re, s5_c_im)
    w_router = jnp.concatenate(
        [moe_w_group, jnp.transpose(moe_w_expert, (0, 2, 1, 3)).reshape(DEPTH, D_MODEL, N_EXPERTS),
         jnp.zeros((DEPTH, D_MODEL, 128 - N_GROUPS - N_EXPERTS), F32)], axis=-1)
    w_router_hi = w_router.astype(BF16)
    w_router = jnp.concatenate([w_router_hi, (w_router - w_router_hi.astype(F32)).astype(BF16)], axis=-1)

    for i in range(DEPTH):
        j = i // 2
        mod = mods[i]
        if i % 2 == 0:
            p = _linear(h, ab_in_b, j, 2048)
            k, vext = _gqa_kv(p, ab_k_norm[j][None, :], cos_a, sin_a)
            o_lat, o_ctx = _gqa_attention(p, k, vext, ab_q_norm[j][None, :], cos_a, sin_a)
            side = _gated_conv(p, ab_conv_w[j])
            w_out = ab_out_b
        else:
            p = _linear(h, cd_in_b, j, CD_IN_PAD)
            q = _norm_linear(p, 0, C_Q_RANK, cd_q_norm[j][None, :], uq_b, j)
            kcat, vext = _mla_kv(p, cd_kv_norm[j][None, :], ukv_b, j, cos_c, sin_c)
            o_lat, o_ctx = _mla_attention(q, kcat, vext, cos_c, sin_c)
            y = _s5(p, mats, j, s5_d.astype(F32)[:, None, :])
            side = _s5_glu(y, glu_b, j)
            w_out = cd_out_b
        xs = _out_linear(o_lat, o_ctx, side, w_out, j, xs, mod, 2)
        if i < DEPTH - 1:
            xs, h = _moe(xs, norm_ffn[i][None, :], mod, w_router, moe_w_gate, moe_w_up, moe_w_down, i,
                         norm_mix[i + 1][None, :], mods[i + 1])
        else:
            out, = _moe(xs, norm_ffn[i][None, :], mod, w_router, moe_w_gate, moe_w_up, moe_w_down, i,
                        final_norm[None, :], None)
    return out.reshape(BATCH, SEQ, D_MODEL)
```

```python
import functools
import math

import numpy as np
import jax
import jax.numpy as jnp
from jax import lax
from jax.experimental import pallas as pl
from jax.experimental.pallas import tpu as pltpu

F32 = jnp.float32
BF16 = jnp.bfloat16

D_MODEL = 2048
BATCH = 4
SEQ = 2048
DEPTH = 4
GRID_W = 64
CTX_LEN = 256
ROPE_THETA = 10000.0
EPS = 1e-6
N_MOD = 6
HEAD_DIM = 128
A_Q_HEADS = 12
A_KV_HEADS = 4
A_GROUP = A_Q_HEADS // A_KV_HEADS
A_WIDTH = A_Q_HEADS * HEAD_DIM
A_KV_WIDTH = A_KV_HEADS * HEAD_DIM
B_WIDTH = 512
AB_IN = A_WIDTH + 2 * A_KV_WIDTH + 3 * B_WIDTH
C_HEADS = 12
C_NOPE = 128
C_ROPE = 64
C_V = 128
C_Q_RANK = 512
C_KV_RANK = 256
C_WIDTH = C_HEADS * C_V
D_WIDTH = 512
S5_GROUP = 16
S5_GROUPS = D_WIDTH // S5_GROUP
S5_STATE = 64
N_GROUPS = 4
EXPERTS_PER_GROUP = 4
N_EXPERTS = N_GROUPS * EXPERTS_PER_GROUP
D_EXPERT = 512

N_LAT = BATCH * SEQ
N_CTX = BATCH * CTX_LEN
N_TOK = N_LAT + N_CTX
CTX_ROW = BATCH
CD_IN_PAD = 1536

TM = 512
TM_LINEAR = 1024
TQ = 1024
TE = 256
ROPE_IDENTITY_ROWS = max(TM, TQ)
N_SORT = 2 * N_TOK + N_EXPERTS * TE
N_ETILES = N_SORT // TE
S5_CHUNK = 16
S5_CTX_CHUNKS = CTX_LEN // S5_CHUNK
S5_ROWS = N_TOK // S5_CHUNK
S5_GSUB = 128 // S5_GROUP
S5_QUADS = S5_GROUPS // S5_GSUB
VMEM_LIMIT = 48 * 1024 * 1024
VMEM_LIMIT_LARGE = 56 * 1024 * 1024
ROW_CHUNK = 256


def _cparams(sem, limit=VMEM_LIMIT):
    return pltpu.CompilerParams(dimension_semantics=sem, vmem_limit_bytes=limit)


def _mod_row(tile, tile_rows):
    r0 = tile * tile_rows
    return jnp.where(r0 >= N_LAT, CTX_ROW, r0 // SEQ)


def _rope_block(tile, tile_rows):
    r0 = tile * tile_rows
    return jnp.where(r0 >= N_LAT, SEQ // tile_rows, (r0 % SEQ) // tile_rows)


def _rms(x):
    return x * lax.rsqrt(jnp.mean(x * x, axis=-1, keepdims=True) + EPS)


def _mod_kernel(cc_ref, w_ref, b_ref, o_ref):
    cc = cc_ref[...]
    s = (cc * jax.nn.sigmoid(cc)).astype(BF16)
    o_ref[...] = jnp.dot(s, w_ref[...].astype(BF16), preferred_element_type=F32) + b_ref[...]


def _modulation(cc, mod_w, mod_b):
    tn = 1024
    nout = N_MOD * D_MODEL
    return pl.pallas_call(
        _mod_kernel,
        out_shape=jax.ShapeDtypeStruct((DEPTH, 8, nout), F32),
        grid=(DEPTH, nout // tn),
        in_specs=[
            pl.BlockSpec((8, D_MODEL), lambda l, j: (0, 0)),
            pl.BlockSpec((None, D_MODEL, tn), lambda l, j: (l, 0, j)),
            pl.BlockSpec((None, 1, tn), lambda l, j: (l, 0, j)),
        ],
        out_specs=pl.BlockSpec((None, 8, tn), lambda l, j: (l, 0, j)),
        compiler_params=_cparams(("arbitrary", "arbitrary")),
        name="modulation",
    )(cc, mod_w, mod_b.reshape(DEPTH, 1, nout))


def _modulated_norm(x, g, sh, sc):
    return _rms(x) * g * (1.0 + sc) + sh


def _embed_kernel(xl_ref, xc_ref, g_ref, sh_ref, sc_ref, xs_ref, h_ref):
    is_ctx = pl.program_id(0) >= N_LAT // TM
    for r in range(0, TM, ROW_CHUNK):
        rows = slice(r, r + ROW_CHUNK)
        x = jnp.where(is_ctx, xc_ref[rows, :], xl_ref[rows, :])
        xs_ref[rows, :] = x
        h_ref[rows, :] = _modulated_norm(x, g_ref[...], sh_ref[...], sc_ref[...]).astype(BF16)


def _embed(x_lat, x_ctx, g, mod):
    lat_tiles = N_LAT // TM
    return pl.pallas_call(
        _embed_kernel,
        out_shape=[jax.ShapeDtypeStruct((N_TOK, D_MODEL), F32), jax.ShapeDtypeStruct((N_TOK, D_MODEL), BF16)],
        grid=(N_TOK // TM,),
        in_specs=[
            pl.BlockSpec((TM, D_MODEL), lambda i: (jnp.minimum(i, lat_tiles - 1), 0)),
            pl.BlockSpec((TM, D_MODEL), lambda i: (jnp.maximum(i - lat_tiles, 0), 0)),
            pl.BlockSpec((1, D_MODEL), lambda i: (0, 0)),
            pl.BlockSpec((None, None, 1, D_MODEL), lambda i: (_mod_row(i, TM), 0, 0, 0)),
            pl.BlockSpec((None, None, 1, D_MODEL), lambda i: (_mod_row(i, TM), 1, 0, 0)),
        ],
        out_specs=[pl.BlockSpec((TM, D_MODEL), lambda i: (i, 0))] * 2,
        compiler_params=_cparams(("arbitrary",)),
        name="embed_norm",
    )(x_lat, x_ctx, g, mod, mod)


def _linear_kernel(h_ref, w_ref, o_ref):
    o_ref[...] = jnp.dot(h_ref[...], w_ref[...], preferred_element_type=F32)


def _linear(h, w, layer, tn):
    kdim, nout = w.shape[1], w.shape[2]
    return pl.pallas_call(
        _linear_kernel,
        out_shape=jax.ShapeDtypeStruct((N_TOK, nout), F32),
        grid=(N_TOK // TM_LINEAR, nout // tn),
        in_specs=[
            pl.BlockSpec((TM_LINEAR, kdim), lambda i, j: (i, 0)),
            pl.BlockSpec((None, kdim, tn), lambda i, j: (layer, 0, j)),
        ],
        out_specs=pl.BlockSpec((TM_LINEAR, tn), lambda i, j: (i, j)),
        compiler_params=_cparams(("arbitrary", "arbitrary")),
        name="linear",
    )(h, w)


def _norm_linear_kernel(x_ref, g_ref, w_ref, o_ref):
    h = (_rms(x_ref[...]) * g_ref[...]).astype(BF16)
    o_ref[...] = jnp.dot(h, w_ref[...], preferred_element_type=F32)


def _norm_linear(x, xcol, kdim, g, w, layer):
    nout = w.shape[2]
    return pl.pallas_call(
        _norm_linear_kernel,
        out_shape=jax.ShapeDtypeStruct((N_TOK, nout), F32),
        grid=(N_TOK // TM,),
        in_specs=[
            pl.BlockSpec((TM, kdim), lambda i: (i, xcol)),
            pl.BlockSpec((1, kdim), lambda i: (0, 0)),
            pl.BlockSpec((None, kdim, nout), lambda i: (layer, 0, 0)),
        ],
        out_specs=pl.BlockSpec((TM, nout), lambda i: (i, 0)),
        compiler_params=_cparams(("arbitrary",)),
        name="norm_linear",
    )(x, g, w)


def _out_linear_kernel(a1l_ref, a1c_ref, a2_ref, w1_ref, w2_ref, x_ref, gate_ref, o_ref):
    a1 = jnp.where(pl.program_id(0) >= N_LAT // TM, a1c_ref[...], a1l_ref[...])
    acc = jnp.dot(a1, w1_ref[...], preferred_element_type=F32)
    acc = acc + jnp.dot(a2_ref[...], w2_ref[...], preferred_element_type=F32)
    o_ref[...] = x_ref[...] + gate_ref[...] * acc


def _out_linear(a1_lat, a1_ctx, a2, w, layer, x, mod, which):
    tn = D_MODEL
    k1, k2 = a1_lat.shape[1], a2.shape[1]
    lat_tiles = N_LAT // TM
    return pl.pallas_call(
        _out_linear_kernel,
        out_shape=jax.ShapeDtypeStruct((N_TOK, D_MODEL), F32),
        grid=(N_TOK // TM, D_MODEL // tn),
        in_specs=[
            pl.BlockSpec((TM, k1), lambda i, j: (jnp.minimum(i, lat_tiles - 1), 0)),
            pl.BlockSpec((TM, k1), lambda i, j: (jnp.maximum(i - lat_tiles, 0), 0)),
            pl.BlockSpec((TM, k2), lambda i, j: (i, 0)),
            pl.BlockSpec((None, k1, tn), lambda i, j: (layer, 0, j)),
            pl.BlockSpec((None, k2, tn), lambda i, j: (layer, k1 // k2, j)),
            pl.BlockSpec((TM, tn), lambda i, j: (i, j)),
            pl.BlockSpec((None, None, 1, tn), lambda i, j: (_mod_row(i, TM), which, 0, j)),
        ],
        out_specs=pl.BlockSpec((TM, tn), lambda i, j: (i, j)),
        compiler_params=_cparams(("arbitrary", "arbitrary")),
        name="out_linear",
    )(a1_lat, a1_ctx, a2, w, w, x, mod)


def _rope_tables(rot_dim):
    rows = SEQ // GRID_W
    row_ids = np.repeat(np.arange(rows, dtype=np.float32), GRID_W)
    col_ids = np.tile(np.arange(GRID_W, dtype=np.float32), rows)
    d_axis = rot_dim // 2
    inv = (np.float32(ROPE_THETA) ** (-np.arange(0, d_axis, 2, dtype=np.float32) / np.float32(d_axis))).astype(np.float32)
    ang = np.concatenate([row_ids[:, None] * inv, col_ids[:, None] * inv], axis=-1).astype(np.float32)
    cos, sin = np.cos(ang).astype(np.float32), np.sin(ang).astype(np.float32)
    reps = 128 // rot_dim
    cos_f = np.tile(np.concatenate([cos, cos], axis=-1), (1, reps))
    sin_f = np.tile(np.concatenate([-sin, sin], axis=-1), (1, reps))
    cos_f = np.concatenate([cos_f, np.ones((ROPE_IDENTITY_ROWS, 128), np.float32)], axis=0)
    sin_f = np.concatenate([sin_f, np.zeros((ROPE_IDENTITY_ROWS, 128), np.float32)], axis=0)
    return jnp.asarray(cos_f), jnp.asarray(sin_f)


def _rope128(x, cos, sin):
    return x * cos + pltpu.roll(x, 64, 1) * sin


def _rope64(x, cos, sin):
    lane = lax.broadcasted_iota(jnp.int32, x.shape, 1)
    swapped = jnp.where((lane % 64) < 32, pltpu.roll(x, 96, 1), pltpu.roll(x, 32, 1))
    return x * cos + swapped * sin


def _ones_column(rows):
    lane = lax.broadcasted_iota(jnp.int32, (rows, 128), 1)
    return jnp.where(lane == 0, 1.0, 0.0).astype(BF16)


def _gqa_kv_kernel(k_ref, v_ref, kn_ref, cos_ref, sin_ref, ko_ref, vo_ref):
    cos, sin = cos_ref[...], sin_ref[...]
    ones = _ones_column(k_ref.shape[0])
    for h in range(A_KV_HEADS):
        k = _rms(k_ref[:, h * 128:(h + 1) * 128]) * kn_ref[...]
        ko_ref[:, h * 128:(h + 1) * 128] = _rope128(k, cos, sin).astype(BF16)
        vo_ref[:, h * 256:h * 256 + 128] = v_ref[:, h * 128:(h + 1) * 128].astype(BF16)
        vo_ref[:, h * 256 + 128:(h + 1) * 256] = ones


def _gqa_kv(p, k_norm, cos, sin):
    t = 512
    return pl.pallas_call(
        _gqa_kv_kernel,
        out_shape=[jax.ShapeDtypeStruct((N_TOK, A_KV_WIDTH), BF16),
                   jax.ShapeDtypeStruct((N_TOK, 2 * A_KV_WIDTH), BF16)],
        grid=(N_TOK // t,),
        in_specs=[
            pl.BlockSpec((t, A_KV_WIDTH), lambda i: (i, A_WIDTH // A_KV_WIDTH)),
            pl.BlockSpec((t, A_KV_WIDTH), lambda i: (i, A_WIDTH // A_KV_WIDTH + 1)),
            pl.BlockSpec((1, 128), lambda i: (0, 0)),
            pl.BlockSpec((t, 128), lambda i: (_rope_block(i, t), 0)),
            pl.BlockSpec((t, 128), lambda i: (_rope_block(i, t), 0)),
        ],
        out_specs=[pl.BlockSpec((t, A_KV_WIDTH), lambda i: (i, 0)),
                   pl.BlockSpec((t, 2 * A_KV_WIDTH), lambda i: (i, 0))],
        compiler_params=_cparams(("arbitrary",)),
        name="gqa_kv",
    )(p, p, k_norm, cos, sin)


LOG2E = math.log2(math.e)
KEY_PIECE = 1024


def _softmax_pv(q, keys, vals):
    dn = (((1,), (1,)), ((), ()))
    m = acc = None
    for k, v in zip(keys, vals):
        s = lax.dot_general(q, k, dn, preferred_element_type=F32)
        m_piece = jnp.max(s, axis=-1, keepdims=True)
        if m is None:
            m_new = m_piece
        else:
            m_new = jnp.maximum(m, m_piece)
            acc = acc * jnp.exp2(m - m_new)
        part = jnp.dot(jnp.exp2(s - m_new).astype(BF16), v, preferred_element_type=F32)
        acc = part if acc is None else acc + part
        m = m_new
    return acc[:, :128] / acc[:, 128:129]


def _attn_qrow(b, t):
    nq = SEQ // TQ
    return jnp.where(t == nq, (N_LAT + b * CTX_LEN) // TQ, b * nq + t)


def _ctx_half():
    return pl.multiple_of((pl.program_id(0) % (TQ // CTX_LEN)) * CTX_LEN, CTX_LEN)


def _gqa_attn_kernel(q_ref, qn_ref, cos_ref, sin_ref, kc_ref, vc_ref, kl_ref, vl_ref, ol_ref, oc_ref):
    scale = HEAD_DIM ** -0.5 * LOG2E
    is_ctx = pl.program_id(2) == SEQ // TQ

    def query(rows, g):
        q = _rms(q_ref[rows, g * 128:(g + 1) * 128]) * qn_ref[...]
        return (_rope128(q, cos_ref[rows, :], sin_ref[rows, :]) * scale).astype(BF16)

    @pl.when(jnp.logical_not(is_ctx))
    def _():
        pieces = [slice(j * KEY_PIECE, (j + 1) * KEY_PIECE) for j in range(SEQ // KEY_PIECE)]
        qs = [query(slice(None), g) for g in range(A_GROUP)]
        for g in range(A_GROUP):
            o = _softmax_pv(qs[g], [kc_ref[...]] + [kl_ref[r, :] for r in pieces],
                            [vc_ref[...]] + [vl_ref[r, :] for r in pieces])
            ol_ref[:, g * 128:(g + 1) * 128] = o.astype(BF16)

    @pl.when(is_ctx)
    def _():
        rows = pl.ds(_ctx_half(), CTX_LEN)
        for g in range(A_GROUP):
            oc_ref[:, g * 128:(g + 1) * 128] = _softmax_pv(query(rows, g), [kc_ref[...]], [vc_ref[...]]).astype(BF16)


def _gqa_attention(p, k, vext, q_norm, cos, sin):
    qw = A_GROUP * HEAD_DIM
    nq = SEQ // TQ
    ctx_blk = N_LAT // CTX_LEN
    rope = pl.BlockSpec((TQ, 128), lambda b, h, t: (_rope_block(_attn_qrow(b, t), TQ), 0))
    return pl.pallas_call(
        _gqa_attn_kernel,
        out_shape=[jax.ShapeDtypeStruct((N_LAT, A_WIDTH), BF16), jax.ShapeDtypeStruct((N_CTX, A_WIDTH), BF16)],
        grid=(BATCH, A_KV_HEADS, nq + 1),
        in_specs=[
            pl.BlockSpec((TQ, qw), lambda b, h, t: (_attn_qrow(b, t), h)),
            pl.BlockSpec((1, 128), lambda b, h, t: (0, 0)),
            rope, rope,
            pl.BlockSpec((CTX_LEN, 128), lambda b, h, t: (ctx_blk + b, h)),
            pl.BlockSpec((CTX_LEN, 256), lambda b, h, t: (ctx_blk + b, h)),
            pl.BlockSpec((SEQ, 128), lambda b, h, t: (b, h)),
            pl.BlockSpec((SEQ, 256), lambda b, h, t: (b, h)),
        ],
        out_specs=[pl.BlockSpec((TQ, qw), lambda b, h, t: (b * nq + jnp.minimum(t, nq - 1), h)),
                   pl.BlockSpec((CTX_LEN, qw), lambda b, h, t: (b, h))],
        compiler_params=_cparams(("arbitrary",) * 3),
        name="gqa_attn",
    )(p, q_norm, cos, sin, k, vext, k, vext)


def _conv_kernel(bg_ref, cg_ref, ug_ref, cgp_ref, ugp_ref, cgn_ref, ugn_ref, w_ref, o_ref, *, rows):
    i = pl.program_id(0)
    per_seq = SEQ // rows
    is_ctx = i >= N_LAT // rows
    is_start = jnp.logical_or(is_ctx, i % per_seq == 0)
    is_end = jnp.logical_or(is_ctx, i % per_seq == per_seq - 1)
    m = cg_ref[...] * ug_ref[...]
    m_prev = jnp.where(is_start, 0.0, cgp_ref[7:8, :] * ugp_ref[7:8, :])
    m_next = jnp.where(is_end, 0.0, cgn_ref[0:1, :] * ugn_ref[0:1, :])
    row = lax.broadcasted_iota(jnp.int32, m.shape, 0)
    down = jnp.where(row == 0, m_prev, pltpu.roll(m, 1, 0))
    up = jnp.where(row == rows - 1, m_next, pltpu.roll(m, rows - 1, 0))
    conv = down * w_ref[0:1, :] + m * w_ref[1:2, :] + up * w_ref[2:3, :]
    o_ref[...] = (bg_ref[...] * conv).astype(BF16)


def _gated_conv(p, conv_w):
    rows = CTX_LEN
    base = (A_WIDTH + 2 * A_KV_WIDTH) // B_WIDTH
    halo = rows // 8
    last = N_TOK // 8 - 1
    main = lambda c: pl.BlockSpec((rows, B_WIDTH), lambda i: (i, base + c))
    prev = lambda c: pl.BlockSpec((8, B_WIDTH), lambda i: (jnp.maximum(i * halo - 1, 0), base + c))
    nxt = lambda c: pl.BlockSpec((8, B_WIDTH), lambda i: (jnp.minimum((i + 1) * halo, last), base + c))
    return pl.pallas_call(
        functools.partial(_conv_kernel, rows=rows),
        out_shape=jax.ShapeDtypeStruct((N_TOK, B_WIDTH), BF16),
        grid=(N_TOK // rows,),
        in_specs=[main(0), main(1), main(2), prev(1), prev(2), nxt(1), nxt(2),
                  pl.BlockSpec((3, B_WIDTH), lambda i: (0, 0))],
        out_specs=pl.BlockSpec((rows, B_WIDTH), lambda i: (i, 0)),
        compiler_params=_cparams(("arbitrary",)),
        name="gated_conv",
    )(p, p, p, p, p, p, p, conv_w)


def _mla_kv_kernel(ckv_ref, g_ref, w_ref, kr_ref, cos_ref, sin_ref, ko_ref, vo_ref):
    ckv = (_rms(ckv_ref[...]) * g_ref[...]).astype(BF16)
    kv = jnp.dot(ckv, w_ref[...], preferred_element_type=F32)
    x = kr_ref[...]
    lane = lax.broadcasted_iota(jnp.int32, x.shape, 1)
    r = jnp.where(lane < C_ROPE, _rope64(x, cos_ref[...], sin_ref[...]), 0.0)
    kr_low = r.astype(BF16)
    kr_high = pltpu.roll(r, 64, 1).astype(BF16)
    ones = _ones_column(x.shape[0])
    for h in range(C_HEADS):
        ko_ref[:, h * 256:h * 256 + 128] = kv[:, h * 256:h * 256 + 128].astype(BF16)
        ko_ref[:, h * 256 + 128:(h + 1) * 256] = kr_low if h % 2 == 0 else kr_high
        vo_ref[:, h * 256:h * 256 + 128] = kv[:, h * 256 + 128:(h + 1) * 256].astype(BF16)
        vo_ref[:, h * 256 + 128:(h + 1) * 256] = ones


def _mla_kv(p, kv_norm, w_ukv, layer, cos, sin):
    t = 512
    col = (C_Q_RANK + C_KV_RANK + D_WIDTH) // 128
    width = C_HEADS * 256
    return pl.pallas_call(
        _mla_kv_kernel,
        out_shape=[jax.ShapeDtypeStruct((N_TOK, width), BF16)] * 2,
        grid=(N_TOK // t,),
        in_specs=[
            pl.BlockSpec((t, C_KV_RANK), lambda i: (i, C_Q_RANK // C_KV_RANK)),
            pl.BlockSpec((1, C_KV_RANK), lambda i: (0, 0)),
            pl.BlockSpec((None, C_KV_RANK, width), lambda i: (layer, 0, 0)),
            pl.BlockSpec((t, 128), lambda i: (i, col)),
            pl.BlockSpec((t, 128), lambda i: (_rope_block(i, t), 0)),
            pl.BlockSpec((t, 128), lambda i: (_rope_block(i, t), 0)),
        ],
        out_specs=[pl.BlockSpec((t, width), lambda i: (i, 0))] * 2,
        compiler_params=_cparams(("arbitrary",)),
        name="mla_kv",
    )(p, kv_norm, w_ukv, p, cos, sin)


def _mla_attn_kernel(qn_ref, qr_ref, cos_ref, sin_ref, kc_ref, vc_ref, kl_ref, vl_ref, ol_ref, oc_ref):
    scale = (C_NOPE + C_ROPE) ** -0.5 * LOG2E
    is_ctx = pl.program_id(2) == SEQ // TQ

    def queries(rows):
        qr = _rope64(qr_ref[rows, :], cos_ref[rows, :], sin_ref[rows, :])
        lane = lax.broadcasted_iota(jnp.int32, qr.shape, 1)
        out = []
        for hh in range(2):
            sel = (lane < 64) if hh == 0 else (lane >= 64)
            q = jnp.concatenate([qn_ref[rows, hh * 128:(hh + 1) * 128], jnp.where(sel, qr, 0.0)], axis=1)
            out.append((q * scale).astype(BF16))
        return out

    @pl.when(jnp.logical_not(is_ctx))
    def _():
        for hh, q in enumerate(queries(slice(None))):
            blk = slice(hh * 256, (hh + 1) * 256)
            pieces = [slice(j * KEY_PIECE, (j + 1) * KEY_PIECE) for j in range(SEQ // KEY_PIECE)]
            o = _softmax_pv(q, [kc_ref[:, blk]] + [kl_ref[r, blk] for r in pieces],
                            [vc_ref[:, blk]] + [vl_ref[r, blk] for r in pieces])
            ol_ref[:, hh * 128:(hh + 1) * 128] = o.astype(BF16)

    @pl.when(is_ctx)
    def _():
        for hh, q in enumerate(queries(pl.ds(_ctx_half(), CTX_LEN))):
            blk = slice(hh * 256, (hh + 1) * 256)
            oc_ref[:, hh * 128:(hh + 1) * 128] = _softmax_pv(q, [kc_ref[:, blk]], [vc_ref[:, blk]]).astype(BF16)


def _mla_attention(q, kcat, vext, cos, sin):
    nq = SEQ // TQ
    ctx_blk = N_LAT // CTX_LEN
    rope_col = C_HEADS * C_NOPE // 128
    qrow = _attn_qrow
    rope = pl.BlockSpec((TQ, 128), lambda b, h, t: (_rope_block(qrow(b, t), TQ), 0))
    ctx = pl.BlockSpec((CTX_LEN, 512), lambda b, h, t: (ctx_blk + b, h))
    lat = pl.BlockSpec((SEQ, 512), lambda b, h, t: (b, h))
    return pl.pallas_call(
        _mla_attn_kernel,
        out_shape=[jax.ShapeDtypeStruct((N_LAT, C_WIDTH), BF16), jax.ShapeDtypeStruct((N_CTX, C_WIDTH), BF16)],
        grid=(BATCH, C_HEADS // 2, nq + 1),
        in_specs=[
            pl.BlockSpec((TQ, 256), lambda b, h, t: (qrow(b, t), h)),
            pl.BlockSpec((TQ, 128), lambda b, h, t: (qrow(b, t), rope_col + h)),
            rope, rope, ctx, ctx, lat, lat,
        ],
        out_specs=[pl.BlockSpec((TQ, 256), lambda b, h, t: (b * nq + jnp.minimum(t, nq - 1), h)),
                   pl.BlockSpec((CTX_LEN, 256), lambda b, h, t: (b, h))],
        compiler_params=_cparams(("arbitrary",) * 3),
        name="mla_attn",
    )(q, q, cos, sin, kcat, vext, kcat, vext)


def _s5_matrices(lam_re, lam_im, log_dt, b_re, b_im, c_re, c_im):
    hi = lax.Precision.HIGHEST
    lam_re, lam_im = lam_re.astype(F32), lam_im.astype(F32)
    dt = jnp.exp(log_dt.astype(F32))[..., None]
    ks = jnp.arange(S5_CHUNK + 1, dtype=F32)[:, None, None, None]
    mag = jnp.exp(lam_re[None] * dt[None] * ks)
    ang = lam_im[None] * dt[None] * ks
    pw_re, pw_im = mag * jnp.cos(ang), mag * jnp.sin(ang)
    a_re, a_im = pw_re[1], pw_im[1]
    den = lam_re * lam_re + lam_im * lam_im
    f_re = ((a_re - 1.0) * lam_re + a_im * lam_im) / den
    f_im = (a_im * lam_re - (a_re - 1.0) * lam_im) / den
    b_re, b_im = b_re.astype(F32), b_im.astype(F32)
    bb_re = f_re[..., None] * b_re - f_im[..., None] * b_im
    bb_im = f_re[..., None] * b_im + f_im[..., None] * b_re
    c_re, c_im = c_re.astype(F32), c_im.astype(F32)

    ab_re = pw_re[:S5_CHUNK, ..., None] * bb_re[None] - pw_im[:S5_CHUNK, ..., None] * bb_im[None]
    ab_im = pw_re[:S5_CHUNK, ..., None] * bb_im[None] + pw_im[:S5_CHUNK, ..., None] * bb_re[None]
    kern = (jnp.einsum('dgcn,ldgne->ldgce', c_re, ab_re, precision=hi)
            - jnp.einsum('dgcn,ldgne->ldgce', c_im, ab_im, precision=hi))
    t_idx = np.arange(S5_CHUNK)
    sub = lambda m, ax: m.reshape(m.shape[:ax] + (S5_QUADS, S5_GSUB) + m.shape[ax + 1:])
    klag = jnp.transpose(sub(kern, 2), (1, 2, 0, 5, 3, 4)).reshape(2, S5_QUADS, S5_CHUNK, S5_GROUP, 128)
    bt_re = jnp.transpose(bb_re, (0, 1, 3, 2))
    bt_im = jnp.transpose(bb_im, (0, 1, 3, 2))
    lane_gc = lambda c: jnp.transpose(sub(c, 0), (0, 3, 1, 2)).reshape(S5_QUADS, S5_STATE, 128)
    ps, qs = [], []
    for d in range(2):
        p_pow = (S5_CHUNK - 1 - t_idx) if d == 0 else t_idx
        ar, ai = pw_re[p_pow, d][:, :, None, :], pw_im[p_pow, d][:, :, None, :]
        pc = jnp.concatenate([ar * bt_re[d][None] - ai * bt_im[d][None],
                              ar * bt_im[d][None] + ai * bt_re[d][None]], axis=-1)
        pc = pc.reshape(S5_CHUNK, S5_QUADS, 128, 128)
        ps.append(jnp.transpose(pc, (1, 0, 2, 3)).reshape(S5_QUADS, S5_CHUNK * 128, 128))
        q_pow = (t_idx + 1) if d == 0 else (S5_CHUNK - t_idx)
        cl_re, cl_im = lane_gc(c_re[d])[:, None], lane_gc(c_im[d])[:, None]
        rep = lambda a: jnp.repeat(jnp.transpose(sub(a, 1), (1, 0, 3, 2)), S5_GROUP, axis=-1)
        al_re, al_im = rep(pw_re[q_pow, d]), rep(pw_im[q_pow, d])
        qs.append(jnp.concatenate([cl_re * al_re - cl_im * al_im,
                                   -(cl_re * al_im + cl_im * al_re)], axis=2))
    a16_re, a16_im = pw_re[S5_CHUNK], pw_im[S5_CHUNK]
    lanes = S5_GROUPS * 2 * S5_STATE
    return dict(
        klag=klag.astype(BF16),
        p=jnp.stack(ps).astype(BF16),
        q=jnp.stack(qs).astype(BF16),
        a_mul=jnp.concatenate([a16_re, a16_re], axis=-1).reshape(2, 1, lanes),
        a_swp=jnp.concatenate([-a16_im, a16_im], axis=-1).reshape(2, 1, lanes),
    )


def _s5_expand_p(pc):
    rep = jnp.concatenate([pc] * S5_GSUB, axis=1)
    row = lax.broadcasted_iota(jnp.int32, rep.shape, 0)
    col = lax.broadcasted_iota(jnp.int32, rep.shape, 1)
    return jnp.where((row // S5_GROUP) % S5_GSUB == col // 128, rep, jnp.zeros_like(rep))


def _s5_expand_q(q_ref):
    row = lax.broadcasted_iota(jnp.int32, (S5_GSUB * 128, 128), 0)
    col = lax.broadcasted_iota(jnp.int32, (S5_GSUB * 128, 128), 1)
    keep = row // 128 == col // S5_GROUP
    blocks = [jnp.where(keep, jnp.concatenate([q_ref[t]] * S5_GSUB, axis=0), jnp.zeros((S5_GSUB * 128, 128), BF16))
              for t in range(S5_CHUNK)]
    return jnp.concatenate(blocks, axis=1)


def _s5_chunk_rows(u_ref):
    n = N_TOK // S5_CHUNK
    return jnp.concatenate([u_ref[pl.ds(s, n, stride=S5_CHUNK), :] for s in range(S5_CHUNK)], axis=-1).astype(BF16)


def _s5_z_kernel(u_ref, p_ref, z_ref):
    z_ref[...] = jnp.dot(_s5_chunk_rows(u_ref), _s5_expand_p(p_ref[...]), preferred_element_type=F32)


def _s5_scan_kernel(z_ref, amul_ref, aswp_ref, s_ref, zs_scr):
    d = pl.program_id(0)
    a_mul, a_swp = amul_ref[...], aswp_ref[...]
    lanes = s_ref.shape[1]
    z = z_ref[...]
    low_half = lax.broadcasted_iota(jnp.int32, z.shape, 1) % 128 < S5_STATE
    zs_scr[...] = jnp.where(low_half, pltpu.roll(z, lanes - S5_STATE, 1), pltpu.roll(z, S5_STATE, 1))
    nl, nc = SEQ // S5_CHUNK, S5_CTX_CHUNKS

    def segment(base, count, carry):
        def body(step, carry):
            s, w = carry
            k = jnp.where(d == 0, step, count - 1 - step)
            rows = [pl.ds(base + b * count + k, 1) for b in range(BATCH)]
            for b in range(BATCH):
                s_ref[rows[b], :] = s[b:b + 1, :]
            z = jnp.concatenate([z_ref[r, :] for r in rows], axis=0)
            zs = jnp.concatenate([zs_scr[r, :] for r in rows], axis=0)
            return a_mul * s + a_swp * w + z, a_mul * w - a_swp * s + zs
        return lax.fori_loop(0, count, body, carry)

    zero = jnp.zeros((BATCH, lanes), F32)
    carry = segment(BATCH * nl, nc, (zero, zero))
    segment(0, nl, carry)


def _s5_y_kernel(u_ref, klag_ref, s_ref, q_ref, dv_ref, y_ref, m_scr):
    d = pl.program_id(1)
    n = N_TOK // S5_CHUNK
    row = lax.broadcasted_iota(jnp.int32, (128, 128), 0)
    col = lax.broadcasted_iota(jnp.int32, (128, 128), 1)
    same_group = row // S5_GROUP == col // S5_GROUP
    blocks = [jnp.where(same_group, jnp.concatenate([klag_ref[lag]] * S5_GSUB, axis=0), jnp.zeros((128, 128), BF16))
              for lag in range(S5_CHUNK)]
    for rev in range(2):
        @pl.when(d == rev)
        def _():
            for s in range(S5_CHUNK):
                for t in range(S5_CHUNK):
                    lag = (s - t) if rev else (t - s)
                    blk = blocks[lag] if lag >= 0 else jnp.zeros((128, 128), BF16)
                    m_scr[s * 128:(s + 1) * 128, t * 128:(t + 1) * 128] = blk
    y = jnp.dot(_s5_chunk_rows(u_ref), m_scr[...], preferred_element_type=F32)
    y = y + jnp.dot(s_ref[...].astype(BF16), _s5_expand_q(q_ref), preferred_element_type=F32)
    for t in range(S5_CHUNK):
        rows = pl.ds(t, n, stride=S5_CHUNK)
        yt = y[:, t * 128:(t + 1) * 128]

        @pl.when(d == 0)
        def _():
            y_ref[rows, :] = yt + u_ref[rows, :] * dv_ref[...]

        @pl.when(d == 1)
        def _():
            y_ref[rows, :] = y_ref[rows, :] + yt


def _s5(p, mats, layer, dskip):
    nstate = S5_GROUPS * 2 * S5_STATE
    qlanes = nstate // S5_QUADS
    slanes = qlanes
    ucol = (C_Q_RANK + C_KV_RANK) // 128
    z = pl.pallas_call(
        _s5_z_kernel,
        out_shape=jax.ShapeDtypeStruct((2, S5_ROWS, nstate), F32),
        grid=(2, S5_QUADS),
        in_specs=[
            pl.BlockSpec((N_TOK, 128), lambda d, q: (0, ucol + q)),
            pl.BlockSpec((None, None, None, S5_CHUNK * 128, 128), lambda d, q: (layer, d, q, 0, 0)),
        ],
        out_specs=pl.BlockSpec((None, S5_ROWS, qlanes), lambda d, q: (d, 0, q)),
        compiler_params=_cparams(("arbitrary", "arbitrary")),
        name="s5_chunk_state",
    )(p, mats["p"])
    s = pl.pallas_call(
        _s5_scan_kernel,
        out_shape=jax.ShapeDtypeStruct((2, S5_ROWS, nstate), F32),
        grid=(2, nstate // slanes),
        in_specs=[
            pl.BlockSpec((None, S5_ROWS, slanes), lambda d, q: (d, 0, q)),
            pl.BlockSpec((None, None, 1, slanes), lambda d, q: (layer, d, 0, q)),
            pl.BlockSpec((None, None, 1, slanes), lambda d, q: (layer, d, 0, q)),
        ],
        out_specs=pl.BlockSpec((None, S5_ROWS, slanes), lambda d, q: (d, 0, q)),
        scratch_shapes=[pltpu.VMEM((S5_ROWS, slanes), F32)],
        compiler_params=_cparams(("arbitrary", "arbitrary")),
        name="s5_scan",
    )(z, mats["a_mul"], mats["a_swp"])
    return pl.pallas_call(
        _s5_y_kernel,
        out_shape=jax.ShapeDtypeStruct((N_TOK, D_WIDTH), F32),
        grid=(S5_QUADS, 2),
        in_specs=[
            pl.BlockSpec((N_TOK, 128), lambda q, d: (0, ucol + q)),
            pl.BlockSpec((None, None, None, S5_CHUNK, S5_GROUP, 128), lambda q, d: (layer, d, q, 0, 0, 0)),
            pl.BlockSpec((None, S5_ROWS, qlanes), lambda q, d: (d, 0, q)),
            pl.BlockSpec((None, None, None, S5_CHUNK, 128, 128), lambda q, d: (layer, d, q, 0, 0, 0)),
            pl.BlockSpec((None, 1, 128), lambda q, d: (layer, 0, q)),
        ],
        out_specs=pl.BlockSpec((N_TOK, 128), lambda q, d: (0, q)),
        scratch_shapes=[pltpu.VMEM((S5_CHUNK * 128, S5_CHUNK * 128), BF16)],
        compiler_params=_cparams(("arbitrary", "arbitrary"), VMEM_LIMIT_LARGE),
        name="s5_output",
    )(p, mats["klag"], s, mats["q"], dskip)


def _glu_kernel(y_ref, w_ref, o_ref):
    y = y_ref[...]
    z = y * (0.5 * (1.0 + jnp.tanh(math.sqrt(2.0 / math.pi) * (y + 0.044715 * (y * y * y)))))
    gate = jnp.dot(z.astype(BF16), w_ref[...], preferred_element_type=F32)
    o_ref[...] = (z * jax.nn.sigmoid(gate)).astype(BF16)


def _s5_glu(y, w_glu, layer):
    t = 512
    return pl.pallas_call(
        _glu_kernel,
        out_shape=jax.ShapeDtypeStruct((N_TOK, D_WIDTH), BF16),
        grid=(N_TOK // t,),
        in_specs=[pl.BlockSpec((t, D_WIDTH), lambda i: (i, 0)),
                  pl.BlockSpec((None, D_WIDTH, D_WIDTH), lambda i: (layer, 0, 0))],
        out_specs=pl.BlockSpec((t, D_WIDTH), lambda i: (i, 0)),
        compiler_params=_cparams(("arbitrary",)),
        name="s5_glu",
    )(y, w_glu)


LANE_CHUNKS = D_MODEL // 128


def _store_token_major(ref, val):
    rows = val.shape[0]
    for c in range(LANE_CHUNKS):
        ref[pl.ds(c, rows, stride=LANE_CHUNKS), :] = val[:, c * 128:(c + 1) * 128]


def _load_token_major(ref, rows):
    return jnp.concatenate([ref[pl.ds(c, rows, stride=LANE_CHUNKS), :] for c in range(LANE_CHUNKS)], axis=-1)


def _router_kernel(x_ref, g_ref, sh_ref, sc_ref, w_ref, h_ref, info_ref, gw_ref, cnt_ref, h_scr, carry):
    i = pl.program_id(0)

    @pl.when(i == 0)
    def _():
        carry[...] = jnp.zeros_like(carry)

    for r in range(0, TM, ROW_CHUNK):
        rows = slice(r, r + ROW_CHUNK)
        h_scr[rows, :] = _modulated_norm(x_ref[rows, :], g_ref[...], sh_ref[...], sc_ref[...])
    _store_token_major(h_ref, h_scr[...])
    h = h_scr[...]
    h_hi = h.astype(BF16)
    h_lo = (h - h_hi.astype(F32)).astype(BF16)
    both = jnp.dot(h_hi, w_ref[...], preferred_element_type=F32)
    logits = both[:, :128] + both[:, 128:] + jnp.dot(h_lo, w_ref[:, :128], preferred_element_type=F32)
    lane = lax.broadcasted_iota(jnp.int32, logits.shape, 1)
    neg = -jnp.inf
    big = jnp.int32(1 << 20)

    def first_argmax(v, vmax):
        return jnp.min(jnp.where(v == vmax, lane, big), axis=-1, keepdims=True)

    lg = jnp.where(lane < N_GROUPS, logits, neg)
    mg = jnp.max(lg, axis=-1, keepdims=True)
    g_w = 1.0 / jnp.sum(jnp.exp(lg - mg), axis=-1, keepdims=True)
    g_idx = first_argmax(lg, mg)
    lo = N_GROUPS + EXPERTS_PER_GROUP * g_idx
    le = jnp.where(jnp.logical_and(lane >= lo, lane < lo + EXPERTS_PER_GROUP), logits, neg)
    m1 = jnp.max(le, axis=-1, keepdims=True)
    i1 = first_argmax(le, m1)
    le2 = jnp.where(lane == i1, neg, le)
    m2 = jnp.max(le2, axis=-1, keepdims=True)
    i2 = first_argmax(le2, m2)
    r21 = jnp.exp(m2 - m1)
    w1 = g_w / (1.0 + r21)
    w2 = g_w * r21 / (1.0 + r21)
    oh = jnp.logical_or(lane == i1, lane == i2)
    ohb = jnp.where(oh, 1.0, 0.0).astype(BF16)
    rr = lax.broadcasted_iota(jnp.int32, (TM, TM), 0)
    cc = lax.broadcasted_iota(jnp.int32, (TM, TM), 1)
    lower = jnp.where(rr > cc, 1.0, 0.0).astype(BF16)
    before = jnp.dot(lower, ohb, preferred_element_type=F32) + carry[...]
    rank1 = jnp.sum(jnp.where(lane == i1, before, 0.0), axis=-1, keepdims=True).astype(jnp.int32)
    rank2 = jnp.sum(jnp.where(lane == i2, before, 0.0), axis=-1, keepdims=True).astype(jnp.int32)
    carry[...] = carry[...] + jnp.sum(ohb.astype(F32), axis=0, keepdims=True)
    cnt_ref[...] = jnp.broadcast_to(carry[...], cnt_ref.shape)
    info = jnp.where(lane == 0, i1 - N_GROUPS, jnp.where(lane == 1, i2 - N_GROUPS,
                     jnp.where(lane == 2, rank1, jnp.where(lane == 3, rank2, 0))))
    info_ref[...] = info
    gw_ref[...] = jnp.where(lane == 0, w1, jnp.where(lane == 1, w2, 0.0))


def _router(x, g, mod, w_router, layer):
    return pl.pallas_call(
        _router_kernel,
        out_shape=[jax.ShapeDtypeStruct((N_TOK * LANE_CHUNKS, 128), F32),
                   jax.ShapeDtypeStruct((N_TOK, 128), jnp.int32),
                   jax.ShapeDtypeStruct((N_TOK, 128), F32),
                   jax.ShapeDtypeStruct((8, 128), F32)],
        grid=(N_TOK // TM,),
        in_specs=[
            pl.BlockSpec((TM, D_MODEL), lambda i: (i, 0)),
            pl.BlockSpec((1, D_MODEL), lambda i: (0, 0)),
            pl.BlockSpec((None, None, 1, D_MODEL), lambda i: (_mod_row(i, TM), 3, 0, 0)),
            pl.BlockSpec((None, None, 1, D_MODEL), lambda i: (_mod_row(i, TM), 4, 0, 0)),
            pl.BlockSpec((None, D_MODEL, 256), lambda i: (layer, 0, 0)),
        ],
        out_specs=[pl.BlockSpec((TM * LANE_CHUNKS, 128), lambda i: (i, 0)),
                   pl.BlockSpec((TM, 128), lambda i: (i, 0)),
                   pl.BlockSpec((TM, 128), lambda i: (i, 0)),
                   pl.BlockSpec((8, 128), lambda i: (0, 0))],
        scratch_shapes=[pltpu.VMEM((TM, D_MODEL), F32), pltpu.VMEM((1, 128), F32)],
        compiler_params=_cparams(("arbitrary",)),
        name="moe_router",
    )(x, g, mod, mod, w_router)


FFN_ISSUE_GROUPS = 8
ISSUE_UNROLL = 8


def _ffn_kernel(pos_ref, te_ref, meta_ref, h_hbm, wg_ref, wu_ref, wd_ref, o_ref,
                src, xbuf, wg_b, wu_b, wd_b, sem):
    t = pl.program_id(0)
    nt = meta_ref[0]

    def row_copy(tile, slot, r):
        tok = src[tile * TE + r]
        return pltpu.make_async_copy(
            h_hbm.at[pl.ds(pl.multiple_of(tok * LANE_CHUNKS, LANE_CHUNKS), LANE_CHUNKS), :],
            xbuf.at[slot, pl.ds(pl.multiple_of(r * LANE_CHUNKS, LANE_CHUNKS), LANE_CHUNKS), :],
            sem.at[slot])

    def gather(tile, slot):
        def body(r, _):
            row_copy(tile, slot, r).start()
            return 0
        lax.fori_loop(0, TE, body, 0, unroll=ISSUE_UNROLL)

    def wait_tile(slot):
        pltpu.make_async_copy(xbuf.at[slot], xbuf.at[slot], sem.at[slot]).wait()

    @pl.when(t == 0)
    def _():
        for e in range(N_EXPERTS):
            def clear(i, _):
                src[i] = 0
                return 0
            lax.fori_loop(meta_ref[1 + e], meta_ref[1 + N_EXPERTS + e], clear, 0)

        def fill(tok, _):
            src[pos_ref[2 * tok]] = tok
            src[pos_ref[2 * tok + 1]] = tok
            return 0
        lax.fori_loop(0, N_TOK, fill, 0, unroll=8)
        gather(0, 0)

    @pl.when(t >= nt)
    def _():
        o_ref[...] = jnp.zeros_like(o_ref)

    @pl.when(t < nt)
    def _():
        slot = t % 2
        wait_tile(slot)

        @pl.when(jnp.logical_or(t == 0, te_ref[t] != te_ref[jnp.maximum(t - 1, 0)]))
        def _():
            wg_b[...] = wg_ref[...].astype(BF16)
            wu_b[...] = wu_ref[...].astype(BF16)
            wd_b[...] = wd_ref[...].astype(BF16)

        nxt = jnp.minimum(t + 1, nt - 1)
        per = TE // FFN_ISSUE_GROUPS

        def issue(g):
            for r in range(g * per, (g + 1) * per):
                row_copy(nxt, 1 - slot, r).start()

        x = _load_token_major(xbuf.at[slot], TE).astype(BF16)
        half = D_EXPERT // 2
        hg, hu = [], []
        for n in range(2):
            issue(n)
            hg.append(jnp.dot(x, wg_b[:, n * half:(n + 1) * half], preferred_element_type=F32))
        for n in range(2):
            issue(2 + n)
            hu.append(jnp.dot(x, wu_b[:, n * half:(n + 1) * half], preferred_element_type=F32))
        hg, hu = jnp.concatenate(hg, axis=1), jnp.concatenate(hu, axis=1)
        act = (hg * jax.nn.sigmoid(hg) * hu).astype(BF16)
        quarter = D_MODEL // 4
        for n in range(4):
            issue(4 + n)
            y = jnp.dot(act, wd_b[:, n * quarter:(n + 1) * quarter], preferred_element_type=F32)
            for c in range(quarter // 128):
                o_ref[pl.ds(n * (quarter // 128) + c, TE, stride=LANE_CHUNKS), :] = y[:, c * 128:(c + 1) * 128]

        @pl.when(t == nt - 1)
        def _():
            wait_tile(1 - slot)


def _expert_ffn(pos_flat, tile_expert, meta, h, w_gate, w_up, w_down, layer):
    wsel = lambda t, pos, te, meta: (layer, te[t], 0, 0)
    return pl.pallas_call(
        _ffn_kernel,
        out_shape=jax.ShapeDtypeStruct((N_SORT * LANE_CHUNKS, 128), F32),
        grid_spec=pltpu.PrefetchScalarGridSpec(
            num_scalar_prefetch=3,
            grid=(N_ETILES,),
            in_specs=[
                pl.BlockSpec(memory_space=pl.ANY),
                pl.BlockSpec((None, None, D_MODEL, D_EXPERT), wsel),
                pl.BlockSpec((None, None, D_MODEL, D_EXPERT), wsel),
                pl.BlockSpec((None, None, D_EXPERT, D_MODEL), wsel),
            ],
            out_specs=pl.BlockSpec((TE * LANE_CHUNKS, 128), lambda t, pos, te, meta: (t, 0)),
            scratch_shapes=[
                pltpu.SMEM((N_SORT,), jnp.int32),
                pltpu.VMEM((2, TE * LANE_CHUNKS, 128), F32),
                pltpu.VMEM((D_MODEL, D_EXPERT), BF16),
                pltpu.VMEM((D_MODEL, D_EXPERT), BF16),
                pltpu.VMEM((D_EXPERT, D_MODEL), BF16),
                pltpu.SemaphoreType.DMA((2,)),
            ],
        ),
        compiler_params=_cparams(("arbitrary",), VMEM_LIMIT_LARGE),
        name="moe_expert_ffn",
    )(pos_flat, tile_expert, meta, h, w_gate, w_up, w_down)


TC = 256


def _combine_kernel(pos_ref, x_ref, gate_ref, gw_ref, g_ref, sh_ref, sc_ref, ys_hbm, *refs, final_norm):
    if final_norm:
        o_ref, buf_a, buf_b, sem = refs
    else:
        o_ref, h_ref, buf_a, buf_b, sem = refs
    i = pl.program_id(0)
    slot = i % 2

    def slab(ref, row):
        return ref.at[pl.ds(pl.multiple_of(row * LANE_CHUNKS, LANE_CHUNKS), LANE_CHUNKS), :]

    def gather(tile, slot):
        def body(r, _):
            tok = tile * TC + r
            pltpu.make_async_copy(slab(ys_hbm, pos_ref[2 * tok]), slab(buf_a.at[slot], r), sem.at[slot, 0]).start()
            pltpu.make_async_copy(slab(ys_hbm, pos_ref[2 * tok + 1]), slab(buf_b.at[slot], r), sem.at[slot, 1]).start()
            return 0
        lax.fori_loop(0, TC, body, 0, unroll=ISSUE_UNROLL)

    @pl.when(i == 0)
    def _():
        gather(0, 0)

    @pl.when(i + 1 < pl.num_programs(0))
    def _():
        gather(i + 1, 1 - slot)

    pltpu.make_async_copy(buf_a.at[slot], buf_a.at[slot], sem.at[slot, 0]).wait()
    pltpu.make_async_copy(buf_b.at[slot], buf_b.at[slot], sem.at[slot, 1]).wait()
    w0 = jnp.broadcast_to(gw_ref[:, 0:1], (TC, 128))
    w1 = jnp.broadcast_to(gw_ref[:, 1:2], (TC, 128))
    sq = jnp.zeros((TC, 128), F32)
    for c in range(LANE_CHUNKS):
        cols = slice(c * 128, (c + 1) * 128)
        rows = pl.ds(c, TC, stride=LANE_CHUNKS)
        y = x_ref[:, cols] + gate_ref[:, cols] * (w0 * buf_a[slot, rows, :] + w1 * buf_b[slot, rows, :])
        o_ref[:, cols] = y
        sq = sq + y * y
    inv = lax.rsqrt(jnp.sum(sq, axis=-1, keepdims=True) / D_MODEL + EPS)
    for c in range(LANE_CHUNKS):
        cols = slice(c * 128, (c + 1) * 128)
        normed = o_ref[:, cols] * inv * g_ref[:, cols]
        if final_norm:
            o_ref[:, cols] = normed
        else:
            h_ref[:, cols] = (normed * (1.0 + sc_ref[:, cols]) + sh_ref[:, cols]).astype(BF16)


def _combine(pos_flat, x, mod, gw, ys, g, mod_next):
    final_norm = mod_next is None
    rows = N_LAT if final_norm else N_TOK
    tile = pl.BlockSpec((TC, D_MODEL), lambda i, pos: (i, 0))
    mod_row = lambda which: pl.BlockSpec((None, None, 1, D_MODEL), lambda i, pos: (_mod_row(i, TC), which, 0, 0))
    next_tab = mod if final_norm else mod_next
    out_shape = [jax.ShapeDtypeStruct((rows, D_MODEL), F32)]
    if not final_norm:
        out_shape.append(jax.ShapeDtypeStruct((rows, D_MODEL), BF16))
    return pl.pallas_call(
        functools.partial(_combine_kernel, final_norm=final_norm),
        out_shape=out_shape,
        grid_spec=pltpu.PrefetchScalarGridSpec(
            num_scalar_prefetch=1,
            grid=(rows // TC,),
            in_specs=[
                tile,
                mod_row(5),
                pl.BlockSpec((TC, 128), lambda i, pos: (i, 0)),
                pl.BlockSpec((1, D_MODEL), lambda i, pos: (0, 0)),
                mod_row(0), mod_row(1),
                pl.BlockSpec(memory_space=pl.ANY),
            ],
            out_specs=[tile] * len(out_shape),
            scratch_shapes=[pltpu.VMEM((2, TC * LANE_CHUNKS, 128), F32), pltpu.VMEM((2, TC * LANE_CHUNKS, 128), F32),
                            pltpu.SemaphoreType.DMA((2, 2))],
        ),
        compiler_params=_cparams(("arbitrary",)),
        name="moe_combine",
    )(pos_flat, x, mod, gw, g, next_tab, next_tab, ys)


def _moe(x, g, mod, w_router, w_gate, w_up, w_down, layer, g_next, mod_next):
    h, info, gw, cnt = _router(x, g, mod, w_router, layer)
    counts = cnt[0, N_GROUPS:N_GROUPS + N_EXPERTS].astype(jnp.int32)
    padded = ((counts + TE - 1) // TE) * TE
    ends = jnp.cumsum(padded)
    starts = ends - padded
    experts = jnp.arange(N_EXPERTS, dtype=jnp.int32)
    start_of = jnp.sum(jnp.where(info[:, 0:2, None] == experts, starts, 0), axis=-1)
    pos_flat = (start_of + info[:, 2:4]).reshape(-1)
    tile_ends = ends // TE
    num_tiles = tile_ends[-1]
    tiles = jnp.minimum(jnp.arange(N_ETILES, dtype=jnp.int32), num_tiles - 1)
    tile_expert = jnp.sum((tile_ends[None, :] <= tiles[:, None]).astype(jnp.int32), axis=-1)
    meta = jnp.concatenate([num_tiles[None], starts + counts, ends]).astype(jnp.int32)
    ys = _expert_ffn(pos_flat, tile_expert, meta, h, w_gate, w_up, w_down, layer)
    return _combine(pos_flat, x, mod, gw, ys, g_next, mod_next)


def kernel(x, c, ctx, c_ctx, mod_w, mod_b, norm_mix, norm_ffn, ab_w_in, ab_q_norm, ab_k_norm, ab_conv_w, ab_w_out, cd_w_in, cd_q_norm, cd_kv_norm, cd_w_uq, cd_w_ukv, s5_lam_re, s5_lam_im, s5_log_dt, s5_b_re, s5_b_im, s5_c_re, s5_c_im, s5_d, s5_w_glu, cd_w_out, moe_w_group, moe_w_expert, moe_w_gate, moe_w_up, moe_w_down, final_norm):
    cc = jnp.concatenate([c, c_ctx[None, :], jnp.zeros((8 - BATCH - 1, D_MODEL), F32)], axis=0)
    mods = _modulation(cc, mod_w, mod_b).reshape(DEPTH, 8, N_MOD, 1, D_MODEL)
    xs, h = _embed(x.reshape(N_LAT, D_MODEL), ctx.reshape(N_CTX, D_MODEL), norm_mix[0][None, :], mods[0])
    cos_a, sin_a = _rope_tables(HEAD_DIM)
    cos_c, sin_c = _rope_tables(C_ROPE)

    ab_in_b, ab_out_b = ab_w_in.astype(BF16), ab_w_out.astype(BF16)
    cd_out_b, ukv_b, glu_b = cd_w_out.astype(BF16), cd_w_ukv.astype(BF16), s5_w_glu.astype(BF16)
    a, b_ = C_Q_RANK + C_KV_RANK, C_Q_RANK + C_KV_RANK + C_ROPE
    pad = jnp.zeros(cd_w_in.shape[:2] + (CD_IN_PAD - cd_w_in.shape[2],), F32)
    cd_in_b = jnp.concatenate([cd_w_in[..., :a], cd_w_in[..., b_:], cd_w_in[..., a:b_], pad], axis=-1).astype(BF16)
    w_uq = cd_w_uq.reshape(-1, C_Q_RANK, C_HEADS, C_NOPE + C_ROPE)
    uq_b = jnp.concatenate([w_uq[..., :C_NOPE].reshape(-1, C_Q_RANK, C_HEADS * C_NOPE),
                            w_uq[..., C_NOPE:].reshape(-1, C_Q_RANK, C_HEADS * C_ROPE)], axis=-1).astype(BF16)

    mats = jax.vmap(_s5_matrices)(s5_lam_re, s5_lam_im, s5_log_dt, s5_b_re, s5_b_im, s5_c_re, s5_c_im)
    w_router = jnp.concatenate(
        [moe_w_group, jnp.transpose(moe_w_expert, (0, 2, 1, 3)).reshape(DEPTH, D_MODEL, N_EXPERTS),
         jnp.zeros((DEPTH, D_MODEL, 128 - N_GROUPS - N_EXPERTS), F32)], axis=-1)
    w_router_hi = w_router.astype(BF16)
    w_router = jnp.concatenate([w_router_hi, (w_router - w_router_hi.astype(F32)).astype(BF16)], axis=-1)

    for i in range(DEPTH):
        j = i // 2
        mod = mods[i]
        if i % 2 == 0:
            p = _linear(h, ab_in_b, j, 2048)
            k, vext = _gqa_kv(p, ab_k_norm[j][None, :], cos_a, sin_a)
            o_lat, o_ctx = _gqa_attention(p, k, vext, ab_q_norm[j][None, :], cos_a, sin_a)
            side = _gated_conv(p, ab_conv_w[j])
            w_out = ab_out_b
        else:
            p = _linear(h, cd_in_b, j, CD_IN_PAD)
            q = _norm_linear(p, 0, C_Q_RANK, cd_q_norm[j][None, :], uq_b, j)
            kcat, vext = _mla_kv(p, cd_kv_norm[j][None, :], ukv_b, j, cos_c, sin_c)
            o_lat, o_ctx = _mla_attention(q, kcat, vext, cos_c, sin_c)
            y = _s5(p, mats, j, s5_d.astype(F32)[:, None, :])
            side = _s5_glu(y, glu_b, j)
            w_out = cd_out_b
        xs = _out_linear(o_lat, o_ctx, side, w_out, j, xs, mod, 2)
        if i < DEPTH - 1:
            xs, h = _moe(xs, norm_ffn[i][None, :], mod, w_router, moe_w_gate, moe_w_up, moe_w_down, i,
                         norm_mix[i + 1][None, :], mods[i + 1])
        else:
            out, = _moe(xs, norm_ffn[i][None, :], mod, w_router, moe_w_gate, moe_w_up, moe_w_down, i,
                        final_norm[None, :], None)
    return out.reshape(BATCH, SEQ, D_MODEL)
```

```python
import functools
import math

import numpy as np
import jax
import jax.numpy as jnp
from jax import lax
from jax.experimental import pallas as pl
from jax.experimental.pallas import tpu as pltpu

F32 = jnp.float32
BF16 = jnp.bfloat16

D_MODEL = 2048
BATCH = 4
SEQ = 2048
DEPTH = 4
GRID_W = 64
CTX_LEN = 256
ROPE_THETA = 10000.0
EPS = 1e-6
N_MOD = 6
HEAD_DIM = 128
A_Q_HEADS = 12
A_KV_HEADS = 4
A_GROUP = A_Q_HEADS // A_KV_HEADS
A_WIDTH = A_Q_HEADS * HEAD_DIM
A_KV_WIDTH = A_KV_HEADS * HEAD_DIM
B_WIDTH = 512
AB_IN = A_WIDTH + 2 * A_KV_WIDTH + 3 * B_WIDTH
C_HEADS = 12
C_NOPE = 128
C_ROPE = 64
C_V = 128
C_Q_RANK = 512
C_KV_RANK = 256
C_WIDTH = C_HEADS * C_V
D_WIDTH = 512
S5_GROUP = 16
S5_GROUPS = D_WIDTH // S5_GROUP
S5_STATE = 64
N_GROUPS = 4
EXPERTS_PER_GROUP = 4
N_EXPERTS = N_GROUPS * EXPERTS_PER_GROUP
D_EXPERT = 512

N_LAT = BATCH * SEQ
N_CTX = BATCH * CTX_LEN
N_TOK = N_LAT + N_CTX
CTX_ROW = BATCH
CD_IN_PAD = 1536

TM = 512
TM_LINEAR = 1024
TQ = 1024
TE = 256
ROPE_IDENTITY_ROWS = max(TM, TQ)
N_SORT = 2 * N_TOK + N_EXPERTS * TE
N_ETILES = N_SORT // TE
S5_CHUNK = 16
S5_CTX_CHUNKS = CTX_LEN // S5_CHUNK
S5_ROWS = N_TOK // S5_CHUNK
S5_GSUB = 128 // S5_GROUP
S5_QUADS = S5_GROUPS // S5_GSUB
VMEM_LIMIT = 48 * 1024 * 1024
VMEM_LIMIT_LARGE = 56 * 1024 * 1024
ROW_CHUNK = 256


def _cparams(sem, limit=VMEM_LIMIT):
    return pltpu.CompilerParams(dimension_semantics=sem, vmem_limit_bytes=limit)


def _mod_row(tile, tile_rows):
    r0 = tile * tile_rows
    return jnp.where(r0 >= N_LAT, CTX_ROW, r0 // SEQ)


def _rope_block(tile, tile_rows):
    r0 = tile * tile_rows
    return jnp.where(r0 >= N_LAT, SEQ // tile_rows, (r0 % SEQ) // tile_rows)


def _rms(x):
    return x * lax.rsqrt(jnp.mean(x * x, axis=-1, keepdims=True) + EPS)


def _mod_kernel(cc_ref, w_ref, b_ref, o_ref):
    cc = cc_ref[...]
    s = (cc * jax.nn.sigmoid(cc)).astype(BF16)
    o_ref[...] = jnp.dot(s, w_ref[...].astype(BF16), preferred_element_type=F32) + b_ref[...]


def _modulation(cc, mod_w, mod_b):
    tn = 1024
    nout = N_MOD * D_MODEL
    return pl.pallas_call(
        _mod_kernel,
        out_shape=jax.ShapeDtypeStruct((DEPTH, 8, nout), F32),
        grid=(DEPTH, nout // tn),
        in_specs=[
            pl.BlockSpec((8, D_MODEL), lambda l, j: (0, 0)),
            pl.BlockSpec((None, D_MODEL, tn), lambda l, j: (l, 0, j)),
            pl.BlockSpec((None, 1, tn), lambda l, j: (l, 0, j)),
        ],
        out_specs=pl.BlockSpec((None, 8, tn), lambda l, j: (l, 0, j)),
        compiler_params=_cparams(("arbitrary", "arbitrary")),
        name="modulation",
    )(cc, mod_w, mod_b.reshape(DEPTH, 1, nout))


def _modulated_norm(x, g, sh, sc):
    return _rms(x) * g * (1.0 + sc) + sh


def _embed_kernel(xl_ref, xc_ref, g_ref, sh_ref, sc_ref, xs_ref, h_ref):
    is_ctx = pl.program_id(0) >= N_LAT // TM
    for r in range(0, TM, ROW_CHUNK):
        rows = slice(r, r + ROW_CHUNK)
        x = jnp.where(is_ctx, xc_ref[rows, :], xl_ref[rows, :])
        xs_ref[rows, :] = x
        h_ref[rows, :] = _modulated_norm(x, g_ref[...], sh_ref[...], sc_ref[...]).astype(BF16)


def _embed(x_lat, x_ctx, g, mod):
    lat_tiles = N_LAT // TM
    return pl.pallas_call(
        _embed_kernel,
        out_shape=[jax.ShapeDtypeStruct((N_TOK, D_MODEL), F32), jax.ShapeDtypeStruct((N_TOK, D_MODEL), BF16)],
        grid=(N_TOK // TM,),
        in_specs=[
            pl.BlockSpec((TM, D_MODEL), lambda i: (jnp.minimum(i, lat_tiles - 1), 0)),
            pl.BlockSpec((TM, D_MODEL), lambda i: (jnp.maximum(i - lat_tiles, 0), 0)),
            pl.BlockSpec((1, D_MODEL), lambda i: (0, 0)),
            pl.BlockSpec((None, None, 1, D_MODEL), lambda i: (_mod_row(i, TM), 0, 0, 0)),
            pl.BlockSpec((None, None, 1, D_MODEL), lambda i: (_mod_row(i, TM), 1, 0, 0)),
        ],
        out_specs=[pl.BlockSpec((TM, D_MODEL), lambda i: (i, 0))] * 2,
        compiler_params=_cparams(("arbitrary",)),
        name="embed_norm",
    )(x_lat, x_ctx, g, mod, mod)


def _linear_kernel(h_ref, w_ref, o_ref):
    o_ref[...] = jnp.dot(h_ref[...], w_ref[...], preferred_element_type=F32)


def _linear(h, w, layer, tn):
    kdim, nout = w.shape[1], w.shape[2]
    return pl.pallas_call(
        _linear_kernel,
        out_shape=jax.ShapeDtypeStruct((N_TOK, nout), F32),
        grid=(N_TOK // TM_LINEAR, nout // tn),
        in_specs=[
            pl.BlockSpec((TM_LINEAR, kdim), lambda i, j: (i, 0)),
            pl.BlockSpec((None, kdim, tn), lambda i, j: (layer, 0, j)),
        ],
        out_specs=pl.BlockSpec((TM_LINEAR, tn), lambda i, j: (i, j)),
        compiler_params=_cparams(("arbitrary", "arbitrary")),
        name="linear",
    )(h, w)


def _norm_linear_kernel(x_ref, g_ref, w_ref, o_ref):
    h = (_rms(x_ref[...]) * g_ref[...]).astype(BF16)
    o_ref[...] = jnp.dot(h, w_ref[...], preferred_element_type=F32)


def _norm_linear(x, xcol, kdim, g, w, layer):
    nout = w.shape[2]
    return pl.pallas_call(
        _norm_linear_kernel,
        out_shape=jax.ShapeDtypeStruct((N_TOK, nout), F32),
        grid=(N_TOK // TM,),
        in_specs=[
            pl.BlockSpec((TM, kdim), lambda i: (i, xcol)),
            pl.BlockSpec((1, kdim), lambda i: (0, 0)),
            pl.BlockSpec((None, kdim, nout), lambda i: (layer, 0, 0)),
        ],
        out_specs=pl.BlockSpec((TM, nout), lambda i: (i, 0)),
        compiler_params=_cparams(("arbitrary",)),
        name="norm_linear",
    )(x, g, w)


def _out_linear_kernel(a1l_ref, a1c_ref, a2_ref, w1_ref, w2_ref, x_ref, gate_ref, o_ref):
    a1 = jnp.where(pl.program_id(0) >= N_LAT // TM, a1c_ref[...], a1l_ref[...])
    acc = jnp.dot(a1, w1_ref[...], preferred_element_type=F32)
    acc = acc + jnp.dot(a2_ref[...], w2_ref[...], preferred_element_type=F32)
    o_ref[...] = x_ref[...] + gate_ref[...] * acc


def _out_linear(a1_lat, a1_ctx, a2, w, layer, x, mod, which, rows):
    tn = D_MODEL
    k1, k2 = a1_lat.shape[1], a2.shape[1]
    lat_tiles = N_LAT // TM
    return pl.pallas_call(
        _out_linear_kernel,
        out_shape=jax.ShapeDtypeStruct((rows, D_MODEL), F32),
        grid=(rows // TM, D_MODEL // tn),
        in_specs=[
            pl.BlockSpec((TM, k1), lambda i, j: (jnp.minimum(i, lat_tiles - 1), 0)),
            pl.BlockSpec((TM, k1), lambda i, j: (jnp.maximum(i - lat_tiles, 0), 0)),
            pl.BlockSpec((TM, k2), lambda i, j: (i, 0)),
            pl.BlockSpec((None, k1, tn), lambda i, j: (layer, 0, j)),
            pl.BlockSpec((None, k2, tn), lambda i, j: (layer, k1 // k2, j)),
            pl.BlockSpec((TM, tn), lambda i, j: (i, j)),
            pl.BlockSpec((None, None, 1, tn), lambda i, j: (_mod_row(i, TM), which, 0, j)),
        ],
        out_specs=pl.BlockSpec((TM, tn), lambda i, j: (i, j)),
        compiler_params=_cparams(("arbitrary", "arbitrary")),
        name="out_linear",
    )(a1_lat, a1_ctx, a2, w, w, x, mod)


def _rope_tables(rot_dim):
    rows = SEQ // GRID_W
    row_ids = np.repeat(np.arange(rows, dtype=np.float32), GRID_W)
    col_ids = np.tile(np.arange(GRID_W, dtype=np.float32), rows)
    d_axis = rot_dim // 2
    inv = (np.float32(ROPE_THETA) ** (-np.arange(0, d_axis, 2, dtype=np.float32) / np.float32(d_axis))).astype(np.float32)
    ang = np.concatenate([row_ids[:, None] * inv, col_ids[:, None] * inv], axis=-1).astype(np.float32)
    cos, sin = np.cos(ang).astype(np.float32), np.sin(ang).astype(np.float32)
    reps = 128 // rot_dim
    cos_f = np.tile(np.concatenate([cos, cos], axis=-1), (1, reps))
    sin_f = np.tile(np.concatenate([-sin, sin], axis=-1), (1, reps))
    cos_f = np.concatenate([cos_f, np.ones((ROPE_IDENTITY_ROWS, 128), np.float32)], axis=0)
    sin_f = np.concatenate([sin_f, np.zeros((ROPE_IDENTITY_ROWS, 128), np.float32)], axis=0)
    return jnp.asarray(cos_f), jnp.asarray(sin_f)


def _rope128(x, cos, sin):
    return x * cos + pltpu.roll(x, 64, 1) * sin


def _rope64(x, cos, sin):
    lane = lax.broadcasted_iota(jnp.int32, x.shape, 1)
    swapped = jnp.where((lane % 64) < 32, pltpu.roll(x, 96, 1), pltpu.roll(x, 32, 1))
    return x * cos + swapped * sin


def _ones_column(rows):
    lane = lax.broadcasted_iota(jnp.int32, (rows, 128), 1)
    return jnp.where(lane == 0, 1.0, 0.0).astype(BF16)


def _gqa_kv_kernel(k_ref, v_ref, kn_ref, cos_ref, sin_ref, ko_ref, vo_ref):
    cos, sin = cos_ref[...], sin_ref[...]
    ones = _ones_column(k_ref.shape[0])
    for h in range(A_KV_HEADS):
        k = _rms(k_ref[:, h * 128:(h + 1) * 128]) * kn_ref[...]
        ko_ref[:, h * 128:(h + 1) * 128] = _rope128(k, cos, sin).astype(BF16)
        vo_ref[:, h * 256:h * 256 + 128] = v_ref[:, h * 128:(h + 1) * 128].astype(BF16)
        vo_ref[:, h * 256 + 128:(h + 1) * 256] = ones


def _gqa_kv(p, k_norm, cos, sin):
    t = 512
    return pl.pallas_call(
        _gqa_kv_kernel,
        out_shape=[jax.ShapeDtypeStruct((N_TOK, A_KV_WIDTH), BF16),
                   jax.ShapeDtypeStruct((N_TOK, 2 * A_KV_WIDTH), BF16)],
        grid=(N_TOK // t,),
        in_specs=[
            pl.BlockSpec((t, A_KV_WIDTH), lambda i: (i, A_WIDTH // A_KV_WIDTH)),
            pl.BlockSpec((t, A_KV_WIDTH), lambda i: (i, A_WIDTH // A_KV_WIDTH + 1)),
            pl.BlockSpec((1, 128), lambda i: (0, 0)),
            pl.BlockSpec((t, 128), lambda i: (_rope_block(i, t), 0)),
            pl.BlockSpec((t, 128), lambda i: (_rope_block(i, t), 0)),
        ],
        out_specs=[pl.BlockSpec((t, A_KV_WIDTH), lambda i: (i, 0)),
                   pl.BlockSpec((t, 2 * A_KV_WIDTH), lambda i: (i, 0))],
        compiler_params=_cparams(("arbitrary",)),
        name="gqa_kv",
    )(p, p, k_norm, cos, sin)


LOG2E = math.log2(math.e)
KEY_PIECE = 1024


def _softmax_pv(q, keys, vals):
    dn = (((1,), (1,)), ((), ()))
    m = acc = None
    for k, v in zip(keys, vals):
        s = lax.dot_general(q, k, dn, preferred_element_type=F32)
        m_piece = jnp.max(s, axis=-1, keepdims=True)
        if m is None:
            m_new = m_piece
        else:
            m_new = jnp.maximum(m, m_piece)
            acc = acc * jnp.exp2(m - m_new)
        part = jnp.dot(jnp.exp2(s - m_new).astype(BF16), v, preferred_element_type=F32)
        acc = part if acc is None else acc + part
        m = m_new
    return acc[:, :128] / acc[:, 128:129]


def _attn_qrow(b, t):
    nq = SEQ // TQ
    return jnp.where(t == nq, (N_LAT + b * CTX_LEN) // TQ, b * nq + t)


def _ctx_half():
    return pl.multiple_of((pl.program_id(0) % (TQ // CTX_LEN)) * CTX_LEN, CTX_LEN)


def _gqa_attn_kernel(q_ref, qn_ref, cos_ref, sin_ref, kc_ref, vc_ref, kl_ref, vl_ref, ol_ref, oc_ref):
    scale = HEAD_DIM ** -0.5 * LOG2E
    is_ctx = pl.program_id(2) == SEQ // TQ

    def query(rows, g):
        q = _rms(q_ref[rows, g * 128:(g + 1) * 128]) * qn_ref[...]
        return (_rope128(q, cos_ref[rows, :], sin_ref[rows, :]) * scale).astype(BF16)

    @pl.when(jnp.logical_not(is_ctx))
    def _():
        pieces = [slice(j * KEY_PIECE, (j + 1) * KEY_PIECE) for j in range(SEQ // KEY_PIECE)]
        qs = [query(slice(None), g) for g in range(A_GROUP)]
        for g in range(A_GROUP):
            o = _softmax_pv(qs[g], [kc_ref[...]] + [kl_ref[r, :] for r in pieces],
                            [vc_ref[...]] + [vl_ref[r, :] for r in pieces])
            ol_ref[:, g * 128:(g + 1) * 128] = o.astype(BF16)

    @pl.when(is_ctx)
    def _():
        rows = pl.ds(_ctx_half(), CTX_LEN)
        for g in range(A_GROUP):
            oc_ref[:, g * 128:(g + 1) * 128] = _softmax_pv(query(rows, g), [kc_ref[...]], [vc_ref[...]]).astype(BF16)


def _gqa_attention(p, k, vext, q_norm, cos, sin):
    qw = A_GROUP * HEAD_DIM
    nq = SEQ // TQ
    ctx_blk = N_LAT // CTX_LEN
    rope = pl.BlockSpec((TQ, 128), lambda b, h, t: (_rope_block(_attn_qrow(b, t), TQ), 0))
    return pl.pallas_call(
        _gqa_attn_kernel,
        out_shape=[jax.ShapeDtypeStruct((N_LAT, A_WIDTH), BF16), jax.ShapeDtypeStruct((N_CTX, A_WIDTH), BF16)],
        grid=(BATCH, A_KV_HEADS, nq + 1),
        in_specs=[
            pl.BlockSpec((TQ, qw), lambda b, h, t: (_attn_qrow(b, t), h)),
            pl.BlockSpec((1, 128), lambda b, h, t: (0, 0)),
            rope, rope,
            pl.BlockSpec((CTX_LEN, 128), lambda b, h, t: (ctx_blk + b, h)),
            pl.BlockSpec((CTX_LEN, 256), lambda b, h, t: (ctx_blk + b, h)),
            pl.BlockSpec((SEQ, 128), lambda b, h, t: (b, h)),
            pl.BlockSpec((SEQ, 256), lambda b, h, t: (b, h)),
        ],
        out_specs=[pl.BlockSpec((TQ, qw), lambda b, h, t: (b * nq + jnp.minimum(t, nq - 1), h)),
                   pl.BlockSpec((CTX_LEN, qw), lambda b, h, t: (b, h))],
        compiler_params=_cparams(("arbitrary",) * 3),
        name="gqa_attn",
    )(p, q_norm, cos, sin, k, vext, k, vext)


def _conv_kernel(bg_ref, cg_ref, ug_ref, cgp_ref, ugp_ref, cgn_ref, ugn_ref, w_ref, o_ref, *, rows):
    i = pl.program_id(0)
    per_seq = SEQ // rows
    is_ctx = i >= N_LAT // rows
    is_start = jnp.logical_or(is_ctx, i % per_seq == 0)
    is_end = jnp.logical_or(is_ctx, i % per_seq == per_seq - 1)
    m = cg_ref[...] * ug_ref[...]
    m_prev = jnp.where(is_start, 0.0, cgp_ref[7:8, :] * ugp_ref[7:8, :])
    m_next = jnp.where(is_end, 0.0, cgn_ref[0:1, :] * ugn_ref[0:1, :])
    row = lax.broadcasted_iota(jnp.int32, m.shape, 0)
    down = jnp.where(row == 0, m_prev, pltpu.roll(m, 1, 0))
    up = jnp.where(row == rows - 1, m_next, pltpu.roll(m, rows - 1, 0))
    conv = down * w_ref[0:1, :] + m * w_ref[1:2, :] + up * w_ref[2:3, :]
    o_ref[...] = (bg_ref[...] * conv).astype(BF16)


def _gated_conv(p, conv_w):
    rows = CTX_LEN
    base = (A_WIDTH + 2 * A_KV_WIDTH) // B_WIDTH
    halo = rows // 8
    last = N_TOK // 8 - 1
    main = lambda c: pl.BlockSpec((rows, B_WIDTH), lambda i: (i, base + c))
    prev = lambda c: pl.BlockSpec((8, B_WIDTH), lambda i: (jnp.maximum(i * halo - 1, 0), base + c))
    nxt = lambda c: pl.BlockSpec((8, B_WIDTH), lambda i: (jnp.minimum((i + 1) * halo, last), base + c))
    return pl.pallas_call(
        functools.partial(_conv_kernel, rows=rows),
        out_shape=jax.ShapeDtypeStruct((N_TOK, B_WIDTH), BF16),
        grid=(N_TOK // rows,),
        in_specs=[main(0), main(1), main(2), prev(1), prev(2), nxt(1), nxt(2),
                  pl.BlockSpec((3, B_WIDTH), lambda i: (0, 0))],
        out_specs=pl.BlockSpec((rows, B_WIDTH), lambda i: (i, 0)),
        compiler_params=_cparams(("arbitrary",)),
        name="gated_conv",
    )(p, p, p, p, p, p, p, conv_w)


def _mla_kv_kernel(ckv_ref, g_ref, w_ref, kr_ref, cos_ref, sin_ref, ko_ref, vo_ref):
    ckv = (_rms(ckv_ref[...]) * g_ref[...]).astype(BF16)
    kv = jnp.dot(ckv, w_ref[...], preferred_element_type=F32)
    x = kr_ref[...]
    lane = lax.broadcasted_iota(jnp.int32, x.shape, 1)
    r = jnp.where(lane < C_ROPE, _rope64(x, cos_ref[...], sin_ref[...]), 0.0)
    kr_low = r.astype(BF16)
    kr_high = pltpu.roll(r, 64, 1).astype(BF16)
    ones = _ones_column(x.shape[0])
    for h in range(C_HEADS):
        ko_ref[:, h * 256:h * 256 + 128] = kv[:, h * 256:h * 256 + 128].astype(BF16)
        ko_ref[:, h * 256 + 128:(h + 1) * 256] = kr_low if h % 2 == 0 else kr_high
        vo_ref[:, h * 256:h * 256 + 128] = kv[:, h * 256 + 128:(h + 1) * 256].astype(BF16)
        vo_ref[:, h * 256 + 128:(h + 1) * 256] = ones


def _mla_kv(p, kv_norm, w_ukv, layer, cos, sin):
    t = 512
    col = (C_Q_RANK + C_KV_RANK + D_WIDTH) // 128
    width = C_HEADS * 256
    return pl.pallas_call(
        _mla_kv_kernel,
        out_shape=[jax.ShapeDtypeStruct((N_TOK, width), BF16)] * 2,
        grid=(N_TOK // t,),
        in_specs=[
            pl.BlockSpec((t, C_KV_RANK), lambda i: (i, C_Q_RANK // C_KV_RANK)),
            pl.BlockSpec((1, C_KV_RANK), lambda i: (0, 0)),
            pl.BlockSpec((None, C_KV_RANK, width), lambda i: (layer, 0, 0)),
            pl.BlockSpec((t, 128), lambda i: (i, col)),
            pl.BlockSpec((t, 128), lambda i: (_rope_block(i, t), 0)),
            pl.BlockSpec((t, 128), lambda i: (_rope_block(i, t), 0)),
        ],
        out_specs=[pl.BlockSpec((t, width), lambda i: (i, 0))] * 2,
        compiler_params=_cparams(("arbitrary",)),
        name="mla_kv",
    )(p, kv_norm, w_ukv, p, cos, sin)


def _mla_attn_kernel(qn_ref, qr_ref, cos_ref, sin_ref, kc_ref, vc_ref, kl_ref, vl_ref, ol_ref, oc_ref):
    scale = (C_NOPE + C_ROPE) ** -0.5 * LOG2E
    is_ctx = pl.program_id(2) == SEQ // TQ

    def queries(rows):
        qr = _rope64(qr_ref[rows, :], cos_ref[rows, :], sin_ref[rows, :])
        lane = lax.broadcasted_iota(jnp.int32, qr.shape, 1)
        out = []
        for hh in range(2):
            sel = (lane < 64) if hh == 0 else (lane >= 64)
            q = jnp.concatenate([qn_ref[rows, hh * 128:(hh + 1) * 128], jnp.where(sel, qr, 0.0)], axis=1)
            out.append((q * scale).astype(BF16))
        return out

    @pl.when(jnp.logical_not(is_ctx))
    def _():
        for hh, q in enumerate(queries(slice(None))):
            blk = slice(hh * 256, (hh + 1) * 256)
            pieces = [slice(j * KEY_PIECE, (j + 1) * KEY_PIECE) for j in range(SEQ // KEY_PIECE)]
            o = _softmax_pv(q, [kc_ref[:, blk]] + [kl_ref[r, blk] for r in pieces],
                            [vc_ref[:, blk]] + [vl_ref[r, blk] for r in pieces])
            ol_ref[:, hh * 128:(hh + 1) * 128] = o.astype(BF16)

    @pl.when(is_ctx)
    def _():
        for hh, q in enumerate(queries(pl.ds(_ctx_half(), CTX_LEN))):
            blk = slice(hh * 256, (hh + 1) * 256)
            oc_ref[:, hh * 128:(hh + 1) * 128] = _softmax_pv(q, [kc_ref[:, blk]], [vc_ref[:, blk]]).astype(BF16)


def _mla_attention(q, kcat, vext, cos, sin):
    nq = SEQ // TQ
    ctx_blk = N_LAT // CTX_LEN
    rope_col = C_HEADS * C_NOPE // 128
    qrow = _attn_qrow
    rope = pl.BlockSpec((TQ, 128), lambda b, h, t: (_rope_block(qrow(b, t), TQ), 0))
    ctx = pl.BlockSpec((CTX_LEN, 512), lambda b, h, t: (ctx_blk + b, h))
    lat = pl.BlockSpec((SEQ, 512), lambda b, h, t: (b, h))
    return pl.pallas_call(
        _mla_attn_kernel,
        out_shape=[jax.ShapeDtypeStruct((N_LAT, C_WIDTH), BF16), jax.ShapeDtypeStruct((N_CTX, C_WIDTH), BF16)],
        grid=(BATCH, C_HEADS // 2, nq + 1),
        in_specs=[
            pl.BlockSpec((TQ, 256), lambda b, h, t: (qrow(b, t), h)),
            pl.BlockSpec((TQ, 128), lambda b, h, t: (qrow(b, t), rope_col + h)),
            rope, rope, ctx, ctx, lat, lat,
        ],
        out_specs=[pl.BlockSpec((TQ, 256), lambda b, h, t: (b * nq + jnp.minimum(t, nq - 1), h)),
                   pl.BlockSpec((CTX_LEN, 256), lambda b, h, t: (b, h))],
        compiler_params=_cparams(("arbitrary",) * 3),
        name="mla_attn",
    )(q, q, cos, sin, kcat, vext, kcat, vext)


def _s5_matrices(lam_re, lam_im, log_dt, b_re, b_im, c_re, c_im):
    hi = lax.Precision.HIGHEST
    lam_re, lam_im = lam_re.astype(F32), lam_im.astype(F32)
    dt = jnp.exp(log_dt.astype(F32))[..., None]
    ks = jnp.arange(S5_CHUNK + 1, dtype=F32)[:, None, None, None]
    mag = jnp.exp(lam_re[None] * dt[None] * ks)
    ang = lam_im[None] * dt[None] * ks
    pw_re, pw_im = mag * jnp.cos(ang), mag * jnp.sin(ang)
    a_re, a_im = pw_re[1], pw_im[1]
    den = lam_re * lam_re + lam_im * lam_im
    f_re = ((a_re - 1.0) * lam_re + a_im * lam_im) / den
    f_im = (a_im * lam_re - (a_re - 1.0) * lam_im) / den
    b_re, b_im = b_re.astype(F32), b_im.astype(F32)
    bb_re = f_re[..., None] * b_re - f_im[..., None] * b_im
    bb_im = f_re[..., None] * b_im + f_im[..., None] * b_re
    c_re, c_im = c_re.astype(F32), c_im.astype(F32)

    ab_re = pw_re[:S5_CHUNK, ..., None] * bb_re[None] - pw_im[:S5_CHUNK, ..., None] * bb_im[None]
    ab_im = pw_re[:S5_CHUNK, ..., None] * bb_im[None] + pw_im[:S5_CHUNK, ..., None] * bb_re[None]
    kern = (jnp.einsum('dgcn,ldgne->ldgce', c_re, ab_re, precision=hi)
            - jnp.einsum('dgcn,ldgne->ldgce', c_im, ab_im, precision=hi))
    t_idx = np.arange(S5_CHUNK)
    sub = lambda m, ax: m.reshape(m.shape[:ax] + (S5_QUADS, S5_GSUB) + m.shape[ax + 1:])
    klag = jnp.transpose(sub(kern, 2), (1, 2, 0, 5, 3, 4)).reshape(2, S5_QUADS, S5_CHUNK, S5_GROUP, 128)
    bt_re = jnp.transpose(bb_re, (0, 1, 3, 2))
    bt_im = jnp.transpose(bb_im, (0, 1, 3, 2))
    lane_gc = lambda c: jnp.transpose(sub(c, 0), (0, 3, 1, 2)).reshape(S5_QUADS, S5_STATE, 128)
    ps, qs = [], []
    for d in range(2):
        p_pow = (S5_CHUNK - 1 - t_idx) if d == 0 else t_idx
        ar, ai = pw_re[p_pow, d][:, :, None, :], pw_im[p_pow, d][:, :, None, :]
        pc = jnp.concatenate([ar * bt_re[d][None] - ai * bt_im[d][None],
                              ar * bt_im[d][None] + ai * bt_re[d][None]], axis=-1)
        pc = pc.reshape(S5_CHUNK, S5_QUADS, 128, 128)
        ps.append(jnp.transpose(pc, (1, 0, 2, 3)).reshape(S5_QUADS, S5_CHUNK * 128, 128))
        q_pow = (t_idx + 1) if d == 0 else (S5_CHUNK - t_idx)
        cl_re, cl_im = lane_gc(c_re[d])[:, None], lane_gc(c_im[d])[:, None]
        rep = lambda a: jnp.repeat(jnp.transpose(sub(a, 1), (1, 0, 3, 2)), S5_GROUP, axis=-1)
        al_re, al_im = rep(pw_re[q_pow, d]), rep(pw_im[q_pow, d])
        qs.append(jnp.concatenate([cl_re * al_re - cl_im * al_im,
                                   -(cl_re * al_im + cl_im * al_re)], axis=2))
    a16_re, a16_im = pw_re[S5_CHUNK], pw_im[S5_CHUNK]
    lanes = S5_GROUPS * 2 * S5_STATE
    return dict(
        klag=klag.astype(BF16),
        p=jnp.stack(ps).astype(BF16),
        q=jnp.stack(qs).astype(BF16),
        a_mul=jnp.concatenate([a16_re, a16_re], axis=-1).reshape(2, 1, lanes),
        a_swp=jnp.concatenate([-a16_im, a16_im], axis=-1).reshape(2, 1, lanes),
    )


def _s5_expand_p(pc):
    rep = jnp.concatenate([pc] * S5_GSUB, axis=1)
    row = lax.broadcasted_iota(jnp.int32, rep.shape, 0)
    col = lax.broadcasted_iota(jnp.int32, rep.shape, 1)
    return jnp.where((row // S5_GROUP) % S5_GSUB == col // 128, rep, jnp.zeros_like(rep))


def _s5_expand_q(q_ref):
    row = lax.broadcasted_iota(jnp.int32, (S5_GSUB * 128, 128), 0)
    col = lax.broadcasted_iota(jnp.int32, (S5_GSUB * 128, 128), 1)
    keep = row // 128 == col // S5_GROUP
    blocks = [jnp.where(keep, jnp.concatenate([q_ref[t]] * S5_GSUB, axis=0), jnp.zeros((S5_GSUB * 128, 128), BF16))
              for t in range(S5_CHUNK)]
    return jnp.concatenate(blocks, axis=1)


def _s5_chunk_rows(u_ref):
    n = N_TOK // S5_CHUNK
    return jnp.concatenate([u_ref[pl.ds(s, n, stride=S5_CHUNK), :] for s in range(S5_CHUNK)], axis=-1).astype(BF16)


def _s5_z_kernel(u_ref, p_ref, z_ref):
    z_ref[...] = jnp.dot(_s5_chunk_rows(u_ref), _s5_expand_p(p_ref[...]), preferred_element_type=F32)


def _s5_scan_kernel(z_ref, amul_ref, aswp_ref, s_ref, zs_scr):
    d = pl.program_id(0)
    a_mul, a_swp = amul_ref[...], aswp_ref[...]
    lanes = s_ref.shape[1]
    z = z_ref[...]
    low_half = lax.broadcasted_iota(jnp.int32, z.shape, 1) % 128 < S5_STATE
    zs_scr[...] = jnp.where(low_half, pltpu.roll(z, lanes - S5_STATE, 1), pltpu.roll(z, S5_STATE, 1))
    nl, nc = SEQ // S5_CHUNK, S5_CTX_CHUNKS

    def segment(base, count, carry):
        def body(step, carry):
            s, w = carry
            k = jnp.where(d == 0, step, count - 1 - step)
            rows = [pl.ds(base + b * count + k, 1) for b in range(BATCH)]
            for b in range(BATCH):
                s_ref[rows[b], :] = s[b:b + 1, :]
            z = jnp.concatenate([z_ref[r, :] for r in rows], axis=0)
            zs = jnp.concatenate([zs_scr[r, :] for r in rows], axis=0)
            return a_mul * s + a_swp * w + z, a_mul * w - a_swp * s + zs
        return lax.fori_loop(0, count, body, carry)

    zero = jnp.zeros((BATCH, lanes), F32)
    carry = segment(BATCH * nl, nc, (zero, zero))
    segment(0, nl, carry)


def _s5_y_kernel(u_ref, klag_ref, s_ref, q_ref, dv_ref, y_ref, m_scr):
    d = pl.program_id(1)
    n = N_TOK // S5_CHUNK
    row = lax.broadcasted_iota(jnp.int32, (128, 128), 0)
    col = lax.broadcasted_iota(jnp.int32, (128, 128), 1)
    same_group = row // S5_GROUP == col // S5_GROUP
    blocks = [jnp.where(same_group, jnp.concatenate([klag_ref[lag]] * S5_GSUB, axis=0), jnp.zeros((128, 128), BF16))
              for lag in range(S5_CHUNK)]
    for rev in range(2):
        @pl.when(d == rev)
        def _():
            for s in range(S5_CHUNK):
                for t in range(S5_CHUNK):
                    lag = (s - t) if rev else (t - s)
                    blk = blocks[lag] if lag >= 0 else jnp.zeros((128, 128), BF16)
                    m_scr[s * 128:(s + 1) * 128, t * 128:(t + 1) * 128] = blk
    y = jnp.dot(_s5_chunk_rows(u_ref), m_scr[...], preferred_element_type=F32)
    y = y + jnp.dot(s_ref[...].astype(BF16), _s5_expand_q(q_ref), preferred_element_type=F32)
    for t in range(S5_CHUNK):
        rows = pl.ds(t, n, stride=S5_CHUNK)
        yt = y[:, t * 128:(t + 1) * 128]

        @pl.when(d == 0)
        def _():
            y_ref[rows, :] = yt + u_ref[rows, :] * dv_ref[...]

        @pl.when(d == 1)
        def _():
            y_ref[rows, :] = y_ref[rows, :] + yt


def _s5(p, mats, layer, dskip):
    nstate = S5_GROUPS * 2 * S5_STATE
    qlanes = nstate // S5_QUADS
    slanes = qlanes
    ucol = (C_Q_RANK + C_KV_RANK) // 128
    z = pl.pallas_call(
        _s5_z_kernel,
        out_shape=jax.ShapeDtypeStruct((2, S5_ROWS, nstate), F32),
        grid=(2, S5_QUADS),
        in_specs=[
            pl.BlockSpec((N_TOK, 128), lambda d, q: (0, ucol + q)),
            pl.BlockSpec((None, None, None, S5_CHUNK * 128, 128), lambda d, q: (layer, d, q, 0, 0)),
        ],
        out_specs=pl.BlockSpec((None, S5_ROWS, qlanes), lambda d, q: (d, 0, q)),
        compiler_params=_cparams(("arbitrary", "arbitrary")),
        name="s5_chunk_state",
    )(p, mats["p"])
    s = pl.pallas_call(
        _s5_scan_kernel,
        out_shape=jax.ShapeDtypeStruct((2, S5_ROWS, nstate), F32),
        grid=(2, nstate // slanes),
        in_specs=[
            pl.BlockSpec((None, S5_ROWS, slanes), lambda d, q: (d, 0, q)),
            pl.BlockSpec((None, None, 1, slanes), lambda d, q: (layer, d, 0, q)),
            pl.BlockSpec((None, None, 1, slanes), lambda d, q: (layer, d, 0, q)),
        ],
        out_specs=pl.BlockSpec((None, S5_ROWS, slanes), lambda d, q: (d, 0, q)),
        scratch_shapes=[pltpu.VMEM((S5_ROWS, slanes), F32)],
        compiler_params=_cparams(("arbitrary", "arbitrary")),
        name="s5_scan",
    )(z, mats["a_mul"], mats["a_swp"])
    return pl.pallas_call(
        _s5_y_kernel,
        out_shape=jax.ShapeDtypeStruct((N_TOK, D_WIDTH), F32),
        grid=(S5_QUADS, 2),
        in_specs=[
            pl.BlockSpec((N_TOK, 128), lambda q, d: (0, ucol + q)),
            pl.BlockSpec((None, None, None, S5_CHUNK, S5_GROUP, 128), lambda q, d: (layer, d, q, 0, 0, 0)),
            pl.BlockSpec((None, S5_ROWS, qlanes), lambda q, d: (d, 0, q)),
            pl.BlockSpec((None, None, None, S5_CHUNK, 128, 128), lambda q, d: (layer, d, q, 0, 0, 0)),
            pl.BlockSpec((None, 1, 128), lambda q, d: (layer, 0, q)),
        ],
        out_specs=pl.BlockSpec((N_TOK, 128), lambda q, d: (0, q)),
        scratch_shapes=[pltpu.VMEM((S5_CHUNK * 128, S5_CHUNK * 128), BF16)],
        compiler_params=_cparams(("arbitrary", "arbitrary"), VMEM_LIMIT_LARGE),
        name="s5_output",
    )(p, mats["klag"], s, mats["q"], dskip)


def _glu_kernel(y_ref, w_ref, o_ref):
    y = y_ref[...]
    z = y * (0.5 * (1.0 + jnp.tanh(math.sqrt(2.0 / math.pi) * (y + 0.044715 * (y * y * y)))))
    gate = jnp.dot(z.astype(BF16), w_ref[...], preferred_element_type=F32)
    o_ref[...] = (z * jax.nn.sigmoid(gate)).astype(BF16)


def _s5_glu(y, w_glu, layer):
    t = 512
    return pl.pallas_call(
        _glu_kernel,
        out_shape=jax.ShapeDtypeStruct((N_TOK, D_WIDTH), BF16),
        grid=(N_TOK // t,),
        in_specs=[pl.BlockSpec((t, D_WIDTH), lambda i: (i, 0)),
                  pl.BlockSpec((None, D_WIDTH, D_WIDTH), lambda i: (layer, 0, 0))],
        out_specs=pl.BlockSpec((t, D_WIDTH), lambda i: (i, 0)),
        compiler_params=_cparams(("arbitrary",)),
        name="s5_glu",
    )(y, w_glu)


LANE_CHUNKS = D_MODEL // 128


def _store_token_major(ref, val):
    rows = val.shape[0]
    for c in range(LANE_CHUNKS):
        ref[pl.ds(c, rows, stride=LANE_CHUNKS), :] = val[:, c * 128:(c + 1) * 128]


def _load_token_major(ref, rows):
    return jnp.concatenate([ref[pl.ds(c, rows, stride=LANE_CHUNKS), :] for c in range(LANE_CHUNKS)], axis=-1)


def _router_kernel(x_ref, g_ref, sh_ref, sc_ref, w_ref, h_ref, info_ref, gw_ref, cnt_ref, h_scr, carry):
    i = pl.program_id(0)

    @pl.when(i == 0)
    def _():
        carry[...] = jnp.zeros_like(carry)

    for r in range(0, TM, ROW_CHUNK):
        rows = slice(r, r + ROW_CHUNK)
        h_scr[rows, :] = _modulated_norm(x_ref[rows, :], g_ref[...], sh_ref[...], sc_ref[...])
    _store_token_major(h_ref, h_scr[...])
    h = h_scr[...]
    h_hi = h.astype(BF16)
    h_lo = (h - h_hi.astype(F32)).astype(BF16)
    both = jnp.dot(h_hi, w_ref[...], preferred_element_type=F32)
    logits = both[:, :128] + both[:, 128:] + jnp.dot(h_lo, w_ref[:, :128], preferred_element_type=F32)
    lane = lax.broadcasted_iota(jnp.int32, logits.shape, 1)
    neg = -jnp.inf
    big = jnp.int32(1 << 20)

    def first_argmax(v, vmax):
        return jnp.min(jnp.where(v == vmax, lane, big), axis=-1, keepdims=True)

    lg = jnp.where(lane < N_GROUPS, logits, neg)
    mg = jnp.max(lg, axis=-1, keepdims=True)
    g_w = 1.0 / jnp.sum(jnp.exp(lg - mg), axis=-1, keepdims=True)
    g_idx = first_argmax(lg, mg)
    lo = N_GROUPS + EXPERTS_PER_GROUP * g_idx
    le = jnp.where(jnp.logical_and(lane >= lo, lane < lo + EXPERTS_PER_GROUP), logits, neg)
    m1 = jnp.max(le, axis=-1, keepdims=True)
    i1 = first_argmax(le, m1)
    le2 = jnp.where(lane == i1, neg, le)
    m2 = jnp.max(le2, axis=-1, keepdims=True)
    i2 = first_argmax(le2, m2)
    r21 = jnp.exp(m2 - m1)
    w1 = g_w / (1.0 + r21)
    w2 = g_w * r21 / (1.0 + r21)
    oh = jnp.logical_or(lane == i1, lane == i2)
    ohb = jnp.where(oh, 1.0, 0.0).astype(BF16)
    rr = lax.broadcasted_iota(jnp.int32, (TM, TM), 0)
    cc = lax.broadcasted_iota(jnp.int32, (TM, TM), 1)
    lower = jnp.where(rr > cc, 1.0, 0.0).astype(BF16)
    before = jnp.dot(lower, ohb, preferred_element_type=F32) + carry[...]
    rank1 = jnp.sum(jnp.where(lane == i1, before, 0.0), axis=-1, keepdims=True).astype(jnp.int32)
    rank2 = jnp.sum(jnp.where(lane == i2, before, 0.0), axis=-1, keepdims=True).astype(jnp.int32)
    carry[...] = carry[...] + jnp.sum(ohb.astype(F32), axis=0, keepdims=True)
    cnt_ref[...] = jnp.broadcast_to(carry[...], cnt_ref.shape)
    info = jnp.where(lane == 0, i1 - N_GROUPS, jnp.where(lane == 1, i2 - N_GROUPS,
                     jnp.where(lane == 2, rank1, jnp.where(lane == 3, rank2, 0))))
    info_ref[...] = info
    gw_ref[...] = jnp.where(lane == 0, w1, jnp.where(lane == 1, w2, 0.0))


def _router(x, g, mod, w_router, layer):
    rows = x.shape[0]
    return pl.pallas_call(
        _router_kernel,
        out_shape=[jax.ShapeDtypeStruct((rows * LANE_CHUNKS, 128), F32),
                   jax.ShapeDtypeStruct((rows, 128), jnp.int32),
                   jax.ShapeDtypeStruct((rows, 128), F32),
                   jax.ShapeDtypeStruct((8, 128), F32)],
        grid=(rows // TM,),
        in_specs=[
            pl.BlockSpec((TM, D_MODEL), lambda i: (i, 0)),
            pl.BlockSpec((1, D_MODEL), lambda i: (0, 0)),
            pl.BlockSpec((None, None, 1, D_MODEL), lambda i: (_mod_row(i, TM), 3, 0, 0)),
            pl.BlockSpec((None, None, 1, D_MODEL), lambda i: (_mod_row(i, TM), 4, 0, 0)),
            pl.BlockSpec((None, D_MODEL, 256), lambda i: (layer, 0, 0)),
        ],
        out_specs=[pl.BlockSpec((TM * LANE_CHUNKS, 128), lambda i: (i, 0)),
                   pl.BlockSpec((TM, 128), lambda i: (i, 0)),
                   pl.BlockSpec((TM, 128), lambda i: (i, 0)),
                   pl.BlockSpec((8, 128), lambda i: (0, 0))],
        scratch_shapes=[pltpu.VMEM((TM, D_MODEL), F32), pltpu.VMEM((1, 128), F32)],
        compiler_params=_cparams(("arbitrary",)),
        name="moe_router",
    )(x, g, mod, mod, w_router)


FFN_ISSUE_GROUPS = 8
ISSUE_UNROLL = 8


def _ffn_kernel(pos_ref, te_ref, meta_ref, h_hbm, wg_ref, wu_ref, wd_ref, o_ref,
                src, xbuf, wg_b, wu_b, wd_b, sem):
    t = pl.program_id(0)
    nt = meta_ref[0]

    def row_copy(tile, slot, r):
        tok = src[tile * TE + r]
        return pltpu.make_async_copy(
            h_hbm.at[pl.ds(pl.multiple_of(tok * LANE_CHUNKS, LANE_CHUNKS), LANE_CHUNKS), :],
            xbuf.at[slot, pl.ds(pl.multiple_of(r * LANE_CHUNKS, LANE_CHUNKS), LANE_CHUNKS), :],
            sem.at[slot])

    def gather(tile, slot):
        def body(r, _):
            row_copy(tile, slot, r).start()
            return 0
        lax.fori_loop(0, TE, body, 0, unroll=ISSUE_UNROLL)

    def wait_tile(slot):
        pltpu.make_async_copy(xbuf.at[slot], xbuf.at[slot], sem.at[slot]).wait()

    @pl.when(t == 0)
    def _():
        for e in range(N_EXPERTS):
            def clear(i, _):
                src[i] = 0
                return 0
            lax.fori_loop(meta_ref[1 + e], meta_ref[1 + N_EXPERTS + e], clear, 0)

        def fill(tok, _):
            src[pos_ref[2 * tok]] = tok
            src[pos_ref[2 * tok + 1]] = tok
            return 0
        lax.fori_loop(0, pos_ref.shape[0] // 2, fill, 0, unroll=8)
        gather(0, 0)

    @pl.when(t >= nt)
    def _():
        o_ref[...] = jnp.zeros_like(o_ref)

    @pl.when(t < nt)
    def _():
        slot = t % 2
        wait_tile(slot)

        @pl.when(jnp.logical_or(t == 0, te_ref[t] != te_ref[jnp.maximum(t - 1, 0)]))
        def _():
            wg_b[...] = wg_ref[...].astype(BF16)
            wu_b[...] = wu_ref[...].astype(BF16)
            wd_b[...] = wd_ref[...].astype(BF16)

        nxt = jnp.minimum(t + 1, nt - 1)
        per = TE // FFN_ISSUE_GROUPS

        def issue(g):
            for r in range(g * per, (g + 1) * per):
                row_copy(nxt, 1 - slot, r).start()

        x = _load_token_major(xbuf.at[slot], TE).astype(BF16)
        half = D_EXPERT // 2
        hg, hu = [], []
        for n in range(2):
            issue(n)
            hg.append(jnp.dot(x, wg_b[:, n * half:(n + 1) * half], preferred_element_type=F32))
        for n in range(2):
            issue(2 + n)
            hu.append(jnp.dot(x, wu_b[:, n * half:(n + 1) * half], preferred_element_type=F32))
        hg, hu = jnp.concatenate(hg, axis=1), jnp.concatenate(hu, axis=1)
        act = (hg * jax.nn.sigmoid(hg) * hu).astype(BF16)
        quarter = D_MODEL // 4
        for n in range(4):
            issue(4 + n)
            y = jnp.dot(act, wd_b[:, n * quarter:(n + 1) * quarter], preferred_element_type=F32)
            for c in range(quarter // 128):
                o_ref[pl.ds(n * (quarter // 128) + c, TE, stride=LANE_CHUNKS), :] = y[:, c * 128:(c + 1) * 128]

        @pl.when(t == nt - 1)
        def _():
            wait_tile(1 - slot)


def _expert_ffn(pos_flat, tile_expert, meta, h, w_gate, w_up, w_down, layer):
    wsel = lambda t, pos, te, meta: (layer, te[t], 0, 0)
    return pl.pallas_call(
        _ffn_kernel,
        out_shape=jax.ShapeDtypeStruct((N_SORT * LANE_CHUNKS, 128), F32),
        grid_spec=pltpu.PrefetchScalarGridSpec(
            num_scalar_prefetch=3,
            grid=(N_ETILES,),
            in_specs=[
                pl.BlockSpec(memory_space=pl.ANY),
                pl.BlockSpec((None, None, D_MODEL, D_EXPERT), wsel),
                pl.BlockSpec((None, None, D_MODEL, D_EXPERT), wsel),
                pl.BlockSpec((None, None, D_EXPERT, D_MODEL), wsel),
            ],
            out_specs=pl.BlockSpec((TE * LANE_CHUNKS, 128), lambda t, pos, te, meta: (t, 0)),
            scratch_shapes=[
                pltpu.SMEM((N_SORT,), jnp.int32),
                pltpu.VMEM((2, TE * LANE_CHUNKS, 128), F32),
                pltpu.VMEM((D_MODEL, D_EXPERT), BF16),
                pltpu.VMEM((D_MODEL, D_EXPERT), BF16),
                pltpu.VMEM((D_EXPERT, D_MODEL), BF16),
                pltpu.SemaphoreType.DMA((2,)),
            ],
        ),
        compiler_params=_cparams(("arbitrary",), VMEM_LIMIT_LARGE),
        name="moe_expert_ffn",
    )(pos_flat, tile_expert, meta, h, w_gate, w_up, w_down)


TC = 256


def _combine_kernel(pos_ref, x_ref, gate_ref, gw_ref, g_ref, sh_ref, sc_ref, ys_hbm, *refs, final_norm):
    if final_norm:
        o_ref, buf_a, buf_b, sem = refs
    else:
        o_ref, h_ref, buf_a, buf_b, sem = refs
    i = pl.program_id(0)
    slot = i % 2

    def slab(ref, row):
        return ref.at[pl.ds(pl.multiple_of(row * LANE_CHUNKS, LANE_CHUNKS), LANE_CHUNKS), :]

    def gather(tile, slot):
        def body(r, _):
            tok = tile * TC + r
            pltpu.make_async_copy(slab(ys_hbm, pos_ref[2 * tok]), slab(buf_a.at[slot], r), sem.at[slot, 0]).start()
            pltpu.make_async_copy(slab(ys_hbm, pos_ref[2 * tok + 1]), slab(buf_b.at[slot], r), sem.at[slot, 1]).start()
            return 0
        lax.fori_loop(0, TC, body, 0, unroll=ISSUE_UNROLL)

    @pl.when(i == 0)
    def _():
        gather(0, 0)

    @pl.when(i + 1 < pl.num_programs(0))
    def _():
        gather(i + 1, 1 - slot)

    pltpu.make_async_copy(buf_a.at[slot], buf_a.at[slot], sem.at[slot, 0]).wait()
    pltpu.make_async_copy(buf_b.at[slot], buf_b.at[slot], sem.at[slot, 1]).wait()
    w0 = jnp.broadcast_to(gw_ref[:, 0:1], (TC, 128))
    w1 = jnp.broadcast_to(gw_ref[:, 1:2], (TC, 128))
    sq = jnp.zeros((TC, 128), F32)
    for c in range(LANE_CHUNKS):
        cols = slice(c * 128, (c + 1) * 128)
        rows = pl.ds(c, TC, stride=LANE_CHUNKS)
        y = x_ref[:, cols] + gate_ref[:, cols] * (w0 * buf_a[slot, rows, :] + w1 * buf_b[slot, rows, :])
        o_ref[:, cols] = y
        sq = sq + y * y
    inv = lax.rsqrt(jnp.sum(sq, axis=-1, keepdims=True) / D_MODEL + EPS)
    for c in range(LANE_CHUNKS):
        cols = slice(c * 128, (c + 1) * 128)
        normed = o_ref[:, cols] * inv * g_ref[:, cols]
        if final_norm:
            o_ref[:, cols] = normed
        else:
            h_ref[:, cols] = (normed * (1.0 + sc_ref[:, cols]) + sh_ref[:, cols]).astype(BF16)


def _combine(pos_flat, x, mod, gw, ys, g, mod_next):
    final_norm = mod_next is None
    rows = x.shape[0]
    tile = pl.BlockSpec((TC, D_MODEL), lambda i, pos: (i, 0))
    mod_row = lambda which: pl.BlockSpec((None, None, 1, D_MODEL), lambda i, pos: (_mod_row(i, TC), which, 0, 0))
    next_tab = mod if final_norm else mod_next
    out_shape = [jax.ShapeDtypeStruct((rows, D_MODEL), F32)]
    if not final_norm:
        out_shape.append(jax.ShapeDtypeStruct((rows, D_MODEL), BF16))
    return pl.pallas_call(
        functools.partial(_combine_kernel, final_norm=final_norm),
        out_shape=out_shape,
        grid_spec=pltpu.PrefetchScalarGridSpec(
            num_scalar_prefetch=1,
            grid=(rows // TC,),
            in_specs=[
                tile,
                mod_row(5),
                pl.BlockSpec((TC, 128), lambda i, pos: (i, 0)),
                pl.BlockSpec((1, D_MODEL), lambda i, pos: (0, 0)),
                mod_row(0), mod_row(1),
                pl.BlockSpec(memory_space=pl.ANY),
            ],
            out_specs=[tile] * len(out_shape),
            scratch_shapes=[pltpu.VMEM((2, TC * LANE_CHUNKS, 128), F32), pltpu.VMEM((2, TC * LANE_CHUNKS, 128), F32),
                            pltpu.SemaphoreType.DMA((2, 2))],
        ),
        compiler_params=_cparams(("arbitrary",)),
        name="moe_combine",
    )(pos_flat, x, mod, gw, g, next_tab, next_tab, ys)


def _moe(x, g, mod, w_router, w_gate, w_up, w_down, layer, g_next, mod_next):
    h, info, gw, cnt = _router(x, g, mod, w_router, layer)
    counts = cnt[0, N_GROUPS:N_GROUPS + N_EXPERTS].astype(jnp.int32)
    padded = ((counts + TE - 1) // TE) * TE
    ends = jnp.cumsum(padded)
    starts = ends - padded
    experts = jnp.arange(N_EXPERTS, dtype=jnp.int32)
    start_of = jnp.sum(jnp.where(info[:, 0:2, None] == experts, starts, 0), axis=-1)
    pos_flat = (start_of + info[:, 2:4]).reshape(-1)
    tile_ends = ends // TE
    num_tiles = tile_ends[-1]
    tiles = jnp.minimum(jnp.arange(N_ETILES, dtype=jnp.int32), num_tiles - 1)
    tile_expert = jnp.sum((tile_ends[None, :] <= tiles[:, None]).astype(jnp.int32), axis=-1)
    meta = jnp.concatenate([num_tiles[None], starts + counts, ends]).astype(jnp.int32)
    ys = _expert_ffn(pos_flat, tile_expert, meta, h, w_gate, w_up, w_down, layer)
    return _combine(pos_flat, x, mod, gw, ys, g_next, mod_next)


def kernel(x, c, ctx, c_ctx, mod_w, mod_b, norm_mix, norm_ffn, ab_w_in, ab_q_norm, ab_k_norm, ab_conv_w, ab_w_out, cd_w_in, cd_q_norm, cd_kv_norm, cd_w_uq, cd_w_ukv, s5_lam_re, s5_lam_im, s5_log_dt, s5_b_re, s5_b_im, s5_c_re, s5_c_im, s5_d, s5_w_glu, cd_w_out, moe_w_group, moe_w_expert, moe_w_gate, moe_w_up, moe_w_down, final_norm):
    cc = jnp.concatenate([c, c_ctx[None, :], jnp.zeros((8 - BATCH - 1, D_MODEL), F32)], axis=0)
    mods = _modulation(cc, mod_w, mod_b).reshape(DEPTH, 8, N_MOD, 1, D_MODEL)
    xs, h = _embed(x.reshape(N_LAT, D_MODEL), ctx.reshape(N_CTX, D_MODEL), norm_mix[0][None, :], mods[0])
    cos_a, sin_a = _rope_tables(HEAD_DIM)
    cos_c, sin_c = _rope_tables(C_ROPE)

    ab_in_b, ab_out_b = ab_w_in.astype(BF16), ab_w_out.astype(BF16)
    cd_out_b, ukv_b, glu_b = cd_w_out.astype(BF16), cd_w_ukv.astype(BF16), s5_w_glu.astype(BF16)
    a, b_ = C_Q_RANK + C_KV_RANK, C_Q_RANK + C_KV_RANK + C_ROPE
    pad = jnp.zeros(cd_w_in.shape[:2] + (CD_IN_PAD - cd_w_in.shape[2],), F32)
    cd_in_b = jnp.concatenate([cd_w_in[..., :a], cd_w_in[..., b_:], cd_w_in[..., a:b_], pad], axis=-1).astype(BF16)
    w_uq = cd_w_uq.reshape(-1, C_Q_RANK, C_HEADS, C_NOPE + C_ROPE)
    uq_b = jnp.concatenate([w_uq[..., :C_NOPE].reshape(-1, C_Q_RANK, C_HEADS * C_NOPE),
                            w_uq[..., C_NOPE:].reshape(-1, C_Q_RANK, C_HEADS * C_ROPE)], axis=-1).astype(BF16)

    mats = jax.vmap(_s5_matrices)(s5_lam_re, s5_lam_im, s5_log_dt, s5_b_re, s5_b_im, s5_c_re, s5_c_im)
    w_router = jnp.concatenate(
        [moe_w_group, jnp.transpose(moe_w_expert, (0, 2, 1, 3)).reshape(DEPTH, D_MODEL, N_EXPERTS),
         jnp.zeros((DEPTH, D_MODEL, 128 - N_GROUPS - N_EXPERTS), F32)], axis=-1)
    w_router_hi = w_router.astype(BF16)
    w_router = jnp.concatenate([w_router_hi, (w_router - w_router_hi.astype(F32)).astype(BF16)], axis=-1)

    for i in range(DEPTH):
        j = i // 2
        mod = mods[i]
        if i % 2 == 0:
            p = _linear(h, ab_in_b, j, 2048)
            k, vext = _gqa_kv(p, ab_k_norm[j][None, :], cos_a, sin_a)
            o_lat, o_ctx = _gqa_attention(p, k, vext, ab_q_norm[j][None, :], cos_a, sin_a)
            side = _gated_conv(p, ab_conv_w[j])
            w_out = ab_out_b
        else:
            p = _linear(h, cd_in_b, j, CD_IN_PAD)
            q = _norm_linear(p, 0, C_Q_RANK, cd_q_norm[j][None, :], uq_b, j)
            kcat, vext = _mla_kv(p, cd_kv_norm[j][None, :], ukv_b, j, cos_c, sin_c)
            o_lat, o_ctx = _mla_attention(q, kcat, vext, cos_c, sin_c)
            y = _s5(p, mats, j, s5_d.astype(F32)[:, None, :])
            side = _s5_glu(y, glu_b, j)
            w_out = cd_out_b
        xs = _out_linear(o_lat, o_ctx, side, w_out, j, xs, mod, 2, N_TOK if i < DEPTH - 1 else N_LAT)
        if i < DEPTH - 1:
            xs, h = _moe(xs, norm_ffn[i][None, :], mod, w_router, moe_w_gate, moe_w_up, moe_w_down, i,
                         norm_mix[i + 1][None, :], mods[i + 1])
        else:
            out, = _moe(xs, norm_ffn[i][None, :], mod, w_router, moe_w_gate, moe_w_up, moe_w_down, i,
                        final_norm[None, :], None)
    return out.reshape(BATCH, SEQ, D_MODEL)
```

```python
import functools
import math

import numpy as np
import jax
import jax.numpy as jnp
from jax import lax
from jax.experimental import pallas as pl
from jax.experimental.pallas import tpu as pltpu

F32 = jnp.float32
BF16 = jnp.bfloat16

D_MODEL = 2048
BATCH = 4
SEQ = 2048
DEPTH = 4
GRID_W = 64
CTX_LEN = 256
ROPE_THETA = 10000.0
EPS = 1e-6
N_MOD = 6
HEAD_DIM = 128
A_Q_HEADS = 12
A_KV_HEADS = 4
A_GROUP = A_Q_HEADS // A_KV_HEADS
A_WIDTH = A_Q_HEADS * HEAD_DIM
A_KV_WIDTH = A_KV_HEADS * HEAD_DIM
B_WIDTH = 512
AB_IN = A_WIDTH + 2 * A_KV_WIDTH + 3 * B_WIDTH
C_HEADS = 12
C_NOPE = 128
C_ROPE = 64
C_V = 128
C_Q_RANK = 512
C_KV_RANK = 256
C_WIDTH = C_HEADS * C_V
D_WIDTH = 512
S5_GROUP = 16
S5_GROUPS = D_WIDTH // S5_GROUP
S5_STATE = 64
N_GROUPS = 4
EXPERTS_PER_GROUP = 4
N_EXPERTS = N_GROUPS * EXPERTS_PER_GROUP
D_EXPERT = 512

N_LAT = BATCH * SEQ
N_CTX = BATCH * CTX_LEN
N_TOK = N_LAT + N_CTX
CTX_ROW = BATCH
CD_IN_PAD = 1536

TM = 512
TM_LINEAR = 1024
TQ = 1024
TE = 256
ROPE_IDENTITY_ROWS = max(TM, TQ)
N_SORT = 2 * N_TOK + N_EXPERTS * TE
N_ETILES = N_SORT // TE
S5_CHUNK = 16
S5_CTX_CHUNKS = CTX_LEN // S5_CHUNK
S5_ROWS = N_TOK // S5_CHUNK
S5_GSUB = 128 // S5_GROUP
S5_QUADS = S5_GROUPS // S5_GSUB
VMEM_LIMIT = 48 * 1024 * 1024
VMEM_LIMIT_LARGE = 56 * 1024 * 1024
ROW_CHUNK = 256


def _cparams(sem, limit=VMEM_LIMIT):
    return pltpu.CompilerParams(dimension_semantics=sem, vmem_limit_bytes=limit)


def _mod_row(tile, tile_rows):
    r0 = tile * tile_rows
    return jnp.where(r0 >= N_LAT, CTX_ROW, r0 // SEQ)


def _rope_block(tile, tile_rows):
    r0 = tile * tile_rows
    return jnp.where(r0 >= N_LAT, SEQ // tile_rows, (r0 % SEQ) // tile_rows)


def _rms(x):
    return x * lax.rsqrt(jnp.mean(x * x, axis=-1, keepdims=True) + EPS)


def _mod_kernel(cc_ref, w_ref, b_ref, o_ref):
    cc = cc_ref[...]
    s = (cc * jax.nn.sigmoid(cc)).astype(BF16)
    o_ref[...] = jnp.dot(s, w_ref[...].astype(BF16), preferred_element_type=F32) + b_ref[...]


def _modulation(cc, mod_w, mod_b):
    tn = 1024
    nout = N_MOD * D_MODEL
    return pl.pallas_call(
        _mod_kernel,
        out_shape=jax.ShapeDtypeStruct((DEPTH, 8, nout), F32),
        grid=(DEPTH, nout // tn),
        in_specs=[
            pl.BlockSpec((8, D_MODEL), lambda l, j: (0, 0)),
            pl.BlockSpec((None, D_MODEL, tn), lambda l, j: (l, 0, j)),
            pl.BlockSpec((None, 1, tn), lambda l, j: (l, 0, j)),
        ],
        out_specs=pl.BlockSpec((None, 8, tn), lambda l, j: (l, 0, j)),
        compiler_params=_cparams(("arbitrary", "arbitrary")),
        name="modulation",
    )(cc, mod_w, mod_b.reshape(DEPTH, 1, nout))


def _modulated_norm(x, g, sh, sc):
    return _rms(x) * g * (1.0 + sc) + sh


def _embed_kernel(xl_ref, xc_ref, g_ref, sh_ref, sc_ref, xs_ref, h_ref):
    is_ctx = pl.program_id(0) >= N_LAT // TM
    for r in range(0, TM, ROW_CHUNK):
        rows = slice(r, r + ROW_CHUNK)
        x = jnp.where(is_ctx, xc_ref[rows, :], xl_ref[rows, :])
        xs_ref[rows, :] = x
        h_ref[rows, :] = _modulated_norm(x, g_ref[...], sh_ref[...], sc_ref[...]).astype(BF16)


def _embed(x_lat, x_ctx, g, mod):
    lat_tiles = N_LAT // TM
    return pl.pallas_call(
        _embed_kernel,
        out_shape=[jax.ShapeDtypeStruct((N_TOK, D_MODEL), F32), jax.ShapeDtypeStruct((N_TOK, D_MODEL), BF16)],
        grid=(N_TOK // TM,),
        in_specs=[
            pl.BlockSpec((TM, D_MODEL), lambda i: (jnp.minimum(i, lat_tiles - 1), 0)),
            pl.BlockSpec((TM, D_MODEL), lambda i: (jnp.maximum(i - lat_tiles, 0), 0)),
            pl.BlockSpec((1, D_MODEL), lambda i: (0, 0)),
            pl.BlockSpec((None, None, 1, D_MODEL), lambda i: (_mod_row(i, TM), 0, 0, 0)),
            pl.BlockSpec((None, None, 1, D_MODEL), lambda i: (_mod_row(i, TM), 1, 0, 0)),
        ],
        out_specs=[pl.BlockSpec((TM, D_MODEL), lambda i: (i, 0))] * 2,
        compiler_params=_cparams(("arbitrary",)),
        name="embed_norm",
    )(x_lat, x_ctx, g, mod, mod)


def _linear_kernel(h_ref, w_ref, o_ref):
    o_ref[...] = jnp.dot(h_ref[...], w_ref[...], preferred_element_type=F32)


def _linear(h, w, layer, tn):
    kdim, nout = w.shape[1], w.shape[2]
    return pl.pallas_call(
        _linear_kernel,
        out_shape=jax.ShapeDtypeStruct((N_TOK, nout), F32),
        grid=(N_TOK // TM_LINEAR, nout // tn),
        in_specs=[
            pl.BlockSpec((TM_LINEAR, kdim), lambda i, j: (i, 0)),
            pl.BlockSpec((None, kdim, tn), lambda i, j: (layer, 0, j)),
        ],
        out_specs=pl.BlockSpec((TM_LINEAR, tn), lambda i, j: (i, j)),
        compiler_params=_cparams(("arbitrary", "arbitrary")),
        name="linear",
    )(h, w)


def _norm_linear_kernel(x_ref, g_ref, w_ref, o_ref):
    h = (_rms(x_ref[...]) * g_ref[...]).astype(BF16)
    o_ref[...] = jnp.dot(h, w_ref[...], preferred_element_type=F32)


def _norm_linear(x, xcol, kdim, g, w, layer):
    nout = w.shape[2]
    return pl.pallas_call(
        _norm_linear_kernel,
        out_shape=jax.ShapeDtypeStruct((N_TOK, nout), F32),
        grid=(N_TOK // TM,),
        in_specs=[
            pl.BlockSpec((TM, kdim), lambda i: (i, xcol)),
            pl.BlockSpec((1, kdim), lambda i: (0, 0)),
            pl.BlockSpec((None, kdim, nout), lambda i: (layer, 0, 0)),
        ],
        out_specs=pl.BlockSpec((TM, nout), lambda i: (i, 0)),
        compiler_params=_cparams(("arbitrary",)),
        name="norm_linear",
    )(x, g, w)


def _out_linear_kernel(a1l_ref, a1c_ref, a2_ref, w1_ref, w2_ref, x_ref, gate_ref, o_ref):
    a1 = jnp.where(pl.program_id(0) >= N_LAT // TM, a1c_ref[...], a1l_ref[...])
    acc = jnp.dot(a1, w1_ref[...], preferred_element_type=F32)
    acc = acc + jnp.dot(a2_ref[...], w2_ref[...], preferred_element_type=F32)
    o_ref[...] = x_ref[...] + gate_ref[...] * acc


def _out_linear(a1_lat, a1_ctx, a2, w, layer, x, mod, which, rows):
    tn = D_MODEL
    k1, k2 = a1_lat.shape[1], a2.shape[1]
    lat_tiles = N_LAT // TM
    return pl.pallas_call(
        _out_linear_kernel,
        out_shape=jax.ShapeDtypeStruct((rows, D_MODEL), F32),
        grid=(rows // TM, D_MODEL // tn),
        in_specs=[
            pl.BlockSpec((TM, k1), lambda i, j: (jnp.minimum(i, lat_tiles - 1), 0)),
            pl.BlockSpec((TM, k1), lambda i, j: (jnp.maximum(i - lat_tiles, 0), 0)),
            pl.BlockSpec((TM, k2), lambda i, j: (i, 0)),
            pl.BlockSpec((None, k1, tn), lambda i, j: (layer, 0, j)),
            pl.BlockSpec((None, k2, tn), lambda i, j: (layer, k1 // k2, j)),
            pl.BlockSpec((TM, tn), lambda i, j: (i, j)),
            pl.BlockSpec((None, None, 1, tn), lambda i, j: (_mod_row(i, TM), which, 0, j)),
        ],
        out_specs=pl.BlockSpec((TM, tn), lambda i, j: (i, j)),
        compiler_params=_cparams(("arbitrary", "arbitrary")),
        name="out_linear",
    )(a1_lat, a1_ctx, a2, w, w, x, mod)


def _rope_tables(rot_dim):
    rows = SEQ // GRID_W
    row_ids = np.repeat(np.arange(rows, dtype=np.float32), GRID_W)
    col_ids = np.tile(np.arange(GRID_W, dtype=np.float32), rows)
    d_axis = rot_dim // 2
    inv = (np.float32(ROPE_THETA) ** (-np.arange(0, d_axis, 2, dtype=np.float32) / np.float32(d_axis))).astype(np.float32)
    ang = np.concatenate([row_ids[:, None] * inv, col_ids[:, None] * inv], axis=-1).astype(np.float32)
    cos, sin = np.cos(ang).astype(np.float32), np.sin(ang).astype(np.float32)
    reps = 128 // rot_dim
    cos_f = np.tile(np.concatenate([cos, cos], axis=-1), (1, reps))
    sin_f = np.tile(np.concatenate([-sin, sin], axis=-1), (1, reps))
    cos_f = np.concatenate([cos_f, np.ones((ROPE_IDENTITY_ROWS, 128), np.float32)], axis=0)
    sin_f = np.concatenate([sin_f, np.zeros((ROPE_IDENTITY_ROWS, 128), np.float32)], axis=0)
    return jnp.asarray(cos_f), jnp.asarray(sin_f)


def _rope128(x, cos, sin):
    return x * cos + pltpu.roll(x, 64, 1) * sin


def _rope64(x, cos, sin):
    lane = lax.broadcasted_iota(jnp.int32, x.shape, 1)
    swapped = jnp.where((lane % 64) < 32, pltpu.roll(x, 96, 1), pltpu.roll(x, 32, 1))
    return x * cos + swapped * sin


def _ones_column(rows):
    lane = lax.broadcasted_iota(jnp.int32, (rows, 128), 1)
    return jnp.where(lane == 0, 1.0, 0.0).astype(BF16)


def _gqa_kv_kernel(k_ref, v_ref, kn_ref, cos_ref, sin_ref, ko_ref, vo_ref):
    cos, sin = cos_ref[...], sin_ref[...]
    ones = _ones_column(k_ref.shape[0])
    for h in range(A_KV_HEADS):
        k = _rms(k_ref[:, h * 128:(h + 1) * 128]) * kn_ref[...]
        ko_ref[:, h * 128:(h + 1) * 128] = _rope128(k, cos, sin).astype(BF16)
        vo_ref[:, h * 256:h * 256 + 128] = v_ref[:, h * 128:(h + 1) * 128].astype(BF16)
        vo_ref[:, h * 256 + 128:(h + 1) * 256] = ones


def _gqa_kv(p, k_norm, cos, sin):
    t = 512
    return pl.pallas_call(
        _gqa_kv_kernel,
        out_shape=[jax.ShapeDtypeStruct((N_TOK, A_KV_WIDTH), BF16),
                   jax.ShapeDtypeStruct((N_TOK, 2 * A_KV_WIDTH), BF16)],
        grid=(N_TOK // t,),
        in_specs=[
            pl.BlockSpec((t, A_KV_WIDTH), lambda i: (i, A_WIDTH // A_KV_WIDTH)),
            pl.BlockSpec((t, A_KV_WIDTH), lambda i: (i, A_WIDTH // A_KV_WIDTH + 1)),
            pl.BlockSpec((1, 128), lambda i: (0, 0)),
            pl.BlockSpec((t, 128), lambda i: (_rope_block(i, t), 0)),
            pl.BlockSpec((t, 128), lambda i: (_rope_block(i, t), 0)),
        ],
        out_specs=[pl.BlockSpec((t, A_KV_WIDTH), lambda i: (i, 0)),
                   pl.BlockSpec((t, 2 * A_KV_WIDTH), lambda i: (i, 0))],
        compiler_params=_cparams(("arbitrary",)),
        name="gqa_kv",
    )(p, p, k_norm, cos, sin)


LOG2E = math.log2(math.e)
KEY_PIECE = 1024


def _softmax_pv(q, keys, vals):
    dn = (((1,), (1,)), ((), ()))
    m = acc = None
    for k, v in zip(keys, vals):
        s = lax.dot_general(q, k, dn, preferred_element_type=F32)
        m_piece = jnp.max(s, axis=-1, keepdims=True)
        if m is None:
            m_new = m_piece
        else:
            m_new = jnp.maximum(m, m_piece)
            acc = acc * jnp.exp2(m - m_new)
        part = jnp.dot(jnp.exp2(s - m_new).astype(BF16), v, preferred_element_type=F32)
        acc = part if acc is None else acc + part
        m = m_new
    return acc[:, :128] / acc[:, 128:129]


def _attn_qrow(b, t):
    nq = SEQ // TQ
    return jnp.where(t == nq, (N_LAT + b * CTX_LEN) // TQ, b * nq + t)


def _ctx_half():
    return pl.multiple_of((pl.program_id(0) % (TQ // CTX_LEN)) * CTX_LEN, CTX_LEN)


def _gqa_attn_kernel(q_ref, qn_ref, cos_ref, sin_ref, kc_ref, vc_ref, kl_ref, vl_ref, ol_ref, oc_ref):
    scale = HEAD_DIM ** -0.5 * LOG2E
    is_ctx = pl.program_id(2) == SEQ // TQ

    def query(rows, g):
        q = _rms(q_ref[rows, g * 128:(g + 1) * 128]) * qn_ref[...]
        return (_rope128(q, cos_ref[rows, :], sin_ref[rows, :]) * scale).astype(BF16)

    @pl.when(jnp.logical_not(is_ctx))
    def _():
        pieces = [slice(j * KEY_PIECE, (j + 1) * KEY_PIECE) for j in range(SEQ // KEY_PIECE)]
        qs = [query(slice(None), g) for g in range(A_GROUP)]
        for g in range(A_GROUP):
            o = _softmax_pv(qs[g], [kc_ref[...]] + [kl_ref[r, :] for r in pieces],
                            [vc_ref[...]] + [vl_ref[r, :] for r in pieces])
            ol_ref[:, g * 128:(g + 1) * 128] = o.astype(BF16)

    @pl.when(is_ctx)
    def _():
        rows = pl.ds(_ctx_half(), CTX_LEN)
        for g in range(A_GROUP):
            oc_ref[:, g * 128:(g + 1) * 128] = _softmax_pv(query(rows, g), [kc_ref[...]], [vc_ref[...]]).astype(BF16)


def _gqa_attention(p, k, vext, q_norm, cos, sin):
    qw = A_GROUP * HEAD_DIM
    nq = SEQ // TQ
    ctx_blk = N_LAT // CTX_LEN
    rope = pl.BlockSpec((TQ, 128), lambda b, h, t: (_rope_block(_attn_qrow(b, t), TQ), 0))
    return pl.pallas_call(
        _gqa_attn_kernel,
        out_shape=[jax.ShapeDtypeStruct((N_LAT, A_WIDTH), BF16), jax.ShapeDtypeStruct((N_CTX, A_WIDTH), BF16)],
        grid=(BATCH, A_KV_HEADS, nq + 1),
        in_specs=[
            pl.BlockSpec((TQ, qw), lambda b, h, t: (_attn_qrow(b, t), h)),
            pl.BlockSpec((1, 128), lambda b, h, t: (0, 0)),
            rope, rope,
            pl.BlockSpec((CTX_LEN, 128), lambda b, h, t: (ctx_blk + b, h)),
            pl.BlockSpec((CTX_LEN, 256), lambda b, h, t: (ctx_blk + b, h)),
            pl.BlockSpec((SEQ, 128), lambda b, h, t: (b, h)),
            pl.BlockSpec((SEQ, 256), lambda b, h, t: (b, h)),
        ],
        out_specs=[pl.BlockSpec((TQ, qw), lambda b, h, t: (b * nq + jnp.minimum(t, nq - 1), h)),
                   pl.BlockSpec((CTX_LEN, qw), lambda b, h, t: (b, h))],
        compiler_params=_cparams(("arbitrary",) * 3),
        name="gqa_attn",
    )(p, q_norm, cos, sin, k, vext, k, vext)


def _conv_kernel(bg_ref, cg_ref, ug_ref, cgp_ref, ugp_ref, cgn_ref, ugn_ref, w_ref, o_ref, *, rows):
    i = pl.program_id(0)
    per_seq = SEQ // rows
    is_ctx = i >= N_LAT // rows
    is_start = jnp.logical_or(is_ctx, i % per_seq == 0)
    is_end = jnp.logical_or(is_ctx, i % per_seq == per_seq - 1)
    m = cg_ref[...] * ug_ref[...]
    m_prev = jnp.where(is_start, 0.0, cgp_ref[7:8, :] * ugp_ref[7:8, :])
    m_next = jnp.where(is_end, 0.0, cgn_ref[0:1, :] * ugn_ref[0:1, :])
    row = lax.broadcasted_iota(jnp.int32, m.shape, 0)
    down = jnp.where(row == 0, m_prev, pltpu.roll(m, 1, 0))
    up = jnp.where(row == rows - 1, m_next, pltpu.roll(m, rows - 1, 0))
    conv = down * w_ref[0:1, :] + m * w_ref[1:2, :] + up * w_ref[2:3, :]
    o_ref[...] = (bg_ref[...] * conv).astype(BF16)


def _gated_conv(p, conv_w):
    rows = CTX_LEN
    base = (A_WIDTH + 2 * A_KV_WIDTH) // B_WIDTH
    halo = rows // 8
    last = N_TOK // 8 - 1
    main = lambda c: pl.BlockSpec((rows, B_WIDTH), lambda i: (i, base + c))
    prev = lambda c: pl.BlockSpec((8, B_WIDTH), lambda i: (jnp.maximum(i * halo - 1, 0), base + c))
    nxt = lambda c: pl.BlockSpec((8, B_WIDTH), lambda i: (jnp.minimum((i + 1) * halo, last), base + c))
    return pl.pallas_call(
        functools.partial(_conv_kernel, rows=rows),
        out_shape=jax.ShapeDtypeStruct((N_TOK, B_WIDTH), BF16),
        grid=(N_TOK // rows,),
        in_specs=[main(0), main(1), main(2), prev(1), prev(2), nxt(1), nxt(2),
                  pl.BlockSpec((3, B_WIDTH), lambda i: (0, 0))],
        out_specs=pl.BlockSpec((rows, B_WIDTH), lambda i: (i, 0)),
        compiler_params=_cparams(("arbitrary",)),
        name="gated_conv",
    )(p, p, p, p, p, p, p, conv_w)


def _mla_kv_kernel(ckv_ref, g_ref, w_ref, kr_ref, cos_ref, sin_ref, ko_ref, vo_ref):
    ckv = (_rms(ckv_ref[...]) * g_ref[...]).astype(BF16)
    kv = jnp.dot(ckv, w_ref[...], preferred_element_type=F32)
    x = kr_ref[...]
    lane = lax.broadcasted_iota(jnp.int32, x.shape, 1)
    r = jnp.where(lane < C_ROPE, _rope64(x, cos_ref[...], sin_ref[...]), 0.0)
    kr_low = r.astype(BF16)
    kr_high = pltpu.roll(r, 64, 1).astype(BF16)
    ones = _ones_column(x.shape[0])
    for h in range(C_HEADS):
        ko_ref[:, h * 256:h * 256 + 128] = kv[:, h * 256:h * 256 + 128].astype(BF16)
        ko_ref[:, h * 256 + 128:(h + 1) * 256] = kr_low if h % 2 == 0 else kr_high
        vo_ref[:, h * 256:h * 256 + 128] = kv[:, h * 256 + 128:(h + 1) * 256].astype(BF16)
        vo_ref[:, h * 256 + 128:(h + 1) * 256] = ones


def _mla_kv(p, kv_norm, w_ukv, layer, cos, sin):
    t = 512
    col = (C_Q_RANK + C_KV_RANK + D_WIDTH) // 128
    width = C_HEADS * 256
    return pl.pallas_call(
        _mla_kv_kernel,
        out_shape=[jax.ShapeDtypeStruct((N_TOK, width), BF16)] * 2,
        grid=(N_TOK // t,),
        in_specs=[
            pl.BlockSpec((t, C_KV_RANK), lambda i: (i, C_Q_RANK // C_KV_RANK)),
            pl.BlockSpec((1, C_KV_RANK), lambda i: (0, 0)),
            pl.BlockSpec((None, C_KV_RANK, width), lambda i: (layer, 0, 0)),
            pl.BlockSpec((t, 128), lambda i: (i, col)),
            pl.BlockSpec((t, 128), lambda i: (_rope_block(i, t), 0)),
            pl.BlockSpec((t, 128), lambda i: (_rope_block(i, t), 0)),
        ],
        out_specs=[pl.BlockSpec((t, width), lambda i: (i, 0))] * 2,
        compiler_params=_cparams(("arbitrary",)),
        name="mla_kv",
    )(p, kv_norm, w_ukv, p, cos, sin)


def _mla_attn_kernel(qn_ref, qr_ref, cos_ref, sin_ref, kc_ref, vc_ref, kl_ref, vl_ref, ol_ref, oc_ref):
    scale = (C_NOPE + C_ROPE) ** -0.5 * LOG2E
    is_ctx = pl.program_id(2) == SEQ // TQ

    def queries(rows):
        qr = _rope64(qr_ref[rows, :], cos_ref[rows, :], sin_ref[rows, :])
        lane = lax.broadcasted_iota(jnp.int32, qr.shape, 1)
        out = []
        for hh in range(2):
            sel = (lane < 64) if hh == 0 else (lane >= 64)
            q = jnp.concatenate([qn_ref[rows, hh * 128:(hh + 1) * 128], jnp.where(sel, qr, 0.0)], axis=1)
            out.append((q * scale).astype(BF16))
        return out

    @pl.when(jnp.logical_not(is_ctx))
    def _():
        for hh, q in enumerate(queries(slice(None))):
            blk = slice(hh * 256, (hh + 1) * 256)
            pieces = [slice(j * KEY_PIECE, (j + 1) * KEY_PIECE) for j in range(SEQ // KEY_PIECE)]
            o = _softmax_pv(q, [kc_ref[:, blk]] + [kl_ref[r, blk] for r in pieces],
                            [vc_ref[:, blk]] + [vl_ref[r, blk] for r in pieces])
            ol_ref[:, hh * 128:(hh + 1) * 128] = o.astype(BF16)

    @pl.when(is_ctx)
    def _():
        for hh, q in enumerate(queries(pl.ds(_ctx_half(), CTX_LEN))):
            blk = slice(hh * 256, (hh + 1) * 256)
            oc_ref[:, hh * 128:(hh + 1) * 128] = _softmax_pv(q, [kc_ref[:, blk]], [vc_ref[:, blk]]).astype(BF16)


def _mla_attention(q, kcat, vext, cos, sin):
    nq = SEQ // TQ
    ctx_blk = N_LAT // CTX_LEN
    rope_col = C_HEADS * C_NOPE // 128
    qrow = _attn_qrow
    rope = pl.BlockSpec((TQ, 128), lambda b, h, t: (_rope_block(qrow(b, t), TQ), 0))
    ctx = pl.BlockSpec((CTX_LEN, 512), lambda b, h, t: (ctx_blk + b, h))
    lat = pl.BlockSpec((SEQ, 512), lambda b, h, t: (b, h))
    return pl.pallas_call(
        _mla_attn_kernel,
        out_shape=[jax.ShapeDtypeStruct((N_LAT, C_WIDTH), BF16), jax.ShapeDtypeStruct((N_CTX, C_WIDTH), BF16)],
        grid=(BATCH, C_HEADS // 2, nq + 1),
        in_specs=[
            pl.BlockSpec((TQ, 256), lambda b, h, t: (qrow(b, t), h)),
            pl.BlockSpec((TQ, 128), lambda b, h, t: (qrow(b, t), rope_col + h)),
            rope, rope, ctx, ctx, lat, lat,
        ],
        out_specs=[pl.BlockSpec((TQ, 256), lambda b, h, t: (b * nq + jnp.minimum(t, nq - 1), h)),
                   pl.BlockSpec((CTX_LEN, 256), lambda b, h, t: (b, h))],
        compiler_params=_cparams(("arbitrary",) * 3),
        name="mla_attn",
    )(q, q, cos, sin, kcat, vext, kcat, vext)


def _s5_matrices(lam_re, lam_im, log_dt, b_re, b_im, c_re, c_im):
    hi = lax.Precision.HIGHEST
    lam_re, lam_im = lam_re.astype(F32), lam_im.astype(F32)
    dt = jnp.exp(log_dt.astype(F32))[..., None]
    ks = jnp.arange(S5_CHUNK + 1, dtype=F32)[:, None, None, None]
    mag = jnp.exp(lam_re[None] * dt[None] * ks)
    ang = lam_im[None] * dt[None] * ks
    pw_re, pw_im = mag * jnp.cos(ang), mag * jnp.sin(ang)
    a_re, a_im = pw_re[1], pw_im[1]
    den = lam_re * lam_re + lam_im * lam_im
    f_re = ((a_re - 1.0) * lam_re + a_im * lam_im) / den
    f_im = (a_im * lam_re - (a_re - 1.0) * lam_im) / den
    b_re, b_im = b_re.astype(F32), b_im.astype(F32)
    bb_re = f_re[..., None] * b_re - f_im[..., None] * b_im
    bb_im = f_re[..., None] * b_im + f_im[..., None] * b_re
    c_re, c_im = c_re.astype(F32), c_im.astype(F32)

    ab_re = pw_re[:S5_CHUNK, ..., None] * bb_re[None] - pw_im[:S5_CHUNK, ..., None] * bb_im[None]
    ab_im = pw_re[:S5_CHUNK, ..., None] * bb_im[None] + pw_im[:S5_CHUNK, ..., None] * bb_re[None]
    kern = (jnp.einsum('dgcn,ldgne->ldgce', c_re, ab_re, precision=hi)
            - jnp.einsum('dgcn,ldgne->ldgce', c_im, ab_im, precision=hi))
    t_idx = np.arange(S5_CHUNK)
    sub = lambda m, ax: m.reshape(m.shape[:ax] + (S5_QUADS, S5_GSUB) + m.shape[ax + 1:])
    klag = jnp.transpose(sub(kern, 2), (1, 2, 0, 5, 3, 4)).reshape(2, S5_QUADS, S5_CHUNK, S5_GROUP, 128)
    bt_re = jnp.transpose(bb_re, (0, 1, 3, 2))
    bt_im = jnp.transpose(bb_im, (0, 1, 3, 2))
    lane_gc = lambda c: jnp.transpose(sub(c, 0), (0, 3, 1, 2)).reshape(S5_QUADS, S5_STATE, 128)
    ps, qs = [], []
    for d in range(2):
        p_pow = (S5_CHUNK - 1 - t_idx) if d == 0 else t_idx
        ar, ai = pw_re[p_pow, d][:, :, None, :], pw_im[p_pow, d][:, :, None, :]
        pc = jnp.concatenate([ar * bt_re[d][None] - ai * bt_im[d][None],
                              ar * bt_im[d][None] + ai * bt_re[d][None]], axis=-1)
        pc = pc.reshape(S5_CHUNK, S5_QUADS, 128, 128)
        ps.append(jnp.transpose(pc, (1, 0, 2, 3)).reshape(S5_QUADS, S5_CHUNK * 128, 128))
        q_pow = (t_idx + 1) if d == 0 else (S5_CHUNK - t_idx)
        cl_re, cl_im = lane_gc(c_re[d])[:, None], lane_gc(c_im[d])[:, None]
        rep = lambda a: jnp.repeat(jnp.transpose(sub(a, 1), (1, 0, 3, 2)), S5_GROUP, axis=-1)
        al_re, al_im = rep(pw_re[q_pow, d]), rep(pw_im[q_pow, d])
        qs.append(jnp.concatenate([cl_re * al_re - cl_im * al_im,
                                   -(cl_re * al_im + cl_im * al_re)], axis=2))
    a16_re, a16_im = pw_re[S5_CHUNK], pw_im[S5_CHUNK]
    lanes = S5_GROUPS * 2 * S5_STATE
    return dict(
        klag=klag.astype(BF16),
        p=jnp.stack(ps).astype(BF16),
        q=jnp.stack(qs).astype(BF16),
        a_mul=jnp.concatenate([a16_re, a16_re], axis=-1).reshape(2, 1, lanes),
        a_swp=jnp.concatenate([-a16_im, a16_im], axis=-1).reshape(2, 1, lanes),
    )


def _s5_expand_p(pc):
    rep = jnp.concatenate([pc] * S5_GSUB, axis=1)
    row = lax.broadcasted_iota(jnp.int32, rep.shape, 0)
    col = lax.broadcasted_iota(jnp.int32, rep.shape, 1)
    return jnp.where((row // S5_GROUP) % S5_GSUB == col // 128, rep, jnp.zeros_like(rep))


def _s5_expand_q(q_ref):
    row = lax.broadcasted_iota(jnp.int32, (S5_GSUB * 128, 128), 0)
    col = lax.broadcasted_iota(jnp.int32, (S5_GSUB * 128, 128), 1)
    keep = row // 128 == col // S5_GROUP
    blocks = [jnp.where(keep, jnp.concatenate([q_ref[t]] * S5_GSUB, axis=0), jnp.zeros((S5_GSUB * 128, 128), BF16))
              for t in range(S5_CHUNK)]
    return jnp.concatenate(blocks, axis=1)


def _s5_chunk_rows(u_ref):
    n = N_TOK // S5_CHUNK
    return jnp.concatenate([u_ref[pl.ds(s, n, stride=S5_CHUNK), :] for s in range(S5_CHUNK)], axis=-1).astype(BF16)


def _s5_z_kernel(u_ref, p_ref, z_ref):
    z_ref[...] = jnp.dot(_s5_chunk_rows(u_ref), _s5_expand_p(p_ref[...]), preferred_element_type=F32)


def _s5_scan_kernel(z_ref, amul_ref, aswp_ref, s_ref, zs_scr):
    d = pl.program_id(0)
    a_mul, a_swp = amul_ref[...], aswp_ref[...]
    lanes = s_ref.shape[1]
    z = z_ref[...]
    low_half = lax.broadcasted_iota(jnp.int32, z.shape, 1) % 128 < S5_STATE
    zs_scr[...] = jnp.where(low_half, pltpu.roll(z, lanes - S5_STATE, 1), pltpu.roll(z, S5_STATE, 1))
    nl, nc = SEQ // S5_CHUNK, S5_CTX_CHUNKS

    def segment(base, count, carry):
        def body(step, carry):
            s, w = carry
            k = jnp.where(d == 0, step, count - 1 - step)
            rows = [pl.ds(base + b * count + k, 1) for b in range(BATCH)]
            for b in range(BATCH):
                s_ref[rows[b], :] = s[b:b + 1, :]
            z = jnp.concatenate([z_ref[r, :] for r in rows], axis=0)
            zs = jnp.concatenate([zs_scr[r, :] for r in rows], axis=0)
            return a_mul * s + a_swp * w + z, a_mul * w - a_swp * s + zs
        return lax.fori_loop(0, count, body, carry)

    zero = jnp.zeros((BATCH, lanes), F32)
    carry = segment(BATCH * nl, nc, (zero, zero))
    segment(0, nl, carry)


def _s5_y_kernel(u_ref, klag_ref, s_ref, q_ref, dv_ref, y_ref, m_scr):
    d = pl.program_id(1)
    n = N_TOK // S5_CHUNK
    row = lax.broadcasted_iota(jnp.int32, (128, 128), 0)
    col = lax.broadcasted_iota(jnp.int32, (128, 128), 1)
    same_group = row // S5_GROUP == col // S5_GROUP
    blocks = [jnp.where(same_group, jnp.concatenate([klag_ref[lag]] * S5_GSUB, axis=0), jnp.zeros((128, 128), BF16))
              for lag in range(S5_CHUNK)]
    for rev in range(2):
        @pl.when(d == rev)
        def _():
            for s in range(S5_CHUNK):
                for t in range(S5_CHUNK):
                    lag = (s - t) if rev else (t - s)
                    blk = blocks[lag] if lag >= 0 else jnp.zeros((128, 128), BF16)
                    m_scr[s * 128:(s + 1) * 128, t * 128:(t + 1) * 128] = blk
    y = jnp.dot(_s5_chunk_rows(u_ref), m_scr[...], preferred_element_type=F32)
    y = y + jnp.dot(s_ref[...].astype(BF16), _s5_expand_q(q_ref), preferred_element_type=F32)
    for t in range(S5_CHUNK):
        rows = pl.ds(t, n, stride=S5_CHUNK)
        yt = y[:, t * 128:(t + 1) * 128]

        @pl.when(d == 0)
        def _():
            y_ref[rows, :] = yt + u_ref[rows, :] * dv_ref[...]

        @pl.when(d == 1)
        def _():
            y_ref[rows, :] = y_ref[rows, :] + yt


def _s5(p, mats, layer, dskip):
    nstate = S5_GROUPS * 2 * S5_STATE
    qlanes = nstate // S5_QUADS
    slanes = qlanes
    ucol = (C_Q_RANK + C_KV_RANK) // 128
    z = pl.pallas_call(
        _s5_z_kernel,
        out_shape=jax.ShapeDtypeStruct((2, S5_ROWS, nstate), F32),
        grid=(2, S5_QUADS),
        in_specs=[
            pl.BlockSpec((N_TOK, 128), lambda d, q: (0, ucol + q)),
            pl.BlockSpec((None, None, None, S5_CHUNK * 128, 128), lambda d, q: (layer, d, q, 0, 0)),
        ],
        out_specs=pl.BlockSpec((None, S5_ROWS, qlanes), lambda d, q: (d, 0, q)),
        compiler_params=_cparams(("arbitrary", "arbitrary")),
        name="s5_chunk_state",
    )(p, mats["p"])
    s = pl.pallas_call(
        _s5_scan_kernel,
        out_shape=jax.ShapeDtypeStruct((2, S5_ROWS, nstate), F32),
        grid=(2, nstate // slanes),
        in_specs=[
            pl.BlockSpec((None, S5_ROWS, slanes), lambda d, q: (d, 0, q)),
            pl.BlockSpec((None, None, 1, slanes), lambda d, q: (layer, d, 0, q)),
            pl.BlockSpec((None, None, 1, slanes), lambda d, q: (layer, d, 0, q)),
        ],
        out_specs=pl.BlockSpec((None, S5_ROWS, slanes), lambda d, q: (d, 0, q)),
        scratch_shapes=[pltpu.VMEM((S5_ROWS, slanes), F32)],
        compiler_params=_cparams(("arbitrary", "arbitrary")),
        name="s5_scan",
    )(z, mats["a_mul"], mats["a_swp"])
    return pl.pallas_call(
        _s5_y_kernel,
        out_shape=jax.ShapeDtypeStruct((N_TOK, D_WIDTH), F32),
        grid=(S5_QUADS, 2),
        in_specs=[
            pl.BlockSpec((N_TOK, 128), lambda q, d: (0, ucol + q)),
            pl.BlockSpec((None, None, None, S5_CHUNK, S5_GROUP, 128), lambda q, d: (layer, d, q, 0, 0, 0)),
            pl.BlockSpec((None, S5_ROWS, qlanes), lambda q, d: (d, 0, q)),
            pl.BlockSpec((None, None, None, S5_CHUNK, 128, 128), lambda q, d: (layer, d, q, 0, 0, 0)),
            pl.BlockSpec((None, 1, 128), lambda q, d: (layer, 0, q)),
        ],
        out_specs=pl.BlockSpec((N_TOK, 128), lambda q, d: (0, q)),
        scratch_shapes=[pltpu.VMEM((S5_CHUNK * 128, S5_CHUNK * 128), BF16)],
        compiler_params=_cparams(("arbitrary", "arbitrary"), VMEM_LIMIT_LARGE),
        name="s5_output",
    )(p, mats["klag"], s, mats["q"], dskip)


def _glu_kernel(y_ref, w_ref, o_ref):
    y = y_ref[...]
    z = y * (0.5 * (1.0 + jnp.tanh(math.sqrt(2.0 / math.pi) * (y + 0.044715 * (y * y * y)))))
    gate = jnp.dot(z.astype(BF16), w_ref[...], preferred_element_type=F32)
    o_ref[...] = (z * jax.nn.sigmoid(gate)).astype(BF16)


def _s5_glu(y, w_glu, layer):
    t = 512
    return pl.pallas_call(
        _glu_kernel,
        out_shape=jax.ShapeDtypeStruct((N_TOK, D_WIDTH), BF16),
        grid=(N_TOK // t,),
        in_specs=[pl.BlockSpec((t, D_WIDTH), lambda i: (i, 0)),
                  pl.BlockSpec((None, D_WIDTH, D_WIDTH), lambda i: (layer, 0, 0))],
        out_specs=pl.BlockSpec((t, D_WIDTH), lambda i: (i, 0)),
        compiler_params=_cparams(("arbitrary",)),
        name="s5_glu",
    )(y, w_glu)


LANE_CHUNKS = D_MODEL // 128


def _store_token_major(ref, val):
    rows = val.shape[0]
    for c in range(LANE_CHUNKS):
        ref[pl.ds(c, rows, stride=LANE_CHUNKS), :] = val[:, c * 128:(c + 1) * 128]


def _load_token_major(ref, rows):
    return jnp.concatenate([ref[pl.ds(c, rows, stride=LANE_CHUNKS), :] for c in range(LANE_CHUNKS)], axis=-1)


def _router_kernel(x_ref, g_ref, sh_ref, sc_ref, w_ref, h_ref, info_ref, gw_ref, cnt_ref, h_scr, carry):
    i = pl.program_id(0)

    @pl.when(i == 0)
    def _():
        carry[...] = jnp.zeros_like(carry)

    for r in range(0, TM, ROW_CHUNK):
        rows = slice(r, r + ROW_CHUNK)
        h_scr[rows, :] = _modulated_norm(x_ref[rows, :], g_ref[...], sh_ref[...], sc_ref[...])
    _store_token_major(h_ref, h_scr[...])
    h = h_scr[...]
    h_hi = h.astype(BF16)
    h_lo = (h - h_hi.astype(F32)).astype(BF16)
    both = jnp.dot(h_hi, w_ref[...], preferred_element_type=F32)
    logits = both[:, :128] + both[:, 128:] + jnp.dot(h_lo, w_ref[:, :128], preferred_element_type=F32)
    lane = lax.broadcasted_iota(jnp.int32, logits.shape, 1)
    neg = -jnp.inf
    big = jnp.int32(1 << 20)

    def first_argmax(v, vmax):
        return jnp.min(jnp.where(v == vmax, lane, big), axis=-1, keepdims=True)

    lg = jnp.where(lane < N_GROUPS, logits, neg)
    mg = jnp.max(lg, axis=-1, keepdims=True)
    g_w = 1.0 / jnp.sum(jnp.exp(lg - mg), axis=-1, keepdims=True)
    g_idx = first_argmax(lg, mg)
    lo = N_GROUPS + EXPERTS_PER_GROUP * g_idx
    le = jnp.where(jnp.logical_and(lane >= lo, lane < lo + EXPERTS_PER_GROUP), logits, neg)
    m1 = jnp.max(le, axis=-1, keepdims=True)
    i1 = first_argmax(le, m1)
    le2 = jnp.where(lane == i1, neg, le)
    m2 = jnp.max(le2, axis=-1, keepdims=True)
    i2 = first_argmax(le2, m2)
    r21 = jnp.exp(m2 - m1)
    w1 = g_w / (1.0 + r21)
    w2 = g_w * r21 / (1.0 + r21)
    oh = jnp.logical_or(lane == i1, lane == i2)
    ohb = jnp.where(oh, 1.0, 0.0).astype(BF16)
    rr = lax.broadcasted_iota(jnp.int32, (TM, TM), 0)
    cc = lax.broadcasted_iota(jnp.int32, (TM, TM), 1)
    lower = jnp.where(rr > cc, 1.0, 0.0).astype(BF16)
    before = jnp.dot(lower, ohb, preferred_element_type=F32) + carry[...]
    rank1 = jnp.sum(jnp.where(lane == i1, before, 0.0), axis=-1, keepdims=True).astype(jnp.int32)
    rank2 = jnp.sum(jnp.where(lane == i2, before, 0.0), axis=-1, keepdims=True).astype(jnp.int32)
    carry[...] = carry[...] + jnp.sum(ohb.astype(F32), axis=0, keepdims=True)
    cnt_ref[...] = jnp.broadcast_to(carry[...], cnt_ref.shape)
    info = jnp.where(lane == 0, i1 - N_GROUPS, jnp.where(lane == 1, i2 - N_GROUPS,
                     jnp.where(lane == 2, rank1, jnp.where(lane == 3, rank2, 0))))
    info_ref[...] = info
    gw_ref[...] = jnp.where(lane == 0, w1, jnp.where(lane == 1, w2, 0.0))


def _router(x, g, mod, w_router, layer):
    rows = x.shape[0]
    return pl.pallas_call(
        _router_kernel,
        out_shape=[jax.ShapeDtypeStruct((rows * LANE_CHUNKS, 128), F32),
                   jax.ShapeDtypeStruct((rows, 128), jnp.int32),
                   jax.ShapeDtypeStruct((rows, 128), F32),
                   jax.ShapeDtypeStruct((8, 128), F32)],
        grid=(rows // TM,),
        in_specs=[
            pl.BlockSpec((TM, D_MODEL), lambda i: (i, 0)),
            pl.BlockSpec((1, D_MODEL), lambda i: (0, 0)),
            pl.BlockSpec((None, None, 1, D_MODEL), lambda i: (_mod_row(i, TM), 3, 0, 0)),
            pl.BlockSpec((None, None, 1, D_MODEL), lambda i: (_mod_row(i, TM), 4, 0, 0)),
            pl.BlockSpec((None, D_MODEL, 256), lambda i: (layer, 0, 0)),
        ],
        out_specs=[pl.BlockSpec((TM * LANE_CHUNKS, 128), lambda i: (i, 0)),
                   pl.BlockSpec((TM, 128), lambda i: (i, 0)),
                   pl.BlockSpec((TM, 128), lambda i: (i, 0)),
                   pl.BlockSpec((8, 128), lambda i: (0, 0))],
        scratch_shapes=[pltpu.VMEM((TM, D_MODEL), F32), pltpu.VMEM((1, 128), F32)],
        compiler_params=_cparams(("arbitrary",)),
        name="moe_router",
    )(x, g, mod, mod, w_router)


FFN_ISSUE_GROUPS = 8
GATHER_SLOTS = 3
ISSUE_UNROLL = 8


def _ffn_kernel(pos_ref, te_ref, meta_ref, h_hbm, wg_ref, wu_ref, wd_ref, o_ref,
                src, xbuf, wg_b, wu_b, wd_b, sem):
    t = pl.program_id(0)
    nt = meta_ref[0]

    def row_copy(tile, slot, r):
        tok = src[tile * TE + r]
        return pltpu.make_async_copy(
            h_hbm.at[pl.ds(pl.multiple_of(tok * LANE_CHUNKS, LANE_CHUNKS), LANE_CHUNKS), :],
            xbuf.at[slot, pl.ds(pl.multiple_of(r * LANE_CHUNKS, LANE_CHUNKS), LANE_CHUNKS), :],
            sem.at[slot])

    def gather(tile, slot):
        def body(r, _):
            row_copy(tile, slot, r).start()
            return 0
        lax.fori_loop(0, TE, body, 0, unroll=ISSUE_UNROLL)

    def wait_tile(slot):
        pltpu.make_async_copy(xbuf.at[slot], xbuf.at[slot], sem.at[slot]).wait()

    @pl.when(t == 0)
    def _():
        for e in range(N_EXPERTS):
            def clear(i, _):
                src[i] = 0
                return 0
            lax.fori_loop(meta_ref[1 + e], meta_ref[1 + N_EXPERTS + e], clear, 0)

        def fill(tok, _):
            src[pos_ref[2 * tok]] = tok
            src[pos_ref[2 * tok + 1]] = tok
            return 0
        lax.fori_loop(0, pos_ref.shape[0] // 2, fill, 0, unroll=8)
        gather(0, 0)
        gather(1, 1)

    @pl.when(t >= nt)
    def _():
        o_ref[...] = jnp.zeros_like(o_ref)

    @pl.when(t < nt)
    def _():
        slot = t % GATHER_SLOTS
        ahead = (t + 2) % GATHER_SLOTS
        wait_tile(slot)

        @pl.when(jnp.logical_or(t == 0, te_ref[t] != te_ref[jnp.maximum(t - 1, 0)]))
        def _():
            wg_b[...] = wg_ref[...].astype(BF16)
            wu_b[...] = wu_ref[...].astype(BF16)
            wd_b[...] = wd_ref[...].astype(BF16)

        nxt = jnp.minimum(t + 2, nt - 1)
        per = TE // FFN_ISSUE_GROUPS

        def issue(g):
            for r in range(g * per, (g + 1) * per):
                row_copy(nxt, ahead, r).start()

        x = _load_token_major(xbuf.at[slot], TE).astype(BF16)
        half = D_EXPERT // 2
        hg, hu = [], []
        for n in range(2):
            issue(n)
            hg.append(jnp.dot(x, wg_b[:, n * half:(n + 1) * half], preferred_element_type=F32))
        for n in range(2):
            issue(2 + n)
            hu.append(jnp.dot(x, wu_b[:, n * half:(n + 1) * half], preferred_element_type=F32))
        hg, hu = jnp.concatenate(hg, axis=1), jnp.concatenate(hu, axis=1)
        act = (hg * jax.nn.sigmoid(hg) * hu).astype(BF16)
        quarter = D_MODEL // 4
        for n in range(4):
            issue(4 + n)
            y = jnp.dot(act, wd_b[:, n * quarter:(n + 1) * quarter], preferred_element_type=F32)
            for c in range(quarter // 128):
                o_ref[pl.ds(n * (quarter // 128) + c, TE, stride=LANE_CHUNKS), :] = y[:, c * 128:(c + 1) * 128]

        @pl.when(t == nt - 1)
        def _():
            wait_tile((t + 1) % GATHER_SLOTS)
            wait_tile(ahead)


def _expert_ffn(pos_flat, tile_expert, meta, h, w_gate, w_up, w_down, layer):
    wsel = lambda t, pos, te, meta: (layer, te[t], 0, 0)
    return pl.pallas_call(
        _ffn_kernel,
        out_shape=jax.ShapeDtypeStruct((N_SORT * LANE_CHUNKS, 128), F32),
        grid_spec=pltpu.PrefetchScalarGridSpec(
            num_scalar_prefetch=3,
            grid=(N_ETILES,),
            in_specs=[
                pl.BlockSpec(memory_space=pl.ANY),
                pl.BlockSpec((None, None, D_MODEL, D_EXPERT), wsel),
                pl.BlockSpec((None, None, D_MODEL, D_EXPERT), wsel),
                pl.BlockSpec((None, None, D_EXPERT, D_MODEL), wsel),
            ],
            out_specs=pl.BlockSpec((TE * LANE_CHUNKS, 128), lambda t, pos, te, meta: (t, 0)),
            scratch_shapes=[
                pltpu.SMEM((N_SORT,), jnp.int32),
                pltpu.VMEM((GATHER_SLOTS, TE * LANE_CHUNKS, 128), F32),
                pltpu.VMEM((D_MODEL, D_EXPERT), BF16),
                pltpu.VMEM((D_MODEL, D_EXPERT), BF16),
                pltpu.VMEM((D_EXPERT, D_MODEL), BF16),
                pltpu.SemaphoreType.DMA((GATHER_SLOTS,)),
            ],
        ),
        compiler_params=_cparams(("arbitrary",), VMEM_LIMIT_LARGE),
        name="moe_expert_ffn",
    )(pos_flat, tile_expert, meta, h, w_gate, w_up, w_down)


TC = 256


def _combine_kernel(pos_ref, x_ref, gate_ref, gw_ref, g_ref, sh_ref, sc_ref, ys_hbm, *refs, final_norm):
    if final_norm:
        o_ref, buf_a, buf_b, sem = refs
    else:
        o_ref, h_ref, buf_a, buf_b, sem = refs
    i = pl.program_id(0)
    slot = i % 2

    def slab(ref, row):
        return ref.at[pl.ds(pl.multiple_of(row * LANE_CHUNKS, LANE_CHUNKS), LANE_CHUNKS), :]

    def gather(tile, slot):
        def body(r, _):
            tok = tile * TC + r
            pltpu.make_async_copy(slab(ys_hbm, pos_ref[2 * tok]), slab(buf_a.at[slot], r), sem.at[slot, 0]).start()
            pltpu.make_async_copy(slab(ys_hbm, pos_ref[2 * tok + 1]), slab(buf_b.at[slot], r), sem.at[slot, 1]).start()
            return 0
        lax.fori_loop(0, TC, body, 0, unroll=ISSUE_UNROLL)

    @pl.when(i == 0)
    def _():
        gather(0, 0)

    @pl.when(i + 1 < pl.num_programs(0))
    def _():
        gather(i + 1, 1 - slot)

    pltpu.make_async_copy(buf_a.at[slot], buf_a.at[slot], sem.at[slot, 0]).wait()
    pltpu.make_async_copy(buf_b.at[slot], buf_b.at[slot], sem.at[slot, 1]).wait()
    w0 = jnp.broadcast_to(gw_ref[:, 0:1], (TC, 128))
    w1 = jnp.broadcast_to(gw_ref[:, 1:2], (TC, 128))
    sq = jnp.zeros((TC, 128), F32)
    for c in range(LANE_CHUNKS):
        cols = slice(c * 128, (c + 1) * 128)
        rows = pl.ds(c, TC, stride=LANE_CHUNKS)
        y = x_ref[:, cols] + gate_ref[:, cols] * (w0 * buf_a[slot, rows, :] + w1 * buf_b[slot, rows, :])
        o_ref[:, cols] = y
        sq = sq + y * y
    inv = lax.rsqrt(jnp.sum(sq, axis=-1, keepdims=True) / D_MODEL + EPS)
    for c in range(LANE_CHUNKS):
        cols = slice(c * 128, (c + 1) * 128)
        normed = o_ref[:, cols] * inv * g_ref[:, cols]
        if final_norm:
            o_ref[:, cols] = normed
        else:
            h_ref[:, cols] = (normed * (1.0 + sc_ref[:, cols]) + sh_ref[:, cols]).astype(BF16)


def _combine(pos_flat, x, mod, gw, ys, g, mod_next):
    final_norm = mod_next is None
    rows = x.shape[0]
    tile = pl.BlockSpec((TC, D_MODEL), lambda i, pos: (i, 0))
    mod_row = lambda which: pl.BlockSpec((None, None, 1, D_MODEL), lambda i, pos: (_mod_row(i, TC), which, 0, 0))
    next_tab = mod if final_norm else mod_next
    out_shape = [jax.ShapeDtypeStruct((rows, D_MODEL), F32)]
    if not final_norm:
        out_shape.append(jax.ShapeDtypeStruct((rows, D_MODEL), BF16))
    return pl.pallas_call(
        functools.partial(_combine_kernel, final_norm=final_norm),
        out_shape=out_shape,
        grid_spec=pltpu.PrefetchScalarGridSpec(
            num_scalar_prefetch=1,
            grid=(rows // TC,),
            in_specs=[
                tile,
                mod_row(5),
                pl.BlockSpec((TC, 128), lambda i, pos: (i, 0)),
                pl.BlockSpec((1, D_MODEL), lambda i, pos: (0, 0)),
                mod_row(0), mod_row(1),
                pl.BlockSpec(memory_space=pl.ANY),
            ],
            out_specs=[tile] * len(out_shape),
            scratch_shapes=[pltpu.VMEM((2, TC * LANE_CHUNKS, 128), F32), pltpu.VMEM((2, TC * LANE_CHUNKS, 128), F32),
                            pltpu.SemaphoreType.DMA((2, 2))],
        ),
        compiler_params=_cparams(("arbitrary",)),
        name="moe_combine",
    )(pos_flat, x, mod, gw, g, next_tab, next_tab, ys)


def _moe(x, g, mod, w_router, w_gate, w_up, w_down, layer, g_next, mod_next):
    h, info, gw, cnt = _router(x, g, mod, w_router, layer)
    counts = cnt[0, N_GROUPS:N_GROUPS + N_EXPERTS].astype(jnp.int32)
    padded = ((counts + TE - 1) // TE) * TE
    ends = jnp.cumsum(padded)
    starts = ends - padded
    experts = jnp.arange(N_EXPERTS, dtype=jnp.int32)
    start_of = jnp.sum(jnp.where(info[:, 0:2, None] == experts, starts, 0), axis=-1)
    pos_flat = (start_of + info[:, 2:4]).reshape(-1)
    tile_ends = ends // TE
    num_tiles = tile_ends[-1]
    tiles = jnp.minimum(jnp.arange(N_ETILES, dtype=jnp.int32), num_tiles - 1)
    tile_expert = jnp.sum((tile_ends[None, :] <= tiles[:, None]).astype(jnp.int32), axis=-1)
    meta = jnp.concatenate([num_tiles[None], starts + counts, ends]).astype(jnp.int32)
    ys = _expert_ffn(pos_flat, tile_expert, meta, h, w_gate, w_up, w_down, layer)
    return _combine(pos_flat, x, mod, gw, ys, g_next, mod_next)


def kernel(x, c, ctx, c_ctx, mod_w, mod_b, norm_mix, norm_ffn, ab_w_in, ab_q_norm, ab_k_norm, ab_conv_w, ab_w_out, cd_w_in, cd_q_norm, cd_kv_norm, cd_w_uq, cd_w_ukv, s5_lam_re, s5_lam_im, s5_log_dt, s5_b_re, s5_b_im, s5_c_re, s5_c_im, s5_d, s5_w_glu, cd_w_out, moe_w_group, moe_w_expert, moe_w_gate, moe_w_up, moe_w_down, final_norm):
    cc = jnp.concatenate([c, c_ctx[None, :], jnp.zeros((8 - BATCH - 1, D_MODEL), F32)], axis=0)
    mods = _modulation(cc, mod_w, mod_b).reshape(DEPTH, 8, N_MOD, 1, D_MODEL)
    xs, h = _embed(x.reshape(N_LAT, D_MODEL), ctx.reshape(N_CTX, D_MODEL), norm_mix[0][None, :], mods[0])
    cos_a, sin_a = _rope_tables(HEAD_DIM)
    cos_c, sin_c = _rope_tables(C_ROPE)

    ab_in_b, ab_out_b = ab_w_in.astype(BF16), ab_w_out.astype(BF16)
    cd_out_b, ukv_b, glu_b = cd_w_out.astype(BF16), cd_w_ukv.astype(BF16), s5_w_glu.astype(BF16)
    a, b_ = C_Q_RANK + C_KV_RANK, C_Q_RANK + C_KV_RANK + C_ROPE
    pad = jnp.zeros(cd_w_in.shape[:2] + (CD_IN_PAD - cd_w_in.shape[2],), F32)
    cd_in_b = jnp.concatenate([cd_w_in[..., :a], cd_w_in[..., b_:], cd_w_in[..., a:b_], pad], axis=-1).astype(BF16)
    w_uq = cd_w_uq.reshape(-1, C_Q_RANK, C_HEADS, C_NOPE + C_ROPE)
    uq_b = jnp.concatenate([w_uq[..., :C_NOPE].reshape(-1, C_Q_RANK, C_HEADS * C_NOPE),
                            w_uq[..., C_NOPE:].reshape(-1, C_Q_RANK, C_HEADS * C_ROPE)], axis=-1).astype(BF16)

    mats = jax.vmap(_s5_matrices)(s5_lam_re, s5_lam_im, s5_log_dt, s5_b_re, s5_b_im, s5_c_re, s5_c_im)
    w_router = jnp.concatenate(
        [moe_w_group, jnp.transpose(moe_w_expert, (0, 2, 1, 3)).reshape(DEPTH, D_MODEL, N_EXPERTS),
         jnp.zeros((DEPTH, D_MODEL, 128 - N_GROUPS - N_EXPERTS), F32)], axis=-1)
    w_router_hi = w_router.astype(BF16)
    w_router = jnp.concatenate([w_router_hi, (w_router - w_router_hi.astype(F32)).astype(BF16)], axis=-1)

    for i in range(DEPTH):
        j = i // 2
        mod = mods[i]
        if i % 2 == 0:
            p = _linear(h, ab_in_b, j, 2048)
            k, vext = _gqa_kv(p, ab_k_norm[j][None, :], cos_a, sin_a)
            o_lat, o_ctx = _gqa_attention(p, k, vext, ab_q_norm[j][None, :], cos_a, sin_a)
            side = _gated_conv(p, ab_conv_w[j])
            w_out = ab_out_b
        else:
            p = _linear(h, cd_in_b, j, CD_IN_PAD)
            q = _norm_linear(p, 0, C_Q_RANK, cd_q_norm[j][None, :], uq_b, j)
            kcat, vext = _mla_kv(p, cd_kv_norm[j][None, :], ukv_b, j, cos_c, sin_c)
            o_lat, o_ctx = _mla_attention(q, kcat, vext, cos_c, sin_c)
            y = _s5(p, mats, j, s5_d.astype(F32)[:, None, :])
            side = _s5_glu(y, glu_b, j)
            w_out = cd_out_b
        xs = _out_linear(o_lat, o_ctx, side, w_out, j, xs, mod, 2, N_TOK if i < DEPTH - 1 else N_LAT)
        if i < DEPTH - 1:
            xs, h = _moe(xs, norm_ffn[i][None, :], mod, w_router, moe_w_gate, moe_w_up, moe_w_down, i,
                         norm_mix[i + 1][None, :], mods[i + 1])
        else:
            out, = _moe(xs, norm_ffn[i][None, :], mod, w_router, moe_w_gate, moe_w_up, moe_w_down, i,
                        final_norm[None, :], None)
    return out.reshape(BATCH, SEQ, D_MODEL)
```

```python
import functools
import math

import numpy as np
import jax
import jax.numpy as jnp
from jax import lax
from jax.experimental import pallas as pl
from jax.experimental.pallas import tpu as pltpu

F32 = jnp.float32
BF16 = jnp.bfloat16

D_MODEL = 2048
BATCH = 4
SEQ = 2048
DEPTH = 4
GRID_W = 64
CTX_LEN = 256
ROPE_THETA = 10000.0
EPS = 1e-6
N_MOD = 6
HEAD_DIM = 128
A_Q_HEADS = 12
A_KV_HEADS = 4
A_GROUP = A_Q_HEADS // A_KV_HEADS
A_WIDTH = A_Q_HEADS * HEAD_DIM
A_KV_WIDTH = A_KV_HEADS * HEAD_DIM
B_WIDTH = 512
AB_IN = A_WIDTH + 2 * A_KV_WIDTH + 3 * B_WIDTH
C_HEADS = 12
C_NOPE = 128
C_ROPE = 64
C_V = 128
C_Q_RANK = 512
C_KV_RANK = 256
C_WIDTH = C_HEADS * C_V
D_WIDTH = 512
S5_GROUP = 16
S5_GROUPS = D_WIDTH // S5_GROUP
S5_STATE = 64
N_GROUPS = 4
EXPERTS_PER_GROUP = 4
N_EXPERTS = N_GROUPS * EXPERTS_PER_GROUP
D_EXPERT = 512

N_LAT = BATCH * SEQ
N_CTX = BATCH * CTX_LEN
N_TOK = N_LAT + N_CTX
CTX_ROW = BATCH
CD_IN_PAD = 1536

TM = 512
TM_LINEAR = 1024
TQ = 1024
TE = 256
ROPE_IDENTITY_ROWS = max(TM, TQ)
N_SORT = 2 * N_TOK + N_EXPERTS * TE
N_ETILES = N_SORT // TE
S5_CHUNK = 16
S5_CTX_CHUNKS = CTX_LEN // S5_CHUNK
S5_ROWS = N_TOK // S5_CHUNK
S5_GSUB = 128 // S5_GROUP
S5_QUADS = S5_GROUPS // S5_GSUB
VMEM_LIMIT = 48 * 1024 * 1024
VMEM_LIMIT_LARGE = 56 * 1024 * 1024
ROW_CHUNK = 256


def _cparams(sem, limit=VMEM_LIMIT):
    return pltpu.CompilerParams(dimension_semantics=sem, vmem_limit_bytes=limit)


def _mod_row(tile, tile_rows):
    r0 = tile * tile_rows
    return jnp.where(r0 >= N_LAT, CTX_ROW, r0 // SEQ)


def _rope_block(tile, tile_rows):
    r0 = tile * tile_rows
    return jnp.where(r0 >= N_LAT, SEQ // tile_rows, (r0 % SEQ) // tile_rows)


def _rms(x):
    return x * lax.rsqrt(jnp.mean(x * x, axis=-1, keepdims=True) + EPS)


def _mod_kernel(cc_ref, w_ref, b_ref, o_ref):
    cc = cc_ref[...]
    s = (cc * jax.nn.sigmoid(cc)).astype(BF16)
    o_ref[...] = jnp.dot(s, w_ref[...].astype(BF16), preferred_element_type=F32) + b_ref[...]


def _modulation(cc, mod_w, mod_b):
    tn = 1024
    nout = N_MOD * D_MODEL
    return pl.pallas_call(
        _mod_kernel,
        out_shape=jax.ShapeDtypeStruct((DEPTH, 8, nout), F32),
        grid=(DEPTH, nout // tn),
        in_specs=[
            pl.BlockSpec((8, D_MODEL), lambda l, j: (0, 0)),
            pl.BlockSpec((None, D_MODEL, tn), lambda l, j: (l, 0, j)),
            pl.BlockSpec((None, 1, tn), lambda l, j: (l, 0, j)),
        ],
        out_specs=pl.BlockSpec((None, 8, tn), lambda l, j: (l, 0, j)),
        compiler_params=_cparams(("arbitrary", "arbitrary")),
        name="modulation",
    )(cc, mod_w, mod_b.reshape(DEPTH, 1, nout))


def _modulated_norm(x, g, sh, sc):
    return _rms(x) * g * (1.0 + sc) + sh


def _embed_kernel(xl_ref, xc_ref, g_ref, sh_ref, sc_ref, xs_ref, h_ref):
    is_ctx = pl.program_id(0) >= N_LAT // TM
    for r in range(0, TM, ROW_CHUNK):
        rows = slice(r, r + ROW_CHUNK)
        x = jnp.where(is_ctx, xc_ref[rows, :], xl_ref[rows, :])
        xs_ref[rows, :] = x
        h_ref[rows, :] = _modulated_norm(x, g_ref[...], sh_ref[...], sc_ref[...]).astype(BF16)


def _embed(x_lat, x_ctx, g, mod):
    lat_tiles = N_LAT // TM
    return pl.pallas_call(
        _embed_kernel,
        out_shape=[jax.ShapeDtypeStruct((N_TOK, D_MODEL), F32), jax.ShapeDtypeStruct((N_TOK, D_MODEL), BF16)],
        grid=(N_TOK // TM,),
        in_specs=[
            pl.BlockSpec((TM, D_MODEL), lambda i: (jnp.minimum(i, lat_tiles - 1), 0)),
            pl.BlockSpec((TM, D_MODEL), lambda i: (jnp.maximum(i - lat_tiles, 0), 0)),
            pl.BlockSpec((1, D_MODEL), lambda i: (0, 0)),
            pl.BlockSpec((None, None, 1, D_MODEL), lambda i: (_mod_row(i, TM), 0, 0, 0)),
            pl.BlockSpec((None, None, 1, D_MODEL), lambda i: (_mod_row(i, TM), 1, 0, 0)),
        ],
        out_specs=[pl.BlockSpec((TM, D_MODEL), lambda i: (i, 0))] * 2,
        compiler_params=_cparams(("arbitrary",)),
        name="embed_norm",
    )(x_lat, x_ctx, g, mod, mod)


def _linear_kernel(h_ref, w_ref, o_ref):
    o_ref[...] = jnp.dot(h_ref[...], w_ref[...], preferred_element_type=F32)


def _linear(h, w, layer, tn):
    kdim, nout = w.shape[1], w.shape[2]
    return pl.pallas_call(
        _linear_kernel,
        out_shape=jax.ShapeDtypeStruct((N_TOK, nout), F32),
        grid=(N_TOK // TM_LINEAR, nout // tn),
        in_specs=[
            pl.BlockSpec((TM_LINEAR, kdim), lambda i, j: (i, 0)),
            pl.BlockSpec((None, kdim, tn), lambda i, j: (layer, 0, j)),
        ],
        out_specs=pl.BlockSpec((TM_LINEAR, tn), lambda i, j: (i, j)),
        compiler_params=_cparams(("arbitrary", "arbitrary")),
        name="linear",
    )(h, w)


def _norm_linear_kernel(x_ref, g_ref, w_ref, o_ref):
    h = (_rms(x_ref[...]) * g_ref[...]).astype(BF16)
    o_ref[...] = jnp.dot(h, w_ref[...], preferred_element_type=F32)


def _norm_linear(x, xcol, kdim, g, w, layer):
    nout = w.shape[2]
    return pl.pallas_call(
        _norm_linear_kernel,
        out_shape=jax.ShapeDtypeStruct((N_TOK, nout), F32),
        grid=(N_TOK // TM,),
        in_specs=[
            pl.BlockSpec((TM, kdim), lambda i: (i, xcol)),
            pl.BlockSpec((1, kdim), lambda i: (0, 0)),
            pl.BlockSpec((None, kdim, nout), lambda i: (layer, 0, 0)),
        ],
        out_specs=pl.BlockSpec((TM, nout), lambda i: (i, 0)),
        compiler_params=_cparams(("arbitrary",)),
        name="norm_linear",
    )(x, g, w)


def _out_linear_kernel(a1l_ref, a1c_ref, a2_ref, w1_ref, w2_ref, x_ref, gate_ref, o_ref):
    a1 = jnp.where(pl.program_id(0) >= N_LAT // TM, a1c_ref[...], a1l_ref[...])
    acc = jnp.dot(a1, w1_ref[...], preferred_element_type=F32)
    acc = acc + jnp.dot(a2_ref[...], w2_ref[...], preferred_element_type=F32)
    o_ref[...] = x_ref[...] + gate_ref[...] * acc


def _out_linear(a1_lat, a1_ctx, a2, w, layer, x, mod, which, rows):
    tn = D_MODEL
    k1, k2 = a1_lat.shape[1], a2.shape[1]
    lat_tiles = N_LAT // TM
    return pl.pallas_call(
        _out_linear_kernel,
        out_shape=jax.ShapeDtypeStruct((rows, D_MODEL), F32),
        grid=(rows // TM, D_MODEL // tn),
        in_specs=[
            pl.BlockSpec((TM, k1), lambda i, j: (jnp.minimum(i, lat_tiles - 1), 0)),
            pl.BlockSpec((TM, k1), lambda i, j: (jnp.maximum(i - lat_tiles, 0), 0)),
            pl.BlockSpec((TM, k2), lambda i, j: (i, 0)),
            pl.BlockSpec((None, k1, tn), lambda i, j: (layer, 0, j)),
            pl.BlockSpec((None, k2, tn), lambda i, j: (layer, k1 // k2, j)),
            pl.BlockSpec((TM, tn), lambda i, j: (i, j)),
            pl.BlockSpec((None, None, 1, tn), lambda i, j: (_mod_row(i, TM), which, 0, j)),
        ],
        out_specs=pl.BlockSpec((TM, tn), lambda i, j: (i, j)),
        compiler_params=_cparams(("arbitrary", "arbitrary")),
        name="out_linear",
    )(a1_lat, a1_ctx, a2, w, w, x, mod)


def _rope_tables(rot_dim):
    rows = SEQ // GRID_W
    row_ids = np.repeat(np.arange(rows, dtype=np.float32), GRID_W)
    col_ids = np.tile(np.arange(GRID_W, dtype=np.float32), rows)
    d_axis = rot_dim // 2
    inv = (np.float32(ROPE_THETA) ** (-np.arange(0, d_axis, 2, dtype=np.float32) / np.float32(d_axis))).astype(np.float32)
    ang = np.concatenate([row_ids[:, None] * inv, col_ids[:, None] * inv], axis=-1).astype(np.float32)
    cos, sin = np.cos(ang).astype(np.float32), np.sin(ang).astype(np.float32)
    reps = 128 // rot_dim
    cos_f = np.tile(np.concatenate([cos, cos], axis=-1), (1, reps))
    sin_f = np.tile(np.concatenate([-sin, sin], axis=-1), (1, reps))
    cos_f = np.concatenate([cos_f, np.ones((ROPE_IDENTITY_ROWS, 128), np.float32)], axis=0)
    sin_f = np.concatenate([sin_f, np.zeros((ROPE_IDENTITY_ROWS, 128), np.float32)], axis=0)
    return jnp.asarray(cos_f), jnp.asarray(sin_f)


def _rope128(x, cos, sin):
    return x * cos + pltpu.roll(x, 64, 1) * sin


def _rope64(x, cos, sin):
    lane = lax.broadcasted_iota(jnp.int32, x.shape, 1)
    swapped = jnp.where((lane % 64) < 32, pltpu.roll(x, 96, 1), pltpu.roll(x, 32, 1))
    return x * cos + swapped * sin


def _ones_column(rows):
    lane = lax.broadcasted_iota(jnp.int32, (rows, 128), 1)
    return jnp.where(lane == 0, 1.0, 0.0).astype(BF16)


def _gqa_kv_kernel(k_ref, v_ref, kn_ref, cos_ref, sin_ref, ko_ref, vo_ref):
    cos, sin = cos_ref[...], sin_ref[...]
    ones = _ones_column(k_ref.shape[0])
    for h in range(A_KV_HEADS):
        k = _rms(k_ref[:, h * 128:(h + 1) * 128]) * kn_ref[...]
        ko_ref[:, h * 128:(h + 1) * 128] = _rope128(k, cos, sin).astype(BF16)
        vo_ref[:, h * 256:h * 256 + 128] = v_ref[:, h * 128:(h + 1) * 128].astype(BF16)
        vo_ref[:, h * 256 + 128:(h + 1) * 256] = ones


def _gqa_kv(p, k_norm, cos, sin):
    t = 512
    return pl.pallas_call(
        _gqa_kv_kernel,
        out_shape=[jax.ShapeDtypeStruct((N_TOK, A_KV_WIDTH), BF16),
                   jax.ShapeDtypeStruct((N_TOK, 2 * A_KV_WIDTH), BF16)],
        grid=(N_TOK // t,),
        in_specs=[
            pl.BlockSpec((t, A_KV_WIDTH), lambda i: (i, A_WIDTH // A_KV_WIDTH)),
            pl.BlockSpec((t, A_KV_WIDTH), lambda i: (i, A_WIDTH // A_KV_WIDTH + 1)),
            pl.BlockSpec((1, 128), lambda i: (0, 0)),
            pl.BlockSpec((t, 128), lambda i: (_rope_block(i, t), 0)),
            pl.BlockSpec((t, 128), lambda i: (_rope_block(i, t), 0)),
        ],
        out_specs=[pl.BlockSpec((t, A_KV_WIDTH), lambda i: (i, 0)),
                   pl.BlockSpec((t, 2 * A_KV_WIDTH), lambda i: (i, 0))],
        compiler_params=_cparams(("arbitrary",)),
        name="gqa_kv",
    )(p, p, k_norm, cos, sin)


LOG2E = math.log2(math.e)
KEY_PIECE = 1024


def _softmax_pv(q, keys, vals):
    dn = (((1,), (1,)), ((), ()))
    m = acc = None
    for k, v in zip(keys, vals):
        s = lax.dot_general(q, k, dn, preferred_element_type=F32)
        m_piece = jnp.max(s, axis=-1, keepdims=True)
        if m is None:
            m_new = m_piece
        else:
            m_new = jnp.maximum(m, m_piece)
            acc = acc * jnp.exp2(m - m_new)
        part = jnp.dot(jnp.exp2(s - m_new).astype(BF16), v, preferred_element_type=F32)
        acc = part if acc is None else acc + part
        m = m_new
    return acc[:, :128] / acc[:, 128:129]


def _attn_qrow(b, t):
    nq = SEQ // TQ
    return jnp.where(t == nq, (N_LAT + b * CTX_LEN) // TQ, b * nq + t)


def _ctx_half():
    return pl.multiple_of((pl.program_id(0) % (TQ // CTX_LEN)) * CTX_LEN, CTX_LEN)


def _gqa_attn_kernel(q_ref, qn_ref, cos_ref, sin_ref, kc_ref, vc_ref, kl_ref, vl_ref, ol_ref, oc_ref):
    scale = HEAD_DIM ** -0.5 * LOG2E
    is_ctx = pl.program_id(2) == SEQ // TQ

    def query(rows, g):
        q = _rms(q_ref[rows, g * 128:(g + 1) * 128]) * qn_ref[...]
        return (_rope128(q, cos_ref[rows, :], sin_ref[rows, :]) * scale).astype(BF16)

    @pl.when(jnp.logical_not(is_ctx))
    def _():
        pieces = [slice(j * KEY_PIECE, (j + 1) * KEY_PIECE) for j in range(SEQ // KEY_PIECE)]
        qs = [query(slice(None), g) for g in range(A_GROUP)]
        for g in range(A_GROUP):
            o = _softmax_pv(qs[g], [kc_ref[...]] + [kl_ref[r, :] for r in pieces],
                            [vc_ref[...]] + [vl_ref[r, :] for r in pieces])
            ol_ref[:, g * 128:(g + 1) * 128] = o.astype(BF16)

    @pl.when(is_ctx)
    def _():
        rows = pl.ds(_ctx_half(), CTX_LEN)
        for g in range(A_GROUP):
            oc_ref[:, g * 128:(g + 1) * 128] = _softmax_pv(query(rows, g), [kc_ref[...]], [vc_ref[...]]).astype(BF16)


def _gqa_attention(p, k, vext, q_norm, cos, sin):
    qw = A_GROUP * HEAD_DIM
    nq = SEQ // TQ
    ctx_blk = N_LAT // CTX_LEN
    rope = pl.BlockSpec((TQ, 128), lambda b, h, t: (_rope_block(_attn_qrow(b, t), TQ), 0))
    return pl.pallas_call(
        _gqa_attn_kernel,
        out_shape=[jax.ShapeDtypeStruct((N_LAT, A_WIDTH), BF16), jax.ShapeDtypeStruct((N_CTX, A_WIDTH), BF16)],
        grid=(BATCH, A_KV_HEADS, nq + 1),
        in_specs=[
            pl.BlockSpec((TQ, qw), lambda b, h, t: (_attn_qrow(b, t), h)),
            pl.BlockSpec((1, 128), lambda b, h, t: (0, 0)),
            rope, rope,
            pl.BlockSpec((CTX_LEN, 128), lambda b, h, t: (ctx_blk + b, h)),
            pl.BlockSpec((CTX_LEN, 256), lambda b, h, t: (ctx_blk + b, h)),
            pl.BlockSpec((SEQ, 128), lambda b, h, t: (b, h)),
            pl.BlockSpec((SEQ, 256), lambda b, h, t: (b, h)),
        ],
        out_specs=[pl.BlockSpec((TQ, qw), lambda b, h, t: (b * nq + jnp.minimum(t, nq - 1), h)),
                   pl.BlockSpec((CTX_LEN, qw), lambda b, h, t: (b, h))],
        compiler_params=_cparams(("arbitrary",) * 3),
        name="gqa_attn",
    )(p, q_norm, cos, sin, k, vext, k, vext)


def _conv_kernel(bg_ref, cg_ref, ug_ref, cgp_ref, ugp_ref, cgn_ref, ugn_ref, w_ref, o_ref, *, rows):
    i = pl.program_id(0)
    per_seq = SEQ // rows
    is_ctx = i >= N_LAT // rows
    is_start = jnp.logical_or(is_ctx, i % per_seq == 0)
    is_end = jnp.logical_or(is_ctx, i % per_seq == per_seq - 1)
    m = cg_ref[...] * ug_ref[...]
    m_prev = jnp.where(is_start, 0.0, cgp_ref[7:8, :] * ugp_ref[7:8, :])
    m_next = jnp.where(is_end, 0.0, cgn_ref[0:1, :] * ugn_ref[0:1, :])
    row = lax.broadcasted_iota(jnp.int32, m.shape, 0)
    down = jnp.where(row == 0, m_prev, pltpu.roll(m, 1, 0))
    up = jnp.where(row == rows - 1, m_next, pltpu.roll(m, rows - 1, 0))
    conv = down * w_ref[0:1, :] + m * w_ref[1:2, :] + up * w_ref[2:3, :]
    o_ref[...] = (bg_ref[...] * conv).astype(BF16)


def _gated_conv(p, conv_w):
    rows = CTX_LEN
    base = (A_WIDTH + 2 * A_KV_WIDTH) // B_WIDTH
    halo = rows // 8
    last = N_TOK // 8 - 1
    main = lambda c: pl.BlockSpec((rows, B_WIDTH), lambda i: (i, base + c))
    prev = lambda c: pl.BlockSpec((8, B_WIDTH), lambda i: (jnp.maximum(i * halo - 1, 0), base + c))
    nxt = lambda c: pl.BlockSpec((8, B_WIDTH), lambda i: (jnp.minimum((i + 1) * halo, last), base + c))
    return pl.pallas_call(
        functools.partial(_conv_kernel, rows=rows),
        out_shape=jax.ShapeDtypeStruct((N_TOK, B_WIDTH), BF16),
        grid=(N_TOK // rows,),
        in_specs=[main(0), main(1), main(2), prev(1), prev(2), nxt(1), nxt(2),
                  pl.BlockSpec((3, B_WIDTH), lambda i: (0, 0))],
        out_specs=pl.BlockSpec((rows, B_WIDTH), lambda i: (i, 0)),
        compiler_params=_cparams(("arbitrary",)),
        name="gated_conv",
    )(p, p, p, p, p, p, p, conv_w)


def _mla_kv_kernel(ckv_ref, g_ref, w_ref, kr_ref, cos_ref, sin_ref, ko_ref, vo_ref):
    ckv = (_rms(ckv_ref[...]) * g_ref[...]).astype(BF16)
    kv = jnp.dot(ckv, w_ref[...], preferred_element_type=F32)
    x = kr_ref[...]
    lane = lax.broadcasted_iota(jnp.int32, x.shape, 1)
    r = jnp.where(lane < C_ROPE, _rope64(x, cos_ref[...], sin_ref[...]), 0.0)
    kr_low = r.astype(BF16)
    kr_high = pltpu.roll(r, 64, 1).astype(BF16)
    ones = _ones_column(x.shape[0])
    for h in range(C_HEADS):
        ko_ref[:, h * 256:h * 256 + 128] = kv[:, h * 256:h * 256 + 128].astype(BF16)
        ko_ref[:, h * 256 + 128:(h + 1) * 256] = kr_low if h % 2 == 0 else kr_high
        vo_ref[:, h * 256:h * 256 + 128] = kv[:, h * 256 + 128:(h + 1) * 256].astype(BF16)
        vo_ref[:, h * 256 + 128:(h + 1) * 256] = ones


def _mla_kv(p, kv_norm, w_ukv, layer, cos, sin):
    t = 512
    col = (C_Q_RANK + C_KV_RANK + D_WIDTH) // 128
    width = C_HEADS * 256
    return pl.pallas_call(
        _mla_kv_kernel,
        out_shape=[jax.ShapeDtypeStruct((N_TOK, width), BF16)] * 2,
        grid=(N_TOK // t,),
        in_specs=[
            pl.BlockSpec((t, C_KV_RANK), lambda i: (i, C_Q_RANK // C_KV_RANK)),
            pl.BlockSpec((1, C_KV_RANK), lambda i: (0, 0)),
            pl.BlockSpec((None, C_KV_RANK, width), lambda i: (layer, 0, 0)),
            pl.BlockSpec((t, 128), lambda i: (i, col)),
            pl.BlockSpec((t, 128), lambda i: (_rope_block(i, t), 0)),
            pl.BlockSpec((t, 128), lambda i: (_rope_block(i, t), 0)),
        ],
        out_specs=[pl.BlockSpec((t, width), lambda i: (i, 0))] * 2,
        compiler_params=_cparams(("arbitrary",)),
        name="mla_kv",
    )(p, kv_norm, w_ukv, p, cos, sin)


def _mla_attn_kernel(qn_ref, qr_ref, cos_ref, sin_ref, kc_ref, vc_ref, kl_ref, vl_ref, ol_ref, oc_ref):
    scale = (C_NOPE + C_ROPE) ** -0.5 * LOG2E
    is_ctx = pl.program_id(2) == SEQ // TQ

    def queries(rows):
        qr = _rope64(qr_ref[rows, :], cos_ref[rows, :], sin_ref[rows, :])
        lane = lax.broadcasted_iota(jnp.int32, qr.shape, 1)
        out = []
        for hh in range(2):
            sel = (lane < 64) if hh == 0 else (lane >= 64)
            q = jnp.concatenate([qn_ref[rows, hh * 128:(hh + 1) * 128], jnp.where(sel, qr, 0.0)], axis=1)
            out.append((q * scale).astype(BF16))
        return out

    @pl.when(jnp.logical_not(is_ctx))
    def _():
        for hh, q in enumerate(queries(slice(None))):
            blk = slice(hh * 256, (hh + 1) * 256)
            pieces = [slice(j * KEY_PIECE, (j + 1) * KEY_PIECE) for j in range(SEQ // KEY_PIECE)]
            o = _softmax_pv(q, [kc_ref[:, blk]] + [kl_ref[r, blk] for r in pieces],
                            [vc_ref[:, blk]] + [vl_ref[r, blk] for r in pieces])
            ol_ref[:, hh * 128:(hh + 1) * 128] = o.astype(BF16)

    @pl.when(is_ctx)
    def _():
        for hh, q in enumerate(queries(pl.ds(_ctx_half(), CTX_LEN))):
            blk = slice(hh * 256, (hh + 1) * 256)
            oc_ref[:, hh * 128:(hh + 1) * 128] = _softmax_pv(q, [kc_ref[:, blk]], [vc_ref[:, blk]]).astype(BF16)


def _mla_attention(q, kcat, vext, cos, sin):
    nq = SEQ // TQ
    ctx_blk = N_LAT // CTX_LEN
    rope_col = C_HEADS * C_NOPE // 128
    qrow = _attn_qrow
    rope = pl.BlockSpec((TQ, 128), lambda b, h, t: (_rope_block(qrow(b, t), TQ), 0))
    ctx = pl.BlockSpec((CTX_LEN, 512), lambda b, h, t: (ctx_blk + b, h))
    lat = pl.BlockSpec((SEQ, 512), lambda b, h, t: (b, h))
    return pl.pallas_call(
        _mla_attn_kernel,
        out_shape=[jax.ShapeDtypeStruct((N_LAT, C_WIDTH), BF16), jax.ShapeDtypeStruct((N_CTX, C_WIDTH), BF16)],
        grid=(BATCH, C_HEADS // 2, nq + 1),
        in_specs=[
            pl.BlockSpec((TQ, 256), lambda b, h, t: (qrow(b, t), h)),
            pl.BlockSpec((TQ, 128), lambda b, h, t: (qrow(b, t), rope_col + h)),
            rope, rope, ctx, ctx, lat, lat,
        ],
        out_specs=[pl.BlockSpec((TQ, 256), lambda b, h, t: (b * nq + jnp.minimum(t, nq - 1), h)),
                   pl.BlockSpec((CTX_LEN, 256), lambda b, h, t: (b, h))],
        compiler_params=_cparams(("arbitrary",) * 3),
        name="mla_attn",
    )(q, q, cos, sin, kcat, vext, kcat, vext)


def _s5_matrices(lam_re, lam_im, log_dt, b_re, b_im, c_re, c_im):
    hi = lax.Precision.HIGHEST
    lam_re, lam_im = lam_re.astype(F32), lam_im.astype(F32)
    dt = jnp.exp(log_dt.astype(F32))[..., None]
    ks = jnp.arange(S5_CHUNK + 1, dtype=F32)[:, None, None, None]
    mag = jnp.exp(lam_re[None] * dt[None] * ks)
    ang = lam_im[None] * dt[None] * ks
    pw_re, pw_im = mag * jnp.cos(ang), mag * jnp.sin(ang)
    a_re, a_im = pw_re[1], pw_im[1]
    den = lam_re * lam_re + lam_im * lam_im
    f_re = ((a_re - 1.0) * lam_re + a_im * lam_im) / den
    f_im = (a_im * lam_re - (a_re - 1.0) * lam_im) / den
    b_re, b_im = b_re.astype(F32), b_im.astype(F32)
    bb_re = f_re[..., None] * b_re - f_im[..., None] * b_im
    bb_im = f_re[..., None] * b_im + f_im[..., None] * b_re
    c_re, c_im = c_re.astype(F32), c_im.astype(F32)

    ab_re = pw_re[:S5_CHUNK, ..., None] * bb_re[None] - pw_im[:S5_CHUNK, ..., None] * bb_im[None]
    ab_im = pw_re[:S5_CHUNK, ..., None] * bb_im[None] + pw_im[:S5_CHUNK, ..., None] * bb_re[None]
    kern = (jnp.einsum('dgcn,ldgne->ldgce', c_re, ab_re, precision=hi)
            - jnp.einsum('dgcn,ldgne->ldgce', c_im, ab_im, precision=hi))
    t_idx = np.arange(S5_CHUNK)
    sub = lambda m, ax: m.reshape(m.shape[:ax] + (S5_QUADS, S5_GSUB) + m.shape[ax + 1:])
    klag = jnp.transpose(sub(kern, 2), (1, 2, 0, 5, 3, 4)).reshape(2, S5_QUADS, S5_CHUNK, S5_GROUP, 128)
    bt_re = jnp.transpose(bb_re, (0, 1, 3, 2))
    bt_im = jnp.transpose(bb_im, (0, 1, 3, 2))
    lane_gc = lambda c: jnp.transpose(sub(c, 0), (0, 3, 1, 2)).reshape(S5_QUADS, S5_STATE, 128)
    ps, qs = [], []
    for d in range(2):
        p_pow = (S5_CHUNK - 1 - t_idx) if d == 0 else t_idx
        ar, ai = pw_re[p_pow, d][:, :, None, :], pw_im[p_pow, d][:, :, None, :]
        pc = jnp.concatenate([ar * bt_re[d][None] - ai * bt_im[d][None],
                              ar * bt_im[d][None] + ai * bt_re[d][None]], axis=-1)
        pc = pc.reshape(S5_CHUNK, S5_QUADS, 128, 128)
        ps.append(jnp.transpose(pc, (1, 0, 2, 3)).reshape(S5_QUADS, S5_CHUNK * 128, 128))
        q_pow = (t_idx + 1) if d == 0 else (S5_CHUNK - t_idx)
        cl_re, cl_im = lane_gc(c_re[d])[:, None], lane_gc(c_im[d])[:, None]
        rep = lambda a: jnp.repeat(jnp.transpose(sub(a, 1), (1, 0, 3, 2)), S5_GROUP, axis=-1)
        al_re, al_im = rep(pw_re[q_pow, d]), rep(pw_im[q_pow, d])
        qs.append(jnp.concatenate([cl_re * al_re - cl_im * al_im,
                                   -(cl_re * al_im + cl_im * al_re)], axis=2))
    a16_re, a16_im = pw_re[S5_CHUNK], pw_im[S5_CHUNK]
    lanes = S5_GROUPS * 2 * S5_STATE
    return dict(
        klag=klag.astype(BF16),
        p=jnp.stack(ps).astype(BF16),
        q=jnp.stack(qs).astype(BF16),
        a_mul=jnp.concatenate([a16_re, a16_re], axis=-1).reshape(2, 1, lanes),
        a_swp=jnp.concatenate([-a16_im, a16_im], axis=-1).reshape(2, 1, lanes),
    )


def _s5_expand_p(pc):
    rep = jnp.concatenate([pc] * S5_GSUB, axis=1)
    row = lax.broadcasted_iota(jnp.int32, rep.shape, 0)
    col = lax.broadcasted_iota(jnp.int32, rep.shape, 1)
    return jnp.where((row // S5_GROUP) % S5_GSUB == col // 128, rep, jnp.zeros_like(rep))


def _s5_expand_q(q_ref):
    row = lax.broadcasted_iota(jnp.int32, (S5_GSUB * 128, 128), 0)
    col = lax.broadcasted_iota(jnp.int32, (S5_GSUB * 128, 128), 1)
    keep = row // 128 == col // S5_GROUP
    blocks = [jnp.where(keep, jnp.concatenate([q_ref[t]] * S5_GSUB, axis=0), jnp.zeros((S5_GSUB * 128, 128), BF16))
              for t in range(S5_CHUNK)]
    return jnp.concatenate(blocks, axis=1)


def _s5_chunk_rows(u_ref):
    n = N_TOK // S5_CHUNK
    return jnp.concatenate([u_ref[pl.ds(s, n, stride=S5_CHUNK), :] for s in range(S5_CHUNK)], axis=-1).astype(BF16)


def _s5_z_kernel(u_ref, p_ref, z_ref):
    z_ref[...] = jnp.dot(_s5_chunk_rows(u_ref), _s5_expand_p(p_ref[...]), preferred_element_type=F32)


def _s5_scan_kernel(z_ref, amul_ref, aswp_ref, s_ref, zs_scr):
    d = pl.program_id(0)
    a_mul, a_swp = amul_ref[...], aswp_ref[...]
    lanes = s_ref.shape[1]
    z = z_ref[...]
    low_half = lax.broadcasted_iota(jnp.int32, z.shape, 1) % 128 < S5_STATE
    zs_scr[...] = jnp.where(low_half, pltpu.roll(z, lanes - S5_STATE, 1), pltpu.roll(z, S5_STATE, 1))
    nl, nc = SEQ // S5_CHUNK, S5_CTX_CHUNKS

    def segment(base, count, carry):
        def body(step, carry):
            s, w = carry
            k = jnp.where(d == 0, step, count - 1 - step)
            rows = [pl.ds(base + b * count + k, 1) for b in range(BATCH)]
            for b in range(BATCH):
                s_ref[rows[b], :] = s[b:b + 1, :]
            z = jnp.concatenate([z_ref[r, :] for r in rows], axis=0)
            zs = jnp.concatenate([zs_scr[r, :] for r in rows], axis=0)
            return a_mul * s + a_swp * w + z, a_mul * w - a_swp * s + zs
        return lax.fori_loop(0, count, body, carry)

    zero = jnp.zeros((BATCH, lanes), F32)
    carry = segment(BATCH * nl, nc, (zero, zero))
    segment(0, nl, carry)


def _s5_y_kernel(u_ref, klag_ref, s_ref, q_ref, dv_ref, y_ref, m_scr):
    d = pl.program_id(1)
    n = N_TOK // S5_CHUNK
    row = lax.broadcasted_iota(jnp.int32, (128, 128), 0)
    col = lax.broadcasted_iota(jnp.int32, (128, 128), 1)
    same_group = row // S5_GROUP == col // S5_GROUP
    blocks = [jnp.where(same_group, jnp.concatenate([klag_ref[lag]] * S5_GSUB, axis=0), jnp.zeros((128, 128), BF16))
              for lag in range(S5_CHUNK)]
    for rev in range(2):
        @pl.when(d == rev)
        def _():
            for s in range(S5_CHUNK):
                for t in range(S5_CHUNK):
                    lag = (s - t) if rev else (t - s)
                    blk = blocks[lag] if lag >= 0 else jnp.zeros((128, 128), BF16)
                    m_scr[s * 128:(s + 1) * 128, t * 128:(t + 1) * 128] = blk
    y = jnp.dot(_s5_chunk_rows(u_ref), m_scr[...], preferred_element_type=F32)
    y = y + jnp.dot(s_ref[...].astype(BF16), _s5_expand_q(q_ref), preferred_element_type=F32)
    for t in range(S5_CHUNK):
        rows = pl.ds(t, n, stride=S5_CHUNK)
        yt = y[:, t * 128:(t + 1) * 128]

        @pl.when(d == 0)
        def _():
            y_ref[rows, :] = yt + u_ref[rows, :] * dv_ref[...]

        @pl.when(d == 1)
        def _():
            y_ref[rows, :] = y_ref[rows, :] + yt


def _s5(p, mats, layer, dskip):
    nstate = S5_GROUPS * 2 * S5_STATE
    qlanes = nstate // S5_QUADS
    slanes = qlanes
    ucol = (C_Q_RANK + C_KV_RANK) // 128
    z = pl.pallas_call(
        _s5_z_kernel,
        out_shape=jax.ShapeDtypeStruct((2, S5_ROWS, nstate), F32),
        grid=(2, S5_QUADS),
        in_specs=[
            pl.BlockSpec((N_TOK, 128), lambda d, q: (0, ucol + q)),
            pl.BlockSpec((None, None, None, S5_CHUNK * 128, 128), lambda d, q: (layer, d, q, 0, 0)),
        ],
        out_specs=pl.BlockSpec((None, S5_ROWS, qlanes), lambda d, q: (d, 0, q)),
        compiler_params=_cparams(("arbitrary", "arbitrary")),
        name="s5_chunk_state",
    )(p, mats["p"])
    s = pl.pallas_call(
        _s5_scan_kernel,
        out_shape=jax.ShapeDtypeStruct((2, S5_ROWS, nstate), F32),
        grid=(2, nstate // slanes),
        in_specs=[
            pl.BlockSpec((None, S5_ROWS, slanes), lambda d, q: (d, 0, q)),
            pl.BlockSpec((None, None, 1, slanes), lambda d, q: (layer, d, 0, q)),
            pl.BlockSpec((None, None, 1, slanes), lambda d, q: (layer, d, 0, q)),
        ],
        out_specs=pl.BlockSpec((None, S5_ROWS, slanes), lambda d, q: (d, 0, q)),
        scratch_shapes=[pltpu.VMEM((S5_ROWS, slanes), F32)],
        compiler_params=_cparams(("arbitrary", "arbitrary")),
        name="s5_scan",
    )(z, mats["a_mul"], mats["a_swp"])
    return pl.pallas_call(
        _s5_y_kernel,
        out_shape=jax.ShapeDtypeStruct((N_TOK, D_WIDTH), F32),
        grid=(S5_QUADS, 2),
        in_specs=[
            pl.BlockSpec((N_TOK, 128), lambda q, d: (0, ucol + q)),
            pl.BlockSpec((None, None, None, S5_CHUNK, S5_GROUP, 128), lambda q, d: (layer, d, q, 0, 0, 0)),
            pl.BlockSpec((None, S5_ROWS, qlanes), lambda q, d: (d, 0, q)),
            pl.BlockSpec((None, None, None, S5_CHUNK, 128, 128), lambda q, d: (layer, d, q, 0, 0, 0)),
            pl.BlockSpec((None, 1, 128), lambda q, d: (layer, 0, q)),
        ],
        out_specs=pl.BlockSpec((N_TOK, 128), lambda q, d: (0, q)),
        scratch_shapes=[pltpu.VMEM((S5_CHUNK * 128, S5_CHUNK * 128), BF16)],
        compiler_params=_cparams(("arbitrary", "arbitrary"), VMEM_LIMIT_LARGE),
        name="s5_output",
    )(p, mats["klag"], s, mats["q"], dskip)


def _glu_kernel(y_ref, w_ref, o_ref):
    y = y_ref[...]
    z = y * (0.5 * (1.0 + jnp.tanh(math.sqrt(2.0 / math.pi) * (y + 0.044715 * (y * y * y)))))
    gate = jnp.dot(z.astype(BF16), w_ref[...], preferred_element_type=F32)
    o_ref[...] = (z * jax.nn.sigmoid(gate)).astype(BF16)


def _s5_glu(y, w_glu, layer):
    t = 512
    return pl.pallas_call(
        _glu_kernel,
        out_shape=jax.ShapeDtypeStruct((N_TOK, D_WIDTH), BF16),
        grid=(N_TOK // t,),
        in_specs=[pl.BlockSpec((t, D_WIDTH), lambda i: (i, 0)),
                  pl.BlockSpec((None, D_WIDTH, D_WIDTH), lambda i: (layer, 0, 0))],
        out_specs=pl.BlockSpec((t, D_WIDTH), lambda i: (i, 0)),
        compiler_params=_cparams(("arbitrary",)),
        name="s5_glu",
    )(y, w_glu)


LANE_CHUNKS = D_MODEL // 128


def _store_token_major(ref, val):
    rows = val.shape[0]
    for c in range(LANE_CHUNKS):
        ref[pl.ds(c, rows, stride=LANE_CHUNKS), :] = val[:, c * 128:(c + 1) * 128]


def _load_token_major(ref, rows):
    return jnp.concatenate([ref[pl.ds(c, rows, stride=LANE_CHUNKS), :] for c in range(LANE_CHUNKS)], axis=-1)


def _router_kernel(x_ref, g_ref, sh_ref, sc_ref, w_ref, h_ref, info_ref, gw_ref, cnt_ref, h_scr, carry):
    i = pl.program_id(0)

    @pl.when(i == 0)
    def _():
        carry[...] = jnp.zeros_like(carry)

    for r in range(0, TM, ROW_CHUNK):
        rows = slice(r, r + ROW_CHUNK)
        h_scr[rows, :] = _modulated_norm(x_ref[rows, :], g_ref[...], sh_ref[...], sc_ref[...])
    _store_token_major(h_ref, h_scr[...])
    h = h_scr[...]
    h_hi = h.astype(BF16)
    h_lo = (h - h_hi.astype(F32)).astype(BF16)
    both = jnp.dot(h_hi, w_ref[...], preferred_element_type=F32)
    logits = both[:, :128] + both[:, 128:] + jnp.dot(h_lo, w_ref[:, :128], preferred_element_type=F32)
    lane = lax.broadcasted_iota(jnp.int32, logits.shape, 1)
    neg = -jnp.inf
    big = jnp.int32(1 << 20)

    def first_argmax(v, vmax):
        return jnp.min(jnp.where(v == vmax, lane, big), axis=-1, keepdims=True)

    lg = jnp.where(lane < N_GROUPS, logits, neg)
    mg = jnp.max(lg, axis=-1, keepdims=True)
    g_w = 1.0 / jnp.sum(jnp.exp(lg - mg), axis=-1, keepdims=True)
    g_idx = first_argmax(lg, mg)
    lo = N_GROUPS + EXPERTS_PER_GROUP * g_idx
    le = jnp.where(jnp.logical_and(lane >= lo, lane < lo + EXPERTS_PER_GROUP), logits, neg)
    m1 = jnp.max(le, axis=-1, keepdims=True)
    i1 = first_argmax(le, m1)
    le2 = jnp.where(lane == i1, neg, le)
    m2 = jnp.max(le2, axis=-1, keepdims=True)
    i2 = first_argmax(le2, m2)
    r21 = jnp.exp(m2 - m1)
    w1 = g_w / (1.0 + r21)
    w2 = g_w * r21 / (1.0 + r21)
    oh = jnp.logical_or(lane == i1, lane == i2)
    ohb = jnp.where(oh, 1.0, 0.0).astype(BF16)
    rr = lax.broadcasted_iota(jnp.int32, (TM, TM), 0)
    cc = lax.broadcasted_iota(jnp.int32, (TM, TM), 1)
    lower = jnp.where(rr > cc, 1.0, 0.0).astype(BF16)
    before = jnp.dot(lower, ohb, preferred_element_type=F32) + carry[...]
    rank1 = jnp.sum(jnp.where(lane == i1, before, 0.0), axis=-1, keepdims=True).astype(jnp.int32)
    rank2 = jnp.sum(jnp.where(lane == i2, before, 0.0), axis=-1, keepdims=True).astype(jnp.int32)
    carry[...] = carry[...] + jnp.sum(ohb.astype(F32), axis=0, keepdims=True)
    cnt_ref[...] = jnp.broadcast_to(carry[...], cnt_ref.shape)
    info = jnp.where(lane == 0, i1 - N_GROUPS, jnp.where(lane == 1, i2 - N_GROUPS,
                     jnp.where(lane == 2, rank1, jnp.where(lane == 3, rank2, 0))))
    info_ref[...] = info
    gw_ref[...] = jnp.where(lane == 0, w1, jnp.where(lane == 1, w2, 0.0))


def _router(x, g, mod, w_router, layer):
    rows = x.shape[0]
    return pl.pallas_call(
        _router_kernel,
        out_shape=[jax.ShapeDtypeStruct((rows * LANE_CHUNKS, 128), F32),
                   jax.ShapeDtypeStruct((rows, 128), jnp.int32),
                   jax.ShapeDtypeStruct((rows, 128), F32),
                   jax.ShapeDtypeStruct((8, 128), F32)],
        grid=(rows // TM,),
        in_specs=[
            pl.BlockSpec((TM, D_MODEL), lambda i: (i, 0)),
            pl.BlockSpec((1, D_MODEL), lambda i: (0, 0)),
            pl.BlockSpec((None, None, 1, D_MODEL), lambda i: (_mod_row(i, TM), 3, 0, 0)),
            pl.BlockSpec((None, None, 1, D_MODEL), lambda i: (_mod_row(i, TM), 4, 0, 0)),
            pl.BlockSpec((None, D_MODEL, 256), lambda i: (layer, 0, 0)),
        ],
        out_specs=[pl.BlockSpec((TM * LANE_CHUNKS, 128), lambda i: (i, 0)),
                   pl.BlockSpec((TM, 128), lambda i: (i, 0)),
                   pl.BlockSpec((TM, 128), lambda i: (i, 0)),
                   pl.BlockSpec((8, 128), lambda i: (0, 0))],
        scratch_shapes=[pltpu.VMEM((TM, D_MODEL), F32), pltpu.VMEM((1, 128), F32)],
        compiler_params=_cparams(("arbitrary",)),
        name="moe_router",
    )(x, g, mod, mod, w_router)


FFN_ISSUE_GROUPS = 8
GATHER_SLOTS = 4
GATHER_AHEAD = GATHER_SLOTS - 1
ISSUE_UNROLL = 8


def _ffn_kernel(pos_ref, te_ref, meta_ref, h_hbm, wg_ref, wu_ref, wd_ref, o_ref,
                src, xbuf, wg_b, wu_b, wd_b, sem):
    t = pl.program_id(0)
    nt = meta_ref[0]

    def row_copy(tile, slot, r):
        tok = src[tile * TE + r]
        return pltpu.make_async_copy(
            h_hbm.at[pl.ds(pl.multiple_of(tok * LANE_CHUNKS, LANE_CHUNKS), LANE_CHUNKS), :],
            xbuf.at[slot, pl.ds(pl.multiple_of(r * LANE_CHUNKS, LANE_CHUNKS), LANE_CHUNKS), :],
            sem.at[slot])

    def gather(tile, slot):
        def body(r, _):
            row_copy(tile, slot, r).start()
            return 0
        lax.fori_loop(0, TE, body, 0, unroll=ISSUE_UNROLL)

    def wait_tile(slot):
        pltpu.make_async_copy(xbuf.at[slot], xbuf.at[slot], sem.at[slot]).wait()

    @pl.when(t == 0)
    def _():
        for e in range(N_EXPERTS):
            def clear(i, _):
                src[i] = 0
                return 0
            lax.fori_loop(meta_ref[1 + e], meta_ref[1 + N_EXPERTS + e], clear, 0)

        def fill(tok, _):
            src[pos_ref[2 * tok]] = tok
            src[pos_ref[2 * tok + 1]] = tok
            return 0
        lax.fori_loop(0, pos_ref.shape[0] // 2, fill, 0, unroll=8)
        for k in range(GATHER_AHEAD):
            gather(k, k)

    @pl.when(t >= nt)
    def _():
        o_ref[...] = jnp.zeros_like(o_ref)

    @pl.when(t < nt)
    def _():
        slot = t % GATHER_SLOTS
        ahead = (t + GATHER_AHEAD) % GATHER_SLOTS
        wait_tile(slot)

        @pl.when(jnp.logical_or(t == 0, te_ref[t] != te_ref[jnp.maximum(t - 1, 0)]))
        def _():
            wg_b[...] = wg_ref[...].astype(BF16)
            wu_b[...] = wu_ref[...].astype(BF16)
            wd_b[...] = wd_ref[...].astype(BF16)

        nxt = jnp.minimum(t + GATHER_AHEAD, nt - 1)
        per = TE // FFN_ISSUE_GROUPS

        def issue(g):
            for r in range(g * per, (g + 1) * per):
                row_copy(nxt, ahead, r).start()

        x = _load_token_major(xbuf.at[slot], TE).astype(BF16)
        half = D_EXPERT // 2
        hg, hu = [], []
        for n in range(2):
            issue(n)
            hg.append(jnp.dot(x, wg_b[:, n * half:(n + 1) * half], preferred_element_type=F32))
        for n in range(2):
            issue(2 + n)
            hu.append(jnp.dot(x, wu_b[:, n * half:(n + 1) * half], preferred_element_type=F32))
        hg, hu = jnp.concatenate(hg, axis=1), jnp.concatenate(hu, axis=1)
        act = (hg * jax.nn.sigmoid(hg) * hu).astype(BF16)
        quarter = D_MODEL // 4
        for n in range(4):
            issue(4 + n)
            y = jnp.dot(act, wd_b[:, n * quarter:(n + 1) * quarter], preferred_element_type=F32)
            for c in range(quarter // 128):
                o_ref[pl.ds(n * (quarter // 128) + c, TE, stride=LANE_CHUNKS), :] = y[:, c * 128:(c + 1) * 128]

        @pl.when(t == nt - 1)
        def _():
            for k in range(1, GATHER_AHEAD + 1):
                wait_tile((t + k) % GATHER_SLOTS)


def _expert_ffn(pos_flat, tile_expert, meta, h, w_gate, w_up, w_down, layer):
    wsel = lambda t, pos, te, meta: (layer, te[t], 0, 0)
    return pl.pallas_call(
        _ffn_kernel,
        out_shape=jax.ShapeDtypeStruct((N_SORT * LANE_CHUNKS, 128), F32),
        grid_spec=pltpu.PrefetchScalarGridSpec(
            num_scalar_prefetch=3,
            grid=(N_ETILES,),
            in_specs=[
                pl.BlockSpec(memory_space=pl.ANY),
                pl.BlockSpec((None, None, D_MODEL, D_EXPERT), wsel),
                pl.BlockSpec((None, None, D_MODEL, D_EXPERT), wsel),
                pl.BlockSpec((None, None, D_EXPERT, D_MODEL), wsel),
            ],
            out_specs=pl.BlockSpec((TE * LANE_CHUNKS, 128), lambda t, pos, te, meta: (t, 0)),
            scratch_shapes=[
                pltpu.SMEM((N_SORT,), jnp.int32),
                pltpu.VMEM((GATHER_SLOTS, TE * LANE_CHUNKS, 128), F32),
                pltpu.VMEM((D_MODEL, D_EXPERT), BF16),
                pltpu.VMEM((D_MODEL, D_EXPERT), BF16),
                pltpu.VMEM((D_EXPERT, D_MODEL), BF16),
                pltpu.SemaphoreType.DMA((GATHER_SLOTS,)),
            ],
        ),
        compiler_params=_cparams(("arbitrary",), VMEM_LIMIT_LARGE),
        name="moe_expert_ffn",
    )(pos_flat, tile_expert, meta, h, w_gate, w_up, w_down)


TC = 256


def _combine_kernel(pos_ref, x_ref, gate_ref, gw_ref, g_ref, sh_ref, sc_ref, ys_hbm, *refs, final_norm):
    if final_norm:
        o_ref, buf_a, buf_b, sem = refs
    else:
        o_ref, h_ref, buf_a, buf_b, sem = refs
    i = pl.program_id(0)
    slot = i % 2

    def slab(ref, row):
        return ref.at[pl.ds(pl.multiple_of(row * LANE_CHUNKS, LANE_CHUNKS), LANE_CHUNKS), :]

    def gather(tile, slot):
        def body(r, _):
            tok = tile * TC + r
            pltpu.make_async_copy(slab(ys_hbm, pos_ref[2 * tok]), slab(buf_a.at[slot], r), sem.at[slot, 0]).start()
            pltpu.make_async_copy(slab(ys_hbm, pos_ref[2 * tok + 1]), slab(buf_b.at[slot], r), sem.at[slot, 1]).start()
            return 0
        lax.fori_loop(0, TC, body, 0, unroll=ISSUE_UNROLL)

    @pl.when(i == 0)
    def _():
        gather(0, 0)

    @pl.when(i + 1 < pl.num_programs(0))
    def _():
        gather(i + 1, 1 - slot)

    pltpu.make_async_copy(buf_a.at[slot], buf_a.at[slot], sem.at[slot, 0]).wait()
    pltpu.make_async_copy(buf_b.at[slot], buf_b.at[slot], sem.at[slot, 1]).wait()
    w0 = jnp.broadcast_to(gw_ref[:, 0:1], (TC, 128))
    w1 = jnp.broadcast_to(gw_ref[:, 1:2], (TC, 128))
    sq = jnp.zeros((TC, 128), F32)
    for c in range(LANE_CHUNKS):
        cols = slice(c * 128, (c + 1) * 128)
        rows = pl.ds(c, TC, stride=LANE_CHUNKS)
        y = x_ref[:, cols] + gate_ref[:, cols] * (w0 * buf_a[slot, rows, :] + w1 * buf_b[slot, rows, :])
        o_ref[:, cols] = y
        sq = sq + y * y
    inv = lax.rsqrt(jnp.sum(sq, axis=-1, keepdims=True) / D_MODEL + EPS)
    for c in range(LANE_CHUNKS):
        cols = slice(c * 128, (c + 1) * 128)
        normed = o_ref[:, cols] * inv * g_ref[:, cols]
        if final_norm:
            o_ref[:, cols] = normed
        else:
            h_ref[:, cols] = (normed * (1.0 + sc_ref[:, cols]) + sh_ref[:, cols]).astype(BF16)


def _combine(pos_flat, x, mod, gw, ys, g, mod_next):
    final_norm = mod_next is None
    rows = x.shape[0]
    tile = pl.BlockSpec((TC, D_MODEL), lambda i, pos: (i, 0))
    mod_row = lambda which: pl.BlockSpec((None, None, 1, D_MODEL), lambda i, pos: (_mod_row(i, TC), which, 0, 0))
    next_tab = mod if final_norm else mod_next
    out_shape = [jax.ShapeDtypeStruct((rows, D_MODEL), F32)]
    if not final_norm:
        out_shape.append(jax.ShapeDtypeStruct((rows, D_MODEL), BF16))
    return pl.pallas_call(
        functools.partial(_combine_kernel, final_norm=final_norm),
        out_shape=out_shape,
        grid_spec=pltpu.PrefetchScalarGridSpec(
            num_scalar_prefetch=1,
            grid=(rows // TC,),
            in_specs=[
                tile,
                mod_row(5),
                pl.BlockSpec((TC, 128), lambda i, pos: (i, 0)),
                pl.BlockSpec((1, D_MODEL), lambda i, pos: (0, 0)),
                mod_row(0), mod_row(1),
                pl.BlockSpec(memory_space=pl.ANY),
            ],
            out_specs=[tile] * len(out_shape),
            scratch_shapes=[pltpu.VMEM((2, TC * LANE_CHUNKS, 128), F32), pltpu.VMEM((2, TC * LANE_CHUNKS, 128), F32),
                            pltpu.SemaphoreType.DMA((2, 2))],
        ),
        compiler_params=_cparams(("arbitrary",)),
        name="moe_combine",
    )(pos_flat, x, mod, gw, g, next_tab, next_tab, ys)


def _moe(x, g, mod, w_router, w_gate, w_up, w_down, layer, g_next, mod_next):
    h, info, gw, cnt = _router(x, g, mod, w_router, layer)
    counts = cnt[0, N_GROUPS:N_GROUPS + N_EXPERTS].astype(jnp.int32)
    padded = ((counts + TE - 1) // TE) * TE
    ends = jnp.cumsum(padded)
    starts = ends - padded
    experts = jnp.arange(N_EXPERTS, dtype=jnp.int32)
    start_of = jnp.sum(jnp.where(info[:, 0:2, None] == experts, starts, 0), axis=-1)
    pos_flat = (start_of + info[:, 2:4]).reshape(-1)
    tile_ends = ends // TE
    num_tiles = tile_ends[-1]
    tiles = jnp.minimum(jnp.arange(N_ETILES, dtype=jnp.int32), num_tiles - 1)
    tile_expert = jnp.sum((tile_ends[None, :] <= tiles[:, None]).astype(jnp.int32), axis=-1)
    meta = jnp.concatenate([num_tiles[None], starts + counts, ends]).astype(jnp.int32)
    ys = _expert_ffn(pos_flat, tile_expert, meta, h, w_gate, w_up, w_down, layer)
    return _combine(pos_flat, x, mod, gw, ys, g_next, mod_next)


def kernel(x, c, ctx, c_ctx, mod_w, mod_b, norm_mix, norm_ffn, ab_w_in, ab_q_norm, ab_k_norm, ab_conv_w, ab_w_out, cd_w_in, cd_q_norm, cd_kv_norm, cd_w_uq, cd_w_ukv, s5_lam_re, s5_lam_im, s5_log_dt, s5_b_re, s5_b_im, s5_c_re, s5_c_im, s5_d, s5_w_glu, cd_w_out, moe_w_group, moe_w_expert, moe_w_gate, moe_w_up, moe_w_down, final_norm):
    cc = jnp.concatenate([c, c_ctx[None, :], jnp.zeros((8 - BATCH - 1, D_MODEL), F32)], axis=0)
    mods = _modulation(cc, mod_w, mod_b).reshape(DEPTH, 8, N_MOD, 1, D_MODEL)
    xs, h = _embed(x.reshape(N_LAT, D_MODEL), ctx.reshape(N_CTX, D_MODEL), norm_mix[0][None, :], mods[0])
    cos_a, sin_a = _rope_tables(HEAD_DIM)
    cos_c, sin_c = _rope_tables(C_ROPE)

    ab_in_b, ab_out_b = ab_w_in.astype(BF16), ab_w_out.astype(BF16)
    cd_out_b, ukv_b, glu_b = cd_w_out.astype(BF16), cd_w_ukv.astype(BF16), s5_w_glu.astype(BF16)
    a, b_ = C_Q_RANK + C_KV_RANK, C_Q_RANK + C_KV_RANK + C_ROPE
    pad = jnp.zeros(cd_w_in.shape[:2] + (CD_IN_PAD - cd_w_in.shape[2],), F32)
    cd_in_b = jnp.concatenate([cd_w_in[..., :a], cd_w_in[..., b_:], cd_w_in[..., a:b_], pad], axis=-1).astype(BF16)
    w_uq = cd_w_uq.reshape(-1, C_Q_RANK, C_HEADS, C_NOPE + C_ROPE)
    uq_b = jnp.concatenate([w_uq[..., :C_NOPE].reshape(-1, C_Q_RANK, C_HEADS * C_NOPE),
                            w_uq[..., C_NOPE:].reshape(-1, C_Q_RANK, C_HEADS * C_ROPE)], axis=-1).astype(BF16)

    mats = jax.vmap(_s5_matrices)(s5_lam_re, s5_lam_im, s5_log_dt, s5_b_re, s5_b_im, s5_c_re, s5_c_im)
    w_router = jnp.concatenate(
        [moe_w_group, jnp.transpose(moe_w_expert, (0, 2, 1, 3)).reshape(DEPTH, D_MODEL, N_EXPERTS),
         jnp.zeros((DEPTH, D_MODEL, 128 - N_GROUPS - N_EXPERTS), F32)], axis=-1)
    w_router_hi = w_router.astype(BF16)
    w_router = jnp.concatenate([w_router_hi, (w_router - w_router_hi.astype(F32)).astype(BF16)], axis=-1)

    for i in range(DEPTH):
        j = i // 2
        mod = mods[i]
        if i % 2 == 0:
            p = _linear(h, ab_in_b, j, 2048)
            k, vext = _gqa_kv(p, ab_k_norm[j][None, :], cos_a, sin_a)
            o_lat, o_ctx = _gqa_attention(p, k, vext, ab_q_norm[j][None, :], cos_a, sin_a)
            side = _gated_conv(p, ab_conv_w[j])
            w_out = ab_out_b
        else:
            p = _linear(h, cd_in_b, j, CD_IN_PAD)
            q = _norm_linear(p, 0, C_Q_RANK, cd_q_norm[j][None, :], uq_b, j)
            kcat, vext = _mla_kv(p, cd_kv_norm[j][None, :], ukv_b, j, cos_c, sin_c)
            o_lat, o_ctx = _mla_attention(q, kcat, vext, cos_c, sin_c)
            y = _s5(p, mats, j, s5_d.astype(F32)[:, None, :])
            side = _s5_glu(y, glu_b, j)
            w_out = cd_out_b
        xs = _out_linear(o_lat, o_ctx, side, w_out, j, xs, mod, 2, N_TOK if i < DEPTH - 1 else N_LAT)
        if i < DEPTH - 1:
            xs, h = _moe(xs, norm_ffn[i][None, :], mod, w_router, moe_w_gate, moe_w_up, moe_w_down, i,
                         norm_mix[i + 1][None, :], mods[i + 1])
        else:
            out, = _moe(xs, norm_ffn[i][None, :], mod, w_router, moe_w_gate, moe_w_up, moe_w_down, i,
                        final_norm[None, :], None)
    return out.reshape(BATCH, SEQ, D_MODEL)
```

```python
import functools
import math

import numpy as np
import jax
import jax.numpy as jnp
from jax import lax
from jax.experimental import pallas as pl
from jax.experimental.pallas import tpu as pltpu

F32 = jnp.float32
BF16 = jnp.bfloat16

D_MODEL = 2048
BATCH = 4
SEQ = 2048
DEPTH = 4
GRID_W = 64
CTX_LEN = 256
ROPE_THETA = 10000.0
EPS = 1e-6
N_MOD = 6
HEAD_DIM = 128
A_Q_HEADS = 12
A_KV_HEADS = 4
A_GROUP = A_Q_HEADS // A_KV_HEADS
A_WIDTH = A_Q_HEADS * HEAD_DIM
A_KV_WIDTH = A_KV_HEADS * HEAD_DIM
B_WIDTH = 512
AB_IN = A_WIDTH + 2 * A_KV_WIDTH + 3 * B_WIDTH
C_HEADS = 12
C_NOPE = 128
C_ROPE = 64
C_V = 128
C_Q_RANK = 512
C_KV_RANK = 256
C_WIDTH = C_HEADS * C_V
D_WIDTH = 512
S5_GROUP = 16
S5_GROUPS = D_WIDTH // S5_GROUP
S5_STATE = 64
N_GROUPS = 4
EXPERTS_PER_GROUP = 4
N_EXPERTS = N_GROUPS * EXPERTS_PER_GROUP
D_EXPERT = 512

N_LAT = BATCH * SEQ
N_CTX = BATCH * CTX_LEN
N_TOK = N_LAT + N_CTX
CTX_ROW = BATCH
CD_IN_PAD = 1536

TM = 512
TM_LINEAR = 1024
TQ = 1024
TE = 256
ROPE_IDENTITY_ROWS = max(TM, TQ)
N_SORT = 2 * N_TOK + N_EXPERTS * TE
N_ETILES = N_SORT // TE
S5_CHUNK = 16
S5_CTX_CHUNKS = CTX_LEN // S5_CHUNK
S5_ROWS = N_TOK // S5_CHUNK
S5_GSUB = 128 // S5_GROUP
S5_QUADS = S5_GROUPS // S5_GSUB
VMEM_LIMIT = 48 * 1024 * 1024
VMEM_LIMIT_LARGE = 56 * 1024 * 1024
ROW_CHUNK = 256


def _cparams(sem, limit=VMEM_LIMIT):
    return pltpu.CompilerParams(dimension_semantics=sem, vmem_limit_bytes=limit)


def _mod_row(tile, tile_rows):
    r0 = tile * tile_rows
    return jnp.where(r0 >= N_LAT, CTX_ROW, r0 // SEQ)


def _rope_block(tile, tile_rows):
    r0 = tile * tile_rows
    return jnp.where(r0 >= N_LAT, SEQ // tile_rows, (r0 % SEQ) // tile_rows)


def _rms(x):
    return x * lax.rsqrt(jnp.mean(x * x, axis=-1, keepdims=True) + EPS)


def _mod_kernel(cc_ref, w_ref, b_ref, o_ref):
    cc = cc_ref[...]
    s = (cc * jax.nn.sigmoid(cc)).astype(BF16)
    o_ref[...] = jnp.dot(s, w_ref[...].astype(BF16), preferred_element_type=F32) + b_ref[...]


def _modulation(cc, mod_w, mod_b):
    tn = 1024
    nout = N_MOD * D_MODEL
    return pl.pallas_call(
        _mod_kernel,
        out_shape=jax.ShapeDtypeStruct((DEPTH, 8, nout), F32),
        grid=(DEPTH, nout // tn),
        in_specs=[
            pl.BlockSpec((8, D_MODEL), lambda l, j: (0, 0)),
            pl.BlockSpec((None, D_MODEL, tn), lambda l, j: (l, 0, j)),
            pl.BlockSpec((None, 1, tn), lambda l, j: (l, 0, j)),
        ],
        out_specs=pl.BlockSpec((None, 8, tn), lambda l, j: (l, 0, j)),
        compiler_params=_cparams(("arbitrary", "arbitrary")),
        name="modulation",
    )(cc, mod_w, mod_b.reshape(DEPTH, 1, nout))


def _modulated_norm(x, g, sh, sc):
    return _rms(x) * g * (1.0 + sc) + sh


def _embed_kernel(xl_ref, xc_ref, g_ref, sh_ref, sc_ref, xs_ref, h_ref):
    is_ctx = pl.program_id(0) >= N_LAT // TM
    for r in range(0, TM, ROW_CHUNK):
        rows = slice(r, r + ROW_CHUNK)
        x = jnp.where(is_ctx, xc_ref[rows, :], xl_ref[rows, :])
        xs_ref[rows, :] = x
        h_ref[rows, :] = _modulated_norm(x, g_ref[...], sh_ref[...], sc_ref[...]).astype(BF16)


def _embed(x_lat, x_ctx, g, mod):
    lat_tiles = N_LAT // TM
    return pl.pallas_call(
        _embed_kernel,
        out_shape=[jax.ShapeDtypeStruct((N_TOK, D_MODEL), F32), jax.ShapeDtypeStruct((N_TOK, D_MODEL), BF16)],
        grid=(N_TOK // TM,),
        in_specs=[
            pl.BlockSpec((TM, D_MODEL), lambda i: (jnp.minimum(i, lat_tiles - 1), 0)),
            pl.BlockSpec((TM, D_MODEL), lambda i: (jnp.maximum(i - lat_tiles, 0), 0)),
            pl.BlockSpec((1, D_MODEL), lambda i: (0, 0)),
            pl.BlockSpec((None, None, 1, D_MODEL), lambda i: (_mod_row(i, TM), 0, 0, 0)),
            pl.BlockSpec((None, None, 1, D_MODEL), lambda i: (_mod_row(i, TM), 1, 0, 0)),
        ],
        out_specs=[pl.BlockSpec((TM, D_MODEL), lambda i: (i, 0))] * 2,
        compiler_params=_cparams(("arbitrary",)),
        name="embed_norm",
    )(x_lat, x_ctx, g, mod, mod)


def _linear_kernel(h_ref, w_ref, o_ref):
    o_ref[...] = jnp.dot(h_ref[...], w_ref[...], preferred_element_type=F32)


def _linear(h, w, layer, tn):
    kdim, nout = w.shape[1], w.shape[2]
    return pl.pallas_call(
        _linear_kernel,
        out_shape=jax.ShapeDtypeStruct((N_TOK, nout), F32),
        grid=(N_TOK // TM_LINEAR, nout // tn),
        in_specs=[
            pl.BlockSpec((TM_LINEAR, kdim), lambda i, j: (i, 0)),
            pl.BlockSpec((None, kdim, tn), lambda i, j: (layer, 0, j)),
        ],
        out_specs=pl.BlockSpec((TM_LINEAR, tn), lambda i, j: (i, j)),
        compiler_params=_cparams(("arbitrary", "arbitrary")),
        name="linear",
    )(h, w)


def _norm_linear_kernel(x_ref, g_ref, w_ref, o_ref):
    h = (_rms(x_ref[...]) * g_ref[...]).astype(BF16)
    o_ref[...] = jnp.dot(h, w_ref[...], preferred_element_type=F32)


def _norm_linear(x, xcol, kdim, g, w, layer):
    nout = w.shape[2]
    return pl.pallas_call(
        _norm_linear_kernel,
        out_shape=jax.ShapeDtypeStruct((N_TOK, nout), F32),
        grid=(N_TOK // TM,),
        in_specs=[
            pl.BlockSpec((TM, kdim), lambda i: (i, xcol)),
            pl.BlockSpec((1, kdim), lambda i: (0, 0)),
            pl.BlockSpec((None, kdim, nout), lambda i: (layer, 0, 0)),
        ],
        out_specs=pl.BlockSpec((TM, nout), lambda i: (i, 0)),
        compiler_params=_cparams(("arbitrary",)),
        name="norm_linear",
    )(x, g, w)


def _out_linear_kernel(a1l_ref, a1c_ref, a2_ref, w1_ref, w2_ref, x_ref, gate_ref, o_ref):
    a1 = jnp.where(pl.program_id(0) >= N_LAT // TM, a1c_ref[...], a1l_ref[...])
    acc = jnp.dot(a1, w1_ref[...], preferred_element_type=F32)
    acc = acc + jnp.dot(a2_ref[...], w2_ref[...], preferred_element_type=F32)
    o_ref[...] = x_ref[...] + gate_ref[...] * acc


def _out_linear(a1_lat, a1_ctx, a2, w, layer, x, mod, which, rows):
    tn = D_MODEL
    k1, k2 = a1_lat.shape[1], a2.shape[1]
    lat_tiles = N_LAT // TM
    return pl.pallas_call(
        _out_linear_kernel,
        out_shape=jax.ShapeDtypeStruct((rows, D_MODEL), F32),
        grid=(rows // TM, D_MODEL // tn),
        in_specs=[
            pl.BlockSpec((TM, k1), lambda i, j: (jnp.minimum(i, lat_tiles - 1), 0)),
            pl.BlockSpec((TM, k1), lambda i, j: (jnp.maximum(i - lat_tiles, 0), 0)),
            pl.BlockSpec((TM, k2), lambda i, j: (i, 0)),
            pl.BlockSpec((None, k1, tn), lambda i, j: (layer, 0, j)),
            pl.BlockSpec((None, k2, tn), lambda i, j: (layer, k1 // k2, j)),
            pl.BlockSpec((TM, tn), lambda i, j: (i, j)),
            pl.BlockSpec((None, None, 1, tn), lambda i, j: (_mod_row(i, TM), which, 0, j)),
        ],
        out_specs=pl.BlockSpec((TM, tn), lambda i, j: (i, j)),
        compiler_params=_cparams(("arbitrary", "arbitrary")),
        name="out_linear",
    )(a1_lat, a1_ctx, a2, w, w, x, mod)


def _rope_tables(rot_dim):
    rows = SEQ // GRID_W
    row_ids = np.repeat(np.arange(rows, dtype=np.float32), GRID_W)
    col_ids = np.tile(np.arange(GRID_W, dtype=np.float32), rows)
    d_axis = rot_dim // 2
    inv = (np.float32(ROPE_THETA) ** (-np.arange(0, d_axis, 2, dtype=np.float32) / np.float32(d_axis))).astype(np.float32)
    ang = np.concatenate([row_ids[:, None] * inv, col_ids[:, None] * inv], axis=-1).astype(np.float32)
    cos, sin = np.cos(ang).astype(np.float32), np.sin(ang).astype(np.float32)
    reps = 128 // rot_dim
    cos_f = np.tile(np.concatenate([cos, cos], axis=-1), (1, reps))
    sin_f = np.tile(np.concatenate([-sin, sin], axis=-1), (1, reps))
    cos_f = np.concatenate([cos_f, np.ones((ROPE_IDENTITY_ROWS, 128), np.float32)], axis=0)
    sin_f = np.concatenate([sin_f, np.zeros((ROPE_IDENTITY_ROWS, 128), np.float32)], axis=0)
    return jnp.asarray(cos_f), jnp.asarray(sin_f)


def _rope128(x, cos, sin):
    return x * cos + pltpu.roll(x, 64, 1) * sin


def _rope64(x, cos, sin):
    lane = lax.broadcasted_iota(jnp.int32, x.shape, 1)
    swapped = jnp.where((lane % 64) < 32, pltpu.roll(x, 96, 1), pltpu.roll(x, 32, 1))
    return x * cos + swapped * sin


def _ones_column(rows):
    lane = lax.broadcasted_iota(jnp.int32, (rows, 128), 1)
    return jnp.where(lane == 0, 1.0, 0.0).astype(BF16)


def _gqa_kv_kernel(k_ref, v_ref, kn_ref, cos_ref, sin_ref, ko_ref, vo_ref):
    cos, sin = cos_ref[...], sin_ref[...]
    ones = _ones_column(k_ref.shape[0])
    for h in range(A_KV_HEADS):
        k = _rms(k_ref[:, h * 128:(h + 1) * 128]) * kn_ref[...]
        ko_ref[:, h * 128:(h + 1) * 128] = _rope128(k, cos, sin).astype(BF16)
        vo_ref[:, h * 256:h * 256 + 128] = v_ref[:, h * 128:(h + 1) * 128].astype(BF16)
        vo_ref[:, h * 256 + 128:(h + 1) * 256] = ones


def _gqa_kv(p, k_norm, cos, sin):
    t = 512
    return pl.pallas_call(
        _gqa_kv_kernel,
        out_shape=[jax.ShapeDtypeStruct((N_TOK, A_KV_WIDTH), BF16),
                   jax.ShapeDtypeStruct((N_TOK, 2 * A_KV_WIDTH), BF16)],
        grid=(N_TOK // t,),
        in_specs=[
            pl.BlockSpec((t, A_KV_WIDTH), lambda i: (i, A_WIDTH // A_KV_WIDTH)),
            pl.BlockSpec((t, A_KV_WIDTH), lambda i: (i, A_WIDTH // A_KV_WIDTH + 1)),
            pl.BlockSpec((1, 128), lambda i: (0, 0)),
            pl.BlockSpec((t, 128), lambda i: (_rope_block(i, t), 0)),
            pl.BlockSpec((t, 128), lambda i: (_rope_block(i, t), 0)),
        ],
        out_specs=[pl.BlockSpec((t, A_KV_WIDTH), lambda i: (i, 0)),
                   pl.BlockSpec((t, 2 * A_KV_WIDTH), lambda i: (i, 0))],
        compiler_params=_cparams(("arbitrary",)),
        name="gqa_kv",
    )(p, p, k_norm, cos, sin)


LOG2E = math.log2(math.e)
KEY_PIECE = 1024


def _softmax_pv(q, keys, vals):
    dn = (((1,), (1,)), ((), ()))
    m = acc = None
    for k, v in zip(keys, vals):
        s = lax.dot_general(q, k, dn, preferred_element_type=F32)
        m_piece = jnp.max(s, axis=-1, keepdims=True)
        if m is None:
            m_new = m_piece
        else:
            m_new = jnp.maximum(m, m_piece)
            acc = acc * jnp.exp2(m - m_new)
        part = jnp.dot(jnp.exp2(s - m_new).astype(BF16), v, preferred_element_type=F32)
        acc = part if acc is None else acc + part
        m = m_new
    return acc[:, :128] / acc[:, 128:129]


def _attn_qrow(b, t):
    nq = SEQ // TQ
    return jnp.where(t == nq, (N_LAT + b * CTX_LEN) // TQ, b * nq + t)


def _ctx_half():
    return pl.multiple_of((pl.program_id(0) % (TQ // CTX_LEN)) * CTX_LEN, CTX_LEN)


def _gqa_attn_kernel(q_ref, qn_ref, cos_ref, sin_ref, kc_ref, vc_ref, kl_ref, vl_ref, ol_ref, oc_ref):
    scale = HEAD_DIM ** -0.5 * LOG2E
    is_ctx = pl.program_id(2) == SEQ // TQ

    def query(rows, g):
        q = _rms(q_ref[rows, g * 128:(g + 1) * 128]) * qn_ref[...]
        return (_rope128(q, cos_ref[rows, :], sin_ref[rows, :]) * scale).astype(BF16)

    @pl.when(jnp.logical_not(is_ctx))
    def _():
        pieces = [slice(j * KEY_PIECE, (j + 1) * KEY_PIECE) for j in range(SEQ // KEY_PIECE)]
        qs = [query(slice(None), g) for g in range(A_GROUP)]
        for g in range(A_GROUP):
            o = _softmax_pv(qs[g], [kc_ref[...]] + [kl_ref[r, :] for r in pieces],
                            [vc_ref[...]] + [vl_ref[r, :] for r in pieces])
            ol_ref[:, g * 128:(g + 1) * 128] = o.astype(BF16)

    @pl.when(is_ctx)
    def _():
        rows = pl.ds(_ctx_half(), CTX_LEN)
        for g in range(A_GROUP):
            oc_ref[:, g * 128:(g + 1) * 128] = _softmax_pv(query(rows, g), [kc_ref[...]], [vc_ref[...]]).astype(BF16)


def _gqa_attention(p, k, vext, q_norm, cos, sin):
    qw = A_GROUP * HEAD_DIM
    nq = SEQ // TQ
    ctx_blk = N_LAT // CTX_LEN
    rope = pl.BlockSpec((TQ, 128), lambda b, h, t: (_rope_block(_attn_qrow(b, t), TQ), 0))
    return pl.pallas_call(
        _gqa_attn_kernel,
        out_shape=[jax.ShapeDtypeStruct((N_LAT, A_WIDTH), BF16), jax.ShapeDtypeStruct((N_CTX, A_WIDTH), BF16)],
        grid=(BATCH, A_KV_HEADS, nq + 1),
        in_specs=[
            pl.BlockSpec((TQ, qw), lambda b, h, t: (_attn_qrow(b, t), h)),
            pl.BlockSpec((1, 128), lambda b, h, t: (0, 0)),
            rope, rope,
            pl.BlockSpec((CTX_LEN, 128), lambda b, h, t: (ctx_blk + b, h)),
            pl.BlockSpec((CTX_LEN, 256), lambda b, h, t: (ctx_blk + b, h)),
            pl.BlockSpec((SEQ, 128), lambda b, h, t: (b, h)),
            pl.BlockSpec((SEQ, 256), lambda b, h, t: (b, h)),
        ],
        out_specs=[pl.BlockSpec((TQ, qw), lambda b, h, t: (b * nq + jnp.minimum(t, nq - 1), h)),
                   pl.BlockSpec((CTX_LEN, qw), lambda b, h, t: (b, h))],
        compiler_params=_cparams(("arbitrary",) * 3),
        name="gqa_attn",
    )(p, q_norm, cos, sin, k, vext, k, vext)


def _conv_kernel(bg_ref, cg_ref, ug_ref, cgp_ref, ugp_ref, cgn_ref, ugn_ref, w_ref, o_ref, *, rows):
    i = pl.program_id(0)
    per_seq = SEQ // rows
    is_ctx = i >= N_LAT // rows
    is_start = jnp.logical_or(is_ctx, i % per_seq == 0)
    is_end = jnp.logical_or(is_ctx, i % per_seq == per_seq - 1)
    m = cg_ref[...] * ug_ref[...]
    m_prev = jnp.where(is_start, 0.0, cgp_ref[7:8, :] * ugp_ref[7:8, :])
    m_next = jnp.where(is_end, 0.0, cgn_ref[0:1, :] * ugn_ref[0:1, :])
    row = lax.broadcasted_iota(jnp.int32, m.shape, 0)
    down = jnp.where(row == 0, m_prev, pltpu.roll(m, 1, 0))
    up = jnp.where(row == rows - 1, m_next, pltpu.roll(m, rows - 1, 0))
    conv = down * w_ref[0:1, :] + m * w_ref[1:2, :] + up * w_ref[2:3, :]
    o_ref[...] = (bg_ref[...] * conv).astype(BF16)


def _gated_conv(p, conv_w):
    rows = CTX_LEN
    base = (A_WIDTH + 2 * A_KV_WIDTH) // B_WIDTH
    halo = rows // 8
    last = N_TOK // 8 - 1
    main = lambda c: pl.BlockSpec((rows, B_WIDTH), lambda i: (i, base + c))
    prev = lambda c: pl.BlockSpec((8, B_WIDTH), lambda i: (jnp.maximum(i * halo - 1, 0), base + c))
    nxt = lambda c: pl.BlockSpec((8, B_WIDTH), lambda i: (jnp.minimum((i + 1) * halo, last), base + c))
    return pl.pallas_call(
        functools.partial(_conv_kernel, rows=rows),
        out_shape=jax.ShapeDtypeStruct((N_TOK, B_WIDTH), BF16),
        grid=(N_TOK // rows,),
        in_specs=[main(0), main(1), main(2), prev(1), prev(2), nxt(1), nxt(2),
                  pl.BlockSpec((3, B_WIDTH), lambda i: (0, 0))],
        out_specs=pl.BlockSpec((rows, B_WIDTH), lambda i: (i, 0)),
        compiler_params=_cparams(("arbitrary",)),
        name="gated_conv",
    )(p, p, p, p, p, p, p, conv_w)


def _mla_kv_kernel(ckv_ref, g_ref, w_ref, kr_ref, cos_ref, sin_ref, ko_ref, vo_ref):
    ckv = (_rms(ckv_ref[...]) * g_ref[...]).astype(BF16)
    kv = jnp.dot(ckv, w_ref[...], preferred_element_type=F32)
    x = kr_ref[...]
    lane = lax.broadcasted_iota(jnp.int32, x.shape, 1)
    r = jnp.where(lane < C_ROPE, _rope64(x, cos_ref[...], sin_ref[...]), 0.0)
    kr_low = r.astype(BF16)
    kr_high = pltpu.roll(r, 64, 1).astype(BF16)
    ones = _ones_column(x.shape[0])
    for h in range(C_HEADS):
        ko_ref[:, h * 256:h * 256 + 128] = kv[:, h * 256:h * 256 + 128].astype(BF16)
        ko_ref[:, h * 256 + 128:(h + 1) * 256] = kr_low if h % 2 == 0 else kr_high
        vo_ref[:, h * 256:h * 256 + 128] = kv[:, h * 256 + 128:(h + 1) * 256].astype(BF16)
        vo_ref[:, h * 256 + 128:(h + 1) * 256] = ones


def _mla_kv(p, kv_norm, w_ukv, layer, cos, sin):
    t = 512
    col = (C_Q_RANK + C_KV_RANK + D_WIDTH) // 128
    width = C_HEADS * 256
    return pl.pallas_call(
        _mla_kv_kernel,
        out_shape=[jax.ShapeDtypeStruct((N_TOK, width), BF16)] * 2,
        grid=(N_TOK // t,),
        in_specs=[
            pl.BlockSpec((t, C_KV_RANK), lambda i: (i, C_Q_RANK // C_KV_RANK)),
            pl.BlockSpec((1, C_KV_RANK), lambda i: (0, 0)),
            pl.BlockSpec((None, C_KV_RANK, width), lambda i: (layer, 0, 0)),
            pl.BlockSpec((t, 128), lambda i: (i, col)),
            pl.BlockSpec((t, 128), lambda i: (_rope_block(i, t), 0)),
            pl.BlockSpec((t, 128), lambda i: (_rope_block(i, t), 0)),
        ],
        out_specs=[pl.BlockSpec((t, width), lambda i: (i, 0))] * 2,
        compiler_params=_cparams(("arbitrary",)),
        name="mla_kv",
    )(p, kv_norm, w_ukv, p, cos, sin)


def _mla_attn_kernel(qn_ref, qr_ref, cos_ref, sin_ref, kc_ref, vc_ref, kl_ref, vl_ref, ol_ref, oc_ref):
    scale = (C_NOPE + C_ROPE) ** -0.5 * LOG2E
    is_ctx = pl.program_id(2) == SEQ // TQ

    def queries(rows):
        qr = _rope64(qr_ref[rows, :], cos_ref[rows, :], sin_ref[rows, :])
        lane = lax.broadcasted_iota(jnp.int32, qr.shape, 1)
        out = []
        for hh in range(2):
            sel = (lane < 64) if hh == 0 else (lane >= 64)
            q = jnp.concatenate([qn_ref[rows, hh * 128:(hh + 1) * 128], jnp.where(sel, qr, 0.0)], axis=1)
            out.append((q * scale).astype(BF16))
        return out

    @pl.when(jnp.logical_not(is_ctx))
    def _():
        for hh, q in enumerate(queries(slice(None))):
            blk = slice(hh * 256, (hh + 1) * 256)
            pieces = [slice(j * KEY_PIECE, (j + 1) * KEY_PIECE) for j in range(SEQ // KEY_PIECE)]
            o = _softmax_pv(q, [kc_ref[:, blk]] + [kl_ref[r, blk] for r in pieces],
                            [vc_ref[:, blk]] + [vl_ref[r, blk] for r in pieces])
            ol_ref[:, hh * 128:(hh + 1) * 128] = o.astype(BF16)

    @pl.when(is_ctx)
    def _():
        for hh, q in enumerate(queries(pl.ds(_ctx_half(), CTX_LEN))):
            blk = slice(hh * 256, (hh + 1) * 256)
            oc_ref[:, hh * 128:(hh + 1) * 128] = _softmax_pv(q, [kc_ref[:, blk]], [vc_ref[:, blk]]).astype(BF16)


def _mla_attention(q, kcat, vext, cos, sin):
    nq = SEQ // TQ
    ctx_blk = N_LAT // CTX_LEN
    rope_col = C_HEADS * C_NOPE // 128
    qrow = _attn_qrow
    rope = pl.BlockSpec((TQ, 128), lambda b, h, t: (_rope_block(qrow(b, t), TQ), 0))
    ctx = pl.BlockSpec((CTX_LEN, 512), lambda b, h, t: (ctx_blk + b, h))
    lat = pl.BlockSpec((SEQ, 512), lambda b, h, t: (b, h))
    return pl.pallas_call(
        _mla_attn_kernel,
        out_shape=[jax.ShapeDtypeStruct((N_LAT, C_WIDTH), BF16), jax.ShapeDtypeStruct((N_CTX, C_WIDTH), BF16)],
        grid=(BATCH, C_HEADS // 2, nq + 1),
        in_specs=[
            pl.BlockSpec((TQ, 256), lambda b, h, t: (qrow(b, t), h)),
            pl.BlockSpec((TQ, 128), lambda b, h, t: (qrow(b, t), rope_col + h)),
            rope, rope, ctx, ctx, lat, lat,
        ],
        out_specs=[pl.BlockSpec((TQ, 256), lambda b, h, t: (b * nq + jnp.minimum(t, nq - 1), h)),
                   pl.BlockSpec((CTX_LEN, 256), lambda b, h, t: (b, h))],
        compiler_params=_cparams(("arbitrary",) * 3),
        name="mla_attn",
    )(q, q, cos, sin, kcat, vext, kcat, vext)


def _s5_matrices(lam_re, lam_im, log_dt, b_re, b_im, c_re, c_im):
    hi = lax.Precision.HIGHEST
    lam_re, lam_im = lam_re.astype(F32), lam_im.astype(F32)
    dt = jnp.exp(log_dt.astype(F32))[..., None]
    ks = jnp.arange(S5_CHUNK + 1, dtype=F32)[:, None, None, None]
    mag = jnp.exp(lam_re[None] * dt[None] * ks)
    ang = lam_im[None] * dt[None] * ks
    pw_re, pw_im = mag * jnp.cos(ang), mag * jnp.sin(ang)
    a_re, a_im = pw_re[1], pw_im[1]
    den = lam_re * lam_re + lam_im * lam_im
    f_re = ((a_re - 1.0) * lam_re + a_im * lam_im) / den
    f_im = (a_im * lam_re - (a_re - 1.0) * lam_im) / den
    b_re, b_im = b_re.astype(F32), b_im.astype(F32)
    bb_re = f_re[..., None] * b_re - f_im[..., None] * b_im
    bb_im = f_re[..., None] * b_im + f_im[..., None] * b_re
    c_re, c_im = c_re.astype(F32), c_im.astype(F32)

    ab_re = pw_re[:S5_CHUNK, ..., None] * bb_re[None] - pw_im[:S5_CHUNK, ..., None] * bb_im[None]
    ab_im = pw_re[:S5_CHUNK, ..., None] * bb_im[None] + pw_im[:S5_CHUNK, ..., None] * bb_re[None]
    kern = (jnp.einsum('dgcn,ldgne->ldgce', c_re, ab_re, precision=hi)
            - jnp.einsum('dgcn,ldgne->ldgce', c_im, ab_im, precision=hi))
    t_idx = np.arange(S5_CHUNK)
    sub = lambda m, ax: m.reshape(m.shape[:ax] + (S5_QUADS, S5_GSUB) + m.shape[ax + 1:])
    klag = jnp.transpose(sub(kern, 2), (1, 2, 0, 5, 3, 4)).reshape(2, S5_QUADS, S5_CHUNK, S5_GROUP, 128)
    bt_re = jnp.transpose(bb_re, (0, 1, 3, 2))
    bt_im = jnp.transpose(bb_im, (0, 1, 3, 2))
    lane_gc = lambda c: jnp.transpose(sub(c, 0), (0, 3, 1, 2)).reshape(S5_QUADS, S5_STATE, 128)
    ps, qs = [], []
    for d in range(2):
        p_pow = (S5_CHUNK - 1 - t_idx) if d == 0 else t_idx
        ar, ai = pw_re[p_pow, d][:, :, None, :], pw_im[p_pow, d][:, :, None, :]
        pc = jnp.concatenate([ar * bt_re[d][None] - ai * bt_im[d][None],
                              ar * bt_im[d][None] + ai * bt_re[d][None]], axis=-1)
        pc = pc.reshape(S5_CHUNK, S5_QUADS, 128, 128)
        ps.append(jnp.transpose(pc, (1, 0, 2, 3)).reshape(S5_QUADS, S5_CHUNK * 128, 128))
        q_pow = (t_idx + 1) if d == 0 else (S5_CHUNK - t_idx)
        cl_re, cl_im = lane_gc(c_re[d])[:, None], lane_gc(c_im[d])[:, None]
        rep = lambda a: jnp.repeat(jnp.transpose(sub(a, 1), (1, 0, 3, 2)), S5_GROUP, axis=-1)
        al_re, al_im = rep(pw_re[q_pow, d]), rep(pw_im[q_pow, d])
        qs.append(jnp.concatenate([cl_re * al_re - cl_im * al_im,
                                   -(cl_re * al_im + cl_im * al_re)], axis=2))
    a16_re, a16_im = pw_re[S5_CHUNK], pw_im[S5_CHUNK]
    lanes = S5_GROUPS * 2 * S5_STATE
    return dict(
        klag=klag.astype(BF16),
        p=jnp.stack(ps).astype(BF16),
        q=jnp.stack(qs).astype(BF16),
        a_mul=jnp.concatenate([a16_re, a16_re], axis=-1).reshape(2, 1, lanes),
        a_swp=jnp.concatenate([-a16_im, a16_im], axis=-1).reshape(2, 1, lanes),
    )


def _s5_expand_p(pc):
    rep = jnp.concatenate([pc] * S5_GSUB, axis=1)
    row = lax.broadcasted_iota(jnp.int32, rep.shape, 0)
    col = lax.broadcasted_iota(jnp.int32, rep.shape, 1)
    return jnp.where((row // S5_GROUP) % S5_GSUB == col // 128, rep, jnp.zeros_like(rep))


def _s5_expand_q(q_ref):
    row = lax.broadcasted_iota(jnp.int32, (S5_GSUB * 128, 128), 0)
    col = lax.broadcasted_iota(jnp.int32, (S5_GSUB * 128, 128), 1)
    keep = row // 128 == col // S5_GROUP
    blocks = [jnp.where(keep, jnp.concatenate([q_ref[t]] * S5_GSUB, axis=0), jnp.zeros((S5_GSUB * 128, 128), BF16))
              for t in range(S5_CHUNK)]
    return jnp.concatenate(blocks, axis=1)


def _s5_chunk_rows(u_ref):
    n = N_TOK // S5_CHUNK
    return jnp.concatenate([u_ref[pl.ds(s, n, stride=S5_CHUNK), :] for s in range(S5_CHUNK)], axis=-1).astype(BF16)


def _s5_z_kernel(u_ref, p_ref, z_ref):
    z_ref[...] = jnp.dot(_s5_chunk_rows(u_ref), _s5_expand_p(p_ref[...]), preferred_element_type=F32)


def _s5_scan_kernel(z_ref, amul_ref, aswp_ref, s_ref, zs_scr):
    d = pl.program_id(0)
    a_mul, a_swp = amul_ref[...], aswp_ref[...]
    lanes = s_ref.shape[1]
    z = z_ref[...]
    low_half = lax.broadcasted_iota(jnp.int32, z.shape, 1) % 128 < S5_STATE
    zs_scr[...] = jnp.where(low_half, pltpu.roll(z, lanes - S5_STATE, 1), pltpu.roll(z, S5_STATE, 1))
    nl, nc = SEQ // S5_CHUNK, S5_CTX_CHUNKS

    def segment(base, count, carry):
        def body(step, carry):
            s, w = carry
            k = jnp.where(d == 0, step, count - 1 - step)
            rows = [pl.ds(base + b * count + k, 1) for b in range(BATCH)]
            for b in range(BATCH):
                s_ref[rows[b], :] = s[b:b + 1, :]
            z = jnp.concatenate([z_ref[r, :] for r in rows], axis=0)
            zs = jnp.concatenate([zs_scr[r, :] for r in rows], axis=0)
            return a_mul * s + a_swp * w + z, a_mul * w - a_swp * s + zs
        return lax.fori_loop(0, count, body, carry)

    zero = jnp.zeros((BATCH, lanes), F32)
    carry = segment(BATCH * nl, nc, (zero, zero))
    segment(0, nl, carry)


def _s5_y_kernel(u_ref, klag_ref, s_ref, q_ref, dv_ref, y_ref, m_scr):
    d = pl.program_id(1)
    n = N_TOK // S5_CHUNK
    row = lax.broadcasted_iota(jnp.int32, (128, 128), 0)
    col = lax.broadcasted_iota(jnp.int32, (128, 128), 1)
    same_group = row // S5_GROUP == col // S5_GROUP
    blocks = [jnp.where(same_group, jnp.concatenate([klag_ref[lag]] * S5_GSUB, axis=0), jnp.zeros((128, 128), BF16))
              for lag in range(S5_CHUNK)]
    for rev in range(2):
        @pl.when(d == rev)
        def _():
            for s in range(S5_CHUNK):
                for t in range(S5_CHUNK):
                    lag = (s - t) if rev else (t - s)
                    blk = blocks[lag] if lag >= 0 else jnp.zeros((128, 128), BF16)
                    m_scr[s * 128:(s + 1) * 128, t * 128:(t + 1) * 128] = blk
    y = jnp.dot(_s5_chunk_rows(u_ref), m_scr[...], preferred_element_type=F32)
    y = y + jnp.dot(s_ref[...].astype(BF16), _s5_expand_q(q_ref), preferred_element_type=F32)
    for t in range(S5_CHUNK):
        rows = pl.ds(t, n, stride=S5_CHUNK)
        yt = y[:, t * 128:(t + 1) * 128]

        @pl.when(d == 0)
        def _():
            y_ref[rows, :] = yt + u_ref[rows, :] * dv_ref[...]

        @pl.when(d == 1)
        def _():
            y_ref[rows, :] = y_ref[rows, :] + yt


def _s5(p, mats, layer, dskip):
    nstate = S5_GROUPS * 2 * S5_STATE
    qlanes = nstate // S5_QUADS
    slanes = qlanes
    ucol = (C_Q_RANK + C_KV_RANK) // 128
    z = pl.pallas_call(
        _s5_z_kernel,
        out_shape=jax.ShapeDtypeStruct((2, S5_ROWS, nstate), F32),
        grid=(2, S5_QUADS),
        in_specs=[
            pl.BlockSpec((N_TOK, 128), lambda d, q: (0, ucol + q)),
            pl.BlockSpec((None, None, None, S5_CHUNK * 128, 128), lambda d, q: (layer, d, q, 0, 0)),
        ],
        out_specs=pl.BlockSpec((None, S5_ROWS, qlanes), lambda d, q: (d, 0, q)),
        compiler_params=_cparams(("arbitrary", "arbitrary")),
        name="s5_chunk_state",
    )(p, mats["p"])
    s = pl.pallas_call(
        _s5_scan_kernel,
        out_shape=jax.ShapeDtypeStruct((2, S5_ROWS, nstate), F32),
        grid=(2, nstate // slanes),
        in_specs=[
            pl.BlockSpec((None, S5_ROWS, slanes), lambda d, q: (d, 0, q)),
            pl.BlockSpec((None, None, 1, slanes), lambda d, q: (layer, d, 0, q)),
            pl.BlockSpec((None, None, 1, slanes), lambda d, q: (layer, d, 0, q)),
        ],
        out_specs=pl.BlockSpec((None, S5_ROWS, slanes), lambda d, q: (d, 0, q)),
        scratch_shapes=[pltpu.VMEM((S5_ROWS, slanes), F32)],
        compiler_params=_cparams(("arbitrary", "arbitrary")),
        name="s5_scan",
    )(z, mats["a_mul"], mats["a_swp"])
    return pl.pallas_call(
        _s5_y_kernel,
        out_shape=jax.ShapeDtypeStruct((N_TOK, D_WIDTH), F32),
        grid=(S5_QUADS, 2),
        in_specs=[
            pl.BlockSpec((N_TOK, 128), lambda q, d: (0, ucol + q)),
            pl.BlockSpec((None, None, None, S5_CHUNK, S5_GROUP, 128), lambda q, d: (layer, d, q, 0, 0, 0)),
            pl.BlockSpec((None, S5_ROWS, qlanes), lambda q, d: (d, 0, q)),
            pl.BlockSpec((None, None, None, S5_CHUNK, 128, 128), lambda q, d: (layer, d, q, 0, 0, 0)),
            pl.BlockSpec((None, 1, 128), lambda q, d: (layer, 0, q)),
        ],
        out_specs=pl.BlockSpec((N_TOK, 128), lambda q, d: (0, q)),
        scratch_shapes=[pltpu.VMEM((S5_CHUNK * 128, S5_CHUNK * 128), BF16)],
        compiler_params=_cparams(("arbitrary", "arbitrary"), VMEM_LIMIT_LARGE),
        name="s5_output",
    )(p, mats["klag"], s, mats["q"], dskip)


def _glu_kernel(y_ref, w_ref, o_ref):
    y = y_ref[...]
    z = y * (0.5 * (1.0 + jnp.tanh(math.sqrt(2.0 / math.pi) * (y + 0.044715 * (y * y * y)))))
    gate = jnp.dot(z.astype(BF16), w_ref[...], preferred_element_type=F32)
    o_ref[...] = (z * jax.nn.sigmoid(gate)).astype(BF16)


def _s5_glu(y, w_glu, layer):
    t = 512
    return pl.pallas_call(
        _glu_kernel,
        out_shape=jax.ShapeDtypeStruct((N_TOK, D_WIDTH), BF16),
        grid=(N_TOK // t,),
        in_specs=[pl.BlockSpec((t, D_WIDTH), lambda i: (i, 0)),
                  pl.BlockSpec((None, D_WIDTH, D_WIDTH), lambda i: (layer, 0, 0))],
        out_specs=pl.BlockSpec((t, D_WIDTH), lambda i: (i, 0)),
        compiler_params=_cparams(("arbitrary",)),
        name="s5_glu",
    )(y, w_glu)


LANE_CHUNKS = D_MODEL // 128


def _store_token_major(ref, val):
    rows = val.shape[0]
    for c in range(LANE_CHUNKS):
        ref[pl.ds(c, rows, stride=LANE_CHUNKS), :] = val[:, c * 128:(c + 1) * 128]


def _load_token_major(ref, rows):
    return jnp.concatenate([ref[pl.ds(c, rows, stride=LANE_CHUNKS), :] for c in range(LANE_CHUNKS)], axis=-1)


def _router_kernel(x_ref, g_ref, sh_ref, sc_ref, w_ref, h_ref, info_ref, gw_ref, cnt_ref, h_scr, carry):
    i = pl.program_id(0)

    @pl.when(i == 0)
    def _():
        carry[...] = jnp.zeros_like(carry)

    for r in range(0, TM, ROW_CHUNK):
        rows = slice(r, r + ROW_CHUNK)
        h_scr[rows, :] = _modulated_norm(x_ref[rows, :], g_ref[...], sh_ref[...], sc_ref[...])
    _store_token_major(h_ref, h_scr[...])
    h = h_scr[...]
    h_hi = h.astype(BF16)
    h_lo = (h - h_hi.astype(F32)).astype(BF16)
    both = jnp.dot(h_hi, w_ref[...], preferred_element_type=F32)
    logits = both[:, :128] + both[:, 128:] + jnp.dot(h_lo, w_ref[:, :128], preferred_element_type=F32)
    lane = lax.broadcasted_iota(jnp.int32, logits.shape, 1)
    neg = -jnp.inf
    big = jnp.int32(1 << 20)

    def first_argmax(v, vmax):
        return jnp.min(jnp.where(v == vmax, lane, big), axis=-1, keepdims=True)

    lg = jnp.where(lane < N_GROUPS, logits, neg)
    mg = jnp.max(lg, axis=-1, keepdims=True)
    g_w = 1.0 / jnp.sum(jnp.exp(lg - mg), axis=-1, keepdims=True)
    g_idx = first_argmax(lg, mg)
    lo = N_GROUPS + EXPERTS_PER_GROUP * g_idx
    le = jnp.where(jnp.logical_and(lane >= lo, lane < lo + EXPERTS_PER_GROUP), logits, neg)
    m1 = jnp.max(le, axis=-1, keepdims=True)
    i1 = first_argmax(le, m1)
    le2 = jnp.where(lane == i1, neg, le)
    m2 = jnp.max(le2, axis=-1, keepdims=True)
    i2 = first_argmax(le2, m2)
    r21 = jnp.exp(m2 - m1)
    w1 = g_w / (1.0 + r21)
    w2 = g_w * r21 / (1.0 + r21)
    oh = jnp.logical_or(lane == i1, lane == i2)
    ohb = jnp.where(oh, 1.0, 0.0).astype(BF16)
    rr = lax.broadcasted_iota(jnp.int32, (TM, TM), 0)
    cc = lax.broadcasted_iota(jnp.int32, (TM, TM), 1)
    lower = jnp.where(rr > cc, 1.0, 0.0).astype(BF16)
    before = jnp.dot(lower, ohb, preferred_element_type=F32) + carry[...]
    rank1 = jnp.sum(jnp.where(lane == i1, before, 0.0), axis=-1, keepdims=True).astype(jnp.int32)
    rank2 = jnp.sum(jnp.where(lane == i2, before, 0.0), axis=-1, keepdims=True).astype(jnp.int32)
    carry[...] = carry[...] + jnp.sum(ohb.astype(F32), axis=0, keepdims=True)
    cnt_ref[...] = jnp.broadcast_to(carry[...], cnt_ref.shape)
    info = jnp.where(lane == 0, i1 - N_GROUPS, jnp.where(lane == 1, i2 - N_GROUPS,
                     jnp.where(lane == 2, rank1, jnp.where(lane == 3, rank2, 0))))
    info_ref[...] = info
    gw_ref[...] = jnp.where(lane == 0, w1, jnp.where(lane == 1, w2, 0.0))


def _router(x, g, mod, w_router, layer):
    rows = x.shape[0]
    return pl.pallas_call(
        _router_kernel,
        out_shape=[jax.ShapeDtypeStruct((rows * LANE_CHUNKS, 128), F32),
                   jax.ShapeDtypeStruct((rows, 128), jnp.int32),
                   jax.ShapeDtypeStruct((rows, 128), F32),
                   jax.ShapeDtypeStruct((8, 128), F32)],
        grid=(rows // TM,),
        in_specs=[
            pl.BlockSpec((TM, D_MODEL), lambda i: (i, 0)),
            pl.BlockSpec((1, D_MODEL), lambda i: (0, 0)),
            pl.BlockSpec((None, None, 1, D_MODEL), lambda i: (_mod_row(i, TM), 3, 0, 0)),
            pl.BlockSpec((None, None, 1, D_MODEL), lambda i: (_mod_row(i, TM), 4, 0, 0)),
            pl.BlockSpec((None, D_MODEL, 256), lambda i: (layer, 0, 0)),
        ],
        out_specs=[pl.BlockSpec((TM * LANE_CHUNKS, 128), lambda i: (i, 0)),
                   pl.BlockSpec((TM, 128), lambda i: (i, 0)),
                   pl.BlockSpec((TM, 128), lambda i: (i, 0)),
                   pl.BlockSpec((8, 128), lambda i: (0, 0))],
        scratch_shapes=[pltpu.VMEM((TM, D_MODEL), F32), pltpu.VMEM((1, 128), F32)],
        compiler_params=_cparams(("arbitrary",)),
        name="moe_router",
    )(x, g, mod, mod, w_router)


FFN_ISSUE_GROUPS = 8
GATHER_SLOTS = 6
GATHER_AHEAD = GATHER_SLOTS - 1
ISSUE_UNROLL = 8


def _ffn_kernel(pos_ref, te_ref, meta_ref, h_hbm, wg_ref, wu_ref, wd_ref, o_ref,
                src, xbuf, wg_b, wu_b, wd_b, sem):
    t = pl.program_id(0)
    nt = meta_ref[0]

    def row_copy(tile, slot, r):
        tok = src[tile * TE + r]
        return pltpu.make_async_copy(
            h_hbm.at[pl.ds(pl.multiple_of(tok * LANE_CHUNKS, LANE_CHUNKS), LANE_CHUNKS), :],
            xbuf.at[slot, pl.ds(pl.multiple_of(r * LANE_CHUNKS, LANE_CHUNKS), LANE_CHUNKS), :],
            sem.at[slot])

    def gather(tile, slot):
        def body(r, _):
            row_copy(tile, slot, r).start()
            return 0
        lax.fori_loop(0, TE, body, 0, unroll=ISSUE_UNROLL)

    def wait_tile(slot):
        pltpu.make_async_copy(xbuf.at[slot], xbuf.at[slot], sem.at[slot]).wait()

    @pl.when(t == 0)
    def _():
        for e in range(N_EXPERTS):
            def clear(i, _):
                src[i] = 0
                return 0
            lax.fori_loop(meta_ref[1 + e], meta_ref[1 + N_EXPERTS + e], clear, 0)

        def fill(tok, _):
            src[pos_ref[2 * tok]] = tok
            src[pos_ref[2 * tok + 1]] = tok
            return 0
        lax.fori_loop(0, pos_ref.shape[0] // 2, fill, 0, unroll=8)
        for k in range(GATHER_AHEAD):
            gather(k, k)

    @pl.when(t >= nt)
    def _():
        o_ref[...] = jnp.zeros_like(o_ref)

    @pl.when(t < nt)
    def _():
        slot = t % GATHER_SLOTS
        ahead = (t + GATHER_AHEAD) % GATHER_SLOTS
        wait_tile(slot)

        @pl.when(jnp.logical_or(t == 0, te_ref[t] != te_ref[jnp.maximum(t - 1, 0)]))
        def _():
            wg_b[...] = wg_ref[...].astype(BF16)
            wu_b[...] = wu_ref[...].astype(BF16)
            wd_b[...] = wd_ref[...].astype(BF16)

        nxt = jnp.minimum(t + GATHER_AHEAD, nt - 1)
        per = TE // FFN_ISSUE_GROUPS

        def issue(g):
            for r in range(g * per, (g + 1) * per):
                row_copy(nxt, ahead, r).start()

        x = _load_token_major(xbuf.at[slot], TE).astype(BF16)
        half = D_EXPERT // 2
        hg, hu = [], []
        for n in range(2):
            issue(n)
            hg.append(jnp.dot(x, wg_b[:, n * half:(n + 1) * half], preferred_element_type=F32))
        for n in range(2):
            issue(2 + n)
            hu.append(jnp.dot(x, wu_b[:, n * half:(n + 1) * half], preferred_element_type=F32))
        hg, hu = jnp.concatenate(hg, axis=1), jnp.concatenate(hu, axis=1)
        act = (hg * jax.nn.sigmoid(hg) * hu).astype(BF16)
        quarter = D_MODEL // 4
        for n in range(4):
            issue(4 + n)
            y = jnp.dot(act, wd_b[:, n * quarter:(n + 1) * quarter], preferred_element_type=F32)
            for c in range(quarter // 128):
                o_ref[pl.ds(n * (quarter // 128) + c, TE, stride=LANE_CHUNKS), :] = y[:, c * 128:(c + 1) * 128]

        @pl.when(t == nt - 1)
        def _():
            for k in range(1, GATHER_AHEAD + 1):
                wait_tile((t + k) % GATHER_SLOTS)


def _expert_ffn(pos_flat, tile_expert, meta, h, w_gate, w_up, w_down, layer):
    wsel = lambda t, pos, te, meta: (layer, te[t], 0, 0)
    return pl.pallas_call(
        _ffn_kernel,
        out_shape=jax.ShapeDtypeStruct((N_SORT * LANE_CHUNKS, 128), F32),
        grid_spec=pltpu.PrefetchScalarGridSpec(
            num_scalar_prefetch=3,
            grid=(N_ETILES,),
            in_specs=[
                pl.BlockSpec(memory_space=pl.ANY),
                pl.BlockSpec((None, None, D_MODEL, D_EXPERT), wsel),
                pl.BlockSpec((None, None, D_MODEL, D_EXPERT), wsel),
                pl.BlockSpec((None, None, D_EXPERT, D_MODEL), wsel),
            ],
            out_specs=pl.BlockSpec((TE * LANE_CHUNKS, 128), lambda t, pos, te, meta: (t, 0)),
            scratch_shapes=[
                pltpu.SMEM((N_SORT,), jnp.int32),
                pltpu.VMEM((GATHER_SLOTS, TE * LANE_CHUNKS, 128), F32),
                pltpu.VMEM((D_MODEL, D_EXPERT), BF16),
                pltpu.VMEM((D_MODEL, D_EXPERT), BF16),
                pltpu.VMEM((D_EXPERT, D_MODEL), BF16),
                pltpu.SemaphoreType.DMA((GATHER_SLOTS,)),
            ],
        ),
        compiler_params=_cparams(("arbitrary",), VMEM_LIMIT_LARGE),
        name="moe_expert_ffn",
    )(pos_flat, tile_expert, meta, h, w_gate, w_up, w_down)


TC = 256


def _combine_kernel(pos_ref, x_ref, gate_ref, gw_ref, g_ref, sh_ref, sc_ref, ys_hbm, *refs, final_norm):
    if final_norm:
        o_ref, buf_a, buf_b, sem = refs
    else:
        o_ref, h_ref, buf_a, buf_b, sem = refs
    i = pl.program_id(0)
    slot = i % 2

    def slab(ref, row):
        return ref.at[pl.ds(pl.multiple_of(row * LANE_CHUNKS, LANE_CHUNKS), LANE_CHUNKS), :]

    def gather(tile, slot):
        def body(r, _):
            tok = tile * TC + r
            pltpu.make_async_copy(slab(ys_hbm, pos_ref[2 * tok]), slab(buf_a.at[slot], r), sem.at[slot, 0]).start()
            pltpu.make_async_copy(slab(ys_hbm, pos_ref[2 * tok + 1]), slab(buf_b.at[slot], r), sem.at[slot, 1]).start()
            return 0
        lax.fori_loop(0, TC, body, 0, unroll=ISSUE_UNROLL)

    @pl.when(i == 0)
    def _():
        gather(0, 0)

    @pl.when(i + 1 < pl.num_programs(0))
    def _():
        gather(i + 1, 1 - slot)

    pltpu.make_async_copy(buf_a.at[slot], buf_a.at[slot], sem.at[slot, 0]).wait()
    pltpu.make_async_copy(buf_b.at[slot], buf_b.at[slot], sem.at[slot, 1]).wait()
    w0 = jnp.broadcast_to(gw_ref[:, 0:1], (TC, 128))
    w1 = jnp.broadcast_to(gw_ref[:, 1:2], (TC, 128))
    sq = jnp.zeros((TC, 128), F32)
    for c in range(LANE_CHUNKS):
        cols = slice(c * 128, (c + 1) * 128)
        rows = pl.ds(c, TC, stride=LANE_CHUNKS)
        y = x_ref[:, cols] + gate_ref[:, cols] * (w0 * buf_a[slot, rows, :] + w1 * buf_b[slot, rows, :])
        o_ref[:, cols] = y
        sq = sq + y * y
    inv = lax.rsqrt(jnp.sum(sq, axis=-1, keepdims=True) / D_MODEL + EPS)
    for c in range(LANE_CHUNKS):
        cols = slice(c * 128, (c + 1) * 128)
        normed = o_ref[:, cols] * inv * g_ref[:, cols]
        if final_norm:
            o_ref[:, cols] = normed
        else:
            h_ref[:, cols] = (normed * (1.0 + sc_ref[:, cols]) + sh_ref[:, cols]).astype(BF16)


def _combine(pos_flat, x, mod, gw, ys, g, mod_next):
    final_norm = mod_next is None
    rows = x.shape[0]
    tile = pl.BlockSpec((TC, D_MODEL), lambda i, pos: (i, 0))
    mod_row = lambda which: pl.BlockSpec((None, None, 1, D_MODEL), lambda i, pos: (_mod_row(i, TC), which, 0, 0))
    next_tab = mod if final_norm else mod_next
    out_shape = [jax.ShapeDtypeStruct((rows, D_MODEL), F32)]
    if not final_norm:
        out_shape.append(jax.ShapeDtypeStruct((rows, D_MODEL), BF16))
    return pl.pallas_call(
        functools.partial(_combine_kernel, final_norm=final_norm),
        out_shape=out_shape,
        grid_spec=pltpu.PrefetchScalarGridSpec(
            num_scalar_prefetch=1,
            grid=(rows // TC,),
            in_specs=[
                tile,
                mod_row(5),
                pl.BlockSpec((TC, 128), lambda i, pos: (i, 0)),
                pl.BlockSpec((1, D_MODEL), lambda i, pos: (0, 0)),
                mod_row(0), mod_row(1),
                pl.BlockSpec(memory_space=pl.ANY),
            ],
            out_specs=[tile] * len(out_shape),
            scratch_shapes=[pltpu.VMEM((2, TC * LANE_CHUNKS, 128), F32), pltpu.VMEM((2, TC * LANE_CHUNKS, 128), F32),
                            pltpu.SemaphoreType.DMA((2, 2))],
        ),
        compiler_params=_cparams(("arbitrary",)),
        name="moe_combine",
    )(pos_flat, x, mod, gw, g, next_tab, next_tab, ys)


def _moe(x, g, mod, w_router, w_gate, w_up, w_down, layer, g_next, mod_next):
    h, info, gw, cnt = _router(x, g, mod, w_router, layer)
    counts = cnt[0, N_GROUPS:N_GROUPS + N_EXPERTS].astype(jnp.int32)
    padded = ((counts + TE - 1) // TE) * TE
    ends = jnp.cumsum(padded)
    starts = ends - padded
    experts = jnp.arange(N_EXPERTS, dtype=jnp.int32)
    start_of = jnp.sum(jnp.where(info[:, 0:2, None] == experts, starts, 0), axis=-1)
    pos_flat = (start_of + info[:, 2:4]).reshape(-1)
    tile_ends = ends // TE
    num_tiles = tile_ends[-1]
    tiles = jnp.minimum(jnp.arange(N_ETILES, dtype=jnp.int32), num_tiles - 1)
    tile_expert = jnp.sum((tile_ends[None, :] <= tiles[:, None]).astype(jnp.int32), axis=-1)
    meta = jnp.concatenate([num_tiles[None], starts + counts, ends]).astype(jnp.int32)
    ys = _expert_ffn(pos_flat, tile_expert, meta, h, w_gate, w_up, w_down, layer)
    return _combine(pos_flat, x, mod, gw, ys, g_next, mod_next)


def kernel(x, c, ctx, c_ctx, mod_w, mod_b, norm_mix, norm_ffn, ab_w_in, ab_q_norm, ab_k_norm, ab_conv_w, ab_w_out, cd_w_in, cd_q_norm, cd_kv_norm, cd_w_uq, cd_w_ukv, s5_lam_re, s5_lam_im, s5_log_dt, s5_b_re, s5_b_im, s5_c_re, s5_c_im, s5_d, s5_w_glu, cd_w_out, moe_w_group, moe_w_expert, moe_w_gate, moe_w_up, moe_w_down, final_norm):
    cc = jnp.concatenate([c, c_ctx[None, :], jnp.zeros((8 - BATCH - 1, D_MODEL), F32)], axis=0)
    mods = _modulation(cc, mod_w, mod_b).reshape(DEPTH, 8, N_MOD, 1, D_MODEL)
    xs, h = _embed(x.reshape(N_LAT, D_MODEL), ctx.reshape(N_CTX, D_MODEL), norm_mix[0][None, :], mods[0])
    cos_a, sin_a = _rope_tables(HEAD_DIM)
    cos_c, sin_c = _rope_tables(C_ROPE)

    ab_in_b, ab_out_b = ab_w_in.astype(BF16), ab_w_out.astype(BF16)
    cd_out_b, ukv_b, glu_b = cd_w_out.astype(BF16), cd_w_ukv.astype(BF16), s5_w_glu.astype(BF16)
    a, b_ = C_Q_RANK + C_KV_RANK, C_Q_RANK + C_KV_RANK + C_ROPE
    pad = jnp.zeros(cd_w_in.shape[:2] + (CD_IN_PAD - cd_w_in.shape[2],), F32)
    cd_in_b = jnp.concatenate([cd_w_in[..., :a], cd_w_in[..., b_:], cd_w_in[..., a:b_], pad], axis=-1).astype(BF16)
    w_uq = cd_w_uq.reshape(-1, C_Q_RANK, C_HEADS, C_NOPE + C_ROPE)
    uq_b = jnp.concatenate([w_uq[..., :C_NOPE].reshape(-1, C_Q_RANK, C_HEADS * C_NOPE),
                            w_uq[..., C_NOPE:].reshape(-1, C_Q_RANK, C_HEADS * C_ROPE)], axis=-1).astype(BF16)

    mats = jax.vmap(_s5_matrices)(s5_lam_re, s5_lam_im, s5_log_dt, s5_b_re, s5_b_im, s5_c_re, s5_c_im)
    w_router = jnp.concatenate(
        [moe_w_group, jnp.transpose(moe_w_expert, (0, 2, 1, 3)).reshape(DEPTH, D_MODEL, N_EXPERTS),
         jnp.zeros((DEPTH, D_MODEL, 128 - N_GROUPS - N_EXPERTS), F32)], axis=-1)
    w_router_hi = w_router.astype(BF16)
    w_router = jnp.concatenate([w_router_hi, (w_router - w_router_hi.astype(F32)).astype(BF16)], axis=-1)

    for i in range(DEPTH):
        j = i // 2
        mod = mods[i]
        if i % 2 == 0:
            p = _linear(h, ab_in_b, j, 2048)
            k, vext = _gqa_kv(p, ab_k_norm[j][None, :], cos_a, sin_a)
            o_lat, o_ctx = _gqa_attention(p, k, vext, ab_q_norm[j][None, :], cos_a, sin_a)
            side = _gated_conv(p, ab_conv_w[j])
            w_out = ab_out_b
        else:
            p = _linear(h, cd_in_b, j, CD_IN_PAD)
            q = _norm_linear(p, 0, C_Q_RANK, cd_q_norm[j][None, :], uq_b, j)
            kcat, vext = _mla_kv(p, cd_kv_norm[j][None, :], ukv_b, j, cos_c, sin_c)
            o_lat, o_ctx = _mla_attention(q, kcat, vext, cos_c, sin_c)
            y = _s5(p, mats, j, s5_d.astype(F32)[:, None, :])
            side = _s5_glu(y, glu_b, j)
            w_out = cd_out_b
        xs = _out_linear(o_lat, o_ctx, side, w_out, j, xs, mod, 2, N_TOK if i < DEPTH - 1 else N_LAT)
        if i < DEPTH - 1:
            xs, h = _moe(xs, norm_ffn[i][None, :], mod, w_router, moe_w_gate, moe_w_up, moe_w_down, i,
                         norm_mix[i + 1][None, :], mods[i + 1])
        else:
            out, = _moe(xs, norm_ffn[i][None, :], mod, w_router, moe_w_gate, moe_w_up, moe_w_down, i,
                        final_norm[None, :], None)
    return out.reshape(BATCH, SEQ, D_MODEL)
```
